```python
import jax, jax.numpy as jnp
from jax import lax
import numpy as np

D_MODEL = 1024
BATCH = 2
SEQ = 8192
DEPTH = 2

GRID_W = 64
CTX_LEN = 256
EPS = 1e-6
N_MOD = 9

MIX_W = D_MODEL
POOL_W = MIX_W // 4
GLA_W = MIX_W // 4
ATT_W = MIX_W // 4
FNET_W = MIX_W - POOL_W - GLA_W - ATT_W

POOL_WINDOWS = (2, 4, 8, 16)
POOL_GW = POOL_W // len(POOL_WINDOWS)

GLA_HEADS = 4
GLA_DV = GLA_W // GLA_HEADS
GLA_DK = GLA_DV // 2
GLA_RANK = 16
GLA_TAU = 16.0
GLA_CHUNK = 64

ATT_HD = 64
ATT_QH = ATT_W // ATT_HD
ATT_KVH = 2
ATT_GROUP = ATT_QH // ATT_KVH
ROPE_FREQS = ATT_HD // 4
ROPE_THETA = 10000.0
Q_BLOCK = 128

FNET_HEADS = 4
FNET_HD = FNET_W // FNET_HEADS

D_FF = 2816

PROJ_SIZES = (POOL_W,
              GLA_HEADS * GLA_DK, GLA_HEADS * GLA_DK, GLA_W, GLA_RANK, GLA_RANK, GLA_W,
              ATT_QH * ATT_HD, ATT_KVH * ATT_HD, ATT_KVH * ATT_HD,
              FNET_W)
D_IN = sum(PROJ_SIZES)

kernel_name = 'hybrid_parallel_group_diffusion_block'


def rmsnorm(x, g):
    xf = x.astype(jnp.float32)
    y = xf * lax.rsqrt(jnp.mean(xf * xf, axis=-1, keepdims=True) + EPS)
    return (y * g.astype(jnp.float32)).astype(x.dtype)


def adaln(h, g, shift, scale):
    return rmsnorm(h, g) * (1 + scale) + shift


def swiglu(h, wg, wu, wd):
    return (jax.nn.silu(h @ wg) * (h @ wu)) @ wd


def half_ffn(h, g, m, wg, wu, wd):
    return h + 0.5 * m[2] * swiglu(adaln(h, g, m[0], m[1]), wg, wu, wd)


def split_proj(p):
    cuts = [int(v) for v in np.cumsum(PROJ_SIZES)[:-1]]
    return jnp.split(p, cuts, axis=-1)


def pool_mix(u, w_pool, s_pool):
    B, N, _ = u.shape
    uf = u.astype(jnp.float32)
    cs = jnp.concatenate([jnp.zeros((B, 1, POOL_W), jnp.float32), jnp.cumsum(uf, axis=1)], axis=1)
    t = jnp.arange(N)
    groups = []
    for gi, w in enumerate(POOL_WINDOWS):
        sl = slice(gi * POOL_GW, (gi + 1) * POOL_GW)
        lo = jnp.clip(t - w // 2, 0, N)
        hi = jnp.clip(t + w - w // 2, 0, N)
        cnt = (hi - lo).astype(jnp.float32)[None, :, None]
        csg = cs[:, :, sl]
        groups.append((csg[:, hi] - csg[:, lo]) / cnt - uf[:, :, sl])
    m = jnp.stack(groups, axis=2).astype(u.dtype)
    y = jnp.einsum('bngc,gcd->bngd', m, w_pool).reshape(B, N, POOL_W)
    return y * s_pool


def gla_scan(q, k, v, g, s0):
    B, H, N, _ = k.shape
    C = GLA_CHUNK
    nc = N // C
    with_out = q is not None

    def to_chunks(a):
        return jnp.moveaxis(a.reshape(B, H, nc, C, a.shape[-1]), 2, 0)

    mask = jnp.tril(jnp.ones((C, C), dtype=bool))

    def step(S, inp):
        if with_out:
            qi, ki, vi, gi = inp
        else:
            ki, vi, gi = inp
        b = jnp.cumsum(gi.astype(jnp.float32), axis=2)
        b_last = b[:, :, -1:, :]
        kf = ki.astype(jnp.float32)
        vf = vi.astype(jnp.float32)
        S_new = jnp.exp(b_last[:, :, 0, :, None]) * S + jnp.einsum('bhck,bhcv->bhkv', kf * jnp.exp(b_last - b), vf)
        if not with_out:
            return S_new, None
        qf = qi.astype(jnp.float32)
        o_inter = jnp.einsum('bhck,bhkv->bhcv', qf * jnp.exp(b), S)
        diff = b[:, :, :, None, :] - b[:, :, None, :, :]
        decay = jnp.exp(jnp.where(mask[:, :, None], diff, -jnp.inf))
        A = jnp.einsum('bhik,bhjk,bhijk->bhij', qf, kf, decay)
        return S_new, o_inter + jnp.einsum('bhij,bhjv->bhiv', A, vf)

    xs = (to_chunks(q), to_chunks(k), to_chunks(v), to_chunks(g)) if with_out else (to_chunks(k), to_chunks(v), to_chunks(g))
    S, o = lax.scan(step, s0, xs)
    if not with_out:
        return None, S
    return jnp.moveaxis(o, 0, 2).reshape(B, H, N, v.shape[-1]), S


def gla_prep(parts, wa, ba):
    q, k, v, rf, rb, og = parts
    B, N, _ = q.shape

    def hd(a, d):
        return a.reshape(B, N, GLA_HEADS, d).transpose(0, 2, 1, 3)

    def log_decay(r, w, b):
        return hd(jax.nn.log_sigmoid((r @ w + b).astype(jnp.float32)) / GLA_TAU, GLA_DK)

    return (hd(q, GLA_DK) * GLA_DK ** -0.5, hd(k, GLA_DK), hd(v, GLA_DV),
            log_decay(rf, wa[0], ba[0]), log_decay(rb, wa[1], ba[1]), og)


def gla_out(o, og, g):
    B, H, N, _ = o.shape
    o = rmsnorm(o, g).transpose(0, 2, 1, 3).reshape(B, N, GLA_W).astype(og.dtype)
    return o * jax.nn.silu(og)


def gla_mix(c_parts, l_parts, wa, ba, g_norm, ctx_out):
    qc, kc, vc, fc, bc, ogc = gla_prep(c_parts, wa, ba)
    ql, kl, vl, fl, bl, ogl = gla_prep(l_parts, wa, ba)
    B = ql.shape[0]
    s0 = jnp.zeros((B, GLA_HEADS, GLA_DK, GLA_DV), jnp.float32)
    rev = lambda a: jnp.flip(a, axis=2)
    oc_f, sc_f = gla_scan(qc if ctx_out else None, kc, vc, fc, s0)
    oc_b, sc_b = gla_scan(rev(qc) if ctx_out else None, rev(kc), rev(vc), rev(bc), s0)
    ol_f, _ = gla_scan(ql, kl, vl, fl, sc_f)
    ol_b, _ = gla_scan(rev(ql), rev(kl), rev(vl), rev(bl), sc_b)
    y_lat = gla_out(ol_f + rev(ol_b), ogl, g_norm)
    if not ctx_out:
        return None, y_lat
    return gla_out(oc_f + rev(oc_b), ogc, g_norm), y_lat


def rope_tables(n):
    rows = n // GRID_W
    row_id = jnp.repeat(jnp.arange(rows, dtype=jnp.float32), GRID_W)
    col_id = jnp.tile(jnp.arange(GRID_W, dtype=jnp.float32), rows)
    freqs = ROPE_THETA ** (-jnp.arange(ROPE_FREQS, dtype=jnp.float32) / ROPE_FREQS)
    ang = jnp.stack([row_id[:, None] * freqs, col_id[:, None] * freqs], axis=1)
    return jnp.cos(ang), jnp.sin(ang)


def rope_2d(x, cos, sin):
    B, N, H, _ = x.shape
    xr = x.astype(jnp.float32).reshape(B, N, H, 2, 2, ROPE_FREQS)
    x1, x2 = xr[..., 0, :], xr[..., 1, :]
    c = cos[None, :, None]
    s = sin[None, :, None]
    out = jnp.stack([x1 * c - x2 * s, x1 * s + x2 * c], axis=-2)
    return out.reshape(B, N, H, ATT_HD).astype(x.dtype)


def gqa_mix(qc, kc, vc, ql, kl, vl, q_g, k_g, cos, sin, ctx_out):
    B, L, _ = kc.shape
    N = ql.shape[1]
    scale = ATT_HD ** -0.5

    def heads(a, h):
        return a.reshape(a.shape[0], a.shape[1], h, ATT_HD)

    kc = rmsnorm(heads(kc, ATT_KVH), k_g)
    vc = heads(vc, ATT_KVH)
    kl = rope_2d(rmsnorm(heads(kl, ATT_KVH), k_g), cos, sin)
    vl = heads(vl, ATT_KVH)
    ql = rope_2d(rmsnorm(heads(ql, ATT_QH), q_g), cos, sin)
    keys = jnp.concatenate([kl, kc], axis=1)
    vals = jnp.concatenate([vl, vc], axis=1)
    nb = N // Q_BLOCK
    qb = ql.reshape(B, nb, Q_BLOCK, ATT_KVH, ATT_GROUP, ATT_HD).swapaxes(0, 1)

    def attend(q_blk):
        s = jnp.einsum('bqkgd,bskd->bkgqs', q_blk, keys).astype(jnp.float32) * scale
        p = jax.nn.softmax(s, axis=-1).astype(vals.dtype)
        return jnp.einsum('bkgqs,bskd->bqkgd', p, vals)

    ol = lax.map(attend, qb).swapaxes(0, 1).reshape(B, N, ATT_W)
    if not ctx_out:
        return None, ol
    qc5 = rmsnorm(heads(qc, ATT_QH), q_g).reshape(B, L, ATT_KVH, ATT_GROUP, ATT_HD)
    s = jnp.einsum('blkgd,bmkd->bkglm', qc5, kc).astype(jnp.float32) * scale
    p = jax.nn.softmax(s, axis=-1).astype(vc.dtype)
    oc = jnp.einsum('bkglm,bmkd->blkgd', p, vc).reshape(B, L, ATT_W)
    return oc, ol


def fnet_mix(u, w_f):
    B, N, _ = u.shape
    z = u.astype(jnp.float32).reshape(B, N, FNET_HEADS, FNET_HD).transpose(0, 2, 1, 3)
    f = jnp.fft.fft2(z, norm='ortho').real
    f = f.transpose(0, 2, 1, 3).reshape(B, N, FNET_W).astype(u.dtype)
    return f @ w_f


def token_mix(pc, pl, pool_w, pool_scale, gla_wa, gla_ba, gla_norm, q_g, k_g, fnet_w, cos, sin, ctx_out):
    cp = split_proj(pc)
    lp = split_proj(pl)
    gc, gl = gla_mix(cp[1:7], lp[1:7], gla_wa, gla_ba, gla_norm, ctx_out)
    ac, al = gqa_mix(cp[7], cp[8], cp[9], lp[7], lp[8], lp[9], q_g, k_g, cos, sin, ctx_out)
    y_lat = jnp.concatenate([pool_mix(lp[0], pool_w, pool_scale), gl, al, fnet_mix(lp[10], fnet_w)], axis=-1)
    if not ctx_out:
        return None, y_lat
    y_ctx = jnp.concatenate([pool_mix(cp[0], pool_w, pool_scale), gc, ac, fnet_mix(cp[10], fnet_w)], axis=-1)
    return y_ctx, y_lat


def setup_inputs(seed: int = 0) -> dict:
    key = jax.random.key(seed)
    ks = jax.random.split(key, 24)
    f32 = jnp.float32
    L, D = DEPTH, D_MODEL

    def nrm(k, shape, s):
        return jax.random.normal(k, shape, f32) * s

    return {
        'x': nrm(ks[0], (BATCH, SEQ, D), 1.0),
        'c': nrm(ks[1], (BATCH, D), 1.0),
        'ctx': nrm(ks[2], (BATCH, CTX_LEN, D), 1.0),
        'c_ctx': nrm(ks[3], (D,), 1.0),
        'w_mod': nrm(ks[4], (L, D, N_MOD * D), 0.5 * D ** -0.5),
        'b_mod': nrm(ks[5], (L, N_MOD * D), 0.01),
        'norm_g': 1.0 + nrm(ks[6], (L, 3, D), 0.02),
        'ffn_wg': nrm(ks[7], (L, 2, D, D_FF), D ** -0.5),
        'ffn_wu': nrm(ks[8], (L, 2, D, D_FF), D ** -0.5),
        'ffn_wd': nrm(ks[9], (L, 2, D_FF, D), D_FF ** -0.5),
        'w_in': nrm(ks[10], (L, D, D_IN), D ** -0.5),
        'w_out': nrm(ks[11], (L, MIX_W, D), MIX_W ** -0.5),
        'pool_w': nrm(ks[12], (L, len(POOL_WINDOWS), POOL_GW, POOL_GW), POOL_GW ** -0.5),
        'pool_scale': 1.0 + nrm(ks[13], (L, POOL_W), 0.1),
        'gla_wa': nrm(ks[14], (L, 2, GLA_RANK, GLA_HEADS * GLA_DK), GLA_RANK ** -0.5),
        'gla_ba': nrm(ks[15], (L, 2, GLA_HEADS * GLA_DK), 0.1),
        'gla_norm': 1.0 + nrm(ks[16], (L, GLA_DV), 0.02),
        'att_qnorm': 1.0 + nrm(ks[17], (L, ATT_HD), 0.02),
        'att_knorm': 1.0 + nrm(ks[18], (L, ATT_HD), 0.02),
        'fnet_w': nrm(ks[19], (L, FNET_W, FNET_W), FNET_W ** -0.5),
        'final_norm': 1.0 + nrm(ks[20], (D,), 0.02),
    }


def reference(x, c, ctx, c_ctx, w_mod, b_mod, norm_g, ffn_wg, ffn_wu, ffn_wd, w_in, w_out,
              pool_w, pool_scale, gla_wa, gla_ba, gla_norm, att_qnorm, att_knorm, fnet_w, final_norm):
    N = x.shape[1]
    cos, sin = rope_tables(N)
    xc = ctx
    for i in range(DEPTH):
        ctx_out = i < DEPTH - 1
        ml = jnp.split((jax.nn.silu(c) @ w_mod[i] + b_mod[i])[:, None, :], N_MOD, axis=-1)
        mc = jnp.split(jax.nn.silu(c_ctx) @ w_mod[i] + b_mod[i], N_MOD, axis=-1)
        x = half_ffn(x, norm_g[i, 0], ml[0:3], ffn_wg[i, 0], ffn_wu[i, 0], ffn_wd[i, 0])
        xc = half_ffn(xc, norm_g[i, 0], mc[0:3], ffn_wg[i, 0], ffn_wu[i, 0], ffn_wd[i, 0])
        pl = adaln(x, norm_g[i, 1], ml[3], ml[4]) @ w_in[i]
        pc = adaln(xc, norm_g[i, 1], mc[3], mc[4]) @ w_in[i]
        yc, yl = token_mix(pc, pl, pool_w[i], pool_scale[i], gla_wa[i], gla_ba[i], gla_norm[i],
                           att_qnorm[i], att_knorm[i], fnet_w[i], cos, sin, ctx_out)
        x = x + ml[5] * (yl @ w_out[i])
        x = half_ffn(x, norm_g[i, 2], ml[6:9], ffn_wg[i, 1], ffn_wu[i, 1], ffn_wd[i, 1])
        if ctx_out:
            xc = xc + mc[5] * (yc @ w_out[i])
            xc = half_ffn(xc, norm_g[i, 2], mc[6:9], ffn_wg[i, 1], ffn_wu[i, 1], ffn_wd[i, 1])
    return rmsnorm(x, final_norm)
```

```python
import functools
import math

import jax
import jax.numpy as jnp
import numpy as np
from jax import lax
from jax.experimental import pallas as pl
from jax.experimental.pallas import tpu as pltpu

F32 = jnp.float32
BF16 = jnp.bfloat16

D_MODEL = 1024
DEPTH = 2
GRID_W = 64
EPS = 1e-6
N_MOD = 9
D_FF = 2816

POOL_W = 256
POOL_WINDOWS = (2, 4, 8, 16)
POOL_GW = 64
POOL_HALO = 8

GLA_HEADS = 4
GLA_W = 256
GLA_DV = 64
GLA_DK = 32
GLA_RANK = 16
GLA_TAU = 16.0
GLA_KW = GLA_HEADS * GLA_DK
GLA_SUB = 16

ATT_W = 256
ATT_HD = 64
ATT_QH = 4
ATT_KVH = 2
ATT_KW = ATT_KVH * ATT_HD
ROPE_FREQS = 16
ROPE_THETA = 10000.0

FNET_W = 256
FNET_HEADS = 4
FNET_HD = 64
FFT_N1 = 64

D_IN = 1824
D_IN_PAD = 1920

O_POOL, O_GQ, O_GK, O_GV, O_OG, O_AQ, O_AK, O_AV, O_FN, O_R = 0, 256, 384, 512, 768, 1024, 1280, 1408, 1536, 1792

TM = 512
TQ = 256
TP = 256
FF_CHUNK = 1408
NEG_BIG = -1e30

VMEM_LIMIT = 56 * 1024 * 1024


def _cparams(sem):
    return pltpu.CompilerParams(dimension_semantics=sem, vmem_limit_bytes=VMEM_LIMIT)


def _dot(a, b):
    return jnp.dot(a, b, preferred_element_type=F32)


def _split(a):
    hi = a.astype(BF16)
    lo = (a - hi.astype(F32)).astype(BF16)
    return hi, lo


def _dot3(a_hi, a_lo, b_hi, b_lo):
    return _dot(a_hi, b_hi) + (_dot(a_hi, b_lo) + _dot(a_lo, b_hi))


def _rms(x):
    return x * lax.rsqrt(jnp.mean(x * x, axis=-1, keepdims=True) + EPS)


def _silu(x):
    return x * jax.nn.sigmoid(x)


def _group_rms(x, e, width):
    s_hi, s_lo = _split(x * x)
    ss = _dot(s_hi, e) + _dot(s_lo, e)
    return x * lax.rsqrt(ss * (1.0 / width) + EPS)


def _mod_kernel(c_ref, w_ref, b_ref, o_ref):
    s = _silu(c_ref[...]).astype(BF16)
    o_ref[...] = _dot(s, w_ref[...].astype(BF16)) + b_ref[...]


def _modulation(cvec, w_mod, b_mod):
    depth, d, nd = w_mod.shape
    tn = nd // 8
    return pl.pallas_call(
        _mod_kernel,
        out_shape=jax.ShapeDtypeStruct((depth, 8, nd), F32),
        grid=(depth, nd // tn),
        in_specs=[
            pl.BlockSpec((8, d), lambda l, j: (0, 0)),
            pl.BlockSpec((None, d, tn), lambda l, j: (l, 0, j)),
            pl.BlockSpec((None, 1, tn), lambda l, j: (l, 0, j)),
        ],
        out_specs=pl.BlockSpec((None, 8, tn), lambda l, j: (l, 0, j)),
        compiler_params=_cparams(("arbitrary", "arbitrary")),
        name="modulation",
    )(cvec, w_mod, b_mod.reshape(depth, 1, nd))


def _ffn_kernel(x_ref, m_ref, g_ref, wg_ref, wu_ref, wd_ref, fg_ref, o_ref, *, mod_base, final):
    x = x_ref[...]
    shift = m_ref[mod_base:mod_base + 1, :]
    scale = m_ref[mod_base + 1:mod_base + 2, :]
    gate = m_ref[mod_base + 2:mod_base + 3, :]
    h = ((_rms(x) * g_ref[...]) * (1.0 + scale) + shift).astype(BF16)
    y = jnp.zeros(x.shape, F32)
    for c in range(D_FF // FF_CHUNK):
        sl = slice(c * FF_CHUNK, (c + 1) * FF_CHUNK)
        a = _dot(h, wg_ref[:, sl])
        u = _dot(h, wu_ref[:, sl])
        y = y + _dot((_silu(a) * u).astype(BF16), wd_ref[sl, :])
    out = x + (0.5 * gate) * y
    if final:
        out = _rms(out) * fg_ref[...]
    o_ref[...] = out


def _mod_row_map(n_lat_tiles, tiles_per_batch, n_batch):
    def index_map(t):
        return (jnp.where(t < n_lat_tiles, t // tiles_per_batch, n_batch), 0, 0)
    return index_map


def _resident(shape):
    zeros = (0,) * len(shape)
    return pl.BlockSpec(shape, lambda t: zeros, pipeline_mode=pl.Buffered(1))


def _half_ffn(x, mods, g, wg, wu, wd, fg, *, mod_base, n_tiles, mod_map, final=False):
    d = x.shape[1]
    return pl.pallas_call(
        functools.partial(_ffn_kernel, mod_base=mod_base, final=final),
        out_shape=jax.ShapeDtypeStruct((n_tiles * TM, d), F32),
        grid=(n_tiles,),
        in_specs=[
            pl.BlockSpec((TM, d), lambda t: (t, 0)),
            pl.BlockSpec((None, N_MOD, d), mod_map),
            pl.BlockSpec((1, d), lambda t: (0, 0)),
            _resident(wg.shape),
            _resident(wu.shape),
            _resident(wd.shape),
            pl.BlockSpec((1, d), lambda t: (0, 0)),
        ],
        out_specs=pl.BlockSpec((TM, d), lambda t: (t, 0)),
        compiler_params=_cparams(("arbitrary",)),
        name="half_ffn",
    )(x, mods, g.reshape(1, d), wg, wu, wd, fg.reshape(1, d))


def _rope(x, c, sa, sb):
    w = x.shape[1]
    return x * c + pltpu.roll(x, w - ROPE_FREQS, 1) * sa + pltpu.roll(x, ROPE_FREQS, 1) * sb


def _inproj_kernel(x_ref, m_ref, g_ref, w_ref, wa_ref, ba_ref, e_ref, qg_ref, kg_ref,
                   rc_ref, rsa_ref, rsb_ref,
                   up_ref, gq_ref, gk_ref, gv_ref, og_ref, gf_ref, gb_ref,
                   aq_ref, ak_ref, av_ref, uf_ref):
    x = x_ref[...]
    shift = m_ref[3:4, :]
    scale = m_ref[4:5, :]
    h = ((_rms(x) * g_ref[...]) * (1.0 + scale) + shift).astype(BF16)
    p = _dot(h, w_ref[...])

    up_ref[...] = p[:, O_POOL:O_POOL + POOL_W]
    uf_ref[...] = p[:, O_FN:O_FN + FNET_W]

    gq_ref[...] = p[:, O_GQ:O_GQ + GLA_KW] * (GLA_DK ** -0.5)
    gk_ref[...] = p[:, O_GK:O_GK + GLA_KW]
    gv_ref[...] = p[:, O_GV:O_GV + GLA_W]
    og_ref[...] = p[:, O_OG:O_OG + GLA_W]
    z = _dot(p[:, O_R:O_R + 128].astype(BF16), wa_ref[...]) + ba_ref[...]
    logsig = jnp.minimum(z, 0.0) - jnp.log(1.0 + jnp.exp(-jnp.abs(z)))
    gdec = logsig * (1.0 / GLA_TAU)
    gf_ref[...] = gdec[:, :GLA_KW]
    gb_ref[...] = gdec[:, GLA_KW:]

    e = e_ref[...]
    rc, rsa, rsb = rc_ref[...], rsa_ref[...], rsb_ref[...]
    q = _group_rms(p[:, O_AQ:O_AQ + ATT_W], e, ATT_HD) * qg_ref[...]
    q = _rope(q, rc, rsa, rsb) * (ATT_HD ** -0.5 * math.log2(math.e))
    aq_ref[...] = q.T.astype(BF16)
    k = _group_rms(p[:, O_AK:O_AK + ATT_KW], e[:ATT_KW, :ATT_KW], ATT_HD) * kg_ref[...]
    k = _rope(k, rc[:, :ATT_KW], rsa[:, :ATT_KW], rsb[:, :ATT_KW])
    ak_ref[...] = k.astype(BF16)
    av_ref[...] = p[:, O_AV:O_AV + ATT_KW].T.astype(BF16)


def _in_projection(x, mods, g, w_in, wa_blk, ba_blk, e256, qg, kg, rope_c, rope_sa, rope_sb,
                   *, n_tiles, mod_map, rope_map):
    t_rows, d = x.shape
    row = lambda w: pl.BlockSpec((TM, w), lambda t: (t, 0))
    const = lambda shape: pl.BlockSpec(shape, lambda t: (0,) * len(shape))
    out_shapes = (
        jax.ShapeDtypeStruct((t_rows, POOL_W), F32),
        jax.ShapeDtypeStruct((t_rows, GLA_KW), F32),
        jax.ShapeDtypeStruct((t_rows, GLA_KW), F32),
        jax.ShapeDtypeStruct((t_rows, GLA_W), F32),
        jax.ShapeDtypeStruct((t_rows, GLA_W), F32),
        jax.ShapeDtypeStruct((t_rows, GLA_KW), F32),
        jax.ShapeDtypeStruct((t_rows, GLA_KW), F32),
        jax.ShapeDtypeStruct((n_tiles, ATT_W, TM), BF16),
        jax.ShapeDtypeStruct((t_rows, ATT_KW), BF16),
        jax.ShapeDtypeStruct((n_tiles, ATT_KW, TM), BF16),
        jax.ShapeDtypeStruct((t_rows, FNET_W), F32),
    )
    out_specs = (
        row(POOL_W), row(GLA_KW), row(GLA_KW), row(GLA_W), row(GLA_W), row(GLA_KW), row(GLA_KW),
        pl.BlockSpec((None, ATT_W, TM), lambda t: (t, 0, 0)),
        row(ATT_KW),
        pl.BlockSpec((None, ATT_KW, TM), lambda t: (t, 0, 0)),
        row(FNET_W),
    )
    return pl.pallas_call(
        _inproj_kernel,
        out_shape=out_shapes,
        grid=(n_tiles,),
        in_specs=[
            pl.BlockSpec((TM, d), lambda t: (t, 0)),
            pl.BlockSpec((None, N_MOD, d), mod_map),
            const((1, d)),
            const(w_in.shape),
            const(wa_blk.shape),
            const(ba_blk.shape),
            const(e256.shape),
            const((1, ATT_W)),
            const((1, ATT_KW)),
            pl.BlockSpec((TM, ATT_W), rope_map),
            pl.BlockSpec((TM, ATT_W), rope_map),
            pl.BlockSpec((TM, ATT_W), rope_map),
        ],
        out_specs=out_specs,
        compiler_params=_cparams(("arbitrary",)),
        name="in_projection",
    )(x, mods, g.reshape(1, d), w_in, wa_blk, ba_blk, e256, qg, kg, rope_c, rope_sa, rope_sb)


def _pool_kernel(u_ref, w_ref, s_ref, o_ref, pad_ref, *, seq, chunk):
    halo = POOL_HALO
    pad_ref[0:halo, :] = jnp.zeros((halo, POOL_W), F32)
    pad_ref[halo + seq:halo + seq + halo, :] = jnp.zeros((halo, POOL_W), F32)
    pad_ref[halo:halo + seq, :] = u_ref[...]
    rows = chunk + 2 * halo
    lane = lax.broadcasted_iota(jnp.int32, (chunk, POOL_W), 1)
    trow = lax.broadcasted_iota(jnp.int32, (chunk, POOL_W), 0)

    def body(ci, carry):
        c0 = pl.multiple_of(ci * chunk, chunk)
        xp = pad_ref[pl.ds(c0, rows), :]
        u = xp[halo:halo + chunk, :]
        t = trow + c0
        acc = xp
        m = None
        for gi, w in enumerate(POOL_WINDOWS):
            acc = acc + pltpu.roll(acc, w // 2, 0)
            lead = w // 2 - 1
            win = acc if lead == 0 else pltpu.roll(acc, rows - lead, 0)
            win = win[halo:halo + chunk, :]
            cnt = (jnp.minimum(t + w // 2, seq) - jnp.maximum(t - w // 2, 0)).astype(F32)
            mg = win / cnt - u
            m = mg if m is None else jnp.where(lane >= gi * POOL_GW, mg, m)
        y = _dot(m.astype(BF16), w_ref[...]) * s_ref[...]
        o_ref[pl.ds(c0, chunk), :] = y
        return carry

    lax.fori_loop(0, seq // chunk, body, 0)


def _pool_mix(u, w_blk, s_pool, *, seq, n_seq, first_block):
    chunk = min(256, seq)
    return pl.pallas_call(
        functools.partial(_pool_kernel, seq=seq, chunk=chunk),
        out_shape=jax.ShapeDtypeStruct((n_seq * seq, POOL_W), F32),
        grid=(n_seq,),
        in_specs=[
            pl.BlockSpec((seq, POOL_W), lambda b: (first_block + b, 0)),
            pl.BlockSpec((POOL_W, POOL_W), lambda b: (0, 0)),
            pl.BlockSpec((1, POOL_W), lambda b: (0, 0)),
        ],
        out_specs=pl.BlockSpec((seq, POOL_W), lambda b: (b, 0)),
        scratch_shapes=[pltpu.VMEM((seq + 2 * POOL_HALO, POOL_W), F32)],
        compiler_params=_cparams(("arbitrary",)),
        name="pool_mix",
    )(u, w_blk, s_pool)


def _gla_kernel(*refs, reverse, finalize):
    if finalize:
        (q_ref, k_ref, v_ref, g_ref, ex_ref, mk_ref, of_ref, og_ref, gn_ref, e_ref,
         o_ref, st_ref, ob_ref) = refs
    else:
        q_ref, k_ref, v_ref, g_ref, ex_ref, mk_ref, o_ref, st_ref = refs
        ob_ref = o_ref

    @pl.when(pl.program_id(1) == 0)
    def _():
        st_ref[...] = jnp.zeros(st_ref.shape, F32)

    sub = GLA_SUB
    n_chunks = TP // sub
    row = lax.broadcasted_iota(jnp.int32, (sub, GLA_KW), 0)
    ex = ex_ref[...]
    mk = mk_ref[...]

    def body(ci, carry):
        c = (n_chunks - 1 - ci) if reverse else ci
        r0 = pl.multiple_of(c * sub, sub)
        q = q_ref[pl.ds(r0, sub), :]
        k = k_ref[pl.ds(r0, sub), :]
        v = v_ref[pl.ds(r0, sub), :]
        b = g_ref[pl.ds(r0, sub), :]
        s = 1
        while s < sub:
            if reverse:
                b = b + jnp.where(row < sub - s, pltpu.roll(b, sub - s, 0), 0.0)
            else:
                b = b + jnp.where(row >= s, pltpu.roll(b, s, 0), 0.0)
            s *= 2
        edge = b[0:1, :] if reverse else b[sub - 1:sub, :]
        st = st_ref[...]
        qd = (q * jnp.exp(b)).astype(BF16)
        o = lax.dot_general(qd, st.astype(BF16), (((1,), (1,)), ((), ())),
                            preferred_element_type=F32)
        parts = []
        for j in range(sub):
            seen = (row <= j) if reverse else (row >= j)
            dec = jnp.exp(jnp.where(seen, b - b[j:j + 1, :], NEG_BIG))
            parts.append((dec * q * k[j:j + 1, :]).astype(BF16))
        r = _dot(jnp.concatenate(parts, axis=0), ex)
        for j in range(sub):
            o = o + r[j * sub:(j + 1) * sub, :] * v[j:j + 1, :]
        kd = (k * jnp.exp(edge - b)).astype(BF16)
        kv = lax.dot_general(v.astype(BF16), kd, (((0,), (0,)), ((), ())),
                             preferred_element_type=F32)
        st_ref[...] = st * jnp.exp(edge) + kv * mk
        ob_ref[pl.ds(r0, sub), :] = o
        return carry

    lax.fori_loop(0, n_chunks, body, 0, unroll=2)

    if finalize:
        tot = ob_ref[...] + of_ref[...]
        y = _group_rms(tot, e_ref[...], GLA_DV) * gn_ref[...]
        o_ref[...] = y * _silu(og_ref[...])


def _gla_tile_map(n_batch, n_lat, n_ctx, reverse):
    def index_map(b, s):
        is_ctx = s < n_ctx
        sc = jnp.where(is_ctx, s, 0)
        sl = jnp.where(is_ctx, 0, s - n_ctx)
        if reverse:
            sc = n_ctx - 1 - sc
            sl = n_lat - 1 - sl
        return (jnp.where(is_ctx, n_batch * n_lat + b * n_ctx + sc, b * n_lat + sl), 0)
    return index_map


def _gla_mix(gq, gk, gv, gf, gb, og, ex, mk, gn, e256, *, n_batch, seq, ctx):
    t_rows = gq.shape[0]
    n_lat, n_ctx = seq // TP, ctx // TP
    grid = (n_batch, n_lat + n_ctx)
    const = lambda shape: pl.BlockSpec(shape, lambda b, s: (0,) * len(shape))

    def run(reverse, finalize, gate, extra):
        tmap = _gla_tile_map(n_batch, n_lat, n_ctx, reverse)
        rows = lambda w: pl.BlockSpec((TP, w), tmap)
        in_specs = [rows(GLA_KW), rows(GLA_KW), rows(GLA_W), rows(GLA_KW), const(ex.shape), const(mk.shape)]
        args = [gq, gk, gv, gate, ex, mk]
        scratch = [pltpu.VMEM((GLA_W, GLA_KW), F32)]
        if finalize:
            in_specs += [rows(GLA_W), rows(GLA_W), const((1, GLA_W)), const(e256.shape)]
            args += extra
            scratch.append(pltpu.VMEM((TP, GLA_W), F32))
        return pl.pallas_call(
            functools.partial(_gla_kernel, reverse=reverse, finalize=finalize),
            out_shape=jax.ShapeDtypeStruct((t_rows, GLA_W), F32),
            grid=grid,
            in_specs=in_specs,
            out_specs=rows(GLA_W),
            scratch_shapes=scratch,
            compiler_params=_cparams(("arbitrary", "arbitrary")),
            name="gla_bwd" if reverse else "gla_fwd",
        )(*args)

    o_f = run(False, False, gf, None)
    return run(True, True, gb, [o_f, og, gn, e256])


def _att_kernel(*refs, n_main, has_ctx):
    if has_ctx:
        q_ref, km_ref, vm_ref, kc_ref, vc_ref, o_ref = refs
    else:
        q_ref, km_ref, vm_ref, o_ref = refs
    q_t = q_ref[...]
    tq = q_t.shape[1]
    tk = vm_ref.shape[2]
    zero = jnp.zeros((ATT_HD, tq), BF16)
    groups = []
    for g in range(ATT_KVH):
        halves = []
        for h in (2 * g, 2 * g + 1):
            qh = q_t[h * ATT_HD:(h + 1) * ATT_HD, :]
            halves.append(jnp.concatenate([qh, zero] if g == 0 else [zero, qh], axis=0))
        groups.append(jnp.concatenate(halves, axis=1))

    def step(kblk, vblk, carry):
        out = []
        for g in range(ATT_KVH):
            m, l, acc = carry[g]
            s = _dot(kblk, groups[g])
            m_new = jnp.maximum(m, jnp.max(s, axis=0, keepdims=True))
            alpha = jnp.exp2(m - m_new)
            p = jnp.exp2(s - m_new)
            l = alpha * l + jnp.sum(p, axis=0, keepdims=True)
            pv = _dot(vblk, p.astype(BF16))
            acc = alpha * acc + pv[g * ATT_HD:(g + 1) * ATT_HD, :]
            out.append((m_new, l, acc))
        return tuple(out)

    init = tuple((jnp.full((1, 2 * tq), -jnp.inf, F32), jnp.zeros((1, 2 * tq), F32),
                  jnp.zeros((ATT_HD, 2 * tq), F32)) for _ in range(ATT_KVH))

    def body(j, carry):
        r0 = pl.multiple_of(j * tk, tk)
        return step(km_ref[pl.ds(r0, tk), :], vm_ref[j], carry)

    carry = lax.fori_loop(0, n_main, body, init)
    if has_ctx:
        carry = step(kc_ref[...], vc_ref[...], carry)
    heads = []
    for g in range(ATT_KVH):
        _, l, acc = carry[g]
        og = acc / l
        heads += [og[:, :tq], og[:, tq:]]
    o_ref[...] = jnp.concatenate(heads, axis=0).T.astype(o_ref.dtype)


def _attention(aq, ak, av, *, n_batch, seq, ctx, ctx_queries):
    lat_tiles = seq // TM
    ctx_tile = n_batch * lat_tiles
    ctx_cols = lambda b, i: (ctx_tile + (b * ctx) // TM, 0, ((b * ctx) % TM) // ctx)
    if not ctx_queries:
        sub = TM // TQ
        grid = (n_batch, seq // TQ)
        in_specs = [
            pl.BlockSpec((None, ATT_W, TQ), lambda b, i: (b * lat_tiles + i // sub, 0, i % sub)),
            pl.BlockSpec((seq, ATT_KW), lambda b, i: (b, 0)),
            pl.BlockSpec((lat_tiles, ATT_KW, TM), lambda b, i: (b, 0, 0)),
            pl.BlockSpec((ctx, ATT_KW), lambda b, i: (n_batch * seq // ctx + b, 0)),
            pl.BlockSpec((None, ATT_KW, ctx), ctx_cols),
        ]
        out_spec = pl.BlockSpec((TQ, ATT_W), lambda b, i: (b * (seq // TQ) + i, 0))
        kern = functools.partial(_att_kernel, n_main=lat_tiles, has_ctx=True)
        out_rows = n_batch * seq
        args = (aq, ak, av, ak, av)
    else:
        grid = (n_batch, 1)
        in_specs = [
            pl.BlockSpec((None, ATT_W, ctx), ctx_cols),
            pl.BlockSpec((ctx, ATT_KW), lambda b, i: (n_batch * seq // ctx + b, 0)),
            pl.BlockSpec((1, ATT_KW, ctx), ctx_cols),
        ]
        out_spec = pl.BlockSpec((ctx, ATT_W), lambda b, i: (b, 0))
        kern = functools.partial(_att_kernel, n_main=1, has_ctx=False)
        out_rows = n_batch * ctx
        args = (aq, ak, av)
    return pl.pallas_call(
        kern,
        out_shape=jax.ShapeDtypeStruct((out_rows, ATT_W), BF16),
        grid=grid,
        in_specs=in_specs,
        out_specs=out_spec,
        compiler_params=_cparams(("arbitrary", "arbitrary")),
        name="attention_ctx" if ctx_queries else "attention",
    )(*args)


def _fft_a_kernel(x_ref, dh_ref, dl_ref, tr_ref, ti_ref, yr_ref, yi_ref):
    x_hi, x_lo = _split(x_ref[...])
    y = _dot3(dh_ref[...], dl_ref[...], x_hi, x_lo)
    yr, yi = y[:FFT_N1, :], y[FFT_N1:, :]
    tr, ti = tr_ref[...], ti_ref[...]
    yr_ref[...] = yr * tr - yi * ti
    yi_ref[...] = yr * ti + yi * tr


def _channel_mix(xr, xi, ch_ref, cl_ref, wf_ref, norm):
    xc = jnp.concatenate([xr, xi], axis=1)
    x_hi, x_lo = _split(xc)
    f = _dot3(x_hi, x_lo, ch_ref[...], cl_ref[...]) * norm
    return _dot(f.astype(BF16), wf_ref[...])


def _fft_c_kernel(yr_ref, yi_ref, mh_ref, ml_ref, ch_ref, cl_ref, wf_ref, o_ref, *, n2, norm):
    mh, ml = mh_ref[...], ml_ref[...]
    for j in range(8):
        blk = jnp.concatenate([yr_ref[j * n2:(j + 1) * n2, :], yi_ref[j * n2:(j + 1) * n2, :]], axis=0)
        b_hi, b_lo = _split(blk)
        x = _dot3(mh, ml, b_hi, b_lo)
        o_ref[:, j, :] = _channel_mix(x[:n2, :], x[n2:, :], ch_ref, cl_ref, wf_ref, norm)


def _fft_dense_kernel(x_ref, dh_ref, dl_ref, ch_ref, cl_ref, wf_ref, o_ref, *, n, norm):
    x_hi, x_lo = _split(x_ref[...])
    x = _dot3(dh_ref[...], dl_ref[...], x_hi, x_lo)
    o_ref[...] = _channel_mix(x[:n, :], x[n:, :], ch_ref, cl_ref, wf_ref, norm)


def _dft_parts(n):
    idx = np.arange(n)
    ang = 2.0 * np.pi * ((idx[:, None] * idx[None, :]) % n) / n
    return np.cos(ang), np.sin(ang)


def _np_split(a):
    a = jnp.asarray(a, F32)
    return _split(a)


def _fnet_consts(seq):
    n1, n2 = FFT_N1, seq // FFT_N1
    c1, s1 = _dft_parts(n1)
    da = np.concatenate([c1, -s1], axis=0)
    k1 = np.arange(n1)[:, None]
    m2 = np.arange(n2)[None, :]
    ang = 2.0 * np.pi * ((k1 * m2) % seq) / seq
    tw_r, tw_i = np.cos(ang), -np.sin(ang)
    c2, s2 = _dft_parts(n2)
    mc = np.block([[c2, s2], [-s2, c2]])
    return da, tw_r, tw_i, mc


def _channel_consts():
    cc, sc = _dft_parts(FNET_HD)
    eye = np.eye(FNET_HEADS)
    return np.concatenate([np.kron(eye, cc), np.kron(eye, sc)], axis=0)


def _fnet_latent(uf2, wf, *, n_batch, seq):
    n1, n2 = FFT_N1, seq // FFT_N1
    da, tw_r, tw_i, mc = _fnet_consts(seq)
    dh, dl = _np_split(da)
    mh, ml = _np_split(mc)
    ch, cl = _np_split(_channel_consts())
    tr = jnp.repeat(jnp.asarray(tw_r, F32), FNET_W, axis=1)
    ti = jnp.repeat(jnp.asarray(tw_i, F32), FNET_W, axis=1)
    width = n2 * FNET_W
    ct = min(width, 4096)
    const2 = lambda shape: pl.BlockSpec(shape, lambda b, j: (0,) * len(shape))
    yr, yi = pl.pallas_call(
        _fft_a_kernel,
        out_shape=(jax.ShapeDtypeStruct((n_batch * n1, width), F32),) * 2,
        grid=(n_batch, width // ct),
        in_specs=[
            pl.BlockSpec((n1, ct), lambda b, j: (b, j)),
            const2(dh.shape), const2(dl.shape),
            pl.BlockSpec((n1, ct), lambda b, j: (0, j)),
            pl.BlockSpec((n1, ct), lambda b, j: (0, j)),
        ],
        out_specs=(pl.BlockSpec((n1, ct), lambda b, j: (b, j)),) * 2,
        compiler_params=_cparams(("arbitrary", "arbitrary")),
        name="fft_stage_a",
    )(uf2, dh, dl, tr, ti)
    yr = yr.reshape(n_batch * n1 * n2, FNET_W)
    yi = yi.reshape(n_batch * n1 * n2, FNET_W)
    norm = 1.0 / math.sqrt(seq * FNET_HD)
    out = pl.pallas_call(
        functools.partial(_fft_c_kernel, n2=n2, norm=norm),
        out_shape=jax.ShapeDtypeStruct((n_batch, n2, n1, FNET_W), F32),
        grid=(n_batch, n1 // 8),
        in_specs=[
            pl.BlockSpec((8 * n2, FNET_W), lambda b, i: (b * (n1 // 8) + i, 0)),
            pl.BlockSpec((8 * n2, FNET_W), lambda b, i: (b * (n1 // 8) + i, 0)),
            const2(mh.shape), const2(ml.shape), const2(ch.shape), const2(cl.shape), const2(wf.shape),
        ],
        out_specs=pl.BlockSpec((None, n2, 8, FNET_W), lambda b, i: (b, 0, i, 0)),
        compiler_params=_cparams(("arbitrary", "arbitrary")),
        name="fft_stage_c",
    )(yr, yi, mh, ml, ch, cl, wf)
    return out.reshape(n_batch * seq, FNET_W)


def _fnet_context(uf, wf, *, n_batch, ctx, first_block):
    c, s = _dft_parts(ctx)
    dh, dl = _np_split(np.concatenate([c, -s], axis=0))
    ch, cl = _np_split(_channel_consts())
    const = lambda shape: pl.BlockSpec(shape, lambda b: (0,) * len(shape))
    return pl.pallas_call(
        functools.partial(_fft_dense_kernel, n=ctx, norm=1.0 / math.sqrt(ctx * FNET_HD)),
        out_shape=jax.ShapeDtypeStruct((n_batch * ctx, FNET_W), F32),
        grid=(n_batch,),
        in_specs=[pl.BlockSpec((ctx, FNET_W), lambda b: (first_block + b, 0)),
                  const(dh.shape), const(dl.shape), const(ch.shape), const(cl.shape), const(wf.shape)],
        out_specs=pl.BlockSpec((ctx, FNET_W), lambda b: (b, 0)),
        compiler_params=_cparams(("arbitrary",)),
        name="fft_context",
    )(uf, dh, dl, ch, cl, wf)


def _outproj_kernel(x_ref, m_ref, yp_ref, yg_ref, ya_ref, yf_ref, w_ref, o_ref):
    acc = _dot(yp_ref[...].astype(BF16), w_ref[0:256, :])
    acc = acc + _dot(yg_ref[...].astype(BF16), w_ref[256:512, :])
    acc = acc + _dot(ya_ref[...].astype(BF16), w_ref[512:768, :])
    acc = acc + _dot(yf_ref[...].astype(BF16), w_ref[768:1024, :])
    o_ref[...] = x_ref[...] + m_ref[5:6, :] * acc


def _out_projection(x, mods, y_pool, y_gla, y_att, y_fnet, w_out, *, n_tiles, mod_map):
    d = x.shape[1]
    part = pl.BlockSpec((TM, 256), lambda t: (t, 0))
    return pl.pallas_call(
        _outproj_kernel,
        out_shape=jax.ShapeDtypeStruct((n_tiles * TM, d), F32),
        grid=(n_tiles,),
        in_specs=[
            pl.BlockSpec((TM, d), lambda t: (t, 0)),
            pl.BlockSpec((None, N_MOD, d), mod_map),
            part, part, part, part,
            pl.BlockSpec(w_out.shape, lambda t: (0, 0)),
        ],
        out_specs=pl.BlockSpec((TM, d), lambda t: (t, 0)),
        compiler_params=_cparams(("arbitrary",)),
        name="out_projection",
    )(x, mods, y_pool, y_gla, y_att, y_fnet, w_out)


def _rope_tables(seq):
    rows = seq // GRID_W
    row_id = jnp.repeat(jnp.arange(rows, dtype=F32), GRID_W)
    col_id = jnp.tile(jnp.arange(GRID_W, dtype=F32), rows)
    freqs = ROPE_THETA ** (-jnp.arange(ROPE_FREQS, dtype=F32) / ROPE_FREQS)
    row_ang = row_id[:, None] * freqs
    col_ang = col_id[:, None] * freqs
    ang = jnp.concatenate([row_ang, row_ang, col_ang, col_ang], axis=1)
    first_half = np.tile(np.repeat(np.array([1.0, 0.0, 1.0, 0.0]), ROPE_FREQS), ATT_QH)[None, :]
    ang = jnp.tile(ang, (1, ATT_QH))
    cos, sin = jnp.cos(ang), jnp.sin(ang)
    first_half = jnp.asarray(first_half, F32)
    pad = lambda a, v: jnp.concatenate([a, jnp.full((TM, ATT_W), v, F32)], axis=0)
    return pad(cos, 1.0), pad(-sin * first_half, 0.0), pad(sin * (1.0 - first_half), 0.0)


def _block_ones(width, group):
    return jnp.asarray(np.kron(np.eye(width // group), np.ones((group, group))), BF16)


def kernel(x, c, ctx, c_ctx, w_mod, b_mod, norm_g, ffn_wg, ffn_wu, ffn_wd, w_in, w_out,
           pool_w, pool_scale, gla_wa, gla_ba, gla_norm, att_qnorm, att_knorm, fnet_w, final_norm):
    n_batch, seq, d = x.shape
    n_ctx = ctx.shape[1]
    assert d == D_MODEL and seq % TM == 0 and (n_batch * n_ctx) % TM == 0 and n_ctx % TP == 0
    assert seq % (8 * FFT_N1) == 0 and n_ctx <= TM and TM % n_ctx == 0 and n_batch + 1 <= 8
    lat_rows = n_batch * seq
    lat_tiles = lat_rows // TM
    all_tiles = lat_tiles + (n_batch * n_ctx) // TM
    tiles_per_batch = seq // TM
    mod_map = _mod_row_map(lat_tiles, tiles_per_batch, n_batch)
    rope_map = lambda t: (jnp.where(t < lat_tiles, t % tiles_per_batch, tiles_per_batch), 0)

    xs = jnp.concatenate([x.reshape(lat_rows, d), ctx.reshape(n_batch * n_ctx, d)], axis=0)
    cvec = jnp.concatenate([c, c_ctx[None, :], jnp.zeros((8 - n_batch - 1, d), F32)], axis=0)
    mods = _modulation(cvec, w_mod, b_mod).reshape(DEPTH, 8, N_MOD, d)

    rope_c, rope_sa, rope_sb = _rope_tables(seq)
    e256 = _block_ones(ATT_W, ATT_HD)
    gla_ex = _block_ones(GLA_W, GLA_DV)[::2, :]
    gla_mk = jnp.asarray(np.kron(np.eye(GLA_HEADS), np.ones((GLA_DV, GLA_DK))), F32)

    for i in range(DEPTH):
        ctx_out = i < DEPTH - 1
        m = mods[i]
        wg = ffn_wg[i].astype(BF16)
        wu = ffn_wu[i].astype(BF16)
        wd = ffn_wd[i].astype(BF16)
        wi = w_in[i]
        wi = jnp.concatenate([wi[:, :768], wi[:, 800:D_IN], wi[:, 768:800],
                              jnp.zeros((d, D_IN_PAD - D_IN), F32)], axis=1).astype(BF16)
        wa_blk = jnp.zeros((128, 2 * GLA_KW), F32)
        wa_blk = wa_blk.at[:GLA_RANK, :GLA_KW].set(gla_wa[i, 0]).at[GLA_RANK:2 * GLA_RANK, GLA_KW:].set(gla_wa[i, 1])
        ba_blk = gla_ba[i].reshape(1, 2 * GLA_KW)
        qg = jnp.tile(att_qnorm[i], ATT_QH)[None, :]
        kg = jnp.tile(att_knorm[i], ATT_KVH)[None, :]
        pool_blk = jax.scipy.linalg.block_diag(*[pool_w[i, g] for g in range(len(POOL_WINDOWS))]).astype(BF16)
        gn = jnp.tile(gla_norm[i], GLA_HEADS)[None, :]
        wf = fnet_w[i].astype(BF16)
        wo = w_out[i].astype(BF16)

        xs = _half_ffn(xs, m, norm_g[i, 0], wg[0], wu[0], wd[0], final_norm,
                       mod_base=0, n_tiles=all_tiles, mod_map=mod_map)
        (u_pool, gq, gk, gv, og, gf, gb, aq, ak, av, uf) = _in_projection(
            xs, m, norm_g[i, 1], wi, wa_blk.astype(BF16), ba_blk, e256, qg, kg, rope_c, rope_sa, rope_sb,
            n_tiles=all_tiles, mod_map=mod_map, rope_map=rope_map)

        y_pool = _pool_mix(u_pool, pool_blk, pool_scale[i][None, :], seq=seq, n_seq=n_batch, first_block=0)
        y_gla = _gla_mix(gq, gk, gv, gf, gb, og, gla_ex, gla_mk, gn, e256, n_batch=n_batch, seq=seq, ctx=n_ctx)
        y_att = _attention(aq, ak, av, n_batch=n_batch, seq=seq, ctx=n_ctx, ctx_queries=False)
        n2 = seq // FFT_N1
        y_fnet = _fnet_latent(uf.reshape(uf.shape[0] // n2, n2 * FNET_W), wf, n_batch=n_batch, seq=seq)
        n_tiles = lat_tiles
        if ctx_out:
            first_ctx = lat_rows // n_ctx
            y_pool = jnp.concatenate([y_pool, _pool_mix(u_pool, pool_blk, pool_scale[i][None, :], seq=n_ctx,
                                                        n_seq=n_batch, first_block=first_ctx)], axis=0)
            y_att = jnp.concatenate([y_att, _attention(aq, ak, av, n_batch=n_batch, seq=seq, ctx=n_ctx,
                                                       ctx_queries=True)], axis=0)
            y_fnet = jnp.concatenate([y_fnet, _fnet_context(uf, wf, n_batch=n_batch, ctx=n_ctx,
                                                            first_block=first_ctx)], axis=0)
            n_tiles = all_tiles
        xs = _out_projection(xs, m, y_pool, y_gla, y_att, y_fnet, wo, n_tiles=n_tiles, mod_map=mod_map)
        xs = _half_ffn(xs, m, norm_g[i, 2], wg[1], wu[1], wd[1], final_norm,
                       mod_base=6, n_tiles=n_tiles, mod_map=mod_map, final=not ctx_out)
    return xs.reshape(n_batch, seq, d)
```

```python
import functools
import math

import jax
import jax.numpy as jnp
import numpy as np
from jax import lax
from jax.experimental import pallas as pl
from jax.experimental.pallas import tpu as pltpu

F32 = jnp.float32
BF16 = jnp.bfloat16

D_MODEL = 1024
DEPTH = 2
GRID_W = 64
EPS = 1e-6
N_MOD = 9
D_FF = 2816

POOL_W = 256
POOL_WINDOWS = (2, 4, 8, 16)
POOL_GW = 64
POOL_HALO = 8

GLA_HEADS = 4
GLA_W = 256
GLA_DV = 64
GLA_DK = 32
GLA_RANK = 16
GLA_TAU = 16.0
GLA_KW = GLA_HEADS * GLA_DK
GLA_SUB = 16

ATT_W = 256
ATT_HD = 64
ATT_QH = 4
ATT_KVH = 2
ATT_KW = ATT_KVH * ATT_HD
ROPE_FREQS = 16
ROPE_THETA = 10000.0

FNET_W = 256
FNET_HEADS = 4
FNET_HD = 64
FFT_N1 = 64

D_IN = 1824
D_IN_PAD = 1920

O_POOL, O_GQ, O_GK, O_GV, O_OG, O_AQ, O_AK, O_AV, O_FN, O_R = 0, 256, 384, 512, 768, 1024, 1280, 1408, 1536, 1792

TM = 512
TQ = 256
TP = 256
FF_CHUNK = 1408
NEG_BIG = -1e30

VMEM_LIMIT = 56 * 1024 * 1024


def _cparams(sem):
    return pltpu.CompilerParams(dimension_semantics=sem, vmem_limit_bytes=VMEM_LIMIT)


def _dot(a, b):
    return jnp.dot(a, b, preferred_element_type=F32)


def _split(a):
    hi = a.astype(BF16)
    lo = (a - hi.astype(F32)).astype(BF16)
    return hi, lo


def _dot3(a_hi, a_lo, b_hi, b_lo):
    return _dot(a_hi, b_hi) + (_dot(a_hi, b_lo) + _dot(a_lo, b_hi))


def _rms(x):
    return x * lax.rsqrt(jnp.mean(x * x, axis=-1, keepdims=True) + EPS)


def _silu(x):
    return x * jax.nn.sigmoid(x)


def _group_rms(x, e, width):
    s_hi, s_lo = _split(x * x)
    ss = _dot(s_hi, e) + _dot(s_lo, e)
    return x * lax.rsqrt(ss * (1.0 / width) + EPS)


def _mod_kernel(c_ref, w_ref, b_ref, o_ref):
    s = _silu(c_ref[...]).astype(BF16)
    o_ref[...] = _dot(s, w_ref[...].astype(BF16)) + b_ref[...]


def _modulation(cvec, w_mod, b_mod):
    depth, d, nd = w_mod.shape
    tn = nd // 8
    return pl.pallas_call(
        _mod_kernel,
        out_shape=jax.ShapeDtypeStruct((depth, 8, nd), F32),
        grid=(depth, nd // tn),
        in_specs=[
            pl.BlockSpec((8, d), lambda l, j: (0, 0)),
            pl.BlockSpec((None, d, tn), lambda l, j: (l, 0, j)),
            pl.BlockSpec((None, 1, tn), lambda l, j: (l, 0, j)),
        ],
        out_specs=pl.BlockSpec((None, 8, tn), lambda l, j: (l, 0, j)),
        compiler_params=_cparams(("arbitrary", "arbitrary")),
        name="modulation",
    )(cvec, w_mod, b_mod.reshape(depth, 1, nd))


def _ffn_kernel(x_ref, m_ref, g_ref, wg_ref, wu_ref, wd_ref, fg_ref, o_ref, *, mod_base, final):
    x = x_ref[...]
    shift = m_ref[mod_base:mod_base + 1, :]
    scale = m_ref[mod_base + 1:mod_base + 2, :]
    gate = m_ref[mod_base + 2:mod_base + 3, :]
    h = ((_rms(x) * g_ref[...]) * (1.0 + scale) + shift).astype(BF16)
    y = jnp.zeros(x.shape, F32)
    for c in range(D_FF // FF_CHUNK):
        sl = slice(c * FF_CHUNK, (c + 1) * FF_CHUNK)
        a = _dot(h, wg_ref[:, sl])
        u = _dot(h, wu_ref[:, sl])
        y = y + _dot((_silu(a) * u).astype(BF16), wd_ref[sl, :])
    out = x + (0.5 * gate) * y
    if final:
        out = _rms(out) * fg_ref[...]
    o_ref[...] = out


def _mod_row_map(n_lat_tiles, tiles_per_batch, n_batch):
    def index_map(t):
        return (jnp.where(t < n_lat_tiles, t // tiles_per_batch, n_batch), 0, 0)
    return index_map


def _resident(shape):
    zeros = (0,) * len(shape)
    return pl.BlockSpec(shape, lambda t: zeros, pipeline_mode=pl.Buffered(1))


def _half_ffn(x, mods, g, wg, wu, wd, fg, *, mod_base, n_tiles, mod_map, final=False):
    d = x.shape[1]
    return pl.pallas_call(
        functools.partial(_ffn_kernel, mod_base=mod_base, final=final),
        out_shape=jax.ShapeDtypeStruct((n_tiles * TM, d), F32),
        grid=(n_tiles,),
        in_specs=[
            pl.BlockSpec((TM, d), lambda t: (t, 0)),
            pl.BlockSpec((None, N_MOD, d), mod_map),
            pl.BlockSpec((1, d), lambda t: (0, 0)),
            _resident(wg.shape),
            _resident(wu.shape),
            _resident(wd.shape),
            pl.BlockSpec((1, d), lambda t: (0, 0)),
        ],
        out_specs=pl.BlockSpec((TM, d), lambda t: (t, 0)),
        compiler_params=_cparams(("arbitrary",)),
        name="half_ffn",
    )(x, mods, g.reshape(1, d), wg, wu, wd, fg.reshape(1, d))


def _rope(x, c, sa, sb):
    w = x.shape[1]
    return x * c + pltpu.roll(x, w - ROPE_FREQS, 1) * sa + pltpu.roll(x, ROPE_FREQS, 1) * sb


def _inproj_kernel(x_ref, m_ref, g_ref, w_ref, wa_ref, ba_ref, e_ref, qg_ref, kg_ref,
                   rc_ref, rsa_ref, rsb_ref,
                   up_ref, gq_ref, gk_ref, gv_ref, og_ref, gf_ref, gb_ref,
                   aq_ref, ak_ref, av_ref, uf_ref):
    x = x_ref[...]
    shift = m_ref[3:4, :]
    scale = m_ref[4:5, :]
    h = ((_rms(x) * g_ref[...]) * (1.0 + scale) + shift).astype(BF16)
    p = _dot(h, w_ref[...])

    up_ref[...] = p[:, O_POOL:O_POOL + POOL_W]
    uf_ref[...] = p[:, O_FN:O_FN + FNET_W]

    gq_ref[...] = p[:, O_GQ:O_GQ + GLA_KW] * (GLA_DK ** -0.5)
    gk_ref[...] = p[:, O_GK:O_GK + GLA_KW]
    gv_ref[...] = p[:, O_GV:O_GV + GLA_W]
    og_ref[...] = p[:, O_OG:O_OG + GLA_W]
    z = _dot(p[:, O_R:O_R + 128].astype(BF16), wa_ref[...]) + ba_ref[...]
    logsig = jnp.minimum(z, 0.0) - jnp.log(1.0 + jnp.exp(-jnp.abs(z)))
    gdec = logsig * (1.0 / GLA_TAU)
    gf_ref[...] = gdec[:, :GLA_KW]
    gb_ref[...] = gdec[:, GLA_KW:]

    e = e_ref[...]
    rc, rsa, rsb = rc_ref[...], rsa_ref[...], rsb_ref[...]
    q = _group_rms(p[:, O_AQ:O_AQ + ATT_W], e, ATT_HD) * qg_ref[...]
    q = _rope(q, rc, rsa, rsb) * (ATT_HD ** -0.5 * math.log2(math.e))
    aq_ref[...] = q.T.astype(BF16)
    k = _group_rms(p[:, O_AK:O_AK + ATT_KW], e[:ATT_KW, :ATT_KW], ATT_HD) * kg_ref[...]
    k = _rope(k, rc[:, :ATT_KW], rsa[:, :ATT_KW], rsb[:, :ATT_KW])
    ak_ref[...] = k.astype(BF16)
    av_ref[...] = p[:, O_AV:O_AV + ATT_KW].T.astype(BF16)


def _in_projection(x, mods, g, w_in, wa_blk, ba_blk, e256, qg, kg, rope_c, rope_sa, rope_sb,
                   *, n_tiles, mod_map, rope_map):
    t_rows, d = x.shape
    row = lambda w: pl.BlockSpec((TM, w), lambda t: (t, 0))
    const = lambda shape: pl.BlockSpec(shape, lambda t: (0,) * len(shape))
    out_shapes = (
        jax.ShapeDtypeStruct((t_rows, POOL_W), F32),
        jax.ShapeDtypeStruct((t_rows, GLA_KW), F32),
        jax.ShapeDtypeStruct((t_rows, GLA_KW), F32),
        jax.ShapeDtypeStruct((t_rows, GLA_W), F32),
        jax.ShapeDtypeStruct((t_rows, GLA_W), F32),
        jax.ShapeDtypeStruct((t_rows, GLA_KW), F32),
        jax.ShapeDtypeStruct((t_rows, GLA_KW), F32),
        jax.ShapeDtypeStruct((n_tiles, ATT_W, TM), BF16),
        jax.ShapeDtypeStruct((t_rows, ATT_KW), BF16),
        jax.ShapeDtypeStruct((n_tiles, ATT_KW, TM), BF16),
        jax.ShapeDtypeStruct((t_rows, FNET_W), F32),
    )
    out_specs = (
        row(POOL_W), row(GLA_KW), row(GLA_KW), row(GLA_W), row(GLA_W), row(GLA_KW), row(GLA_KW),
        pl.BlockSpec((None, ATT_W, TM), lambda t: (t, 0, 0)),
        row(ATT_KW),
        pl.BlockSpec((None, ATT_KW, TM), lambda t: (t, 0, 0)),
        row(FNET_W),
    )
    return pl.pallas_call(
        _inproj_kernel,
        out_shape=out_shapes,
        grid=(n_tiles,),
        in_specs=[
            pl.BlockSpec((TM, d), lambda t: (t, 0)),
            pl.BlockSpec((None, N_MOD, d), mod_map),
            const((1, d)),
            const(w_in.shape),
            const(wa_blk.shape),
            const(ba_blk.shape),
            const(e256.shape),
            const((1, ATT_W)),
            const((1, ATT_KW)),
            pl.BlockSpec((TM, ATT_W), rope_map),
            pl.BlockSpec((TM, ATT_W), rope_map),
            pl.BlockSpec((TM, ATT_W), rope_map),
        ],
        out_specs=out_specs,
        compiler_params=_cparams(("arbitrary",)),
        name="in_projection",
    )(x, mods, g.reshape(1, d), w_in, wa_blk, ba_blk, e256, qg, kg, rope_c, rope_sa, rope_sb)


def _pool_kernel(u_ref, w_ref, s_ref, o_ref, pad_ref, *, seq, chunk):
    halo = POOL_HALO
    pad_ref[0:halo, :] = jnp.zeros((halo, POOL_W), F32)
    pad_ref[halo + seq:halo + seq + halo, :] = jnp.zeros((halo, POOL_W), F32)
    pad_ref[halo:halo + seq, :] = u_ref[...]
    rows = chunk + 2 * halo
    lane = lax.broadcasted_iota(jnp.int32, (chunk, POOL_W), 1)
    trow = lax.broadcasted_iota(jnp.int32, (chunk, POOL_W), 0)

    def body(ci, carry):
        c0 = pl.multiple_of(ci * chunk, chunk)
        xp = pad_ref[pl.ds(c0, rows), :]
        u = xp[halo:halo + chunk, :]
        t = trow + c0
        acc = xp
        m = None
        for gi, w in enumerate(POOL_WINDOWS):
            acc = acc + pltpu.roll(acc, w // 2, 0)
            lead = w // 2 - 1
            win = acc if lead == 0 else pltpu.roll(acc, rows - lead, 0)
            win = win[halo:halo + chunk, :]
            cnt = (jnp.minimum(t + w // 2, seq) - jnp.maximum(t - w // 2, 0)).astype(F32)
            mg = win / cnt - u
            m = mg if m is None else jnp.where(lane >= gi * POOL_GW, mg, m)
        y = _dot(m.astype(BF16), w_ref[...]) * s_ref[...]
        o_ref[pl.ds(c0, chunk), :] = y
        return carry

    lax.fori_loop(0, seq // chunk, body, 0)


def _pool_mix(u, w_blk, s_pool, *, seq, n_seq, first_block):
    chunk = min(256, seq)
    return pl.pallas_call(
        functools.partial(_pool_kernel, seq=seq, chunk=chunk),
        out_shape=jax.ShapeDtypeStruct((n_seq * seq, POOL_W), F32),
        grid=(n_seq,),
        in_specs=[
            pl.BlockSpec((seq, POOL_W), lambda b: (first_block + b, 0)),
            pl.BlockSpec((POOL_W, POOL_W), lambda b: (0, 0)),
            pl.BlockSpec((1, POOL_W), lambda b: (0, 0)),
        ],
        out_specs=pl.BlockSpec((seq, POOL_W), lambda b: (b, 0)),
        scratch_shapes=[pltpu.VMEM((seq + 2 * POOL_HALO, POOL_W), F32)],
        compiler_params=_cparams(("arbitrary",)),
        name="pool_mix",
    )(u, w_blk, s_pool)


def _gla_kernel(*refs, reverse, finalize):
    if finalize:
        (q_ref, k_ref, v_ref, g_ref, ex_ref, mk_ref, of_ref, og_ref, gn_ref, e_ref,
         o_ref, st_ref, ob_ref) = refs
    else:
        q_ref, k_ref, v_ref, g_ref, ex_ref, mk_ref, o_ref, st_ref = refs
        ob_ref = o_ref

    @pl.when(pl.program_id(1) == 0)
    def _():
        st_ref[...] = jnp.zeros(st_ref.shape, F32)

    sub = GLA_SUB
    n_chunks = TP // sub
    row = lax.broadcasted_iota(jnp.int32, (sub, GLA_KW), 0)
    ex = ex_ref[...]
    mk = mk_ref[...]

    def body(ci, carry):
        c = (n_chunks - 1 - ci) if reverse else ci
        r0 = pl.multiple_of(c * sub, sub)
        q = q_ref[pl.ds(r0, sub), :]
        k = k_ref[pl.ds(r0, sub), :]
        v = v_ref[pl.ds(r0, sub), :]
        b = g_ref[pl.ds(r0, sub), :]
        s = 1
        while s < sub:
            if reverse:
                b = b + jnp.where(row < sub - s, pltpu.roll(b, sub - s, 0), 0.0)
            else:
                b = b + jnp.where(row >= s, pltpu.roll(b, s, 0), 0.0)
            s *= 2
        edge = b[0:1, :] if reverse else b[sub - 1:sub, :]
        st = st_ref[...]
        qd = (q * jnp.exp(b)).astype(BF16)
        o = lax.dot_general(qd, st.astype(BF16), (((1,), (1,)), ((), ())),
                            preferred_element_type=F32)
        parts = []
        for j in range(sub):
            seen = (row <= j) if reverse else (row >= j)
            dec = jnp.exp(jnp.where(seen, b - b[j:j + 1, :], NEG_BIG))
            parts.append((dec * q * k[j:j + 1, :]).astype(BF16))
        r = _dot(jnp.concatenate(parts, axis=0), ex)
        for j in range(sub):
            o = o + r[j * sub:(j + 1) * sub, :] * v[j:j + 1, :]
        kd = (k * jnp.exp(edge - b)).astype(BF16)
        kv = lax.dot_general(v.astype(BF16), kd, (((0,), (0,)), ((), ())),
                             preferred_element_type=F32)
        st_ref[...] = st * jnp.exp(edge) + kv * mk
        ob_ref[pl.ds(r0, sub), :] = o
        return carry

    lax.fori_loop(0, n_chunks, body, 0, unroll=2)

    if finalize:
        tot = ob_ref[...] + of_ref[...]
        y = _group_rms(tot, e_ref[...], GLA_DV) * gn_ref[...]
        o_ref[...] = y * _silu(og_ref[...])


def _gla_tile_map(n_batch, n_lat, n_ctx, reverse):
    def index_map(b, s):
        is_ctx = s < n_ctx
        sc = jnp.where(is_ctx, s, 0)
        sl = jnp.where(is_ctx, 0, s - n_ctx)
        if reverse:
            sc = n_ctx - 1 - sc
            sl = n_lat - 1 - sl
        return (jnp.where(is_ctx, n_batch * n_lat + b * n_ctx + sc, b * n_lat + sl), 0)
    return index_map


def _gla_mix(gq, gk, gv, gf, gb, og, ex, mk, gn, e256, *, n_batch, seq, ctx):
    t_rows = gq.shape[0]
    n_lat, n_ctx = seq // TP, ctx // TP
    grid = (n_batch, n_lat + n_ctx)
    const = lambda shape: pl.BlockSpec(shape, lambda b, s: (0,) * len(shape))

    def run(reverse, finalize, gate, extra):
        tmap = _gla_tile_map(n_batch, n_lat, n_ctx, reverse)
        rows = lambda w: pl.BlockSpec((TP, w), tmap)
        in_specs = [rows(GLA_KW), rows(GLA_KW), rows(GLA_W), rows(GLA_KW), const(ex.shape), const(mk.shape)]
        args = [gq, gk, gv, gate, ex, mk]
        scratch = [pltpu.VMEM((GLA_W, GLA_KW), F32)]
        if finalize:
            in_specs += [rows(GLA_W), rows(GLA_W), const((1, GLA_W)), const(e256.shape)]
            args += extra
            scratch.append(pltpu.VMEM((TP, GLA_W), F32))
        return pl.pallas_call(
            functools.partial(_gla_kernel, reverse=reverse, finalize=finalize),
            out_shape=jax.ShapeDtypeStruct((t_rows, GLA_W), F32),
            grid=grid,
            in_specs=in_specs,
            out_specs=rows(GLA_W),
            scratch_shapes=scratch,
            compiler_params=_cparams(("arbitrary", "arbitrary")),
            name="gla_bwd" if reverse else "gla_fwd",
        )(*args)

    o_f = run(False, False, gf, None)
    return run(True, True, gb, [o_f, og, gn, e256])


def _att_kernel(*refs, n_main, has_ctx):
    if has_ctx:
        q_ref, km_ref, vm_ref, kc_ref, vc_ref, o_ref, s0_ref, s1_ref = refs
    else:
        q_ref, km_ref, vm_ref, o_ref, s0_ref, s1_ref = refs
    q_t = q_ref[...]
    tq = q_t.shape[1]
    tk = vm_ref.shape[2]
    zero = jnp.zeros((ATT_HD, tq), BF16)
    groups = []
    for g in range(ATT_KVH):
        halves = []
        for h in (2 * g, 2 * g + 1):
            qh = q_t[h * ATT_HD:(h + 1) * ATT_HD, :]
            halves.append(jnp.concatenate([qh, zero] if g == 0 else [zero, qh], axis=0))
        groups.append(jnp.concatenate(halves, axis=1))

    def scores(kblk, buf):
        for g in range(ATT_KVH):
            buf[g, 0:kblk.shape[0], :] = _dot(kblk, groups[g])

    def consume(buf, rows, vblk, carry):
        out = []
        for g in range(ATT_KVH):
            m, l, acc = carry[g]
            s = buf[g, 0:rows, :]
            m_new = jnp.maximum(m, jnp.max(s, axis=0, keepdims=True))
            alpha = jnp.exp2(m - m_new)
            p = jnp.exp2(s - m_new)
            l = alpha * l + jnp.sum(p, axis=0, keepdims=True)
            pv = _dot(vblk, p.astype(BF16))
            acc = alpha * acc + pv[g * ATT_HD:(g + 1) * ATT_HD, :]
            out.append((m_new, l, acc))
        return tuple(out)

    def k_main(j):
        return km_ref[pl.ds(pl.multiple_of(j * tk, tk), tk), :]

    carry = tuple((jnp.full((1, 2 * tq), -jnp.inf, F32), jnp.zeros((1, 2 * tq), F32),
                   jnp.zeros((ATT_HD, 2 * tq), F32)) for _ in range(ATT_KVH))

    scores(k_main(0), s0_ref)
    if n_main > 1:
        def body(i, carry):
            j = 2 * i
            scores(k_main(j + 1), s1_ref)
            carry = consume(s0_ref, tk, vm_ref[j], carry)
            scores(k_main(j + 2), s0_ref)
            return consume(s1_ref, tk, vm_ref[j + 1], carry)

        carry = lax.fori_loop(0, n_main // 2 - 1, body, carry)
        scores(k_main(n_main - 1), s1_ref)
        carry = consume(s0_ref, tk, vm_ref[n_main - 2], carry)
        last_buf, other_buf = s1_ref, s0_ref
    else:
        last_buf, other_buf = s0_ref, s1_ref
    if has_ctx:
        scores(kc_ref[...], other_buf)
    carry = consume(last_buf, tk, vm_ref[n_main - 1], carry)
    if has_ctx:
        carry = consume(other_buf, kc_ref.shape[0], vc_ref[...], carry)
    heads = []
    for g in range(ATT_KVH):
        _, l, acc = carry[g]
        og = acc / l
        heads += [og[:, :tq], og[:, tq:]]
    o_ref[...] = jnp.concatenate(heads, axis=0).T.astype(o_ref.dtype)


def _attention(aq, ak, av, *, n_batch, seq, ctx, ctx_queries):
    lat_tiles = seq // TM
    ctx_tile = n_batch * lat_tiles
    ctx_cols = lambda b, i: (ctx_tile + (b * ctx) // TM, 0, ((b * ctx) % TM) // ctx)
    if not ctx_queries:
        sub = TM // TQ
        grid = (n_batch, seq // TQ)
        in_specs = [
            pl.BlockSpec((None, ATT_W, TQ), lambda b, i: (b * lat_tiles + i // sub, 0, i % sub)),
            pl.BlockSpec((seq, ATT_KW), lambda b, i: (b, 0)),
            pl.BlockSpec((lat_tiles, ATT_KW, TM), lambda b, i: (b, 0, 0)),
            pl.BlockSpec((ctx, ATT_KW), lambda b, i: (n_batch * seq // ctx + b, 0)),
            pl.BlockSpec((None, ATT_KW, ctx), ctx_cols),
        ]
        out_spec = pl.BlockSpec((TQ, ATT_W), lambda b, i: (b * (seq // TQ) + i, 0))
        assert lat_tiles % 2 == 0
        kern = functools.partial(_att_kernel, n_main=lat_tiles, has_ctx=True)
        out_rows = n_batch * seq
        args = (aq, ak, av, ak, av)
        s_shape = (ATT_KVH, TM, 2 * TQ)
    else:
        grid = (n_batch, 1)
        in_specs = [
            pl.BlockSpec((None, ATT_W, ctx), ctx_cols),
            pl.BlockSpec((ctx, ATT_KW), lambda b, i: (n_batch * seq // ctx + b, 0)),
            pl.BlockSpec((1, ATT_KW, ctx), ctx_cols),
        ]
        out_spec = pl.BlockSpec((ctx, ATT_W), lambda b, i: (b, 0))
        kern = functools.partial(_att_kernel, n_main=1, has_ctx=False)
        out_rows = n_batch * ctx
        args = (aq, ak, av)
        s_shape = (ATT_KVH, ctx, 2 * ctx)
    return pl.pallas_call(
        kern,
        out_shape=jax.ShapeDtypeStruct((out_rows, ATT_W), BF16),
        grid=grid,
        in_specs=in_specs,
        out_specs=out_spec,
        scratch_shapes=[pltpu.VMEM(s_shape, F32), pltpu.VMEM(s_shape, F32)],
        compiler_params=_cparams(("arbitrary", "arbitrary")),
        name="attention_ctx" if ctx_queries else "attention",
    )(*args)


def _fft_a_kernel(x_ref, dh_ref, dl_ref, tr_ref, ti_ref, yr_ref, yi_ref):
    x_hi, x_lo = _split(x_ref[...])
    y = _dot3(dh_ref[...], dl_ref[...], x_hi, x_lo)
    yr, yi = y[:FFT_N1, :], y[FFT_N1:, :]
    tr, ti = tr_ref[...], ti_ref[...]
    yr_ref[...] = yr * tr - yi * ti
    yi_ref[...] = yr * ti + yi * tr


def _channel_mix(xr, xi, ch_ref, cl_ref, wf_ref, norm):
    xc = jnp.concatenate([xr, xi], axis=1)
    x_hi, x_lo = _split(xc)
    f = _dot3(x_hi, x_lo, ch_ref[...], cl_ref[...]) * norm
    return _dot(f.astype(BF16), wf_ref[...])


def _fft_c_kernel(yr_ref, yi_ref, mh_ref, ml_ref, ch_ref, cl_ref, wf_ref, o_ref, *, n2, norm):
    mh, ml = mh_ref[...], ml_ref[...]
    for j in range(8):
        blk = jnp.concatenate([yr_ref[j * n2:(j + 1) * n2, :], yi_ref[j * n2:(j + 1) * n2, :]], axis=0)
        b_hi, b_lo = _split(blk)
        x = _dot3(mh, ml, b_hi, b_lo)
        o_ref[:, j, :] = _channel_mix(x[:n2, :], x[n2:, :], ch_ref, cl_ref, wf_ref, norm)


def _fft_dense_kernel(x_ref, dh_ref, dl_ref, ch_ref, cl_ref, wf_ref, o_ref, *, n, norm):
    x_hi, x_lo = _split(x_ref[...])
    x = _dot3(dh_ref[...], dl_ref[...], x_hi, x_lo)
    o_ref[...] = _channel_mix(x[:n, :], x[n:, :], ch_ref, cl_ref, wf_ref, norm)


def _dft_parts(n):
    idx = np.arange(n)
    ang = 2.0 * np.pi * ((idx[:, None] * idx[None, :]) % n) / n
    return np.cos(ang), np.sin(ang)


def _np_split(a):
    a = jnp.asarray(a, F32)
    return _split(a)


def _fnet_consts(seq):
    n1, n2 = FFT_N1, seq // FFT_N1
    c1, s1 = _dft_parts(n1)
    da = np.concatenate([c1, -s1], axis=0)
    k1 = np.arange(n1)[:, None]
    m2 = np.arange(n2)[None, :]
    ang = 2.0 * np.pi * ((k1 * m2) % seq) / seq
    tw_r, tw_i = np.cos(ang), -np.sin(ang)
    c2, s2 = _dft_parts(n2)
    mc = np.block([[c2, s2], [-s2, c2]])
    return da, tw_r, tw_i, mc


def _channel_consts():
    cc, sc = _dft_parts(FNET_HD)
    eye = np.eye(FNET_HEADS)
    return np.concatenate([np.kron(eye, cc), np.kron(eye, sc)], axis=0)


def _fnet_latent(uf2, wf, *, n_batch, seq):
    n1, n2 = FFT_N1, seq // FFT_N1
    da, tw_r, tw_i, mc = _fnet_consts(seq)
    dh, dl = _np_split(da)
    mh, ml = _np_split(mc)
    ch, cl = _np_split(_channel_consts())
    tr = jnp.repeat(jnp.asarray(tw_r, F32), FNET_W, axis=1)
    ti = jnp.repeat(jnp.asarray(tw_i, F32), FNET_W, axis=1)
    width = n2 * FNET_W
    ct = min(width, 4096)
    const2 = lambda shape: pl.BlockSpec(shape, lambda b, j: (0,) * len(shape))
    yr, yi = pl.pallas_call(
        _fft_a_kernel,
        out_shape=(jax.ShapeDtypeStruct((n_batch * n1, width), F32),) * 2,
        grid=(n_batch, width // ct),
        in_specs=[
            pl.BlockSpec((n1, ct), lambda b, j: (b, j)),
            const2(dh.shape), const2(dl.shape),
            pl.BlockSpec((n1, ct), lambda b, j: (0, j)),
            pl.BlockSpec((n1, ct), lambda b, j: (0, j)),
        ],
        out_specs=(pl.BlockSpec((n1, ct), lambda b, j: (b, j)),) * 2,
        compiler_params=_cparams(("arbitrary", "arbitrary")),
        name="fft_stage_a",
    )(uf2, dh, dl, tr, ti)
    yr = yr.reshape(n_batch * n1 * n2, FNET_W)
    yi = yi.reshape(n_batch * n1 * n2, FNET_W)
    norm = 1.0 / math.sqrt(seq * FNET_HD)
    out = pl.pallas_call(
        functools.partial(_fft_c_kernel, n2=n2, norm=norm),
        out_shape=jax.ShapeDtypeStruct((n_batch, n2, n1, FNET_W), F32),
        grid=(n_batch, n1 // 8),
        in_specs=[
            pl.BlockSpec((8 * n2, FNET_W), lambda b, i: (b * (n1 // 8) + i, 0)),
            pl.BlockSpec((8 * n2, FNET_W), lambda b, i: (b * (n1 // 8) + i, 0)),
            const2(mh.shape), const2(ml.shape), const2(ch.shape), const2(cl.shape), const2(wf.shape),
        ],
        out_specs=pl.BlockSpec((None, n2, 8, FNET_W), lambda b, i: (b, 0, i, 0)),
        compiler_params=_cparams(("arbitrary", "arbitrary")),
        name="fft_stage_c",
    )(yr, yi, mh, ml, ch, cl, wf)
    return out.reshape(n_batch * seq, FNET_W)


def _fnet_context(uf, wf, *, n_batch, ctx, first_block):
    c, s = _dft_parts(ctx)
    dh, dl = _np_split(np.concatenate([c, -s], axis=0))
    ch, cl = _np_split(_channel_consts())
    const = lambda shape: pl.BlockSpec(shape, lambda b: (0,) * len(shape))
    return pl.pallas_call(
        functools.partial(_fft_dense_kernel, n=ctx, norm=1.0 / math.sqrt(ctx * FNET_HD)),
        out_shape=jax.ShapeDtypeStruct((n_batch * ctx, FNET_W), F32),
        grid=(n_batch,),
        in_specs=[pl.BlockSpec((ctx, FNET_W), lambda b: (first_block + b, 0)),
                  const(dh.shape), const(dl.shape), const(ch.shape), const(cl.shape), const(wf.shape)],
        out_specs=pl.BlockSpec((ctx, FNET_W), lambda b: (b, 0)),
        compiler_params=_cparams(("arbitrary",)),
        name="fft_context",
    )(uf, dh, dl, ch, cl, wf)


def _outproj_kernel(x_ref, m_ref, yp_ref, yg_ref, ya_ref, yf_ref, w_ref, o_ref):
    acc = _dot(yp_ref[...].astype(BF16), w_ref[0:256, :])
    acc = acc + _dot(yg_ref[...].astype(BF16), w_ref[256:512, :])
    acc = acc + _dot(ya_ref[...].astype(BF16), w_ref[512:768, :])
    acc = acc + _dot(yf_ref[...].astype(BF16), w_ref[768:1024, :])
    o_ref[...] = x_ref[...] + m_ref[5:6, :] * acc


def _out_projection(x, mods, y_pool, y_gla, y_att, y_fnet, w_out, *, n_tiles, mod_map):
    d = x.shape[1]
    part = pl.BlockSpec((TM, 256), lambda t: (t, 0))
    return pl.pallas_call(
        _outproj_kernel,
        out_shape=jax.ShapeDtypeStruct((n_tiles * TM, d), F32),
        grid=(n_tiles,),
        in_specs=[
            pl.BlockSpec((TM, d), lambda t: (t, 0)),
            pl.BlockSpec((None, N_MOD, d), mod_map),
            part, part, part, part,
            pl.BlockSpec(w_out.shape, lambda t: (0, 0)),
        ],
        out_specs=pl.BlockSpec((TM, d), lambda t: (t, 0)),
        compiler_params=_cparams(("arbitrary",)),
        name="out_projection",
    )(x, mods, y_pool, y_gla, y_att, y_fnet, w_out)


def _rope_tables(seq):
    rows = seq // GRID_W
    row_id = jnp.repeat(jnp.arange(rows, dtype=F32), GRID_W)
    col_id = jnp.tile(jnp.arange(GRID_W, dtype=F32), rows)
    freqs = ROPE_THETA ** (-jnp.arange(ROPE_FREQS, dtype=F32) / ROPE_FREQS)
    row_ang = row_id[:, None] * freqs
    col_ang = col_id[:, None] * freqs
    ang = jnp.concatenate([row_ang, row_ang, col_ang, col_ang], axis=1)
    first_half = np.tile(np.repeat(np.array([1.0, 0.0, 1.0, 0.0]), ROPE_FREQS), ATT_QH)[None, :]
    ang = jnp.tile(ang, (1, ATT_QH))
    cos, sin = jnp.cos(ang), jnp.sin(ang)
    first_half = jnp.asarray(first_half, F32)
    pad = lambda a, v: jnp.concatenate([a, jnp.full((TM, ATT_W), v, F32)], axis=0)
    return pad(cos, 1.0), pad(-sin * first_half, 0.0), pad(sin * (1.0 - first_half), 0.0)


def _block_ones(width, group):
    return jnp.asarray(np.kron(np.eye(width // group), np.ones((group, group))), BF16)


def kernel(x, c, ctx, c_ctx, w_mod, b_mod, norm_g, ffn_wg, ffn_wu, ffn_wd, w_in, w_out,
           pool_w, pool_scale, gla_wa, gla_ba, gla_norm, att_qnorm, att_knorm, fnet_w, final_norm):
    n_batch, seq, d = x.shape
    n_ctx = ctx.shape[1]
    assert d == D_MODEL and seq % TM == 0 and (n_batch * n_ctx) % TM == 0 and n_ctx % TP == 0
    assert seq % (8 * FFT_N1) == 0 and n_ctx <= TM and TM % n_ctx == 0 and n_batch + 1 <= 8
    lat_rows = n_batch * seq
    lat_tiles = lat_rows // TM
    all_tiles = lat_tiles + (n_batch * n_ctx) // TM
    tiles_per_batch = seq // TM
    mod_map = _mod_row_map(lat_tiles, tiles_per_batch, n_batch)
    rope_map = lambda t: (jnp.where(t < lat_tiles, t % tiles_per_batch, tiles_per_batch), 0)

    xs = jnp.concatenate([x.reshape(lat_rows, d), ctx.reshape(n_batch * n_ctx, d)], axis=0)
    cvec = jnp.concatenate([c, c_ctx[None, :], jnp.zeros((8 - n_batch - 1, d), F32)], axis=0)
    mods = _modulation(cvec, w_mod, b_mod).reshape(DEPTH, 8, N_MOD, d)

    rope_c, rope_sa, rope_sb = _rope_tables(seq)
    e256 = _block_ones(ATT_W, ATT_HD)
    gla_ex = _block_ones(GLA_W, GLA_DV)[::2, :]
    gla_mk = jnp.asarray(np.kron(np.eye(GLA_HEADS), np.ones((GLA_DV, GLA_DK))), F32)

    for i in range(DEPTH):
        ctx_out = i < DEPTH - 1
        m = mods[i]
        wg = ffn_wg[i].astype(BF16)
        wu = ffn_wu[i].astype(BF16)
        wd = ffn_wd[i].astype(BF16)
        wi = w_in[i]
        wi = jnp.concatenate([wi[:, :768], wi[:, 800:D_IN], wi[:, 768:800],
                              jnp.zeros((d, D_IN_PAD - D_IN), F32)], axis=1).astype(BF16)
        wa_blk = jnp.zeros((128, 2 * GLA_KW), F32)
        wa_blk = wa_blk.at[:GLA_RANK, :GLA_KW].set(gla_wa[i, 0]).at[GLA_RANK:2 * GLA_RANK, GLA_KW:].set(gla_wa[i, 1])
        ba_blk = gla_ba[i].reshape(1, 2 * GLA_KW)
        qg = jnp.tile(att_qnorm[i], ATT_QH)[None, :]
        kg = jnp.tile(att_knorm[i], ATT_KVH)[None, :]
        pool_blk = jax.scipy.linalg.block_diag(*[pool_w[i, g] for g in range(len(POOL_WINDOWS))]).astype(BF16)
        gn = jnp.tile(gla_norm[i], GLA_HEADS)[None, :]
        wf = fnet_w[i].astype(BF16)
        wo = w_out[i].astype(BF16)

        xs = _half_ffn(xs, m, norm_g[i, 0], wg[0], wu[0], wd[0], final_norm,
                       mod_base=0, n_tiles=all_tiles, mod_map=mod_map)
        (u_pool, gq, gk, gv, og, gf, gb, aq, ak, av, uf) = _in_projection(
            xs, m, norm_g[i, 1], wi, wa_blk.astype(BF16), ba_blk, e256, qg, kg, rope_c, rope_sa, rope_sb,
            n_tiles=all_tiles, mod_map=mod_map, rope_map=rope_map)

        y_pool = _pool_mix(u_pool, pool_blk, pool_scale[i][None, :], seq=seq, n_seq=n_batch, first_block=0)
        y_gla = _gla_mix(gq, gk, gv, gf, gb, og, gla_ex, gla_mk, gn, e256, n_batch=n_batch, seq=seq, ctx=n_ctx)
        y_att = _attention(aq, ak, av, n_batch=n_batch, seq=seq, ctx=n_ctx, ctx_queries=False)
        n2 = seq // FFT_N1
        y_fnet = _fnet_latent(uf.reshape(uf.shape[0] // n2, n2 * FNET_W), wf, n_batch=n_batch, seq=seq)
        n_tiles = lat_tiles
        if ctx_out:
            first_ctx = lat_rows // n_ctx
            y_pool = jnp.concatenate([y_pool, _pool_mix(u_pool, pool_blk, pool_scale[i][None, :], seq=n_ctx,
                                                        n_seq=n_batch, first_block=first_ctx)], axis=0)
            y_att = jnp.concatenate([y_att, _attention(aq, ak, av, n_batch=n_batch, seq=seq, ctx=n_ctx,
                                                       ctx_queries=True)], axis=0)
            y_fnet = jnp.concatenate([y_fnet, _fnet_context(uf, wf, n_batch=n_batch, ctx=n_ctx,
                                                            first_block=first_ctx)], axis=0)
            n_tiles = all_tiles
        xs = _out_projection(xs, m, y_pool, y_gla, y_att, y_fnet, wo, n_tiles=n_tiles, mod_map=mod_map)
        xs = _half_ffn(xs, m, norm_g[i, 2], wg[1], wu[1], wd[1], final_norm,
                       mod_base=6, n_tiles=n_tiles, mod_map=mod_map, final=not ctx_out)
    return xs.reshape(n_batch, seq, d)
```

```python
import functools
import math

import jax
import jax.numpy as jnp
import numpy as np
from jax import lax
from jax.experimental import pallas as pl
from jax.experimental.pallas import tpu as pltpu

F32 = jnp.float32
BF16 = jnp.bfloat16

D_MODEL = 1024
DEPTH = 2
GRID_W = 64
EPS = 1e-6
N_MOD = 9
D_FF = 2816

POOL_W = 256
POOL_WINDOWS = (2, 4, 8, 16)
POOL_GW = 64
POOL_HALO = 8

GLA_HEADS = 4
GLA_W = 256
GLA_DV = 64
GLA_DK = 32
GLA_RANK = 16
GLA_TAU = 16.0
GLA_KW = GLA_HEADS * GLA_DK
GLA_SUB = 16
GLA_CHUNK = 64
GLA_FAST_RANGE = 150.0

ATT_W = 256
ATT_HD = 64
ATT_QH = 4
ATT_KVH = 2
ATT_KW = ATT_KVH * ATT_HD
ROPE_FREQS = 16
ROPE_THETA = 10000.0

FNET_W = 256
FNET_HEADS = 4
FNET_HD = 64
FFT_N1 = 64

D_IN = 1824
D_IN_PAD = 1920

O_POOL, O_GQ, O_GK, O_GV, O_OG, O_AQ, O_AK, O_AV, O_FN, O_R = 0, 256, 384, 512, 768, 1024, 1280, 1408, 1536, 1792

TM = 512
TQ = 256
TP = 256
FF_CHUNK = 1408
NEG_BIG = -1e30

VMEM_LIMIT = 56 * 1024 * 1024


def _cparams(sem):
    return pltpu.CompilerParams(dimension_semantics=sem, vmem_limit_bytes=VMEM_LIMIT)


def _dot(a, b):
    return jnp.dot(a, b, preferred_element_type=F32)


def _split(a):
    hi = a.astype(BF16)
    lo = (a - hi.astype(F32)).astype(BF16)
    return hi, lo


def _dot3(a_hi, a_lo, b_hi, b_lo):
    return _dot(a_hi, b_hi) + (_dot(a_hi, b_lo) + _dot(a_lo, b_hi))


def _rms(x):
    return x * lax.rsqrt(jnp.mean(x * x, axis=-1, keepdims=True) + EPS)


def _silu(x):
    return x * jax.nn.sigmoid(x)


def _group_rms(x, e, width):
    s_hi, s_lo = _split(x * x)
    ss = _dot(s_hi, e) + _dot(s_lo, e)
    return x * lax.rsqrt(ss * (1.0 / width) + EPS)


def _mod_kernel(c_ref, w_ref, b_ref, o_ref):
    s = _silu(c_ref[...]).astype(BF16)
    o_ref[...] = _dot(s, w_ref[...].astype(BF16)) + b_ref[...]


def _modulation(cvec, w_mod, b_mod):
    depth, d, nd = w_mod.shape
    tn = nd // 8
    return pl.pallas_call(
        _mod_kernel,
        out_shape=jax.ShapeDtypeStruct((depth, 8, nd), F32),
        grid=(depth, nd // tn),
        in_specs=[
            pl.BlockSpec((8, d), lambda l, j: (0, 0)),
            pl.BlockSpec((None, d, tn), lambda l, j: (l, 0, j)),
            pl.BlockSpec((None, 1, tn), lambda l, j: (l, 0, j)),
        ],
        out_specs=pl.BlockSpec((None, 8, tn), lambda l, j: (l, 0, j)),
        compiler_params=_cparams(("arbitrary", "arbitrary")),
        name="modulation",
    )(cvec, w_mod, b_mod.reshape(depth, 1, nd))


def _ffn_kernel(*refs, mod_base, final, n_lat_tiles):
    if n_lat_tiles is None:
        x_ref, m_ref, g_ref, wg_ref, wu_ref, wd_ref, fg_ref, o_ref = refs
        x = x_ref[...]
    else:
        x_ref, c_ref, m_ref, g_ref, wg_ref, wu_ref, wd_ref, fg_ref, o_ref = refs
        x = jnp.where(pl.program_id(0) < n_lat_tiles, x_ref[...], c_ref[...])
    shift = m_ref[mod_base:mod_base + 1, :]
    scale = m_ref[mod_base + 1:mod_base + 2, :]
    gate = m_ref[mod_base + 2:mod_base + 3, :]
    h = ((_rms(x) * g_ref[...]) * (1.0 + scale) + shift).astype(BF16)
    y = jnp.zeros(x.shape, F32)
    for c in range(D_FF // FF_CHUNK):
        sl = slice(c * FF_CHUNK, (c + 1) * FF_CHUNK)
        a = _dot(h, wg_ref[:, sl])
        u = _dot(h, wu_ref[:, sl])
        y = y + _dot((_silu(a) * u).astype(BF16), wd_ref[sl, :])
    out = x + (0.5 * gate) * y
    if final:
        out = _rms(out) * fg_ref[...]
    o_ref[...] = out


def _mod_spec(layer, n_lat_tiles, tiles_per_batch, n_batch, d):
    def index_map(t):
        return (layer, jnp.where(t < n_lat_tiles, t // tiles_per_batch, n_batch), 0, 0)
    return pl.BlockSpec((None, None, N_MOD, d), index_map)


def _resident(arr, lead):
    block = (None,) * len(lead) + arr.shape[len(lead):]
    index = tuple(lead) + (0,) * (arr.ndim - len(lead))
    return pl.BlockSpec(block, lambda t: index, pipeline_mode=pl.Buffered(1))


def _half_ffn(x, mods, g, wg, wu, wd, fg, *, layer, half, n_tiles, mod_map, final=False, ctx_rows=None):
    d = x.shape[1]
    lead = (layer, half)
    n_lat = None if ctx_rows is None else x.shape[0] // TM
    if ctx_rows is None:
        rows, row_specs = [x], [pl.BlockSpec((TM, d), lambda t: (t, 0))]
    else:
        rows = [x, ctx_rows]
        row_specs = [pl.BlockSpec((TM, d), lambda t: (jnp.minimum(t, n_lat - 1), 0)),
                     pl.BlockSpec((TM, d), lambda t: (jnp.maximum(t - n_lat, 0), 0))]
    return pl.pallas_call(
        functools.partial(_ffn_kernel, mod_base=3 * half * 2, final=final, n_lat_tiles=n_lat),
        out_shape=jax.ShapeDtypeStruct((n_tiles * TM, d), F32),
        grid=(n_tiles,),
        in_specs=row_specs + [
            mod_map,
            pl.BlockSpec((None, None, 1, d), lambda t: (layer, 2 * half, 0, 0)),
            _resident(wg, lead),
            _resident(wu, lead),
            _resident(wd, lead),
            pl.BlockSpec((1, d), lambda t: (0, 0)),
        ],
        out_specs=pl.BlockSpec((TM, d), lambda t: (t, 0)),
        compiler_params=_cparams(("arbitrary",)),
        name="half_ffn",
    )(*rows, mods, g, wg, wu, wd, fg.reshape(1, d))


def _rope(x, c, sa, sb):
    w = x.shape[1]
    return x * c + pltpu.roll(x, w - ROPE_FREQS, 1) * sa + pltpu.roll(x, ROPE_FREQS, 1) * sb


def _tile_rope_table(line_ref, col_ref):
    lines, col = line_ref[...], col_ref[...]
    return jnp.concatenate([col + lines[r:r + 1, :] for r in range(lines.shape[0])], axis=0)


def _inproj_kernel(x_ref, m_ref, g_ref, w_ref, wa_ref, ba_ref, e_ref, qg_ref, kg_ref,
                   lc_ref, lsa_ref, lsb_ref, cc_ref, csa_ref, csb_ref,
                   up_ref, gq_ref, gk_ref, gv_ref, og_ref, gf_ref, gb_ref,
                   aq_ref, ak_ref, av_ref, uf_ref):
    x = x_ref[...]
    shift = m_ref[3:4, :]
    scale = m_ref[4:5, :]
    h = ((_rms(x) * g_ref[...]) * (1.0 + scale) + shift).astype(BF16)
    p = _dot(h, w_ref[...])

    up_ref[...] = p[:, O_POOL:O_POOL + POOL_W]
    uf_ref[...] = p[:, O_FN:O_FN + FNET_W]

    gq_ref[...] = p[:, O_GQ:O_GQ + GLA_KW] * (GLA_DK ** -0.5)
    gk_ref[...] = p[:, O_GK:O_GK + GLA_KW]
    gv_ref[...] = p[:, O_GV:O_GV + GLA_W]
    og_ref[...] = p[:, O_OG:O_OG + GLA_W]
    z = _dot(p[:, O_R:O_R + 128].astype(BF16), wa_ref[...]) + ba_ref[...]
    logsig = jnp.minimum(z, 0.0) - jnp.log(1.0 + jnp.exp(-jnp.abs(z)))
    gdec = logsig * (1.0 / GLA_TAU)
    gf_ref[...] = gdec[:, :GLA_KW]
    gb_ref[...] = gdec[:, GLA_KW:]

    e = e_ref[...]
    rc = _tile_rope_table(lc_ref, cc_ref)
    rsa = _tile_rope_table(lsa_ref, csa_ref)
    rsb = _tile_rope_table(lsb_ref, csb_ref)
    q = _group_rms(p[:, O_AQ:O_AQ + ATT_W], e, ATT_HD) * qg_ref[...]
    q = _rope(q, rc, rsa, rsb) * (ATT_HD ** -0.5 * math.log2(math.e))
    aq_ref[...] = q.T.astype(BF16)
    k = _group_rms(p[:, O_AK:O_AK + ATT_KW], e[:ATT_KW, :ATT_KW], ATT_HD) * kg_ref[...]
    k = _rope(k, rc[:, :ATT_KW], rsa[:, :ATT_KW], rsb[:, :ATT_KW])
    ak_ref[...] = k.astype(BF16)
    av_ref[...] = p[:, O_AV:O_AV + ATT_KW].T.astype(BF16)


def _in_projection(x, mods, g, w_in, wa_blk, ba_blk, e256, qg, kg, rope_lines, rope_cols,
                   *, layer, n_tiles, mod_map, line_map, col_map):
    t_rows, d = x.shape
    row = lambda w: pl.BlockSpec((TM, w), lambda t: (t, 0))
    const = lambda shape: pl.BlockSpec(shape, lambda t: (0,) * len(shape))
    per_layer = lambda arr: pl.BlockSpec((None,) + arr.shape[1:], lambda t: (layer,) + (0,) * (arr.ndim - 1))
    out_shapes = (
        jax.ShapeDtypeStruct((t_rows, POOL_W), F32),
        jax.ShapeDtypeStruct((t_rows, GLA_KW), F32),
        jax.ShapeDtypeStruct((t_rows, GLA_KW), F32),
        jax.ShapeDtypeStruct((t_rows, GLA_W), F32),
        jax.ShapeDtypeStruct((t_rows, GLA_W), F32),
        jax.ShapeDtypeStruct((t_rows, GLA_KW), F32),
        jax.ShapeDtypeStruct((t_rows, GLA_KW), F32),
        jax.ShapeDtypeStruct((n_tiles, ATT_W, TM), BF16),
        jax.ShapeDtypeStruct((t_rows, ATT_KW), BF16),
        jax.ShapeDtypeStruct((n_tiles, ATT_KW, TM), BF16),
        jax.ShapeDtypeStruct((t_rows, FNET_W), F32),
    )
    out_specs = (
        row(POOL_W), row(GLA_KW), row(GLA_KW), row(GLA_W), row(GLA_W), row(GLA_KW), row(GLA_KW),
        pl.BlockSpec((None, ATT_W, TM), lambda t: (t, 0, 0)),
        row(ATT_KW),
        pl.BlockSpec((None, ATT_KW, TM), lambda t: (t, 0, 0)),
        row(FNET_W),
    )
    return pl.pallas_call(
        _inproj_kernel,
        out_shape=out_shapes,
        grid=(n_tiles,),
        in_specs=[
            pl.BlockSpec((TM, d), lambda t: (t, 0)),
            mod_map,
            pl.BlockSpec((None, None, 1, d), lambda t: (layer, 1, 0, 0)),
            per_layer(w_in),
            per_layer(wa_blk),
            per_layer(ba_blk),
            const(e256.shape),
            per_layer(qg),
            per_layer(kg),
        ] + [pl.BlockSpec((TM // GRID_W, ATT_W), line_map)] * 3
          + [pl.BlockSpec((None, GRID_W, ATT_W), col_map)] * 3,
        out_specs=out_specs,
        compiler_params=_cparams(("arbitrary",)),
        name="in_projection",
    )(x, mods, g, w_in, wa_blk, ba_blk, e256, qg, kg, *rope_lines, *rope_cols)


def _pool_kernel(u_ref, w_ref, s_ref, o_ref, pad_ref, *, seq, chunk):
    halo = POOL_HALO
    pad_ref[0:halo, :] = jnp.zeros((halo, POOL_W), F32)
    pad_ref[halo + seq:halo + seq + halo, :] = jnp.zeros((halo, POOL_W), F32)
    pad_ref[halo:halo + seq, :] = u_ref[...]
    rows = chunk + 2 * halo
    lane = lax.broadcasted_iota(jnp.int32, (chunk, POOL_W), 1)
    trow = lax.broadcasted_iota(jnp.int32, (chunk, POOL_W), 0)

    def body(ci, carry):
        c0 = pl.multiple_of(ci * chunk, chunk)
        xp = pad_ref[pl.ds(c0, rows), :]
        u = xp[halo:halo + chunk, :]
        t = trow + c0
        acc = xp
        m = None
        for gi, w in enumerate(POOL_WINDOWS):
            acc = acc + pltpu.roll(acc, w // 2, 0)
            lead = w // 2 - 1
            win = acc if lead == 0 else pltpu.roll(acc, rows - lead, 0)
            win = win[halo:halo + chunk, :]
            cnt = (jnp.minimum(t + w // 2, seq) - jnp.maximum(t - w // 2, 0)).astype(F32)
            mg = win / cnt - u
            m = mg if m is None else jnp.where(lane >= gi * POOL_GW, mg, m)
        y = _dot(m.astype(BF16), w_ref[...]) * s_ref[...]
        o_ref[pl.ds(c0, chunk), :] = y
        return carry

    lax.fori_loop(0, seq // chunk, body, 0)


def _pool_mix(u, w_blk, s_pool, *, layer, seq, n_seq, first_block):
    chunk = min(256, seq)
    return pl.pallas_call(
        functools.partial(_pool_kernel, seq=seq, chunk=chunk),
        out_shape=jax.ShapeDtypeStruct((n_seq * seq, POOL_W), F32),
        grid=(n_seq,),
        in_specs=[
            pl.BlockSpec((seq, POOL_W), lambda b: (first_block + b, 0)),
            pl.BlockSpec((None, POOL_W, POOL_W), lambda b: (layer, 0, 0)),
            pl.BlockSpec((None, 1, POOL_W), lambda b: (layer, 0, 0)),
        ],
        out_specs=pl.BlockSpec((seq, POOL_W), lambda b: (b, 0)),
        scratch_shapes=[pltpu.VMEM((seq + 2 * POOL_HALO, POOL_W), F32)],
        compiler_params=_cparams(("arbitrary",)),
        name="pool_mix",
    )(u, w_blk, s_pool)


def _gla_kernel(*refs, reverse, finalize):
    if finalize:
        (q_ref, k_ref, v_ref, g_ref, ex_ref, mk_ref, bm_ref, of_ref, og_ref, gn_ref, e_ref,
         o_ref, st_ref, ob_ref) = refs
    else:
        q_ref, k_ref, v_ref, g_ref, ex_ref, mk_ref, bm_ref, o_ref, st_ref = refs
        ob_ref = o_ref

    @pl.when(pl.program_id(1) == 0)
    def _():
        st_ref[...] = jnp.zeros(st_ref.shape, F32)

    g = g_ref[...]
    trow = lax.broadcasted_iota(jnp.int32, (TP, GLA_KW), 0) % GLA_CHUNK
    pre, suf = g, g
    s = 1
    while s < GLA_CHUNK:
        pre = pre + jnp.where(trow >= s, pltpu.roll(pre, s, 0), 0.0)
        suf = suf + jnp.where(trow < GLA_CHUNK - s, pltpu.roll(suf, TP - s, 0), 0.0)
        s *= 2
    tot = pre + suf - g
    in_range = jnp.max(jnp.abs(tot)) <= GLA_FAST_RANGE

    @pl.when(in_range)
    def _():
        _gla_fast_tile(q_ref, k_ref, v_ref, mk_ref, bm_ref, st_ref, ob_ref,
                       suf if reverse else pre, tot, reverse)

    @pl.when(jnp.logical_not(in_range))
    def _():
        _gla_exact_tile(q_ref, k_ref, v_ref, g_ref, ex_ref, mk_ref, st_ref, ob_ref, reverse)

    if finalize:
        both = ob_ref[...] + of_ref[...]
        y = _group_rms(both, e_ref[...], GLA_DV) * gn_ref[...]
        o_ref[...] = y * _silu(og_ref[...])


def _gla_fast_tile(q_ref, k_ref, v_ref, mk_ref, bm_ref, st_ref, ob_ref, b, tot, reverse):
    ch = GLA_CHUNK
    half = 0.5 * tot
    q, k = q_ref[...], k_ref[...]
    q_in = (q * jnp.exp(b - half)).astype(BF16)
    k_in = (k * jnp.exp(half - b)).astype(BF16)
    q_st = (q * jnp.exp(b)).astype(BF16)
    k_st = (k * jnp.exp(tot - b)).astype(BF16)
    dec = jnp.exp(tot)
    mk = mk_ref[...]
    mk16 = mk.astype(BF16)
    bm16 = bm_ref[...]
    irow = lax.broadcasted_iota(jnp.int32, (ch, GLA_HEADS * ch), 0)
    jcol = lax.broadcasted_iota(jnp.int32, (ch, GLA_HEADS * ch), 1) % ch
    seen = (jcol >= irow) if reverse else (jcol <= irow)
    nt = (((1,), (1,)), ((), ()))
    st = st_ref[...]
    n_chunks = TP // ch
    for ci in range(n_chunks):
        c = (n_chunks - 1 - ci) if reverse else ci
        rows = slice(c * ch, (c + 1) * ch)
        kb = jnp.concatenate([k_in[rows]] * GLA_HEADS, axis=0) * mk16
        a = lax.dot_general(q_in[rows], kb, nt, preferred_element_type=F32)
        a = jnp.where(seen, a, 0.0).astype(BF16)
        v = v_ref[rows, :]
        vb = jnp.concatenate([v.astype(BF16)] * GLA_HEADS, axis=0) * bm16
        o = _dot(a, vb) + lax.dot_general(q_st[rows], st.astype(BF16), nt, preferred_element_type=F32)
        kv = lax.dot_general(v.astype(BF16), k_st[rows], (((0,), (0,)), ((), ())),
                             preferred_element_type=F32)
        st = st * dec[c * ch:c * ch + 1, :] + kv * mk
        ob_ref[rows, :] = o
    st_ref[...] = st


def _gla_exact_tile(q_ref, k_ref, v_ref, g_ref, ex_ref, mk_ref, st_ref, ob_ref, reverse):
    sub = GLA_SUB
    n_chunks = TP // sub
    row = lax.broadcasted_iota(jnp.int32, (sub, GLA_KW), 0)
    ex = ex_ref[...]
    mk = mk_ref[...]

    def body(ci, carry):
        c = (n_chunks - 1 - ci) if reverse else ci
        r0 = pl.multiple_of(c * sub, sub)
        q = q_ref[pl.ds(r0, sub), :]
        k = k_ref[pl.ds(r0, sub), :]
        v = v_ref[pl.ds(r0, sub), :]
        b = g_ref[pl.ds(r0, sub), :]
        s = 1
        while s < sub:
            if reverse:
                b = b + jnp.where(row < sub - s, pltpu.roll(b, sub - s, 0), 0.0)
            else:
                b = b + jnp.where(row >= s, pltpu.roll(b, s, 0), 0.0)
            s *= 2
        edge = b[0:1, :] if reverse else b[sub - 1:sub, :]
        st = st_ref[...]
        qd = (q * jnp.exp(b)).astype(BF16)
        o = lax.dot_general(qd, st.astype(BF16), (((1,), (1,)), ((), ())),
                            preferred_element_type=F32)
        parts = []
        for j in range(sub):
            seen = (row <= j) if reverse else (row >= j)
            dec = jnp.exp(jnp.where(seen, b - b[j:j + 1, :], NEG_BIG))
            parts.append((dec * q * k[j:j + 1, :]).astype(BF16))
        r = _dot(jnp.concatenate(parts, axis=0), ex)
        for j in range(sub):
            o = o + r[j * sub:(j + 1) * sub, :] * v[j:j + 1, :]
        kd = (k * jnp.exp(edge - b)).astype(BF16)
        kv = lax.dot_general(v.astype(BF16), kd, (((0,), (0,)), ((), ())),
                             preferred_element_type=F32)
        st_ref[...] = st * jnp.exp(edge) + kv * mk
        ob_ref[pl.ds(r0, sub), :] = o
        return carry

    lax.fori_loop(0, n_chunks, body, 0, unroll=2)


def _gla_tile_map(n_batch, n_lat, n_ctx, reverse):
    def index_map(b, s):
        is_ctx = s < n_ctx
        sc = jnp.where(is_ctx, s, 0)
        sl = jnp.where(is_ctx, 0, s - n_ctx)
        if reverse:
            sc = n_ctx - 1 - sc
            sl = n_lat - 1 - sl
        return (jnp.where(is_ctx, n_batch * n_lat + b * n_ctx + sc, b * n_lat + sl), 0)
    return index_map


def _gla_mix(gq, gk, gv, gf, gb, og, ex, mk, gn, e256, *, n_batch, seq, ctx):
    t_rows = gq.shape[0]
    n_lat, n_ctx = seq // TP, ctx // TP
    grid = (n_batch, n_lat + n_ctx)
    const = lambda shape: pl.BlockSpec(shape, lambda b, s: (0,) * len(shape))

    def run(reverse, finalize, gate, extra):
        tmap = _gla_tile_map(n_batch, n_lat, n_ctx, reverse)
        rows = lambda w: pl.BlockSpec((TP, w), tmap)
        in_specs = [rows(GLA_KW), rows(GLA_KW), rows(GLA_W), rows(GLA_KW), const(ex.shape), const(mk.shape),
                    const(e256.shape)]
        args = [gq, gk, gv, gate, ex, mk, e256]
        scratch = [pltpu.VMEM((GLA_W, GLA_KW), F32)]
        if finalize:
            in_specs += [rows(GLA_W), rows(GLA_W), const((1, GLA_W)), const(e256.shape)]
            args += extra
            scratch.append(pltpu.VMEM((TP, GLA_W), F32))
        return pl.pallas_call(
            functools.partial(_gla_kernel, reverse=reverse, finalize=finalize),
            out_shape=jax.ShapeDtypeStruct((t_rows, GLA_W), F32),
            grid=grid,
            in_specs=in_specs,
            out_specs=rows(GLA_W),
            scratch_shapes=scratch,
            compiler_params=_cparams(("arbitrary", "arbitrary")),
            name="gla_bwd" if reverse else "gla_fwd",
        )(*args)

    o_f = run(False, False, gf, None)
    return run(True, True, gb, [o_f, og, gn, e256])


def _att_kernel(*refs, n_main, has_ctx):
    if has_ctx:
        q_ref, km_ref, vm_ref, kc_ref, vc_ref, o_ref, s0_ref, s1_ref = refs
    else:
        q_ref, km_ref, vm_ref, o_ref, s0_ref, s1_ref = refs
    q_t = q_ref[...]
    tq = q_t.shape[1]
    tk = vm_ref.shape[2]
    zero = jnp.zeros((ATT_HD, tq), BF16)
    groups = []
    for g in range(ATT_KVH):
        halves = []
        for h in (2 * g, 2 * g + 1):
            qh = q_t[h * ATT_HD:(h + 1) * ATT_HD, :]
            halves.append(jnp.concatenate([qh, zero] if g == 0 else [zero, qh], axis=0))
        groups.append(jnp.concatenate(halves, axis=1))

    def scores(kblk, buf):
        for g in range(ATT_KVH):
            buf[g, 0:kblk.shape[0], :] = _dot(kblk, groups[g])

    def consume(buf, rows, vblk, carry):
        out = []
        for g in range(ATT_KVH):
            m, l, acc = carry[g]
            s = buf[g, 0:rows, :]
            m_new = jnp.maximum(m, jnp.max(s, axis=0, keepdims=True))
            alpha = jnp.exp2(m - m_new)
            p = jnp.exp2(s - m_new)
            l = alpha * l + jnp.sum(p, axis=0, keepdims=True)
            pv = _dot(vblk, p.astype(BF16))
            acc = alpha * acc + pv[g * ATT_HD:(g + 1) * ATT_HD, :]
            out.append((m_new, l, acc))
        return tuple(out)

    def k_main(j):
        return km_ref[pl.ds(pl.multiple_of(j * tk, tk), tk), :]

    carry = tuple((jnp.full((1, 2 * tq), -jnp.inf, F32), jnp.zeros((1, 2 * tq), F32),
                   jnp.zeros((ATT_HD, 2 * tq), F32)) for _ in range(ATT_KVH))

    scores(k_main(0), s0_ref)
    if n_main > 1:
        def body(i, carry):
            j = 2 * i
            scores(k_main(j + 1), s1_ref)
            carry = consume(s0_ref, tk, vm_ref[j], carry)
            scores(k_main(j + 2), s0_ref)
            return consume(s1_ref, tk, vm_ref[j + 1], carry)

        carry = lax.fori_loop(0, n_main // 2 - 1, body, carry)
        scores(k_main(n_main - 1), s1_ref)
        carry = consume(s0_ref, tk, vm_ref[n_main - 2], carry)
        last_buf, other_buf = s1_ref, s0_ref
    else:
        last_buf, other_buf = s0_ref, s1_ref
    if has_ctx:
        scores(kc_ref[...], other_buf)
    carry = consume(last_buf, tk, vm_ref[n_main - 1], carry)
    if has_ctx:
        carry = consume(other_buf, kc_ref.shape[0], vc_ref[...], carry)
    heads = []
    for g in range(ATT_KVH):
        _, l, acc = carry[g]
        og = acc / l
        heads += [og[:, :tq], og[:, tq:]]
    o_ref[...] = jnp.concatenate(heads, axis=0).T.astype(o_ref.dtype)


def _attention(aq, ak, av, *, n_batch, seq, ctx, ctx_queries):
    lat_tiles = seq // TM
    ctx_tile = n_batch * lat_tiles
    ctx_cols = lambda b, i: (ctx_tile + (b * ctx) // TM, 0, ((b * ctx) % TM) // ctx)
    if not ctx_queries:
        sub = TM // TQ
        grid = (n_batch, seq // TQ)
        in_specs = [
            pl.BlockSpec((None, ATT_W, TQ), lambda b, i: (b * lat_tiles + i // sub, 0, i % sub)),
            pl.BlockSpec((seq, ATT_KW), lambda b, i: (b, 0)),
            pl.BlockSpec((lat_tiles, ATT_KW, TM), lambda b, i: (b, 0, 0)),
            pl.BlockSpec((ctx, ATT_KW), lambda b, i: (n_batch * seq // ctx + b, 0)),
            pl.BlockSpec((None, ATT_KW, ctx), ctx_cols),
        ]
        out_spec = pl.BlockSpec((TQ, ATT_W), lambda b, i: (b * (seq // TQ) + i, 0))
        assert lat_tiles % 2 == 0
        kern = functools.partial(_att_kernel, n_main=lat_tiles, has_ctx=True)
        out_rows = n_batch * seq
        args = (aq, ak, av, ak, av)
        s_shape = (ATT_KVH, TM, 2 * TQ)
    else:
        grid = (n_batch, 1)
        in_specs = [
            pl.BlockSpec((None, ATT_W, ctx), ctx_cols),
            pl.BlockSpec((ctx, ATT_KW), lambda b, i: (n_batch * seq // ctx + b, 0)),
            pl.BlockSpec((1, ATT_KW, ctx), ctx_cols),
        ]
        out_spec = pl.BlockSpec((ctx, ATT_W), lambda b, i: (b, 0))
        kern = functools.partial(_att_kernel, n_main=1, has_ctx=False)
        out_rows = n_batch * ctx
        args = (aq, ak, av)
        s_shape = (ATT_KVH, ctx, 2 * ctx)
    return pl.pallas_call(
        kern,
        out_shape=jax.ShapeDtypeStruct((out_rows, ATT_W), BF16),
        grid=grid,
        in_specs=in_specs,
        out_specs=out_spec,
        scratch_shapes=[pltpu.VMEM(s_shape, F32), pltpu.VMEM(s_shape, F32)],
        compiler_params=_cparams(("arbitrary", "arbitrary")),
        name="attention_ctx" if ctx_queries else "attention",
    )(*args)


def _fft_a_kernel(x_ref, dh_ref, dl_ref, tr_ref, ti_ref, yr_ref, yi_ref):
    dh, dl = dh_ref[...], dl_ref[...]
    tr, ti = tr_ref[...], ti_ref[...]
    for r in range(x_ref.shape[1]):
        x_hi, x_lo = _split(x_ref[:, r, :])
        y = _dot3(dh, dl, x_hi, x_lo)
        yr, yi = y[:FFT_N1, :], y[FFT_N1:, :]
        c, s = tr[:, r:r + 1], ti[:, r:r + 1]
        yr_ref[:, r, :] = yr * c - yi * s
        yi_ref[:, r, :] = yr * s + yi * c


def _channel_mix(xr, xi, ch_ref, cl_ref, wf_ref, norm):
    xc = jnp.concatenate([xr, xi], axis=1)
    x_hi, x_lo = _split(xc)
    f = _dot3(x_hi, x_lo, ch_ref[...], cl_ref[...]) * norm
    return _dot(f.astype(BF16), wf_ref[...])


def _fft_c_kernel(yr_ref, yi_ref, mh_ref, ml_ref, ch_ref, cl_ref, wf_ref, o_ref, *, n2, norm):
    mh, ml = mh_ref[...], ml_ref[...]
    for j in range(8):
        blk = jnp.concatenate([yr_ref[j * n2:(j + 1) * n2, :], yi_ref[j * n2:(j + 1) * n2, :]], axis=0)
        b_hi, b_lo = _split(blk)
        x = _dot3(mh, ml, b_hi, b_lo)
        o_ref[:, j, :] = _channel_mix(x[:n2, :], x[n2:, :], ch_ref, cl_ref, wf_ref, norm)


def _fft_dense_kernel(x_ref, dh_ref, dl_ref, ch_ref, cl_ref, wf_ref, o_ref, *, n, norm):
    x_hi, x_lo = _split(x_ref[...])
    x = _dot3(dh_ref[...], dl_ref[...], x_hi, x_lo)
    o_ref[...] = _channel_mix(x[:n, :], x[n:, :], ch_ref, cl_ref, wf_ref, norm)


def _dft_parts(n):
    idx = np.arange(n)
    ang = 2.0 * np.pi * ((idx[:, None] * idx[None, :]) % n) / n
    return np.cos(ang), np.sin(ang)


def _np_split(a):
    a = jnp.asarray(a, F32)
    return _split(a)


def _fnet_consts(seq):
    n1, n2 = FFT_N1, seq // FFT_N1
    c1, s1 = _dft_parts(n1)
    da = np.concatenate([c1, -s1], axis=0)
    k1 = np.arange(n1)[:, None]
    m2 = np.arange(n2)[None, :]
    ang = 2.0 * np.pi * ((k1 * m2) % seq) / seq
    tw_r, tw_i = np.cos(ang), -np.sin(ang)
    c2, s2 = _dft_parts(n2)
    mc = np.block([[c2, s2], [-s2, c2]])
    return da, tw_r, tw_i, mc


def _channel_consts():
    cc, sc = _dft_parts(FNET_HD)
    eye = np.eye(FNET_HEADS)
    return np.concatenate([np.kron(eye, cc), np.kron(eye, sc)], axis=0)


def _fnet_latent(uf3, wf, *, layer, n_batch, seq):
    n1, n2 = FFT_N1, seq // FFT_N1
    da, tw_r, tw_i, mc = _fnet_consts(seq)
    dh, dl = _np_split(da)
    mh, ml = _np_split(mc)
    ch, cl = _np_split(_channel_consts())
    rb = min(n2, 32)
    blocked = lambda tw: jnp.asarray(tw.reshape(n1, n2 // rb, rb).transpose(1, 0, 2), F32)
    const2 = lambda shape: pl.BlockSpec(shape, lambda b, j: (0,) * len(shape))
    slab = pl.BlockSpec((n1, rb, FNET_W), lambda b, j: (b, j, 0))
    yr, yi = pl.pallas_call(
        _fft_a_kernel,
        out_shape=(jax.ShapeDtypeStruct((n_batch * n1, n2, FNET_W), F32),) * 2,
        grid=(n_batch, n2 // rb),
        in_specs=[
            slab,
            const2(dh.shape), const2(dl.shape),
            pl.BlockSpec((None, n1, rb), lambda b, j: (j, 0, 0)),
            pl.BlockSpec((None, n1, rb), lambda b, j: (j, 0, 0)),
        ],
        out_specs=(slab, slab),
        compiler_params=_cparams(("arbitrary", "arbitrary")),
        name="fft_stage_a",
    )(uf3, dh, dl, blocked(tw_r), blocked(tw_i))
    yr = yr.reshape(n_batch * n1 * n2, FNET_W)
    yi = yi.reshape(n_batch * n1 * n2, FNET_W)
    norm = 1.0 / math.sqrt(seq * FNET_HD)
    out = pl.pallas_call(
        functools.partial(_fft_c_kernel, n2=n2, norm=norm),
        out_shape=jax.ShapeDtypeStruct((n_batch, n2, n1, FNET_W), F32),
        grid=(n_batch, n1 // 8),
        in_specs=[
            pl.BlockSpec((8 * n2, FNET_W), lambda b, i: (b * (n1 // 8) + i, 0)),
            pl.BlockSpec((8 * n2, FNET_W), lambda b, i: (b * (n1 // 8) + i, 0)),
            const2(mh.shape), const2(ml.shape), const2(ch.shape), const2(cl.shape),
            pl.BlockSpec((None,) + wf.shape[1:], lambda b, i: (layer, 0, 0)),
        ],
        out_specs=pl.BlockSpec((None, n2, 8, FNET_W), lambda b, i: (b, 0, i, 0)),
        compiler_params=_cparams(("arbitrary", "arbitrary")),
        name="fft_stage_c",
    )(yr, yi, mh, ml, ch, cl, wf)
    return out.reshape(n_batch * seq, FNET_W)


def _fnet_context(uf, wf, *, layer, n_batch, ctx, first_block):
    c, s = _dft_parts(ctx)
    dh, dl = _np_split(np.concatenate([c, -s], axis=0))
    ch, cl = _np_split(_channel_consts())
    const = lambda shape: pl.BlockSpec(shape, lambda b: (0,) * len(shape))
    return pl.pallas_call(
        functools.partial(_fft_dense_kernel, n=ctx, norm=1.0 / math.sqrt(ctx * FNET_HD)),
        out_shape=jax.ShapeDtypeStruct((n_batch * ctx, FNET_W), F32),
        grid=(n_batch,),
        in_specs=[pl.BlockSpec((ctx, FNET_W), lambda b: (first_block + b, 0)),
                  const(dh.shape), const(dl.shape), const(ch.shape), const(cl.shape),
                  pl.BlockSpec((None,) + wf.shape[1:], lambda b: (layer, 0, 0))],
        out_specs=pl.BlockSpec((ctx, FNET_W), lambda b: (b, 0)),
        compiler_params=_cparams(("arbitrary",)),
        name="fft_context",
    )(uf, dh, dl, ch, cl, wf)


def _outproj_kernel(*refs, n_lat_tiles):
    if n_lat_tiles is None:
        x_ref, m_ref, yp_ref, yg_ref, ya_ref, yf_ref, w_ref, o_ref = refs
        yp, ya, yf = yp_ref[...], ya_ref[...], yf_ref[...]
    else:
        x_ref, m_ref, yp_ref, yg_ref, ya_ref, yf_ref, cp_ref, ca_ref, cf_ref, w_ref, o_ref = refs
        lat = pl.program_id(0) < n_lat_tiles
        yp = jnp.where(lat, yp_ref[...], cp_ref[...])
        ya = jnp.where(lat, ya_ref[...], ca_ref[...])
        yf = jnp.where(lat, yf_ref[...], cf_ref[...])
    acc = _dot(yp.astype(BF16), w_ref[0:256, :])
    acc = acc + _dot(yg_ref[...].astype(BF16), w_ref[256:512, :])
    acc = acc + _dot(ya.astype(BF16), w_ref[512:768, :])
    acc = acc + _dot(yf.astype(BF16), w_ref[768:1024, :])
    o_ref[...] = x_ref[...] + m_ref[5:6, :] * acc


def _out_projection(x, mods, y_pool, y_gla, y_att, y_fnet, w_out, *, layer, n_tiles, mod_map, ctx_parts=None):
    d = x.shape[1]
    part = pl.BlockSpec((TM, 256), lambda t: (t, 0))
    n_lat = None
    parts, part_specs = [y_pool, y_gla, y_att, y_fnet], [part, part, part, part]
    if ctx_parts is not None:
        n_lat = y_pool.shape[0] // TM
        lat_part = pl.BlockSpec((TM, 256), lambda t: (jnp.minimum(t, n_lat - 1), 0))
        ctx_part = pl.BlockSpec((TM, 256), lambda t: (jnp.maximum(t - n_lat, 0), 0))
        parts += list(ctx_parts)
        part_specs = [lat_part, part, lat_part, lat_part, ctx_part, ctx_part, ctx_part]
    return pl.pallas_call(
        functools.partial(_outproj_kernel, n_lat_tiles=n_lat),
        out_shape=jax.ShapeDtypeStruct((n_tiles * TM, d), F32),
        grid=(n_tiles,),
        in_specs=[pl.BlockSpec((TM, d), lambda t: (t, 0)), mod_map] + part_specs + [
            pl.BlockSpec((None,) + w_out.shape[1:], lambda t: (layer, 0, 0))],
        out_specs=pl.BlockSpec((TM, d), lambda t: (t, 0)),
        compiler_params=_cparams(("arbitrary",)),
        name="out_projection",
    )(x, mods, *parts, w_out)


def _rope_tables(seq):
    freqs = ROPE_THETA ** (-jnp.arange(ROPE_FREQS, dtype=F32) / ROPE_FREQS)
    row_ang = jnp.arange(seq // GRID_W, dtype=F32)[:, None] * freqs
    col_ang = jnp.arange(GRID_W, dtype=F32)[:, None] * freqs
    lanes = np.arange(ATT_HD)
    is_row = jnp.asarray(np.tile(lanes < 2 * ROPE_FREQS, ATT_QH)[None, :], F32)
    first_half = jnp.asarray(np.tile((lanes // ROPE_FREQS) % 2 == 0, ATT_QH)[None, :], F32)
    spread = lambda a: jnp.tile(a, (1, ATT_W // ROPE_FREQS))

    def tables(ang, mask, ident_lines):
        cos, sin = spread(jnp.cos(ang)) * mask, spread(jnp.sin(ang)) * mask
        ident = jnp.zeros((ident_lines, ATT_W), F32)
        return (jnp.concatenate([cos, ident + mask], axis=0),
                jnp.concatenate([-sin * first_half, ident], axis=0),
                jnp.concatenate([sin * (1.0 - first_half), ident], axis=0))

    rows = tables(row_ang, is_row, TM // GRID_W)
    cols = tuple(t.reshape(2, GRID_W, ATT_W) for t in tables(col_ang, 1.0 - is_row, GRID_W))
    return rows, cols


def _block_ones(width, group):
    return jnp.asarray(np.kron(np.eye(width // group), np.ones((group, group))), BF16)


def kernel(x, c, ctx, c_ctx, w_mod, b_mod, norm_g, ffn_wg, ffn_wu, ffn_wd, w_in, w_out,
           pool_w, pool_scale, gla_wa, gla_ba, gla_norm, att_qnorm, att_knorm, fnet_w, final_norm):
    n_batch, seq, d = x.shape
    n_ctx = ctx.shape[1]
    assert d == D_MODEL and seq % TM == 0 and (n_batch * n_ctx) % TM == 0 and n_ctx % TP == 0
    assert seq % (8 * FFT_N1) == 0 and n_ctx <= TM and TM % n_ctx == 0 and n_batch + 1 <= 8
    lat_rows = n_batch * seq
    lat_tiles = lat_rows // TM
    all_tiles = lat_tiles + (n_batch * n_ctx) // TM
    tiles_per_batch = seq // TM
    line_map = lambda t: (jnp.where(t < lat_tiles, t % tiles_per_batch, tiles_per_batch), 0)
    col_map = lambda t: (jnp.where(t < lat_tiles, 0, 1), 0, 0)

    cvec = jnp.concatenate([c, c_ctx[None, :], jnp.zeros((8 - n_batch - 1, d), F32)], axis=0)
    mods = _modulation(cvec, w_mod, b_mod).reshape(DEPTH, 8, N_MOD, d)

    rope_lines, rope_cols = _rope_tables(seq)
    e256 = _block_ones(ATT_W, ATT_HD)
    gla_ex = _block_ones(GLA_W, GLA_DV)[::2, :]
    gla_mk = jnp.asarray(np.kron(np.eye(GLA_HEADS), np.ones((GLA_DV, GLA_DK))), F32)

    wg, wu, wd = ffn_wg.astype(BF16), ffn_wu.astype(BF16), ffn_wd.astype(BF16)
    wi = jnp.concatenate([w_in[..., :768], w_in[..., 800:D_IN], w_in[..., 768:800],
                          jnp.zeros((DEPTH, d, D_IN_PAD - D_IN), F32)], axis=-1).astype(BF16)
    wa_blk = jnp.zeros((DEPTH, 128, 2 * GLA_KW), F32)
    wa_blk = wa_blk.at[:, :GLA_RANK, :GLA_KW].set(gla_wa[:, 0]).at[:, GLA_RANK:2 * GLA_RANK, GLA_KW:].set(gla_wa[:, 1])
    wa_blk = wa_blk.astype(BF16)
    ba_blk = gla_ba.reshape(DEPTH, 1, 2 * GLA_KW)
    qg = jnp.tile(att_qnorm, (1, ATT_QH))[:, None, :]
    kg = jnp.tile(att_knorm, (1, ATT_KVH))[:, None, :]
    pool_blk = jnp.zeros((DEPTH, POOL_W, POOL_W), F32)
    for gi in range(len(POOL_WINDOWS)):
        sl = slice(gi * POOL_GW, (gi + 1) * POOL_GW)
        pool_blk = pool_blk.at[:, sl, sl].set(pool_w[:, gi])
    pool_blk = pool_blk.astype(BF16)
    pool_s = pool_scale[:, None, :]
    wf = fnet_w.astype(BF16)
    wo = w_out.astype(BF16)
    norm4 = norm_g[:, :, None, :]
    n2 = seq // FFT_N1

    xs = x.reshape(lat_rows, d)
    ctx_rows = ctx.reshape(n_batch * n_ctx, d)
    for i in range(DEPTH):
        ctx_out = i < DEPTH - 1
        mod_map = _mod_spec(i, lat_tiles, tiles_per_batch, n_batch, d)
        gn = jnp.tile(gla_norm[i], GLA_HEADS)[None, :]

        xs = _half_ffn(xs, mods, norm4, wg, wu, wd, final_norm, layer=i, half=0, n_tiles=all_tiles,
                       mod_map=mod_map, ctx_rows=ctx_rows if i == 0 else None)
        (u_pool, gq, gk, gv, og, gf, gb, aq, ak, av, uf) = _in_projection(
            xs, mods, norm4, wi, wa_blk, ba_blk, e256, qg, kg, rope_lines, rope_cols,
            layer=i, n_tiles=all_tiles, mod_map=mod_map, line_map=line_map, col_map=col_map)

        y_pool = _pool_mix(u_pool, pool_blk, pool_s, layer=i, seq=seq, n_seq=n_batch, first_block=0)
        y_gla = _gla_mix(gq, gk, gv, gf, gb, og, gla_ex, gla_mk, gn, e256, n_batch=n_batch, seq=seq, ctx=n_ctx)
        y_att = _attention(aq, ak, av, n_batch=n_batch, seq=seq, ctx=n_ctx, ctx_queries=False)
        y_fnet = _fnet_latent(uf.reshape(uf.shape[0] // n2, n2, FNET_W), wf, layer=i, n_batch=n_batch, seq=seq)
        n_tiles, ctx_parts = lat_tiles, None
        if ctx_out:
            first_ctx = lat_rows // n_ctx
            ctx_parts = (
                _pool_mix(u_pool, pool_blk, pool_s, layer=i, seq=n_ctx, n_seq=n_batch, first_block=first_ctx),
                _attention(aq, ak, av, n_batch=n_batch, seq=seq, ctx=n_ctx, ctx_queries=True),
                _fnet_context(uf, wf, layer=i, n_batch=n_batch, ctx=n_ctx, first_block=first_ctx),
            )
            n_tiles = all_tiles
        xs = _out_projection(xs, mods, y_pool, y_gla, y_att, y_fnet, wo, layer=i, n_tiles=n_tiles,
                             mod_map=mod_map, ctx_parts=ctx_parts)
        xs = _half_ffn(xs, mods, norm4, wg, wu, wd, final_norm, layer=i, half=1, n_tiles=n_tiles,
                       mod_map=mod_map, final=not ctx_out)
    return xs.reshape(n_batch, seq, d)
```

```python
import functools
import math

import jax
import jax.numpy as jnp
import numpy as np
from jax import lax
from jax.experimental import pallas as pl
from jax.experimental.pallas import tpu as pltpu

F32 = jnp.float32
BF16 = jnp.bfloat16

D_MODEL = 1024
DEPTH = 2
GRID_W = 64
EPS = 1e-6
N_MOD = 9
D_FF = 2816

POOL_W = 256
POOL_WINDOWS = (2, 4, 8, 16)
POOL_GW = 64
POOL_HALO = 8

GLA_HEADS = 4
GLA_W = 256
GLA_DV = 64
GLA_DK = 32
GLA_RANK = 16
GLA_TAU = 16.0
GLA_KW = GLA_HEADS * GLA_DK
GLA_SUB = 16
GLA_CHUNK = 64
GLA_FAST_RANGE = 150.0

ATT_W = 256
ATT_HD = 64
ATT_QH = 4
ATT_KVH = 2
ATT_KW = ATT_KVH * ATT_HD
ATT_VROWS = ATT_HD + 16
ROPE_FREQS = 16
ROPE_THETA = 10000.0

FNET_W = 256
FNET_HEADS = 4
FNET_HD = 64
FFT_N1 = 64

D_IN = 1824
D_IN_PAD = 1920

O_POOL, O_GQ, O_GK, O_GV, O_OG, O_AQ, O_AK, O_AV, O_FN, O_R = 0, 256, 384, 512, 768, 1024, 1280, 1408, 1536, 1792

TM = 512
TQ = 256
TP = 256
MXU_TILE = 256
FF_CHUNKS = ((0, 6 * MXU_TILE), (6 * MXU_TILE, D_FF))
NEG_BIG = -1e30

VMEM_LIMIT = 56 * 1024 * 1024


def _cparams(sem):
    return pltpu.CompilerParams(dimension_semantics=sem, vmem_limit_bytes=VMEM_LIMIT)


def _dot(a, b):
    return jnp.dot(a, b, preferred_element_type=F32)


def _split(a):
    hi = a.astype(BF16)
    lo = (a - hi.astype(F32)).astype(BF16)
    return hi, lo


def _dot3(a_hi, a_lo, b_hi, b_lo):
    return _dot(a_hi, b_hi) + (_dot(a_hi, b_lo) + _dot(a_lo, b_hi))


def _rms(x):
    return x * lax.rsqrt(jnp.mean(x * x, axis=-1, keepdims=True) + EPS)


def _silu(x):
    return x * jax.nn.sigmoid(x)


def _group_rms(x, e, width):
    s_hi, s_lo = _split(x * x)
    ss = _dot(s_hi, e) + _dot(s_lo, e)
    return x * lax.rsqrt(ss * (1.0 / width) + EPS)


def _mod_kernel(c_ref, w_ref, b_ref, o_ref):
    s = _silu(c_ref[...]).astype(BF16)
    o_ref[...] = _dot(s, w_ref[...].astype(BF16)) + b_ref[...]


def _modulation(cvec, w_mod, b_mod):
    depth, d, nd = w_mod.shape
    tn = nd // 8
    return pl.pallas_call(
        _mod_kernel,
        out_shape=jax.ShapeDtypeStruct((depth, 8, nd), F32),
        grid=(depth, nd // tn),
        in_specs=[
            pl.BlockSpec((8, d), lambda l, j: (0, 0)),
            pl.BlockSpec((None, d, tn), lambda l, j: (l, 0, j)),
            pl.BlockSpec((None, 1, tn), lambda l, j: (l, 0, j)),
        ],
        out_specs=pl.BlockSpec((None, 8, tn), lambda l, j: (l, 0, j)),
        compiler_params=_cparams(("arbitrary", "arbitrary")),
        name="modulation",
    )(cvec, w_mod, b_mod.reshape(depth, 1, nd))


def _half_ffn_rows(x, m_ref, g_ref, wg_ref, wu_ref, wd_ref, mod_base):
    shift = m_ref[mod_base:mod_base + 1, :]
    scale = m_ref[mod_base + 1:mod_base + 2, :]
    gate = m_ref[mod_base + 2:mod_base + 3, :]
    h = ((_rms(x) * g_ref[...]) * (1.0 + scale) + shift).astype(BF16)
    y = jnp.zeros(x.shape, F32)
    for lo, hi in FF_CHUNKS:
        sl = slice(lo, hi)
        a = _dot(h, wg_ref[:, sl])
        u = _dot(h, wu_ref[:, sl])
        y = y + _dot((_silu(a) * u).astype(BF16), wd_ref[sl, :])
    return x + (0.5 * gate) * y


def _ffn_kernel(*refs, n_lat_tiles):
    if n_lat_tiles is None:
        x_ref, m_ref, g_ref, wg_ref, wu_ref, wd_ref, o_ref = refs
        x = x_ref[...]
    else:
        x_ref, c_ref, m_ref, g_ref, wg_ref, wu_ref, wd_ref, o_ref = refs
        x = jnp.where(pl.program_id(0) < n_lat_tiles, x_ref[...], c_ref[...])
    o_ref[...] = _half_ffn_rows(x, m_ref, g_ref, wg_ref, wu_ref, wd_ref, 0)


def _mix_ffn_kernel(*refs, n_lat_tiles, final):
    if n_lat_tiles is None:
        (x_ref, m_ref, yp_ref, yg_ref, ya_ref, yf_ref, wo_ref,
         g_ref, wg_ref, wu_ref, wd_ref, fg_ref, o_ref) = refs
        yp, ya, yf = yp_ref[...], ya_ref[...], yf_ref[...]
    else:
        (x_ref, m_ref, yp_ref, yg_ref, ya_ref, yf_ref, cp_ref, ca_ref, cf_ref, wo_ref,
         g_ref, wg_ref, wu_ref, wd_ref, fg_ref, o_ref) = refs
        lat = pl.program_id(0) < n_lat_tiles
        yp = jnp.where(lat, yp_ref[...], cp_ref[...])
        ya = jnp.where(lat, ya_ref[...], ca_ref[...])
        yf = jnp.where(lat, yf_ref[...], cf_ref[...])
    acc = _dot(yp.astype(BF16), wo_ref[0:256, :])
    acc = acc + _dot(yg_ref[...].astype(BF16), wo_ref[256:512, :])
    acc = acc + _dot(ya.astype(BF16), wo_ref[512:768, :])
    acc = acc + _dot(yf.astype(BF16), wo_ref[768:1024, :])
    x = x_ref[...] + m_ref[5:6, :] * acc
    out = _half_ffn_rows(x, m_ref, g_ref, wg_ref, wu_ref, wd_ref, 6)
    if final:
        out = _rms(out) * fg_ref[...]
    o_ref[...] = out


def _mod_spec(layer, n_lat_tiles, tiles_per_batch, n_batch, d):
    def index_map(t):
        return (layer, jnp.where(t < n_lat_tiles, t // tiles_per_batch, n_batch), 0, 0)
    return pl.BlockSpec((None, None, N_MOD, d), index_map)


def _resident(arr, lead):
    block = (None,) * len(lead) + arr.shape[len(lead):]
    index = tuple(lead) + (0,) * (arr.ndim - len(lead))
    return pl.BlockSpec(block, lambda t: index, pipeline_mode=pl.Buffered(1))


def _ffn_weight_specs(g, wg, wu, wd, layer, half):
    d = g.shape[-1]
    lead = (layer, half)
    return [pl.BlockSpec((None, None, 1, d), lambda t: (layer, 2 * half, 0, 0)),
            _resident(wg, lead), _resident(wu, lead), _resident(wd, lead)]


def _half_ffn(x, mods, g, wg, wu, wd, *, layer, n_tiles, mod_map, ctx_rows=None):
    d = x.shape[1]
    n_lat = None if ctx_rows is None else x.shape[0] // TM
    if ctx_rows is None:
        rows, row_specs = [x], [pl.BlockSpec((TM, d), lambda t: (t, 0))]
    else:
        rows = [x, ctx_rows]
        row_specs = [pl.BlockSpec((TM, d), lambda t: (jnp.minimum(t, n_lat - 1), 0)),
                     pl.BlockSpec((TM, d), lambda t: (jnp.maximum(t - n_lat, 0), 0))]
    return pl.pallas_call(
        functools.partial(_ffn_kernel, n_lat_tiles=n_lat),
        out_shape=jax.ShapeDtypeStruct((n_tiles * TM, d), F32),
        grid=(n_tiles,),
        in_specs=row_specs + [mod_map] + _ffn_weight_specs(g, wg, wu, wd, layer, 0),
        out_specs=pl.BlockSpec((TM, d), lambda t: (t, 0)),
        compiler_params=_cparams(("arbitrary",)),
        name="half_ffn",
    )(*rows, mods, g, wg, wu, wd)


def _mix_ffn(x, mods, y_pool, y_gla, y_att, y_fnet, w_out, g, wg, wu, wd, fg,
             *, layer, n_tiles, mod_map, final, ctx_parts=None):
    d = x.shape[1]
    part = pl.BlockSpec((TM, 256), lambda t: (t, 0))
    n_lat = None
    parts, part_specs = [y_pool, y_gla, y_att, y_fnet], [part, part, part, part]
    if ctx_parts is not None:
        n_lat = y_pool.shape[0] // TM
        lat_part = pl.BlockSpec((TM, 256), lambda t: (jnp.minimum(t, n_lat - 1), 0))
        ctx_part = pl.BlockSpec((TM, 256), lambda t: (jnp.maximum(t - n_lat, 0), 0))
        parts += list(ctx_parts)
        part_specs = [lat_part, part, lat_part, lat_part, ctx_part, ctx_part, ctx_part]
    return pl.pallas_call(
        functools.partial(_mix_ffn_kernel, n_lat_tiles=n_lat, final=final),
        out_shape=jax.ShapeDtypeStruct((n_tiles * TM, d), F32),
        grid=(n_tiles,),
        in_specs=[pl.BlockSpec((TM, d), lambda t: (t, 0)), mod_map] + part_specs
                 + [_resident(w_out, (layer,))] + _ffn_weight_specs(g, wg, wu, wd, layer, 1)
                 + [pl.BlockSpec((1, d), lambda t: (0, 0))],
        out_specs=pl.BlockSpec((TM, d), lambda t: (t, 0)),
        compiler_params=_cparams(("arbitrary",)),
        name="mix_ffn",
    )(x, mods, *parts, w_out, g, wg, wu, wd, fg.reshape(1, d))


def _rope(x, c, sa, sb):
    w = x.shape[1]
    return x * c + pltpu.roll(x, w - ROPE_FREQS, 1) * sa + pltpu.roll(x, ROPE_FREQS, 1) * sb


def _tile_rope_table(line_ref, col_ref):
    lines, col = line_ref[...], col_ref[...]
    return jnp.concatenate([col + lines[r:r + 1, :] for r in range(lines.shape[0])], axis=0)


def _inproj_kernel(x_ref, m_ref, g_ref, w_ref, wa_ref, ba_ref, e_ref, qg_ref, kg_ref,
                   lc_ref, lsa_ref, lsb_ref, cc_ref, csa_ref, csb_ref,
                   up_ref, gq_ref, gk_ref, gv_ref, og_ref, gf_ref, gb_ref,
                   aq_ref, ak_ref, av_ref, uf_ref):
    x = x_ref[...]
    shift = m_ref[3:4, :]
    scale = m_ref[4:5, :]
    h = ((_rms(x) * g_ref[...]) * (1.0 + scale) + shift).astype(BF16)
    p = _dot(h, w_ref[...])

    up_ref[...] = p[:, O_POOL:O_POOL + POOL_W]
    uf_ref[...] = p[:, O_FN:O_FN + FNET_W]

    gq_ref[...] = p[:, O_GQ:O_GQ + GLA_KW] * (GLA_DK ** -0.5)
    gk_ref[...] = p[:, O_GK:O_GK + GLA_KW]
    gv_ref[...] = p[:, O_GV:O_GV + GLA_W]
    og_ref[...] = p[:, O_OG:O_OG + GLA_W]
    z = _dot(p[:, O_R:O_R + 128].astype(BF16), wa_ref[...]) + ba_ref[...]
    logsig = jnp.minimum(z, 0.0) - jnp.log(1.0 + jnp.exp(-jnp.abs(z)))
    gdec = logsig * (1.0 / GLA_TAU)
    gf_ref[...] = gdec[:, :GLA_KW]
    gb_ref[...] = gdec[:, GLA_KW:]

    e = e_ref[...]
    rc = _tile_rope_table(lc_ref, cc_ref)
    rsa = _tile_rope_table(lsa_ref, csa_ref)
    rsb = _tile_rope_table(lsb_ref, csb_ref)
    q = _group_rms(p[:, O_AQ:O_AQ + ATT_W], e, ATT_HD) * qg_ref[...]
    q = _rope(q, rc, rsa, rsb) * (ATT_HD ** -0.5 * math.log2(math.e))
    aq_ref[...] = q.T.astype(BF16)
    k = _group_rms(p[:, O_AK:O_AK + ATT_KW], e[:ATT_KW, :ATT_KW], ATT_HD) * kg_ref[...]
    k = _rope(k, rc[:, :ATT_KW], rsa[:, :ATT_KW], rsb[:, :ATT_KW])
    ak_ref[...] = k.astype(BF16)
    v_t = p[:, O_AV:O_AV + ATT_KW].T
    ones = jnp.ones((ATT_VROWS - ATT_HD, v_t.shape[1]), F32)
    for kvh in range(ATT_KVH):
        av_ref[kvh] = jnp.concatenate([v_t[kvh * ATT_HD:(kvh + 1) * ATT_HD, :], ones], axis=0).astype(BF16)


def _in_projection(x, mods, g, w_in, wa_blk, ba_blk, e256, qg, kg, rope_lines, rope_cols,
                   *, layer, n_tiles, mod_map, line_map, col_map):
    t_rows, d = x.shape
    row = lambda w: pl.BlockSpec((TM, w), lambda t: (t, 0))
    const = lambda shape: pl.BlockSpec(shape, lambda t: (0,) * len(shape))
    per_layer = lambda arr: pl.BlockSpec((None,) + arr.shape[1:], lambda t: (layer,) + (0,) * (arr.ndim - 1))
    out_shapes = (
        jax.ShapeDtypeStruct((t_rows, POOL_W), F32),
        jax.ShapeDtypeStruct((t_rows, GLA_KW), F32),
        jax.ShapeDtypeStruct((t_rows, GLA_KW), F32),
        jax.ShapeDtypeStruct((t_rows, GLA_W), F32),
        jax.ShapeDtypeStruct((t_rows, GLA_W), F32),
        jax.ShapeDtypeStruct((t_rows, GLA_KW), F32),
        jax.ShapeDtypeStruct((t_rows, GLA_KW), F32),
        jax.ShapeDtypeStruct((n_tiles, ATT_W, TM), BF16),
        jax.ShapeDtypeStruct((t_rows, ATT_KW), BF16),
        jax.ShapeDtypeStruct((n_tiles, ATT_KVH, ATT_VROWS, TM), BF16),
        jax.ShapeDtypeStruct((t_rows, FNET_W), F32),
    )
    out_specs = (
        row(POOL_W), row(GLA_KW), row(GLA_KW), row(GLA_W), row(GLA_W), row(GLA_KW), row(GLA_KW),
        pl.BlockSpec((None, ATT_W, TM), lambda t: (t, 0, 0)),
        row(ATT_KW),
        pl.BlockSpec((None, ATT_KVH, ATT_VROWS, TM), lambda t: (t, 0, 0, 0)),
        row(FNET_W),
    )
    return pl.pallas_call(
        _inproj_kernel,
        out_shape=out_shapes,
        grid=(n_tiles,),
        in_specs=[
            pl.BlockSpec((TM, d), lambda t: (t, 0)),
            mod_map,
            pl.BlockSpec((None, None, 1, d), lambda t: (layer, 1, 0, 0)),
            per_layer(w_in),
            per_layer(wa_blk),
            per_layer(ba_blk),
            const(e256.shape),
            per_layer(qg),
            per_layer(kg),
        ] + [pl.BlockSpec((TM // GRID_W, ATT_W), line_map)] * 3
          + [pl.BlockSpec((None, GRID_W, ATT_W), col_map)] * 3,
        out_specs=out_specs,
        compiler_params=_cparams(("arbitrary",)),
        name="in_projection",
    )(x, mods, g, w_in, wa_blk, ba_blk, e256, qg, kg, *rope_lines, *rope_cols)


def _pool_kernel(u_ref, w_ref, s_ref, o_ref, pad_ref, *, seq, chunk):
    halo = POOL_HALO
    pad_ref[0:halo, :] = jnp.zeros((halo, POOL_W), F32)
    pad_ref[halo + seq:halo + seq + halo, :] = jnp.zeros((halo, POOL_W), F32)
    pad_ref[halo:halo + seq, :] = u_ref[...]
    rows = chunk + 2 * halo
    lane = lax.broadcasted_iota(jnp.int32, (chunk, POOL_W), 1)
    trow = lax.broadcasted_iota(jnp.int32, (chunk, POOL_W), 0)

    def body(ci, carry):
        c0 = pl.multiple_of(ci * chunk, chunk)
        xp = pad_ref[pl.ds(c0, rows), :]
        u = xp[halo:halo + chunk, :]
        t = trow + c0
        acc = xp
        m = None
        for gi, w in enumerate(POOL_WINDOWS):
            acc = acc + pltpu.roll(acc, w // 2, 0)
            lead = w // 2 - 1
            win = acc if lead == 0 else pltpu.roll(acc, rows - lead, 0)
            win = win[halo:halo + chunk, :]
            cnt = (jnp.minimum(t + w // 2, seq) - jnp.maximum(t - w // 2, 0)).astype(F32)
            mg = win / cnt - u
            m = mg if m is None else jnp.where(lane >= gi * POOL_GW, mg, m)
        y = _dot(m.astype(BF16), w_ref[...]) * s_ref[...]
        o_ref[pl.ds(c0, chunk), :] = y
        return carry

    lax.fori_loop(0, seq // chunk, body, 0)


def _pool_mix(u, w_blk, s_pool, *, layer, seq, n_seq, first_block):
    chunk = min(256, seq)
    return pl.pallas_call(
        functools.partial(_pool_kernel, seq=seq, chunk=chunk),
        out_shape=jax.ShapeDtypeStruct((n_seq * seq, POOL_W), F32),
        grid=(n_seq,),
        in_specs=[
            pl.BlockSpec((seq, POOL_W), lambda b: (first_block + b, 0)),
            pl.BlockSpec((None, POOL_W, POOL_W), lambda b: (layer, 0, 0)),
            pl.BlockSpec((None, 1, POOL_W), lambda b: (layer, 0, 0)),
        ],
        out_specs=pl.BlockSpec((seq, POOL_W), lambda b: (b, 0)),
        scratch_shapes=[pltpu.VMEM((seq + 2 * POOL_HALO, POOL_W), F32)],
        compiler_params=_cparams(("arbitrary",)),
        name="pool_mix",
    )(u, w_blk, s_pool)


def _gla_kernel(*refs, reverse, finalize):
    if finalize:
        (q_ref, k_ref, v_ref, g_ref, ex_ref, mk_ref, bm_ref, of_ref, og_ref, gn_ref, e_ref,
         o_ref, st_ref, ob_ref) = refs
    else:
        q_ref, k_ref, v_ref, g_ref, ex_ref, mk_ref, bm_ref, o_ref, st_ref = refs
        ob_ref = o_ref

    @pl.when(pl.program_id(1) == 0)
    def _():
        st_ref[...] = jnp.zeros(st_ref.shape, F32)

    g = g_ref[...]
    trow = lax.broadcasted_iota(jnp.int32, (TP, GLA_KW), 0) % GLA_CHUNK
    pre, suf = g, g
    s = 1
    while s < GLA_CHUNK:
        pre = pre + jnp.where(trow >= s, pltpu.roll(pre, s, 0), 0.0)
        suf = suf + jnp.where(trow < GLA_CHUNK - s, pltpu.roll(suf, TP - s, 0), 0.0)
        s *= 2
    tot = pre + suf - g
    in_range = jnp.max(jnp.abs(tot)) <= GLA_FAST_RANGE

    @pl.when(in_range)
    def _():
        _gla_fast_tile(q_ref, k_ref, v_ref, mk_ref, bm_ref, st_ref, ob_ref,
                       suf if reverse else pre, tot, reverse)

    @pl.when(jnp.logical_not(in_range))
    def _():
        _gla_exact_tile(q_ref, k_ref, v_ref, g_ref, ex_ref, mk_ref, st_ref, ob_ref, reverse)

    if finalize:
        both = ob_ref[...] + of_ref[...]
        y = _group_rms(both, e_ref[...], GLA_DV) * gn_ref[...]
        o_ref[...] = y * _silu(og_ref[...])


def _gla_fast_tile(q_ref, k_ref, v_ref, mk_ref, bm_ref, st_ref, ob_ref, b, tot, reverse):
    ch = GLA_CHUNK
    half = 0.5 * tot
    q, k = q_ref[...], k_ref[...]
    q_in = (q * jnp.exp(b - half)).astype(BF16)
    k_in = (k * jnp.exp(half - b)).astype(BF16)
    q_st = (q * jnp.exp(b)).astype(BF16)
    k_st = (k * jnp.exp(tot - b)).astype(BF16)
    dec = jnp.exp(tot)
    mk = mk_ref[...]
    mk16 = mk.astype(BF16)
    bm16 = bm_ref[...]
    irow = lax.broadcasted_iota(jnp.int32, (ch, GLA_HEADS * ch), 0)
    jcol = lax.broadcasted_iota(jnp.int32, (ch, GLA_HEADS * ch), 1) % ch
    seen = (jcol >= irow) if reverse else (jcol <= irow)
    nt = (((1,), (1,)), ((), ()))
    st = st_ref[...]
    n_chunks = TP // ch
    for ci in range(n_chunks):
        c = (n_chunks - 1 - ci) if reverse else ci
        rows = slice(c * ch, (c + 1) * ch)
        kb = jnp.concatenate([k_in[rows]] * GLA_HEADS, axis=0) * mk16
        a = lax.dot_general(q_in[rows], kb, nt, preferred_element_type=F32)
        a = jnp.where(seen, a, 0.0).astype(BF16)
        v = v_ref[rows, :]
        vb = jnp.concatenate([v.astype(BF16)] * GLA_HEADS, axis=0) * bm16
        o = _dot(a, vb) + lax.dot_general(q_st[rows], st.astype(BF16), nt, preferred_element_type=F32)
        kv = lax.dot_general(v.astype(BF16), k_st[rows], (((0,), (0,)), ((), ())),
                             preferred_element_type=F32)
        st = st * dec[c * ch:c * ch + 1, :] + kv * mk
        ob_ref[rows, :] = o
    st_ref[...] = st


def _gla_exact_tile(q_ref, k_ref, v_ref, g_ref, ex_ref, mk_ref, st_ref, ob_ref, reverse):
    sub = GLA_SUB
    n_chunks = TP // sub
    row = lax.broadcasted_iota(jnp.int32, (sub, GLA_KW), 0)
    ex = ex_ref[...]
    mk = mk_ref[...]

    def body(ci, carry):
        c = (n_chunks - 1 - ci) if reverse else ci
        r0 = pl.multiple_of(c * sub, sub)
        q = q_ref[pl.ds(r0, sub), :]
        k = k_ref[pl.ds(r0, sub), :]
        v = v_ref[pl.ds(r0, sub), :]
        b = g_ref[pl.ds(r0, sub), :]
        s = 1
        while s < sub:
            if reverse:
                b = b + jnp.where(row < sub - s, pltpu.roll(b, sub - s, 0), 0.0)
            else:
                b = b + jnp.where(row >= s, pltpu.roll(b, s, 0), 0.0)
            s *= 2
        edge = b[0:1, :] if reverse else b[sub - 1:sub, :]
        st = st_ref[...]
        qd = (q * jnp.exp(b)).astype(BF16)
        o = lax.dot_general(qd, st.astype(BF16), (((1,), (1,)), ((), ())),
                            preferred_element_type=F32)
        parts = []
        for j in range(sub):
            seen = (row <= j) if reverse else (row >= j)
            dec = jnp.exp(jnp.where(seen, b - b[j:j + 1, :], NEG_BIG))
            parts.append((dec * q * k[j:j + 1, :]).astype(BF16))
        r = _dot(jnp.concatenate(parts, axis=0), ex)
        for j in range(sub):
            o = o + r[j * sub:(j + 1) * sub, :] * v[j:j + 1, :]
        kd = (k * jnp.exp(edge - b)).astype(BF16)
        kv = lax.dot_general(v.astype(BF16), kd, (((0,), (0,)), ((), ())),
                             preferred_element_type=F32)
        st_ref[...] = st * jnp.exp(edge) + kv * mk
        ob_ref[pl.ds(r0, sub), :] = o
        return carry

    lax.fori_loop(0, n_chunks, body, 0, unroll=2)


def _gla_tile_map(n_batch, n_lat, n_ctx, reverse):
    def index_map(b, s):
        is_ctx = s < n_ctx
        sc = jnp.where(is_ctx, s, 0)
        sl = jnp.where(is_ctx, 0, s - n_ctx)
        if reverse:
            sc = n_ctx - 1 - sc
            sl = n_lat - 1 - sl
        return (jnp.where(is_ctx, n_batch * n_lat + b * n_ctx + sc, b * n_lat + sl), 0)
    return index_map


def _gla_mix(gq, gk, gv, gf, gb, og, ex, mk, gn, e256, *, n_batch, seq, ctx):
    t_rows = gq.shape[0]
    n_lat, n_ctx = seq // TP, ctx // TP
    grid = (n_batch, n_lat + n_ctx)
    const = lambda shape: pl.BlockSpec(shape, lambda b, s: (0,) * len(shape))

    def run(reverse, finalize, gate, extra):
        tmap = _gla_tile_map(n_batch, n_lat, n_ctx, reverse)
        rows = lambda w: pl.BlockSpec((TP, w), tmap)
        in_specs = [rows(GLA_KW), rows(GLA_KW), rows(GLA_W), rows(GLA_KW), const(ex.shape), const(mk.shape),
                    const(e256.shape)]
        args = [gq, gk, gv, gate, ex, mk, e256]
        scratch = [pltpu.VMEM((GLA_W, GLA_KW), F32)]
        if finalize:
            in_specs += [rows(GLA_W), rows(GLA_W), const((1, GLA_W)), const(e256.shape)]
            args += extra
            scratch.append(pltpu.VMEM((TP, GLA_W), F32))
        return pl.pallas_call(
            functools.partial(_gla_kernel, reverse=reverse, finalize=finalize),
            out_shape=jax.ShapeDtypeStruct((t_rows, GLA_W), F32),
            grid=grid,
            in_specs=in_specs,
            out_specs=rows(GLA_W),
            scratch_shapes=scratch,
            compiler_params=_cparams(("arbitrary", "arbitrary")),
            name="gla_bwd" if reverse else "gla_fwd",
        )(*args)

    o_f = run(False, False, gf, None)
    return run(True, True, gb, [o_f, og, gn, e256])


def _att_kernel(*refs, n_main, has_ctx):
    if has_ctx:
        q_ref, km_ref, vm_ref, kc_ref, vc_ref, o_ref, s0_ref, s1_ref = refs
    else:
        q_ref, km_ref, vm_ref, o_ref, s0_ref, s1_ref = refs
    q_t = q_ref[...]
    tq = q_t.shape[1]
    tk = vm_ref.shape[3]
    zero = jnp.zeros((ATT_HD, tq), BF16)
    groups = []
    for g in range(ATT_KVH):
        halves = []
        for h in (2 * g, 2 * g + 1):
            qh = q_t[h * ATT_HD:(h + 1) * ATT_HD, :]
            halves.append(jnp.concatenate([qh, zero] if g == 0 else [zero, qh], axis=0))
        groups.append(jnp.concatenate(halves, axis=1))

    def scores(kblk, buf):
        for g in range(ATT_KVH):
            buf[g, 0:kblk.shape[0], :] = _dot(kblk, groups[g])

    def consume(buf, rows, v_of, carry):
        out = []
        for g in range(ATT_KVH):
            m, l, acc = carry[g]
            s = buf[g, 0:rows, :]
            m_new = jnp.maximum(m, jnp.max(s, axis=0, keepdims=True))
            alpha = jnp.exp2(m - m_new)
            p = jnp.exp2(s - m_new).astype(BF16)
            pv = _dot(v_of(g), p)
            acc = alpha * acc + pv[:ATT_HD, :]
            l = alpha * l + pv[ATT_HD:ATT_HD + 1, :]
            out.append((m_new, l, acc))
        return tuple(out)

    def k_main(j):
        return km_ref[pl.ds(pl.multiple_of(j * tk, tk), tk), :]

    def v_main(j):
        return lambda g: vm_ref[j, g]

    def pair(j, carry, issue_next):
        scores(k_main(j + 1), s1_ref)
        carry = consume(s0_ref, tk, v_main(j), carry)
        issue_next(j + 2)
        return consume(s1_ref, tk, v_main(j + 1), carry)

    carry = tuple((jnp.full((1, 2 * tq), -jnp.inf, F32), jnp.zeros((1, 2 * tq), F32),
                   jnp.zeros((ATT_HD, 2 * tq), F32)) for _ in range(ATT_KVH))
    next_main = lambda j: scores(k_main(j), s0_ref)
    if has_ctx:
        after_main = lambda j: scores(kc_ref[...], s0_ref)
    else:
        after_main = lambda j: None

    scores(k_main(0), s0_ref)
    if n_main > 1:
        pairs_per_trip = 2 if n_main % 4 == 0 else 1

        def body(i, carry):
            for u in range(pairs_per_trip):
                carry = pair(2 * (pairs_per_trip * i + u), carry, next_main)
            return carry

        n_trips = n_main // (2 * pairs_per_trip)
        carry = lax.fori_loop(0, n_trips - 1, body, carry)
        for u in range(pairs_per_trip):
            last = u == pairs_per_trip - 1
            carry = pair(2 * (pairs_per_trip * (n_trips - 1) + u), carry, after_main if last else next_main)
    else:
        assert not has_ctx
        carry = consume(s0_ref, tk, v_main(0), carry)
    if has_ctx:
        carry = consume(s0_ref, kc_ref.shape[0], lambda g: vc_ref[g], carry)
    heads = []
    for g in range(ATT_KVH):
        _, l, acc = carry[g]
        og = acc / l
        heads += [og[:, :tq], og[:, tq:]]
    o_ref[...] = jnp.concatenate(heads, axis=0).T.astype(o_ref.dtype)


def _attention(aq, ak, av, *, n_batch, seq, ctx, ctx_queries):
    lat_tiles = seq // TM
    ctx_tile = n_batch * lat_tiles
    ctx_cols = lambda b, i: (ctx_tile + (b * ctx) // TM, 0, ((b * ctx) % TM) // ctx)
    ctx_vcols = lambda b, i: (ctx_tile + (b * ctx) // TM, 0, 0, ((b * ctx) % TM) // ctx)
    if not ctx_queries:
        sub = TM // TQ
        grid = (n_batch, seq // TQ)
        in_specs = [
            pl.BlockSpec((None, ATT_W, TQ), lambda b, i: (b * lat_tiles + i // sub, 0, i % sub)),
            pl.BlockSpec((seq, ATT_KW), lambda b, i: (b, 0)),
            pl.BlockSpec((lat_tiles, ATT_KVH, ATT_VROWS, TM), lambda b, i: (b, 0, 0, 0)),
            pl.BlockSpec((ctx, ATT_KW), lambda b, i: (n_batch * seq // ctx + b, 0)),
            pl.BlockSpec((None, ATT_KVH, ATT_VROWS, ctx), ctx_vcols),
        ]
        out_spec = pl.BlockSpec((TQ, ATT_W), lambda b, i: (b * (seq // TQ) + i, 0))
        assert lat_tiles % 2 == 0
        kern = functools.partial(_att_kernel, n_main=lat_tiles, has_ctx=True)
        out_rows = n_batch * seq
        args = (aq, ak, av, ak, av)
        s_shape = (ATT_KVH, TM, 2 * TQ)
    else:
        grid = (n_batch, 1)
        in_specs = [
            pl.BlockSpec((None, ATT_W, ctx), ctx_cols),
            pl.BlockSpec((ctx, ATT_KW), lambda b, i: (n_batch * seq // ctx + b, 0)),
            pl.BlockSpec((1, ATT_KVH, ATT_VROWS, ctx), ctx_vcols),
        ]
        out_spec = pl.BlockSpec((ctx, ATT_W), lambda b, i: (b, 0))
        kern = functools.partial(_att_kernel, n_main=1, has_ctx=False)
        out_rows = n_batch * ctx
        args = (aq, ak, av)
        s_shape = (ATT_KVH, ctx, 2 * ctx)
    return pl.pallas_call(
        kern,
        out_shape=jax.ShapeDtypeStruct((out_rows, ATT_W), BF16),
        grid=grid,
        in_specs=in_specs,
        out_specs=out_spec,
        scratch_shapes=[pltpu.VMEM(s_shape, F32), pltpu.VMEM(s_shape, F32)],
        compiler_params=_cparams(("arbitrary", "arbitrary")),
        name="attention_ctx" if ctx_queries else "attention",
    )(*args)


def _fft_a_kernel(x_ref, dh_ref, dl_ref, tr_ref, ti_ref, yr_ref, yi_ref):
    rb, w = x_ref.shape[1], x_ref.shape[2]
    x = jnp.concatenate([x_ref[:, r, :] for r in range(rb)], axis=1)
    x_hi, x_lo = _split(x)
    y = _dot3(dh_ref[...], dl_ref[...], x_hi, x_lo)
    tr, ti = tr_ref[...], ti_ref[...]
    for r in range(rb):
        yr, yi = y[:FFT_N1, r * w:(r + 1) * w], y[FFT_N1:, r * w:(r + 1) * w]
        c, s = tr[:, r:r + 1], ti[:, r:r + 1]
        yr_ref[:, r, :] = yr * c - yi * s
        yi_ref[:, r, :] = yr * s + yi * c


def _channel_mix(xr, xi, ch_ref, cl_ref, wf_ref, norm):
    xc = jnp.concatenate([xr, xi], axis=1)
    x_hi, x_lo = _split(xc)
    f = _dot3(x_hi, x_lo, ch_ref[...], cl_ref[...]) * norm
    return _dot(f.astype(BF16), wf_ref[...])


def _fft_c_kernel(yr_ref, yi_ref, mh_ref, ml_ref, ch_ref, cl_ref, wf_ref, o_ref, *, n2, norm):
    mh, ml = mh_ref[...], ml_ref[...]
    for j in range(8):
        blk = jnp.concatenate([yr_ref[j * n2:(j + 1) * n2, :], yi_ref[j * n2:(j + 1) * n2, :]], axis=0)
        b_hi, b_lo = _split(blk)
        x = _dot3(mh, ml, b_hi, b_lo)
        o_ref[:, j, :] = _channel_mix(x[:n2, :], x[n2:, :], ch_ref, cl_ref, wf_ref, norm)


def _fft_dense_kernel(x_ref, dh_ref, dl_ref, ch_ref, cl_ref, wf_ref, o_ref, *, n, norm):
    x_hi, x_lo = _split(x_ref[...])
    x = _dot3(dh_ref[...], dl_ref[...], x_hi, x_lo)
    o_ref[...] = _channel_mix(x[:n, :], x[n:, :], ch_ref, cl_ref, wf_ref, norm)


def _dft_parts(n):
    idx = np.arange(n)
    ang = 2.0 * np.pi * ((idx[:, None] * idx[None, :]) % n) / n
    return np.cos(ang), np.sin(ang)


def _np_split(a):
    a = jnp.asarray(a, F32)
    return _split(a)


def _fnet_consts(seq):
    n1, n2 = FFT_N1, seq // FFT_N1
    c1, s1 = _dft_parts(n1)
    da = np.concatenate([c1, -s1], axis=0)
    k1 = np.arange(n1)[:, None]
    m2 = np.arange(n2)[None, :]
    ang = 2.0 * np.pi * ((k1 * m2) % seq) / seq
    tw_r, tw_i = np.cos(ang), -np.sin(ang)
    c2, s2 = _dft_parts(n2)
    mc = np.block([[c2, s2], [-s2, c2]])
    return da, tw_r, tw_i, mc


def _channel_consts():
    cc, sc = _dft_parts(FNET_HD)
    eye = np.eye(FNET_HEADS)
    return np.concatenate([np.kron(eye, cc), np.kron(eye, sc)], axis=0)


def _fnet_latent(uf3, wf, *, layer, n_batch, seq):
    n1, n2 = FFT_N1, seq // FFT_N1
    da, tw_r, tw_i, mc = _fnet_consts(seq)
    dh, dl = _np_split(da)
    mh, ml = _np_split(mc)
    ch, cl = _np_split(_channel_consts())
    rb = min(n2, 32)
    blocked = lambda tw: jnp.asarray(tw.reshape(n1, n2 // rb, rb).transpose(1, 0, 2), F32)
    const2 = lambda shape: pl.BlockSpec(shape, lambda b, j: (0,) * len(shape))
    slab = pl.BlockSpec((n1, rb, FNET_W), lambda b, j: (b, j, 0))
    yr, yi = pl.pallas_call(
        _fft_a_kernel,
        out_shape=(jax.ShapeDtypeStruct((n_batch * n1, n2, FNET_W), F32),) * 2,
        grid=(n_batch, n2 // rb),
        in_specs=[
            slab,
            const2(dh.shape), const2(dl.shape),
            pl.BlockSpec((None, n1, rb), lambda b, j: (j, 0, 0)),
            pl.BlockSpec((None, n1, rb), lambda b, j: (j, 0, 0)),
        ],
        out_specs=(slab, slab),
        compiler_params=_cparams(("arbitrary", "arbitrary")),
        name="fft_stage_a",
    )(uf3, dh, dl, blocked(tw_r), blocked(tw_i))
    yr = yr.reshape(n_batch * n1 * n2, FNET_W)
    yi = yi.reshape(n_batch * n1 * n2, FNET_W)
    norm = 1.0 / math.sqrt(seq * FNET_HD)
    out = pl.pallas_call(
        functools.partial(_fft_c_kernel, n2=n2, norm=norm),
        out_shape=jax.ShapeDtypeStruct((n_batch, n2, n1, FNET_W), F32),
        grid=(n_batch, n1 // 8),
        in_specs=[
            pl.BlockSpec((8 * n2, FNET_W), lambda b, i: (b * (n1 // 8) + i, 0)),
            pl.BlockSpec((8 * n2, FNET_W), lambda b, i: (b * (n1 // 8) + i, 0)),
            const2(mh.shape), const2(ml.shape), const2(ch.shape), const2(cl.shape),
            pl.BlockSpec((None,) + wf.shape[1:], lambda b, i: (layer, 0, 0)),
        ],
        out_specs=pl.BlockSpec((None, n2, 8, FNET_W), lambda b, i: (b, 0, i, 0)),
        compiler_params=_cparams(("arbitrary", "arbitrary")),
        name="fft_stage_c",
    )(yr, yi, mh, ml, ch, cl, wf)
    return out.reshape(n_batch * seq, FNET_W)


def _fnet_context(uf, wf, *, layer, n_batch, ctx, first_block):
    c, s = _dft_parts(ctx)
    dh, dl = _np_split(np.concatenate([c, -s], axis=0))
    ch, cl = _np_split(_channel_consts())
    const = lambda shape: pl.BlockSpec(shape, lambda b: (0,) * len(shape))
    return pl.pallas_call(
        functools.partial(_fft_dense_kernel, n=ctx, norm=1.0 / math.sqrt(ctx * FNET_HD)),
        out_shape=jax.ShapeDtypeStruct((n_batch * ctx, FNET_W), F32),
        grid=(n_batch,),
        in_specs=[pl.BlockSpec((ctx, FNET_W), lambda b: (first_block + b, 0)),
                  const(dh.shape), const(dl.shape), const(ch.shape), const(cl.shape),
                  pl.BlockSpec((None,) + wf.shape[1:], lambda b: (layer, 0, 0))],
        out_specs=pl.BlockSpec((ctx, FNET_W), lambda b: (b, 0)),
        compiler_params=_cparams(("arbitrary",)),
        name="fft_context",
    )(uf, dh, dl, ch, cl, wf)


def _rope_tables(seq):
    freqs = ROPE_THETA ** (-jnp.arange(ROPE_FREQS, dtype=F32) / ROPE_FREQS)
    row_ang = jnp.arange(seq // GRID_W, dtype=F32)[:, None] * freqs
    col_ang = jnp.arange(GRID_W, dtype=F32)[:, None] * freqs
    lanes = np.arange(ATT_HD)
    is_row = jnp.asarray(np.tile(lanes < 2 * ROPE_FREQS, ATT_QH)[None, :], F32)
    first_half = jnp.asarray(np.tile((lanes // ROPE_FREQS) % 2 == 0, ATT_QH)[None, :], F32)
    spread = lambda a: jnp.tile(a, (1, ATT_W // ROPE_FREQS))

    def tables(ang, mask, ident_lines):
        cos, sin = spread(jnp.cos(ang)) * mask, spread(jnp.sin(ang)) * mask
        ident = jnp.zeros((ident_lines, ATT_W), F32)
        return (jnp.concatenate([cos, ident + mask], axis=0),
                jnp.concatenate([-sin * first_half, ident], axis=0),
                jnp.concatenate([sin * (1.0 - first_half), ident], axis=0))

    rows = tables(row_ang, is_row, TM // GRID_W)
    cols = tuple(t.reshape(2, GRID_W, ATT_W) for t in tables(col_ang, 1.0 - is_row, GRID_W))
    return rows, cols


def _block_ones(width, group):
    return jnp.asarray(np.kron(np.eye(width // group), np.ones((group, group))), BF16)


def kernel(x, c, ctx, c_ctx, w_mod, b_mod, norm_g, ffn_wg, ffn_wu, ffn_wd, w_in, w_out,
           pool_w, pool_scale, gla_wa, gla_ba, gla_norm, att_qnorm, att_knorm, fnet_w, final_norm):
    n_batch, seq, d = x.shape
    n_ctx = ctx.shape[1]
    assert d == D_MODEL and seq % TM == 0 and (n_batch * n_ctx) % TM == 0 and n_ctx % TP == 0
    assert seq % (8 * FFT_N1) == 0 and n_ctx <= TM and TM % n_ctx == 0 and n_batch + 1 <= 8
    lat_rows = n_batch * seq
    lat_tiles = lat_rows // TM
    all_tiles = lat_tiles + (n_batch * n_ctx) // TM
    tiles_per_batch = seq // TM
    line_map = lambda t: (jnp.where(t < lat_tiles, t % tiles_per_batch, tiles_per_batch), 0)
    col_map = lambda t: (jnp.where(t < lat_tiles, 0, 1), 0, 0)

    cvec = jnp.concatenate([c, c_ctx[None, :], jnp.zeros((8 - n_batch - 1, d), F32)], axis=0)
    mods = _modulation(cvec, w_mod, b_mod).reshape(DEPTH, 8, N_MOD, d)

    rope_lines, rope_cols = _rope_tables(seq)
    e256 = _block_ones(ATT_W, ATT_HD)
    gla_ex = _block_ones(GLA_W, GLA_DV)[::2, :]
    gla_mk = jnp.asarray(np.kron(np.eye(GLA_HEADS), np.ones((GLA_DV, GLA_DK))), F32)

    wg, wu, wd = ffn_wg.astype(BF16), ffn_wu.astype(BF16), ffn_wd.astype(BF16)
    wi = jnp.concatenate([w_in[..., :768], w_in[..., 800:D_IN], w_in[..., 768:800],
                          jnp.zeros((DEPTH, d, D_IN_PAD - D_IN), F32)], axis=-1).astype(BF16)
    wa_blk = jnp.zeros((DEPTH, 128, 2 * GLA_KW), F32)
    wa_blk = wa_blk.at[:, :GLA_RANK, :GLA_KW].set(gla_wa[:, 0]).at[:, GLA_RANK:2 * GLA_RANK, GLA_KW:].set(gla_wa[:, 1])
    wa_blk = wa_blk.astype(BF16)
    ba_blk = gla_ba.reshape(DEPTH, 1, 2 * GLA_KW)
    qg = jnp.tile(att_qnorm, (1, ATT_QH))[:, None, :]
    kg = jnp.tile(att_knorm, (1, ATT_KVH))[:, None, :]
    pool_blk = jnp.zeros((DEPTH, POOL_W, POOL_W), F32)
    for gi in range(len(POOL_WINDOWS)):
        sl = slice(gi * POOL_GW, (gi + 1) * POOL_GW)
        pool_blk = pool_blk.at[:, sl, sl].set(pool_w[:, gi])
    pool_blk = pool_blk.astype(BF16)
    pool_s = pool_scale[:, None, :]
    wf = fnet_w.astype(BF16)
    wo = w_out.astype(BF16)
    norm4 = norm_g[:, :, None, :]
    n2 = seq // FFT_N1

    xs = x.reshape(lat_rows, d)
    ctx_rows = ctx.reshape(n_batch * n_ctx, d)
    for i in range(DEPTH):
        ctx_out = i < DEPTH - 1
        mod_map = _mod_spec(i, lat_tiles, tiles_per_batch, n_batch, d)
        gn = jnp.tile(gla_norm[i], GLA_HEADS)[None, :]

        xs = _half_ffn(xs, mods, norm4, wg, wu, wd, layer=i, n_tiles=all_tiles,
                       mod_map=mod_map, ctx_rows=ctx_rows if i == 0 else None)
        (u_pool, gq, gk, gv, og, gf, gb, aq, ak, av, uf) = _in_projection(
            xs, mods, norm4, wi, wa_blk, ba_blk, e256, qg, kg, rope_lines, rope_cols,
            layer=i, n_tiles=all_tiles, mod_map=mod_map, line_map=line_map, col_map=col_map)

        y_pool = _pool_mix(u_pool, pool_blk, pool_s, layer=i, seq=seq, n_seq=n_batch, first_block=0)
        y_gla = _gla_mix(gq, gk, gv, gf, gb, og, gla_ex, gla_mk, gn, e256, n_batch=n_batch, seq=seq, ctx=n_ctx)
        y_att = _attention(aq, ak, av, n_batch=n_batch, seq=seq, ctx=n_ctx, ctx_queries=False)
        y_fnet = _fnet_latent(uf.reshape(uf.shape[0] // n2, n2, FNET_W), wf, layer=i, n_batch=n_batch, seq=seq)
        n_tiles, ctx_parts = lat_tiles, None
        if ctx_out:
            first_ctx = lat_rows // n_ctx
            ctx_parts = (
                _pool_mix(u_pool, pool_blk, pool_s, layer=i, seq=n_ctx, n_seq=n_batch, first_block=first_ctx),
                _attention(aq, ak, av, n_batch=n_batch, seq=seq, ctx=n_ctx, ctx_queries=True),
                _fnet_context(uf, wf, layer=i, n_batch=n_batch, ctx=n_ctx, first_block=first_ctx),
            )
            n_tiles = all_tiles
        xs = _mix_ffn(xs, mods, y_pool, y_gla, y_att, y_fnet, wo, norm4, wg, wu, wd, final_norm,
                      layer=i, n_tiles=n_tiles, mod_map=mod_map, final=not ctx_out, ctx_parts=ctx_parts)
    return xs.reshape(n_batch, seq, d)
```

```python
import functools
import math

import jax
import jax.numpy as jnp
import numpy as np
from jax import lax
from jax.experimental import pallas as pl
from jax.experimental.pallas import tpu as pltpu

F32 = jnp.float32
BF16 = jnp.bfloat16

D_MODEL = 1024
DEPTH = 2
GRID_W = 64
EPS = 1e-6
N_MOD = 9
D_FF = 2816

POOL_W = 256
POOL_WINDOWS = (2, 4, 8, 16)
POOL_GW = 64
POOL_HALO = 8

GLA_HEADS = 4
GLA_W = 256
GLA_DV = 64
GLA_DK = 32
GLA_RANK = 16
GLA_TAU = 16.0
GLA_KW = GLA_HEADS * GLA_DK
GLA_SUB = 16
GLA_CHUNK = 64
GLA_FAST_RANGE = 150.0

ATT_W = 256
ATT_HD = 64
ATT_QH = 4
ATT_KVH = 2
ATT_KW = ATT_KVH * ATT_HD
ATT_VROWS = ATT_HD + 16
ATT_BOUND_SLACK = 1.02
ATT_BOUND_MAX = 50.0
ROPE_FREQS = 16
ROPE_THETA = 10000.0

FNET_W = 256
FNET_HEADS = 4
FNET_HD = 64
FFT_N1 = 64

D_IN = 1824
D_IN_PAD = 1920

O_POOL, O_GQ, O_GK, O_GV, O_OG, O_AQ, O_AK, O_AV, O_FN, O_R = 0, 256, 384, 512, 768, 1024, 1280, 1408, 1536, 1792

TM = 512
TQ = 256
TP = 256
MXU_TILE = 256
FF_CHUNKS = ((0, 6 * MXU_TILE), (6 * MXU_TILE, D_FF))
NEG_BIG = -1e30

VMEM_LIMIT = 56 * 1024 * 1024


def _cparams(sem):
    return pltpu.CompilerParams(dimension_semantics=sem, vmem_limit_bytes=VMEM_LIMIT)


def _dot(a, b):
    return jnp.dot(a, b, preferred_element_type=F32)


def _split(a):
    hi = a.astype(BF16)
    lo = (a - hi.astype(F32)).astype(BF16)
    return hi, lo


def _dot3(a_hi, a_lo, b_hi, b_lo):
    return _dot(a_hi, b_hi) + (_dot(a_hi, b_lo) + _dot(a_lo, b_hi))


def _rms(x):
    return x * lax.rsqrt(jnp.mean(x * x, axis=-1, keepdims=True) + EPS)


def _silu(x):
    return x * jax.nn.sigmoid(x)


def _group_rms(x, e, width):
    s_hi, s_lo = _split(x * x)
    ss = _dot(s_hi, e) + _dot(s_lo, e)
    return x * lax.rsqrt(ss * (1.0 / width) + EPS)


def _mod_kernel(c_ref, w_ref, b_ref, o_ref):
    s = _silu(c_ref[...]).astype(BF16)
    o_ref[...] = _dot(s, w_ref[...].astype(BF16)) + b_ref[...]


def _modulation(cvec, w_mod, b_mod):
    depth, d, nd = w_mod.shape
    tn = nd // 8
    return pl.pallas_call(
        _mod_kernel,
        out_shape=jax.ShapeDtypeStruct((depth, 8, nd), F32),
        grid=(depth, nd // tn),
        in_specs=[
            pl.BlockSpec((8, d), lambda l, j: (0, 0)),
            pl.BlockSpec((None, d, tn), lambda l, j: (l, 0, j)),
            pl.BlockSpec((None, 1, tn), lambda l, j: (l, 0, j)),
        ],
        out_specs=pl.BlockSpec((None, 8, tn), lambda l, j: (l, 0, j)),
        compiler_params=_cparams(("arbitrary", "arbitrary")),
        name="modulation",
    )(cvec, w_mod, b_mod.reshape(depth, 1, nd))


def _half_ffn_rows(x, m_ref, g_ref, wg_ref, wu_ref, wd_ref, mod_base):
    shift = m_ref[mod_base:mod_base + 1, :]
    scale = m_ref[mod_base + 1:mod_base + 2, :]
    gate = m_ref[mod_base + 2:mod_base + 3, :]
    h = ((_rms(x) * g_ref[...]) * (1.0 + scale) + shift).astype(BF16)
    y = jnp.zeros(x.shape, F32)
    for lo, hi in FF_CHUNKS:
        sl = slice(lo, hi)
        a = _dot(h, wg_ref[:, sl])
        u = _dot(h, wu_ref[:, sl])
        y = y + _dot((_silu(a) * u).astype(BF16), wd_ref[sl, :])
    return x + (0.5 * gate) * y


def _ffn_kernel(*refs, n_lat_tiles):
    if n_lat_tiles is None:
        x_ref, m_ref, g_ref, wg_ref, wu_ref, wd_ref, o_ref = refs
        x = x_ref[...]
    else:
        x_ref, c_ref, m_ref, g_ref, wg_ref, wu_ref, wd_ref, o_ref = refs
        x = jnp.where(pl.program_id(0) < n_lat_tiles, x_ref[...], c_ref[...])
    o_ref[...] = _half_ffn_rows(x, m_ref, g_ref, wg_ref, wu_ref, wd_ref, 0)


def _mix_ffn_kernel(*refs, n_lat_tiles, final):
    if n_lat_tiles is None:
        (x_ref, m_ref, yp_ref, yg_ref, ya_ref, yf_ref, wo_ref,
         g_ref, wg_ref, wu_ref, wd_ref, fg_ref, o_ref) = refs
        yp, ya, yf = yp_ref[...], ya_ref[...], yf_ref[...]
    else:
        (x_ref, m_ref, yp_ref, yg_ref, ya_ref, yf_ref, cp_ref, ca_ref, cf_ref, wo_ref,
         g_ref, wg_ref, wu_ref, wd_ref, fg_ref, o_ref) = refs
        lat = pl.program_id(0) < n_lat_tiles
        yp = jnp.where(lat, yp_ref[...], cp_ref[...])
        ya = jnp.where(lat, ya_ref[...], ca_ref[...])
        yf = jnp.where(lat, yf_ref[...], cf_ref[...])
    acc = _dot(yp.astype(BF16), wo_ref[0:256, :])
    acc = acc + _dot(yg_ref[...].astype(BF16), wo_ref[256:512, :])
    acc = acc + _dot(ya.astype(BF16), wo_ref[512:768, :])
    acc = acc + _dot(yf.astype(BF16), wo_ref[768:1024, :])
    x = x_ref[...] + m_ref[5:6, :] * acc
    out = _half_ffn_rows(x, m_ref, g_ref, wg_ref, wu_ref, wd_ref, 6)
    if final:
        out = _rms(out) * fg_ref[...]
    o_ref[...] = out


def _mod_spec(layer, n_lat_tiles, tiles_per_batch, n_batch, d):
    def index_map(t):
        return (layer, jnp.where(t < n_lat_tiles, t // tiles_per_batch, n_batch), 0, 0)
    return pl.BlockSpec((None, None, N_MOD, d), index_map)


def _resident(arr, lead):
    block = (None,) * len(lead) + arr.shape[len(lead):]
    index = tuple(lead) + (0,) * (arr.ndim - len(lead))
    return pl.BlockSpec(block, lambda t: index, pipeline_mode=pl.Buffered(1))


def _ffn_weight_specs(g, wg, wu, wd, layer, half):
    d = g.shape[-1]
    lead = (layer, half)
    return [pl.BlockSpec((None, None, 1, d), lambda t: (layer, 2 * half, 0, 0)),
            _resident(wg, lead), _resident(wu, lead), _resident(wd, lead)]


def _half_ffn(x, mods, g, wg, wu, wd, *, layer, n_tiles, mod_map, ctx_rows=None):
    d = x.shape[1]
    n_lat = None if ctx_rows is None else x.shape[0] // TM
    if ctx_rows is None:
        rows, row_specs = [x], [pl.BlockSpec((TM, d), lambda t: (t, 0))]
    else:
        rows = [x, ctx_rows]
        row_specs = [pl.BlockSpec((TM, d), lambda t: (jnp.minimum(t, n_lat - 1), 0)),
                     pl.BlockSpec((TM, d), lambda t: (jnp.maximum(t - n_lat, 0), 0))]
    return pl.pallas_call(
        functools.partial(_ffn_kernel, n_lat_tiles=n_lat),
        out_shape=jax.ShapeDtypeStruct((n_tiles * TM, d), F32),
        grid=(n_tiles,),
        in_specs=row_specs + [mod_map] + _ffn_weight_specs(g, wg, wu, wd, layer, 0),
        out_specs=pl.BlockSpec((TM, d), lambda t: (t, 0)),
        compiler_params=_cparams(("arbitrary",)),
        name="half_ffn",
    )(*rows, mods, g, wg, wu, wd)


def _mix_ffn(x, mods, y_pool, y_gla, y_att, y_fnet, w_out, g, wg, wu, wd, fg,
             *, layer, n_tiles, mod_map, final, ctx_parts=None):
    d = x.shape[1]
    part = pl.BlockSpec((TM, 256), lambda t: (t, 0))
    n_lat = None
    parts, part_specs = [y_pool, y_gla, y_att, y_fnet], [part, part, part, part]
    if ctx_parts is not None:
        n_lat = y_pool.shape[0] // TM
        lat_part = pl.BlockSpec((TM, 256), lambda t: (jnp.minimum(t, n_lat - 1), 0))
        ctx_part = pl.BlockSpec((TM, 256), lambda t: (jnp.maximum(t - n_lat, 0), 0))
        parts += list(ctx_parts)
        part_specs = [lat_part, part, lat_part, lat_part, ctx_part, ctx_part, ctx_part]
    return pl.pallas_call(
        functools.partial(_mix_ffn_kernel, n_lat_tiles=n_lat, final=final),
        out_shape=jax.ShapeDtypeStruct((n_tiles * TM, d), F32),
        grid=(n_tiles,),
        in_specs=[pl.BlockSpec((TM, d), lambda t: (t, 0)), mod_map] + part_specs
                 + [_resident(w_out, (layer,))] + _ffn_weight_specs(g, wg, wu, wd, layer, 1)
                 + [pl.BlockSpec((1, d), lambda t: (0, 0))],
        out_specs=pl.BlockSpec((TM, d), lambda t: (t, 0)),
        compiler_params=_cparams(("arbitrary",)),
        name="mix_ffn",
    )(x, mods, *parts, w_out, g, wg, wu, wd, fg.reshape(1, d))


def _rope(x, c, sa, sb):
    w = x.shape[1]
    return x * c + pltpu.roll(x, w - ROPE_FREQS, 1) * sa + pltpu.roll(x, ROPE_FREQS, 1) * sb


def _tile_rope_table(lines, col):
    return jnp.concatenate([col + lines[r:r + 1, :] for r in range(lines.shape[0])], axis=0)


def _inproj_kernel(x_ref, m_ref, g_ref, w_ref, wa_ref, ba_ref, e_ref, qg_ref, kg_ref,
                   lc_ref, lsa_ref, lsb_ref, cc_ref, csa_ref, csb_ref,
                   up_ref, gq_ref, gk_ref, gv_ref, og_ref, gf_ref, gb_ref,
                   aq_ref, ak_ref, av_ref, uf_ref, qn_ref, kn_ref):
    shift = m_ref[3:4, :]
    scale = m_ref[4:5, :]
    e = e_ref[...]
    n_parts = 2
    rp = TM // n_parts
    lp = rp // GRID_W
    for part in range(n_parts):
        rows = slice(part * rp, (part + 1) * rp)
        lines = slice(part * lp, (part + 1) * lp)
        h = ((_rms(x_ref[rows, :]) * g_ref[...]) * (1.0 + scale) + shift).astype(BF16)
        p = _dot(h, w_ref[...])

        up_ref[rows, :] = p[:, O_POOL:O_POOL + POOL_W]
        uf_ref[rows, :] = p[:, O_FN:O_FN + FNET_W]

        gq_ref[rows, :] = p[:, O_GQ:O_GQ + GLA_KW] * (GLA_DK ** -0.5)
        gk_ref[rows, :] = p[:, O_GK:O_GK + GLA_KW]
        gv_ref[rows, :] = p[:, O_GV:O_GV + GLA_W]
        og_ref[rows, :] = p[:, O_OG:O_OG + GLA_W]
        z = _dot(p[:, O_R:O_R + 128].astype(BF16), wa_ref[...]) + ba_ref[...]
        logsig = jnp.minimum(z, 0.0) - jnp.log(1.0 + jnp.exp(-jnp.abs(z)))
        gdec = logsig * (1.0 / GLA_TAU)
        gf_ref[rows, :] = gdec[:, :GLA_KW]
        gb_ref[rows, :] = gdec[:, GLA_KW:]

        rc = _tile_rope_table(lc_ref[lines, :], cc_ref[...])
        rsa = _tile_rope_table(lsa_ref[lines, :], csa_ref[...])
        rsb = _tile_rope_table(lsb_ref[lines, :], csb_ref[...])
        q = _group_rms(p[:, O_AQ:O_AQ + ATT_W], e, ATT_HD) * qg_ref[...]
        q = _rope(q, rc, rsa, rsb) * (ATT_HD ** -0.5 * math.log2(math.e))
        aq_ref[:, rows] = q.T.astype(BF16)
        k = _group_rms(p[:, O_AK:O_AK + ATT_KW], e[:ATT_KW, :ATT_KW], ATT_HD) * kg_ref[...]
        k = _rope(k, rc[:, :ATT_KW], rsa[:, :ATT_KW], rsb[:, :ATT_KW])
        ak_ref[rows, :] = k.astype(BF16)
        qsq = jnp.max(_dot((q * q).astype(BF16), e), axis=0, keepdims=True)
        ksq = jnp.max(_dot((k * k).astype(BF16), e[:ATT_KW, :ATT_KW]), axis=0, keepdims=True)
        qmax = qsq if part == 0 else jnp.maximum(qmax, qsq)
        kmax = ksq if part == 0 else jnp.maximum(kmax, ksq)
        v_t = p[:, O_AV:O_AV + ATT_KW].T
        ones = jnp.ones((ATT_VROWS - ATT_HD, rp), F32)
        for kvh in range(ATT_KVH):
            av_ref[kvh, :, rows] = jnp.concatenate(
                [v_t[kvh * ATT_HD:(kvh + 1) * ATT_HD, :], ones], axis=0).astype(BF16)
    qn_ref[...] = jnp.broadcast_to(jnp.sqrt(qmax), qn_ref.shape)
    kn_ref[...] = jnp.broadcast_to(jnp.sqrt(kmax), kn_ref.shape)


def _in_projection(x, mods, g, w_in, wa_blk, ba_blk, e256, qg, kg, rope_lines, rope_cols,
                   *, layer, n_tiles, mod_map, line_map, col_map):
    t_rows, d = x.shape
    row = lambda w: pl.BlockSpec((TM, w), lambda t: (t, 0))
    const = lambda shape: pl.BlockSpec(shape, lambda t: (0,) * len(shape))
    per_layer = lambda arr: pl.BlockSpec((None,) + arr.shape[1:], lambda t: (layer,) + (0,) * (arr.ndim - 1))
    out_shapes = (
        jax.ShapeDtypeStruct((t_rows, POOL_W), F32),
        jax.ShapeDtypeStruct((t_rows, GLA_KW), F32),
        jax.ShapeDtypeStruct((t_rows, GLA_KW), F32),
        jax.ShapeDtypeStruct((t_rows, GLA_W), F32),
        jax.ShapeDtypeStruct((t_rows, GLA_W), F32),
        jax.ShapeDtypeStruct((t_rows, GLA_KW), F32),
        jax.ShapeDtypeStruct((t_rows, GLA_KW), F32),
        jax.ShapeDtypeStruct((n_tiles, ATT_W, TM), BF16),
        jax.ShapeDtypeStruct((t_rows, ATT_KW), BF16),
        jax.ShapeDtypeStruct((n_tiles, ATT_KVH, ATT_VROWS, TM), BF16),
        jax.ShapeDtypeStruct((t_rows, FNET_W), F32),
        jax.ShapeDtypeStruct((n_tiles, 8, ATT_W), F32),
        jax.ShapeDtypeStruct((n_tiles, 8, ATT_KW), F32),
    )
    out_specs = (
        row(POOL_W), row(GLA_KW), row(GLA_KW), row(GLA_W), row(GLA_W), row(GLA_KW), row(GLA_KW),
        pl.BlockSpec((None, ATT_W, TM), lambda t: (t, 0, 0)),
        row(ATT_KW),
        pl.BlockSpec((None, ATT_KVH, ATT_VROWS, TM), lambda t: (t, 0, 0, 0)),
        row(FNET_W),
        pl.BlockSpec((None, 8, ATT_W), lambda t: (t, 0, 0)),
        pl.BlockSpec((None, 8, ATT_KW), lambda t: (t, 0, 0)),
    )
    return pl.pallas_call(
        _inproj_kernel,
        out_shape=out_shapes,
        grid=(n_tiles,),
        in_specs=[
            pl.BlockSpec((TM, d), lambda t: (t, 0)),
            mod_map,
            pl.BlockSpec((None, None, 1, d), lambda t: (layer, 1, 0, 0)),
            per_layer(w_in),
            per_layer(wa_blk),
            per_layer(ba_blk),
            const(e256.shape),
            per_layer(qg),
            per_layer(kg),
        ] + [pl.BlockSpec((TM // GRID_W, ATT_W), line_map)] * 3
          + [pl.BlockSpec((None, GRID_W, ATT_W), col_map)] * 3,
        out_specs=out_specs,
        compiler_params=_cparams(("arbitrary",)),
        name="in_projection",
    )(x, mods, g, w_in, wa_blk, ba_blk, e256, qg, kg, *rope_lines, *rope_cols)


def _pool_kernel(u_ref, w_ref, s_ref, o_ref, pad_ref, *, seq, chunk):
    halo = POOL_HALO
    pad_ref[0:halo, :] = jnp.zeros((halo, POOL_W), F32)
    pad_ref[halo + seq:halo + seq + halo, :] = jnp.zeros((halo, POOL_W), F32)
    pad_ref[halo:halo + seq, :] = u_ref[...]
    rows = chunk + 2 * halo
    lane = lax.broadcasted_iota(jnp.int32, (chunk, POOL_W), 1)
    trow = lax.broadcasted_iota(jnp.int32, (chunk, POOL_W), 0)

    def body(ci, carry):
        c0 = pl.multiple_of(ci * chunk, chunk)
        xp = pad_ref[pl.ds(c0, rows), :]
        u = xp[halo:halo + chunk, :]
        t = trow + c0
        acc = xp
        m = None
        for gi, w in enumerate(POOL_WINDOWS):
            acc = acc + pltpu.roll(acc, w // 2, 0)
            lead = w // 2 - 1
            win = acc if lead == 0 else pltpu.roll(acc, rows - lead, 0)
            win = win[halo:halo + chunk, :]
            cnt = (jnp.minimum(t + w // 2, seq) - jnp.maximum(t - w // 2, 0)).astype(F32)
            mg = win / cnt - u
            m = mg if m is None else jnp.where(lane >= gi * POOL_GW, mg, m)
        y = _dot(m.astype(BF16), w_ref[...]) * s_ref[...]
        o_ref[pl.ds(c0, chunk), :] = y
        return carry

    lax.fori_loop(0, seq // chunk, body, 0)


def _pool_mix(u, w_blk, s_pool, *, layer, seq, n_seq, first_block):
    chunk = min(256, seq)
    return pl.pallas_call(
        functools.partial(_pool_kernel, seq=seq, chunk=chunk),
        out_shape=jax.ShapeDtypeStruct((n_seq * seq, POOL_W), F32),
        grid=(n_seq,),
        in_specs=[
            pl.BlockSpec((seq, POOL_W), lambda b: (first_block + b, 0)),
            pl.BlockSpec((None, POOL_W, POOL_W), lambda b: (layer, 0, 0)),
            pl.BlockSpec((None, 1, POOL_W), lambda b: (layer, 0, 0)),
        ],
        out_specs=pl.BlockSpec((seq, POOL_W), lambda b: (b, 0)),
        scratch_shapes=[pltpu.VMEM((seq + 2 * POOL_HALO, POOL_W), F32)],
        compiler_params=_cparams(("arbitrary",)),
        name="pool_mix",
    )(u, w_blk, s_pool)


def _gla_kernel(*refs, reverse, finalize):
    if finalize:
        (q_ref, k_ref, v_ref, g_ref, ex_ref, mk_ref, bm_ref, of_ref, og_ref, gn_ref, e_ref,
         o_ref, st_ref, ob_ref) = refs
    else:
        q_ref, k_ref, v_ref, g_ref, ex_ref, mk_ref, bm_ref, o_ref, st_ref = refs
        ob_ref = o_ref

    @pl.when(pl.program_id(1) == 0)
    def _():
        st_ref[...] = jnp.zeros(st_ref.shape, F32)

    g = g_ref[...]
    trow = lax.broadcasted_iota(jnp.int32, (TP, GLA_KW), 0) % GLA_CHUNK
    pre, suf = g, g
    s = 1
    while s < GLA_CHUNK:
        pre = pre + jnp.where(trow >= s, pltpu.roll(pre, s, 0), 0.0)
        suf = suf + jnp.where(trow < GLA_CHUNK - s, pltpu.roll(suf, TP - s, 0), 0.0)
        s *= 2
    tot = pre + suf - g
    in_range = jnp.max(jnp.abs(tot)) <= GLA_FAST_RANGE

    @pl.when(in_range)
    def _():
        _gla_fast_tile(q_ref, k_ref, v_ref, mk_ref, bm_ref, st_ref, ob_ref,
                       suf if reverse else pre, tot, reverse)

    @pl.when(jnp.logical_not(in_range))
    def _():
        _gla_exact_tile(q_ref, k_ref, v_ref, g_ref, ex_ref, mk_ref, st_ref, ob_ref, reverse)

    if finalize:
        both = ob_ref[...] + of_ref[...]
        y = _group_rms(both, e_ref[...], GLA_DV) * gn_ref[...]
        o_ref[...] = y * _silu(og_ref[...])


def _gla_fast_tile(q_ref, k_ref, v_ref, mk_ref, bm_ref, st_ref, ob_ref, b, tot, reverse):
    ch = GLA_CHUNK
    half = 0.5 * tot
    q, k = q_ref[...], k_ref[...]
    q_in = (q * jnp.exp(b - half)).astype(BF16)
    k_in = (k * jnp.exp(half - b)).astype(BF16)
    q_st = (q * jnp.exp(b)).astype(BF16)
    k_st = (k * jnp.exp(tot - b)).astype(BF16)
    dec = jnp.exp(tot)
    mk = mk_ref[...]
    mk16 = mk.astype(BF16)
    bm16 = bm_ref[...]
    irow = lax.broadcasted_iota(jnp.int32, (ch, GLA_HEADS * ch), 0)
    jcol = lax.broadcasted_iota(jnp.int32, (ch, GLA_HEADS * ch), 1) % ch
    seen = (jcol >= irow) if reverse else (jcol <= irow)
    nt = (((1,), (1,)), ((), ()))
    st = st_ref[...]
    n_chunks = TP // ch
    for ci in range(n_chunks):
        c = (n_chunks - 1 - ci) if reverse else ci
        rows = slice(c * ch, (c + 1) * ch)
        kb = jnp.concatenate([k_in[rows]] * GLA_HEADS, axis=0) * mk16
        a = lax.dot_general(q_in[rows], kb, nt, preferred_element_type=F32)
        a = jnp.where(seen, a, 0.0).astype(BF16)
        v = v_ref[rows, :]
        vb = jnp.concatenate([v.astype(BF16)] * GLA_HEADS, axis=0) * bm16
        o = _dot(a, vb) + lax.dot_general(q_st[rows], st.astype(BF16), nt, preferred_element_type=F32)
        kv = lax.dot_general(v.astype(BF16), k_st[rows], (((0,), (0,)), ((), ())),
                             preferred_element_type=F32)
        st = st * dec[c * ch:c * ch + 1, :] + kv * mk
        ob_ref[rows, :] = o
    st_ref[...] = st


def _gla_exact_tile(q_ref, k_ref, v_ref, g_ref, ex_ref, mk_ref, st_ref, ob_ref, reverse):
    sub = GLA_SUB
    n_chunks = TP // sub
    row = lax.broadcasted_iota(jnp.int32, (sub, GLA_KW), 0)
    ex = ex_ref[...]
    mk = mk_ref[...]

    def body(ci, carry):
        c = (n_chunks - 1 - ci) if reverse else ci
        r0 = pl.multiple_of(c * sub, sub)
        q = q_ref[pl.ds(r0, sub), :]
        k = k_ref[pl.ds(r0, sub), :]
        v = v_ref[pl.ds(r0, sub), :]
        b = g_ref[pl.ds(r0, sub), :]
        s = 1
        while s < sub:
            if reverse:
                b = b + jnp.where(row < sub - s, pltpu.roll(b, sub - s, 0), 0.0)
            else:
                b = b + jnp.where(row >= s, pltpu.roll(b, s, 0), 0.0)
            s *= 2
        edge = b[0:1, :] if reverse else b[sub - 1:sub, :]
        st = st_ref[...]
        qd = (q * jnp.exp(b)).astype(BF16)
        o = lax.dot_general(qd, st.astype(BF16), (((1,), (1,)), ((), ())),
                            preferred_element_type=F32)
        parts = []
        for j in range(sub):
            seen = (row <= j) if reverse else (row >= j)
            dec = jnp.exp(jnp.where(seen, b - b[j:j + 1, :], NEG_BIG))
            parts.append((dec * q * k[j:j + 1, :]).astype(BF16))
        r = _dot(jnp.concatenate(parts, axis=0), ex)
        for j in range(sub):
            o = o + r[j * sub:(j + 1) * sub, :] * v[j:j + 1, :]
        kd = (k * jnp.exp(edge - b)).astype(BF16)
        kv = lax.dot_general(v.astype(BF16), kd, (((0,), (0,)), ((), ())),
                             preferred_element_type=F32)
        st_ref[...] = st * jnp.exp(edge) + kv * mk
        ob_ref[pl.ds(r0, sub), :] = o
        return carry

    lax.fori_loop(0, n_chunks, body, 0, unroll=2)


def _gla_tile_map(n_batch, n_lat, n_ctx, reverse):
    def index_map(b, s):
        is_ctx = s < n_ctx
        sc = jnp.where(is_ctx, s, 0)
        sl = jnp.where(is_ctx, 0, s - n_ctx)
        if reverse:
            sc = n_ctx - 1 - sc
            sl = n_lat - 1 - sl
        return (jnp.where(is_ctx, n_batch * n_lat + b * n_ctx + sc, b * n_lat + sl), 0)
    return index_map


def _gla_mix(gq, gk, gv, gf, gb, og, ex, mk, gn, e256, *, n_batch, seq, ctx):
    t_rows = gq.shape[0]
    n_lat, n_ctx = seq // TP, ctx // TP
    grid = (n_batch, n_lat + n_ctx)
    const = lambda shape: pl.BlockSpec(shape, lambda b, s: (0,) * len(shape))

    def run(reverse, finalize, gate, extra):
        tmap = _gla_tile_map(n_batch, n_lat, n_ctx, reverse)
        rows = lambda w: pl.BlockSpec((TP, w), tmap)
        in_specs = [rows(GLA_KW), rows(GLA_KW), rows(GLA_W), rows(GLA_KW), const(ex.shape), const(mk.shape),
                    const(e256.shape)]
        args = [gq, gk, gv, gate, ex, mk, e256]
        scratch = [pltpu.VMEM((GLA_W, GLA_KW), F32)]
        if finalize:
            in_specs += [rows(GLA_W), rows(GLA_W), const((1, GLA_W)), const(e256.shape)]
            args += extra
            scratch.append(pltpu.VMEM((TP, GLA_W), F32))
        return pl.pallas_call(
            functools.partial(_gla_kernel, reverse=reverse, finalize=finalize),
            out_shape=jax.ShapeDtypeStruct((t_rows, GLA_W), F32),
            grid=grid,
            in_specs=in_specs,
            out_specs=rows(GLA_W),
            scratch_shapes=scratch,
            compiler_params=_cparams(("arbitrary", "arbitrary")),
            name="gla_bwd" if reverse else "gla_fwd",
        )(*args)

    o_f = run(False, False, gf, None)
    return run(True, True, gb, [o_f, og, gn, e256])


def _att_query_groups(q_t):
    tq = q_t.shape[1]
    zero = jnp.zeros((ATT_HD, tq), BF16)
    groups = []
    for g in range(ATT_KVH):
        halves = []
        for h in (2 * g, 2 * g + 1):
            qh = q_t[h * ATT_HD:(h + 1) * ATT_HD, :]
            halves.append(jnp.concatenate([qh, zero] if g == 0 else [zero, qh], axis=0))
        groups.append(jnp.concatenate(halves, axis=1))
    return groups


def _att_store(o_ref, weighted, denom, tq):
    heads = []
    for g in range(ATT_KVH):
        og = weighted[g] / denom[g]
        heads += [og[:, :tq], og[:, tq:]]
    o_ref[...] = jnp.concatenate(heads, axis=0).T.astype(o_ref.dtype)


def _att_kernel(*refs, n_main, has_ctx, bounded):
    refs = list(refs)
    q_ref, km_ref, vm_ref = refs.pop(0), refs.pop(0), refs.pop(0)
    kc_ref, vc_ref = (refs.pop(0), refs.pop(0)) if has_ctx else (None, None)
    o_ref, s0_ref, s1_ref = refs
    q_t = q_ref[...]
    tq = q_t.shape[1]
    tk = vm_ref.shape[3]
    groups = _att_query_groups(q_t)

    def scores(kblk, buf):
        for g in range(ATT_KVH):
            buf[g, 0:kblk.shape[0], :] = _dot(kblk, groups[g])

    def consume(buf, rows, v_of, carry):
        out = []
        for g in range(ATT_KVH):
            m, l, acc = carry[g]
            s = buf[g, 0:rows, :]
            if bounded:
                pv = _dot(v_of(g), jnp.exp2(s).astype(BF16))
                out.append((m, l + pv[ATT_HD:ATT_HD + 1, :], acc + pv[:ATT_HD, :]))
                continue
            m_new = jnp.maximum(m, jnp.max(s, axis=0, keepdims=True))
            alpha = jnp.exp2(m - m_new)
            p = jnp.exp2(s - m_new).astype(BF16)
            pv = _dot(v_of(g), p)
            acc = alpha * acc + pv[:ATT_HD, :]
            l = alpha * l + pv[ATT_HD:ATT_HD + 1, :]
            out.append((m_new, l, acc))
        return tuple(out)

    def k_main(j):
        return km_ref[pl.ds(pl.multiple_of(j * tk, tk), tk), :]

    def v_main(j):
        return lambda g: vm_ref[j, g]

    def pair(j, carry, issue_next):
        scores(k_main(j + 1), s1_ref)
        carry = consume(s0_ref, tk, v_main(j), carry)
        issue_next(j + 2)
        return consume(s1_ref, tk, v_main(j + 1), carry)

    carry = tuple((jnp.full((1, 2 * tq), -jnp.inf, F32), jnp.zeros((1, 2 * tq), F32),
                   jnp.zeros((ATT_HD, 2 * tq), F32)) for _ in range(ATT_KVH))
    next_main = lambda j: scores(k_main(j), s0_ref)
    if has_ctx:
        after_main = lambda j: scores(kc_ref[...], s0_ref)
    else:
        after_main = lambda j: None

    scores(k_main(0), s0_ref)
    if n_main > 1:
        pairs_per_trip = 2 if n_main % 4 == 0 else 1

        def body(i, carry):
            for u in range(pairs_per_trip):
                carry = pair(2 * (pairs_per_trip * i + u), carry, next_main)
            return carry

        n_trips = n_main // (2 * pairs_per_trip)
        carry = lax.fori_loop(0, n_trips - 1, body, carry)
        for u in range(pairs_per_trip):
            last = u == pairs_per_trip - 1
            carry = pair(2 * (pairs_per_trip * (n_trips - 1) + u), carry, after_main if last else next_main)
    else:
        assert not has_ctx
        carry = consume(s0_ref, tk, v_main(0), carry)
    if has_ctx:
        carry = consume(s0_ref, kc_ref.shape[0], lambda g: vc_ref[g], carry)
    _att_store(o_ref, [c[2] for c in carry], [c[1] for c in carry], tq)


def _attention(aq, ak, av, *, n_batch, seq, ctx, ctx_queries, bounded=False):
    lat_tiles = seq // TM
    ctx_tile = n_batch * lat_tiles
    ctx_cols = lambda b, i: (ctx_tile + (b * ctx) // TM, 0, ((b * ctx) % TM) // ctx)
    ctx_vcols = lambda b, i: (ctx_tile + (b * ctx) // TM, 0, 0, ((b * ctx) % TM) // ctx)
    scratch = []
    if not ctx_queries:
        sub = TM // TQ
        grid = (n_batch, seq // TQ)
        q_map = lambda b, i: (b * lat_tiles + i // sub, 0, i % sub)
        in_specs = [
            pl.BlockSpec((None, ATT_W, TQ), q_map),
            pl.BlockSpec((seq, ATT_KW), lambda b, i: (b, 0)),
            pl.BlockSpec((lat_tiles, ATT_KVH, ATT_VROWS, TM), lambda b, i: (b, 0, 0, 0)),
            pl.BlockSpec((ctx, ATT_KW), lambda b, i: (n_batch * seq // ctx + b, 0)),
            pl.BlockSpec((None, ATT_KVH, ATT_VROWS, ctx), ctx_vcols),
        ]
        out_spec = pl.BlockSpec((TQ, ATT_W), lambda b, i: (b * (seq // TQ) + i, 0))
        assert lat_tiles % 2 == 0
        out_rows = n_batch * seq
        args = (aq, ak, av, ak, av)
        kern = functools.partial(_att_kernel, n_main=lat_tiles, has_ctx=True, bounded=bounded)
        s_shape = (ATT_KVH, TM, 2 * TQ)
    else:
        grid = (n_batch, 1)
        in_specs = [
            pl.BlockSpec((None, ATT_W, ctx), ctx_cols),
            pl.BlockSpec((ctx, ATT_KW), lambda b, i: (n_batch * seq // ctx + b, 0)),
            pl.BlockSpec((1, ATT_KVH, ATT_VROWS, ctx), ctx_vcols),
        ]
        out_spec = pl.BlockSpec((ctx, ATT_W), lambda b, i: (b, 0))
        kern = functools.partial(_att_kernel, n_main=1, has_ctx=False, bounded=False)
        out_rows = n_batch * ctx
        args = (aq, ak, av)
        s_shape = (ATT_KVH, ctx, 2 * ctx)
    return pl.pallas_call(
        kern,
        out_shape=jax.ShapeDtypeStruct((out_rows, ATT_W), BF16),
        grid=grid,
        in_specs=in_specs,
        out_specs=out_spec,
        scratch_shapes=[pltpu.VMEM(s_shape, F32), pltpu.VMEM(s_shape, F32)],
        compiler_params=_cparams(("arbitrary", "arbitrary")),
        name="attention_ctx" if ctx_queries else ("attention_bounded" if bounded else "attention"),
    )(*args)


def _fft_a_kernel(x_ref, dh_ref, dl_ref, tr_ref, ti_ref, yr_ref, yi_ref):
    rb, w = x_ref.shape[1], x_ref.shape[2]
    x = jnp.concatenate([x_ref[:, r, :] for r in range(rb)], axis=1)
    x_hi, x_lo = _split(x)
    y = _dot3(dh_ref[...], dl_ref[...], x_hi, x_lo)
    tr, ti = tr_ref[...], ti_ref[...]
    for r in range(rb):
        yr, yi = y[:FFT_N1, r * w:(r + 1) * w], y[FFT_N1:, r * w:(r + 1) * w]
        c, s = tr[:, r:r + 1], ti[:, r:r + 1]
        yr_ref[:, r, :] = yr * c - yi * s
        yi_ref[:, r, :] = yr * s + yi * c


def _channel_mix(xr, xi, ch_ref, cl_ref, wf_ref, norm):
    xc = jnp.concatenate([xr, xi], axis=1)
    x_hi, x_lo = _split(xc)
    f = _dot3(x_hi, x_lo, ch_ref[...], cl_ref[...]) * norm
    return _dot(f.astype(BF16), wf_ref[...])


def _fft_c_kernel(yr_ref, yi_ref, mh_ref, ml_ref, ch_ref, cl_ref, wf_ref, o_ref, *, n2, norm):
    w = yr_ref.shape[1]
    blk = jnp.concatenate(
        [jnp.concatenate([yr_ref[j * n2:(j + 1) * n2, :] for j in range(8)], axis=1),
         jnp.concatenate([yi_ref[j * n2:(j + 1) * n2, :] for j in range(8)], axis=1)], axis=0)
    b_hi, b_lo = _split(blk)
    x = _dot3(mh_ref[...], ml_ref[...], b_hi, b_lo)
    xr = jnp.concatenate([x[:n2, j * w:(j + 1) * w] for j in range(8)], axis=0)
    xi = jnp.concatenate([x[n2:, j * w:(j + 1) * w] for j in range(8)], axis=0)
    y = _channel_mix(xr, xi, ch_ref, cl_ref, wf_ref, norm)
    for j in range(8):
        o_ref[:, j, :] = y[j * n2:(j + 1) * n2, :]


def _fft_dense_kernel(x_ref, dh_ref, dl_ref, ch_ref, cl_ref, wf_ref, o_ref, *, n, norm):
    x_hi, x_lo = _split(x_ref[...])
    x = _dot3(dh_ref[...], dl_ref[...], x_hi, x_lo)
    o_ref[...] = _channel_mix(x[:n, :], x[n:, :], ch_ref, cl_ref, wf_ref, norm)


def _dft_parts(n):
    idx = np.arange(n)
    ang = 2.0 * np.pi * ((idx[:, None] * idx[None, :]) % n) / n
    return np.cos(ang), np.sin(ang)


def _np_split(a):
    a = jnp.asarray(a, F32)
    return _split(a)


def _fnet_consts(seq):
    n1, n2 = FFT_N1, seq // FFT_N1
    c1, s1 = _dft_parts(n1)
    da = np.concatenate([c1, -s1], axis=0)
    k1 = np.arange(n1)[:, None]
    m2 = np.arange(n2)[None, :]
    ang = 2.0 * np.pi * ((k1 * m2) % seq) / seq
    tw_r, tw_i = np.cos(ang), -np.sin(ang)
    c2, s2 = _dft_parts(n2)
    mc = np.block([[c2, s2], [-s2, c2]])
    return da, tw_r, tw_i, mc


def _channel_consts():
    cc, sc = _dft_parts(FNET_HD)
    eye = np.eye(FNET_HEADS)
    return np.concatenate([np.kron(eye, cc), np.kron(eye, sc)], axis=0)


def _fnet_latent(uf3, wf, *, layer, n_batch, seq):
    n1, n2 = FFT_N1, seq // FFT_N1
    da, tw_r, tw_i, mc = _fnet_consts(seq)
    dh, dl = _np_split(da)
    mh, ml = _np_split(mc)
    ch, cl = _np_split(_channel_consts())
    rb = min(n2, 32)
    blocked = lambda tw: jnp.asarray(tw.reshape(n1, n2 // rb, rb).transpose(1, 0, 2), F32)
    const2 = lambda shape: pl.BlockSpec(shape, lambda b, j: (0,) * len(shape))
    slab = pl.BlockSpec((n1, rb, FNET_W), lambda b, j: (b, j, 0))
    yr, yi = pl.pallas_call(
        _fft_a_kernel,
        out_shape=(jax.ShapeDtypeStruct((n_batch * n1, n2, FNET_W), F32),) * 2,
        grid=(n_batch, n2 // rb),
        in_specs=[
            slab,
            const2(dh.shape), const2(dl.shape),
            pl.BlockSpec((None, n1, rb), lambda b, j: (j, 0, 0)),
            pl.BlockSpec((None, n1, rb), lambda b, j: (j, 0, 0)),
        ],
        out_specs=(slab, slab),
        compiler_params=_cparams(("arbitrary", "arbitrary")),
        name="fft_stage_a",
    )(uf3, dh, dl, blocked(tw_r), blocked(tw_i))
    yr = yr.reshape(n_batch * n1 * n2, FNET_W)
    yi = yi.reshape(n_batch * n1 * n2, FNET_W)
    norm = 1.0 / math.sqrt(seq * FNET_HD)
    out = pl.pallas_call(
        functools.partial(_fft_c_kernel, n2=n2, norm=norm),
        out_shape=jax.ShapeDtypeStruct((n_batch, n2, n1, FNET_W), F32),
        grid=(n_batch, n1 // 8),
        in_specs=[
            pl.BlockSpec((8 * n2, FNET_W), lambda b, i: (b * (n1 // 8) + i, 0)),
            pl.BlockSpec((8 * n2, FNET_W), lambda b, i: (b * (n1 // 8) + i, 0)),
            const2(mh.shape), const2(ml.shape), const2(ch.shape), const2(cl.shape),
            pl.BlockSpec((None,) + wf.shape[1:], lambda b, i: (layer, 0, 0)),
        ],
        out_specs=pl.BlockSpec((None, n2, 8, FNET_W), lambda b, i: (b, 0, i, 0)),
        compiler_params=_cparams(("arbitrary", "arbitrary")),
        name="fft_stage_c",
    )(yr, yi, mh, ml, ch, cl, wf)
    return out.reshape(n_batch * seq, FNET_W)


def _fnet_context(uf, wf, *, layer, n_batch, ctx, first_block):
    c, s = _dft_parts(ctx)
    dh, dl = _np_split(np.concatenate([c, -s], axis=0))
    ch, cl = _np_split(_channel_consts())
    const = lambda shape: pl.BlockSpec(shape, lambda b: (0,) * len(shape))
    return pl.pallas_call(
        functools.partial(_fft_dense_kernel, n=ctx, norm=1.0 / math.sqrt(ctx * FNET_HD)),
        out_shape=jax.ShapeDtypeStruct((n_batch * ctx, FNET_W), F32),
        grid=(n_batch,),
        in_specs=[pl.BlockSpec((ctx, FNET_W), lambda b: (first_block + b, 0)),
                  const(dh.shape), const(dl.shape), const(ch.shape), const(cl.shape),
                  pl.BlockSpec((None,) + wf.shape[1:], lambda b: (layer, 0, 0))],
        out_specs=pl.BlockSpec((ctx, FNET_W), lambda b: (b, 0)),
        compiler_params=_cparams(("arbitrary",)),
        name="fft_context",
    )(uf, dh, dl, ch, cl, wf)


def _rope_tables(seq):
    freqs = ROPE_THETA ** (-jnp.arange(ROPE_FREQS, dtype=F32) / ROPE_FREQS)
    row_ang = jnp.arange(seq // GRID_W, dtype=F32)[:, None] * freqs
    col_ang = jnp.arange(GRID_W, dtype=F32)[:, None] * freqs
    lanes = np.arange(ATT_HD)
    is_row = jnp.asarray(np.tile(lanes < 2 * ROPE_FREQS, ATT_QH)[None, :], F32)
    first_half = jnp.asarray(np.tile((lanes // ROPE_FREQS) % 2 == 0, ATT_QH)[None, :], F32)
    spread = lambda a: jnp.tile(a, (1, ATT_W // ROPE_FREQS))

    def tables(ang, mask, ident_lines):
        cos, sin = spread(jnp.cos(ang)) * mask, spread(jnp.sin(ang)) * mask
        ident = jnp.zeros((ident_lines, ATT_W), F32)
        return (jnp.concatenate([cos, ident + mask], axis=0),
                jnp.concatenate([-sin * first_half, ident], axis=0),
                jnp.concatenate([sin * (1.0 - first_half), ident], axis=0))

    rows = tables(row_ang, is_row, TM // GRID_W)
    cols = tuple(t.reshape(2, GRID_W, ATT_W) for t in tables(col_ang, 1.0 - is_row, GRID_W))
    return rows, cols


def _block_ones(width, group):
    return jnp.asarray(np.kron(np.eye(width // group), np.ones((group, group))), BF16)


def kernel(x, c, ctx, c_ctx, w_mod, b_mod, norm_g, ffn_wg, ffn_wu, ffn_wd, w_in, w_out,
           pool_w, pool_scale, gla_wa, gla_ba, gla_norm, att_qnorm, att_knorm, fnet_w, final_norm):
    n_batch, seq, d = x.shape
    n_ctx = ctx.shape[1]
    assert d == D_MODEL and seq % TM == 0 and (n_batch * n_ctx) % TM == 0 and n_ctx % TP == 0
    assert seq % (8 * FFT_N1) == 0 and n_ctx <= TM and TM % n_ctx == 0 and n_batch + 1 <= 8
    lat_rows = n_batch * seq
    lat_tiles = lat_rows // TM
    all_tiles = lat_tiles + (n_batch * n_ctx) // TM
    tiles_per_batch = seq // TM
    line_map = lambda t: (jnp.where(t < lat_tiles, t % tiles_per_batch, tiles_per_batch), 0)
    col_map = lambda t: (jnp.where(t < lat_tiles, 0, 1), 0, 0)

    cvec = jnp.concatenate([c, c_ctx[None, :], jnp.zeros((8 - n_batch - 1, d), F32)], axis=0)
    mods = _modulation(cvec, w_mod, b_mod).reshape(DEPTH, 8, N_MOD, d)

    rope_lines, rope_cols = _rope_tables(seq)
    e256 = _block_ones(ATT_W, ATT_HD)
    gla_ex = _block_ones(GLA_W, GLA_DV)[::2, :]
    gla_mk = jnp.asarray(np.kron(np.eye(GLA_HEADS), np.ones((GLA_DV, GLA_DK))), F32)

    wg, wu, wd = ffn_wg.astype(BF16), ffn_wu.astype(BF16), ffn_wd.astype(BF16)
    wi = jnp.concatenate([w_in[..., :768], w_in[..., 800:D_IN], w_in[..., 768:800],
                          jnp.zeros((DEPTH, d, D_IN_PAD - D_IN), F32)], axis=-1).astype(BF16)
    wa_blk = jnp.zeros((DEPTH, 128, 2 * GLA_KW), F32)
    wa_blk = wa_blk.at[:, :GLA_RANK, :GLA_KW].set(gla_wa[:, 0]).at[:, GLA_RANK:2 * GLA_RANK, GLA_KW:].set(gla_wa[:, 1])
    wa_blk = wa_blk.astype(BF16)
    ba_blk = gla_ba.reshape(DEPTH, 1, 2 * GLA_KW)
    qg = jnp.tile(att_qnorm, (1, ATT_QH))[:, None, :]
    kg = jnp.tile(att_knorm, (1, ATT_KVH))[:, None, :]
    pool_blk = jnp.zeros((DEPTH, POOL_W, POOL_W), F32)
    for gi in range(len(POOL_WINDOWS)):
        sl = slice(gi * POOL_GW, (gi + 1) * POOL_GW)
        pool_blk = pool_blk.at[:, sl, sl].set(pool_w[:, gi])
    pool_blk = pool_blk.astype(BF16)
    pool_s = pool_scale[:, None, :]
    wf = fnet_w.astype(BF16)
    wo = w_out.astype(BF16)
    norm4 = norm_g[:, :, None, :]
    n2 = seq // FFT_N1

    xs = x.reshape(lat_rows, d)
    ctx_rows = ctx.reshape(n_batch * n_ctx, d)
    for i in range(DEPTH):
        ctx_out = i < DEPTH - 1
        mod_map = _mod_spec(i, lat_tiles, tiles_per_batch, n_batch, d)
        gn = jnp.tile(gla_norm[i], GLA_HEADS)[None, :]

        xs = _half_ffn(xs, mods, norm4, wg, wu, wd, layer=i, n_tiles=all_tiles,
                       mod_map=mod_map, ctx_rows=ctx_rows if i == 0 else None)
        (u_pool, gq, gk, gv, og, gf, gb, aq, ak, av, uf, qn, kn) = _in_projection(
            xs, mods, norm4, wi, wa_blk, ba_blk, e256, qg, kg, rope_lines, rope_cols,
            layer=i, n_tiles=all_tiles, mod_map=mod_map, line_map=line_map, col_map=col_map)

        y_pool = _pool_mix(u_pool, pool_blk, pool_s, layer=i, seq=seq, n_seq=n_batch, first_block=0)
        y_gla = _gla_mix(gq, gk, gv, gf, gb, og, gla_ex, gla_mk, gn, e256, n_batch=n_batch, seq=seq, ctx=n_ctx)
        qx = jnp.max(qn[:, 0, ::ATT_HD].reshape(-1, ATT_KVH, ATT_QH // ATT_KVH), axis=(0, 2))
        kx = jnp.max(kn[:, 0, ::ATT_HD], axis=0)
        score_bound = jnp.max(qx * kx) * ATT_BOUND_SLACK
        attend = functools.partial(_attention, aq, ak, av, n_batch=n_batch, seq=seq, ctx=n_ctx, ctx_queries=False)
        y_att = lax.cond(score_bound <= ATT_BOUND_MAX, lambda: attend(bounded=True), lambda: attend())
        y_fnet = _fnet_latent(uf.reshape(uf.shape[0] // n2, n2, FNET_W), wf, layer=i, n_batch=n_batch, seq=seq)
        n_tiles, ctx_parts = lat_tiles, None
        if ctx_out:
            first_ctx = lat_rows // n_ctx
            ctx_parts = (
                _pool_mix(u_pool, pool_blk, pool_s, layer=i, seq=n_ctx, n_seq=n_batch, first_block=first_ctx),
                _attention(aq, ak, av, n_batch=n_batch, seq=seq, ctx=n_ctx, ctx_queries=True),
                _fnet_context(uf, wf, layer=i, n_batch=n_batch, ctx=n_ctx, first_block=first_ctx),
            )
            n_tiles = all_tiles
        xs = _mix_ffn(xs, mods, y_pool, y_gla, y_att, y_fnet, wo, norm4, wg, wu, wd, final_norm,
                      layer=i, n_tiles=n_tiles, mod_map=mod_map, final=not ctx_out, ctx_parts=ctx_parts)
    return xs.reshape(n_batch, seq, d)
```

```python
import functools
import math

import jax
import jax.numpy as jnp
import numpy as np
from jax import lax
from jax.experimental import pallas as pl
from jax.experimental.pallas import tpu as pltpu

F32 = jnp.float32
BF16 = jnp.bfloat16

D_MODEL = 1024
DEPTH = 2
GRID_W = 64
EPS = 1e-6
N_MOD = 9
D_FF = 2816

POOL_W = 256
POOL_WINDOWS = (2, 4, 8, 16)
POOL_GW = 64
POOL_HALO = 8

GLA_HEADS = 4
GLA_W = 256
GLA_DV = 64
GLA_DK = 32
GLA_RANK = 16
GLA_TAU = 16.0
GLA_KW = GLA_HEADS * GLA_DK
GLA_SUB = 16
GLA_CHUNK = 64
GLA_FAST_RANGE = 150.0

ATT_W = 256
ATT_HD = 64
ATT_QH = 4
ATT_KVH = 2
ATT_KW = ATT_KVH * ATT_HD
ATT_VROWS = ATT_HD + 16
ATT_BOUND_SLACK = 1.02
ATT_BOUND_MAX = 50.0
ROPE_FREQS = 16
ROPE_THETA = 10000.0

FNET_W = 256
FNET_HEADS = 4
FNET_HD = 64
FFT_N1 = 64

D_IN = 1824
D_IN_PAD = 1920

O_POOL, O_GQ, O_GK, O_GV, O_OG, O_AQ, O_AK, O_AV, O_FN, O_R = 0, 256, 384, 512, 768, 1024, 1280, 1408, 1536, 1792

TM = 512
TQ = 256
TP = 256
MXU_TILE = 256
FF_CHUNKS = ((0, 6 * MXU_TILE), (6 * MXU_TILE, D_FF))
NEG_BIG = -1e30

VMEM_LIMIT = 56 * 1024 * 1024


def _cparams(sem):
    return pltpu.CompilerParams(dimension_semantics=sem, vmem_limit_bytes=VMEM_LIMIT)


def _dot(a, b):
    return jnp.dot(a, b, preferred_element_type=F32)


def _split(a):
    hi = a.astype(BF16)
    lo = (a - hi.astype(F32)).astype(BF16)
    return hi, lo


def _dot3(a_hi, a_lo, b_hi, b_lo):
    return _dot(a_hi, b_hi) + (_dot(a_hi, b_lo) + _dot(a_lo, b_hi))


def _rms(x):
    return x * lax.rsqrt(jnp.mean(x * x, axis=-1, keepdims=True) + EPS)


def _silu(x):
    return x * jax.nn.sigmoid(x)


def _group_rms(x, e, width):
    ss = _dot((x * x).astype(BF16), e)
    return x * lax.rsqrt(ss * (1.0 / width) + EPS)


def _mod_kernel(c_ref, w_ref, b_ref, o_ref):
    s = _silu(c_ref[...]).astype(BF16)
    o_ref[...] = _dot(s, w_ref[...].astype(BF16)) + b_ref[...]


def _modulation(cvec, w_mod, b_mod):
    depth, d, nd = w_mod.shape
    tn = nd // 8
    return pl.pallas_call(
        _mod_kernel,
        out_shape=jax.ShapeDtypeStruct((depth, 8, nd), F32),
        grid=(depth, nd // tn),
        in_specs=[
            pl.BlockSpec((8, d), lambda l, j: (0, 0)),
            pl.BlockSpec((None, d, tn), lambda l, j: (l, 0, j)),
            pl.BlockSpec((None, 1, tn), lambda l, j: (l, 0, j)),
        ],
        out_specs=pl.BlockSpec((None, 8, tn), lambda l, j: (l, 0, j)),
        compiler_params=_cparams(("arbitrary", "arbitrary")),
        name="modulation",
    )(cvec, w_mod, b_mod.reshape(depth, 1, nd))


def _half_ffn_rows(x, m_ref, g_ref, wg_ref, wu_ref, wd_ref, mod_base):
    shift = m_ref[mod_base:mod_base + 1, :]
    scale = m_ref[mod_base + 1:mod_base + 2, :]
    gate = m_ref[mod_base + 2:mod_base + 3, :]
    h = ((_rms(x) * g_ref[...]) * (1.0 + scale) + shift).astype(BF16)
    y = jnp.zeros(x.shape, F32)
    for lo, hi in FF_CHUNKS:
        sl = slice(lo, hi)
        a = _dot(h, wg_ref[:, sl])
        u = _dot(h, wu_ref[:, sl])
        y = y + _dot((_silu(a) * u).astype(BF16), wd_ref[sl, :])
    return x + (0.5 * gate) * y


def _ffn_kernel(*refs, n_lat_tiles):
    if n_lat_tiles is None:
        x_ref, m_ref, g_ref, wg_ref, wu_ref, wd_ref, o_ref = refs
        x = x_ref[...]
    else:
        x_ref, c_ref, m_ref, g_ref, wg_ref, wu_ref, wd_ref, o_ref = refs
        x = jnp.where(pl.program_id(0) < n_lat_tiles, x_ref[...], c_ref[...])
    o_ref[...] = _half_ffn_rows(x, m_ref, g_ref, wg_ref, wu_ref, wd_ref, 0)


def _mix_ffn_kernel(*refs, n_lat_tiles, final):
    if n_lat_tiles is None:
        (x_ref, m_ref, yp_ref, yg_ref, ya_ref, yf_ref, wo_ref,
         g_ref, wg_ref, wu_ref, wd_ref, fg_ref, o_ref) = refs
        yp, ya, yf = yp_ref[...], ya_ref[...], yf_ref[...]
    else:
        (x_ref, m_ref, yp_ref, yg_ref, ya_ref, yf_ref, cp_ref, ca_ref, cf_ref, wo_ref,
         g_ref, wg_ref, wu_ref, wd_ref, fg_ref, o_ref) = refs
        lat = pl.program_id(0) < n_lat_tiles
        yp = jnp.where(lat, yp_ref[...], cp_ref[...])
        ya = jnp.where(lat, ya_ref[...], ca_ref[...])
        yf = jnp.where(lat, yf_ref[...], cf_ref[...])
    acc = _dot(yp.astype(BF16), wo_ref[0:256, :])
    acc = acc + _dot(yg_ref[...].astype(BF16), wo_ref[256:512, :])
    acc = acc + _dot(ya.astype(BF16), wo_ref[512:768, :])
    acc = acc + _dot(yf.astype(BF16), wo_ref[768:1024, :])
    x = x_ref[...] + m_ref[5:6, :] * acc
    out = _half_ffn_rows(x, m_ref, g_ref, wg_ref, wu_ref, wd_ref, 6)
    if final:
        out = _rms(out) * fg_ref[...]
    o_ref[...] = out


def _mod_spec(layer, n_lat_tiles, tiles_per_batch, n_batch, d):
    def index_map(t):
        return (layer, jnp.where(t < n_lat_tiles, t // tiles_per_batch, n_batch), 0, 0)
    return pl.BlockSpec((None, None, N_MOD, d), index_map)


def _resident(arr, lead):
    block = (None,) * len(lead) + arr.shape[len(lead):]
    index = tuple(lead) + (0,) * (arr.ndim - len(lead))
    return pl.BlockSpec(block, lambda t: index, pipeline_mode=pl.Buffered(1))


def _ffn_weight_specs(g, wg, wu, wd, layer, half):
    d = g.shape[-1]
    lead = (layer, half)
    return [pl.BlockSpec((None, None, 1, d), lambda t: (layer, 2 * half, 0, 0)),
            _resident(wg, lead), _resident(wu, lead), _resident(wd, lead)]


def _half_ffn(x, mods, g, wg, wu, wd, *, layer, n_tiles, mod_map, ctx_rows=None):
    d = x.shape[1]
    n_lat = None if ctx_rows is None else x.shape[0] // TM
    if ctx_rows is None:
        rows, row_specs = [x], [pl.BlockSpec((TM, d), lambda t: (t, 0))]
    else:
        rows = [x, ctx_rows]
        row_specs = [pl.BlockSpec((TM, d), lambda t: (jnp.minimum(t, n_lat - 1), 0)),
                     pl.BlockSpec((TM, d), lambda t: (jnp.maximum(t - n_lat, 0), 0))]
    return pl.pallas_call(
        functools.partial(_ffn_kernel, n_lat_tiles=n_lat),
        out_shape=jax.ShapeDtypeStruct((n_tiles * TM, d), F32),
        grid=(n_tiles,),
        in_specs=row_specs + [mod_map] + _ffn_weight_specs(g, wg, wu, wd, layer, 0),
        out_specs=pl.BlockSpec((TM, d), lambda t: (t, 0)),
        compiler_params=_cparams(("arbitrary",)),
        name="half_ffn",
    )(*rows, mods, g, wg, wu, wd)


def _mix_ffn(x, mods, y_pool, y_gla, y_att, y_fnet, w_out, g, wg, wu, wd, fg,
             *, layer, n_tiles, mod_map, final, ctx_parts=None):
    d = x.shape[1]
    part = pl.BlockSpec((TM, 256), lambda t: (t, 0))
    n_lat = None
    parts, part_specs = [y_pool, y_gla, y_att, y_fnet], [part, part, part, part]
    if ctx_parts is not None:
        n_lat = y_pool.shape[0] // TM
        lat_part = pl.BlockSpec((TM, 256), lambda t: (jnp.minimum(t, n_lat - 1), 0))
        ctx_part = pl.BlockSpec((TM, 256), lambda t: (jnp.maximum(t - n_lat, 0), 0))
        parts += list(ctx_parts)
        part_specs = [lat_part, part, lat_part, lat_part, ctx_part, ctx_part, ctx_part]
    return pl.pallas_call(
        functools.partial(_mix_ffn_kernel, n_lat_tiles=n_lat, final=final),
        out_shape=jax.ShapeDtypeStruct((n_tiles * TM, d), F32),
        grid=(n_tiles,),
        in_specs=[pl.BlockSpec((TM, d), lambda t: (t, 0)), mod_map] + part_specs
                 + [_resident(w_out, (layer,))] + _ffn_weight_specs(g, wg, wu, wd, layer, 1)
                 + [pl.BlockSpec((1, d), lambda t: (0, 0))],
        out_specs=pl.BlockSpec((TM, d), lambda t: (t, 0)),
        compiler_params=_cparams(("arbitrary",)),
        name="mix_ffn",
    )(x, mods, *parts, w_out, g, wg, wu, wd, fg.reshape(1, d))


def _rope(x, c, sa, sb):
    w = x.shape[1]
    return x * c + pltpu.roll(x, w - ROPE_FREQS, 1) * sa + pltpu.roll(x, ROPE_FREQS, 1) * sb


def _tile_rope_table(lines, col):
    return jnp.concatenate([col + lines[r:r + 1, :] for r in range(lines.shape[0])], axis=0)


def _inproj_kernel(x_ref, m_ref, g_ref, w_ref, wa_ref, ba_ref, e_ref, qg_ref, kg_ref,
                   lc_ref, lsa_ref, lsb_ref, cc_ref, csa_ref, csb_ref,
                   up_ref, gq_ref, gk_ref, gv_ref, og_ref, gf_ref, gb_ref,
                   aq_ref, ak_ref, av_ref, uf_ref):
    shift = m_ref[3:4, :]
    scale = m_ref[4:5, :]
    e = e_ref[...]
    n_parts = 2
    rp = TM // n_parts
    lp = rp // GRID_W
    for part in range(n_parts):
        rows = slice(part * rp, (part + 1) * rp)
        lines = slice(part * lp, (part + 1) * lp)
        h = ((_rms(x_ref[rows, :]) * g_ref[...]) * (1.0 + scale) + shift).astype(BF16)
        p = _dot(h, w_ref[...])

        up_ref[rows, :] = p[:, O_POOL:O_POOL + POOL_W]
        uf_ref[rows, :] = p[:, O_FN:O_FN + FNET_W]

        gq_ref[rows, :] = p[:, O_GQ:O_GQ + GLA_KW] * (GLA_DK ** -0.5)
        gk_ref[rows, :] = p[:, O_GK:O_GK + GLA_KW]
        gv_ref[rows, :] = p[:, O_GV:O_GV + GLA_W]
        og_ref[rows, :] = p[:, O_OG:O_OG + GLA_W]
        z = _dot(p[:, O_R:O_R + 128].astype(BF16), wa_ref[...]) + ba_ref[...]
        logsig = jnp.minimum(z, 0.0) - jnp.log(1.0 + jnp.exp(-jnp.abs(z)))
        gdec = logsig * (1.0 / GLA_TAU)
        gf_ref[rows, :] = gdec[:, :GLA_KW]
        gb_ref[rows, :] = gdec[:, GLA_KW:]

        rc = _tile_rope_table(lc_ref[lines, :], cc_ref[...])
        rsa = _tile_rope_table(lsa_ref[lines, :], csa_ref[...])
        rsb = _tile_rope_table(lsb_ref[lines, :], csb_ref[...])
        q = _group_rms(p[:, O_AQ:O_AQ + ATT_W], e, ATT_HD) * qg_ref[...]
        q = _rope(q, rc, rsa, rsb) * (ATT_HD ** -0.5 * math.log2(math.e))
        aq_ref[:, rows] = q.T.astype(BF16)
        k = _group_rms(p[:, O_AK:O_AK + ATT_KW], e[:ATT_KW, :ATT_KW], ATT_HD) * kg_ref[...]
        k = _rope(k, rc[:, :ATT_KW], rsa[:, :ATT_KW], rsb[:, :ATT_KW])
        ak_ref[rows, :] = k.astype(BF16)
        v_t = p[:, O_AV:O_AV + ATT_KW].T
        ones = jnp.ones((ATT_VROWS - ATT_HD, rp), F32)
        for kvh in range(ATT_KVH):
            av_ref[kvh, :, rows] = jnp.concatenate(
                [v_t[kvh * ATT_HD:(kvh + 1) * ATT_HD, :], ones], axis=0).astype(BF16)


def _in_projection(x, mods, g, w_in, wa_blk, ba_blk, e256, qg, kg, rope_lines, rope_cols,
                   *, layer, n_tiles, mod_map, line_map, col_map):
    t_rows, d = x.shape
    row = lambda w: pl.BlockSpec((TM, w), lambda t: (t, 0))
    const = lambda shape: pl.BlockSpec(shape, lambda t: (0,) * len(shape))
    per_layer = lambda arr: pl.BlockSpec((None,) + arr.shape[1:], lambda t: (layer,) + (0,) * (arr.ndim - 1))
    out_shapes = (
        jax.ShapeDtypeStruct((t_rows, POOL_W), F32),
        jax.ShapeDtypeStruct((t_rows, GLA_KW), F32),
        jax.ShapeDtypeStruct((t_rows, GLA_KW), F32),
        jax.ShapeDtypeStruct((t_rows, GLA_W), F32),
        jax.ShapeDtypeStruct((t_rows, GLA_W), F32),
        jax.ShapeDtypeStruct((t_rows, GLA_KW), F32),
        jax.ShapeDtypeStruct((t_rows, GLA_KW), F32),
        jax.ShapeDtypeStruct((n_tiles, ATT_W, TM), BF16),
        jax.ShapeDtypeStruct((t_rows, ATT_KW), BF16),
        jax.ShapeDtypeStruct((n_tiles, ATT_KVH, ATT_VROWS, TM), BF16),
        jax.ShapeDtypeStruct((t_rows, FNET_W), F32),
    )
    out_specs = (
        row(POOL_W), row(GLA_KW), row(GLA_KW), row(GLA_W), row(GLA_W), row(GLA_KW), row(GLA_KW),
        pl.BlockSpec((None, ATT_W, TM), lambda t: (t, 0, 0)),
        row(ATT_KW),
        pl.BlockSpec((None, ATT_KVH, ATT_VROWS, TM), lambda t: (t, 0, 0, 0)),
        row(FNET_W),
    )
    return pl.pallas_call(
        _inproj_kernel,
        out_shape=out_shapes,
        grid=(n_tiles,),
        in_specs=[
            pl.BlockSpec((TM, d), lambda t: (t, 0)),
            mod_map,
            pl.BlockSpec((None, None, 1, d), lambda t: (layer, 1, 0, 0)),
            per_layer(w_in),
            per_layer(wa_blk),
            per_layer(ba_blk),
            const(e256.shape),
            per_layer(qg),
            per_layer(kg),
        ] + [pl.BlockSpec((TM // GRID_W, ATT_W), line_map)] * 3
          + [pl.BlockSpec((None, GRID_W, ATT_W), col_map)] * 3,
        out_specs=out_specs,
        compiler_params=_cparams(("arbitrary",)),
        name="in_projection",
    )(x, mods, g, w_in, wa_blk, ba_blk, e256, qg, kg, *rope_lines, *rope_cols)


def _pool_kernel(u_ref, inv_ref, w_ref, s_ref, o_ref, pad_ref, *, seq, chunk):
    halo = POOL_HALO
    pad_ref[0:halo, :] = jnp.zeros((halo, POOL_W), F32)
    pad_ref[halo + seq:halo + seq + halo, :] = jnp.zeros((halo, POOL_W), F32)
    pad_ref[halo:halo + seq, :] = u_ref[...]
    rows = chunk + 2 * halo
    lane = lax.broadcasted_iota(jnp.int32, (chunk, POOL_W), 1)

    def body(ci, carry):
        c0 = pl.multiple_of(ci * chunk, chunk)
        xp = pad_ref[pl.ds(c0, rows), :]
        u = xp[halo:halo + chunk, :]
        acc = xp
        wsum = None
        for gi, w in enumerate(POOL_WINDOWS):
            acc = acc + pltpu.roll(acc, w // 2, 0)
            lead = w // 2 - 1
            win = acc if lead == 0 else pltpu.roll(acc, rows - lead, 0)
            win = win[halo:halo + chunk, :]
            wsum = win if wsum is None else jnp.where(lane >= gi * POOL_GW, win, wsum)
        m = wsum * inv_ref[pl.ds(c0, chunk), :] - u
        y = _dot(m.astype(BF16), w_ref[...]) * s_ref[...]
        o_ref[pl.ds(c0, chunk), :] = y
        return carry

    lax.fori_loop(0, seq // chunk, body, 0)


def _pool_inverse_counts(seq):
    t = jnp.arange(seq, dtype=jnp.int32)
    cols = []
    for w in POOL_WINDOWS:
        cnt = jnp.minimum(t + w // 2, seq) - jnp.maximum(t - w // 2, 0)
        cols.append(jnp.broadcast_to((1.0 / cnt.astype(F32))[:, None], (seq, POOL_GW)))
    return jnp.concatenate(cols, axis=1)


def _pool_mix(u, inv_cnt, w_blk, s_pool, *, layer, seq, n_seq, first_block):
    chunk = min(256, seq)
    return pl.pallas_call(
        functools.partial(_pool_kernel, seq=seq, chunk=chunk),
        out_shape=jax.ShapeDtypeStruct((n_seq * seq, POOL_W), F32),
        grid=(n_seq,),
        in_specs=[
            pl.BlockSpec((seq, POOL_W), lambda b: (first_block + b, 0)),
            pl.BlockSpec((seq, POOL_W), lambda b: (0, 0)),
            pl.BlockSpec((None, POOL_W, POOL_W), lambda b: (layer, 0, 0)),
            pl.BlockSpec((None, 1, POOL_W), lambda b: (layer, 0, 0)),
        ],
        out_specs=pl.BlockSpec((seq, POOL_W), lambda b: (b, 0)),
        scratch_shapes=[pltpu.VMEM((seq + 2 * POOL_HALO, POOL_W), F32)],
        compiler_params=_cparams(("arbitrary",)),
        name="pool_mix",
    )(u, inv_cnt, w_blk, s_pool)


def _gla_kernel(*refs, reverse, finalize):
    if finalize:
        (q_ref, k_ref, v_ref, g_ref, ex_ref, mk_ref, bm_ref, of_ref, og_ref, gn_ref, e_ref,
         o_ref, st_ref, ob_ref) = refs
    else:
        q_ref, k_ref, v_ref, g_ref, ex_ref, mk_ref, bm_ref, o_ref, st_ref = refs
        ob_ref = o_ref

    @pl.when(pl.program_id(1) == 0)
    def _():
        st_ref[...] = jnp.zeros(st_ref.shape, F32)

    g = g_ref[...]
    trow = lax.broadcasted_iota(jnp.int32, (TP, GLA_KW), 0) % GLA_CHUNK
    pre, suf = g, g
    s = 1
    while s < GLA_CHUNK:
        pre = pre + jnp.where(trow >= s, pltpu.roll(pre, s, 0), 0.0)
        suf = suf + jnp.where(trow < GLA_CHUNK - s, pltpu.roll(suf, TP - s, 0), 0.0)
        s *= 2
    tot = pre + suf - g
    in_range = jnp.max(jnp.abs(tot)) <= GLA_FAST_RANGE

    @pl.when(in_range)
    def _():
        _gla_fast_tile(q_ref, k_ref, v_ref, mk_ref, bm_ref, st_ref, ob_ref,
                       suf if reverse else pre, tot, reverse)

    @pl.when(jnp.logical_not(in_range))
    def _():
        _gla_exact_tile(q_ref, k_ref, v_ref, g_ref, ex_ref, mk_ref, st_ref, ob_ref, reverse)

    if finalize:
        both = ob_ref[...] + of_ref[...]
        y = _group_rms(both, e_ref[...], GLA_DV) * gn_ref[...]
        o_ref[...] = y * _silu(og_ref[...])


def _gla_fast_tile(q_ref, k_ref, v_ref, mk_ref, bm_ref, st_ref, ob_ref, b, tot, reverse):
    ch = GLA_CHUNK
    half = 0.5 * tot
    q, k = q_ref[...], k_ref[...]
    q_in = (q * jnp.exp(b - half)).astype(BF16)
    k_in = (k * jnp.exp(half - b)).astype(BF16)
    q_st = (q * jnp.exp(b)).astype(BF16)
    k_st = (k * jnp.exp(tot - b)).astype(BF16)
    dec = jnp.exp(tot)
    mk = mk_ref[...]
    mk16 = mk.astype(BF16)
    bm16 = bm_ref[...]
    irow = lax.broadcasted_iota(jnp.int32, (ch, GLA_HEADS * ch), 0)
    jcol = lax.broadcasted_iota(jnp.int32, (ch, GLA_HEADS * ch), 1) % ch
    seen = (jcol >= irow) if reverse else (jcol <= irow)
    nt = (((1,), (1,)), ((), ()))
    st = st_ref[...]
    n_chunks = TP // ch
    for ci in range(n_chunks):
        c = (n_chunks - 1 - ci) if reverse else ci
        rows = slice(c * ch, (c + 1) * ch)
        kb = jnp.concatenate([k_in[rows]] * GLA_HEADS, axis=0) * mk16
        a = lax.dot_general(q_in[rows], kb, nt, preferred_element_type=F32)
        a = jnp.where(seen, a, 0.0).astype(BF16)
        v = v_ref[rows, :]
        vb = jnp.concatenate([v.astype(BF16)] * GLA_HEADS, axis=0) * bm16
        o = _dot(a, vb) + lax.dot_general(q_st[rows], st.astype(BF16), nt, preferred_element_type=F32)
        kv = lax.dot_general(v.astype(BF16), k_st[rows], (((0,), (0,)), ((), ())),
                             preferred_element_type=F32)
        st = st * dec[c * ch:c * ch + 1, :] + kv * mk
        ob_ref[rows, :] = o
    st_ref[...] = st


def _gla_exact_tile(q_ref, k_ref, v_ref, g_ref, ex_ref, mk_ref, st_ref, ob_ref, reverse):
    sub = GLA_SUB
    n_chunks = TP // sub
    row = lax.broadcasted_iota(jnp.int32, (sub, GLA_KW), 0)
    ex = ex_ref[...]
    mk = mk_ref[...]

    def body(ci, carry):
        c = (n_chunks - 1 - ci) if reverse else ci
        r0 = pl.multiple_of(c * sub, sub)
        q = q_ref[pl.ds(r0, sub), :]
        k = k_ref[pl.ds(r0, sub), :]
        v = v_ref[pl.ds(r0, sub), :]
        b = g_ref[pl.ds(r0, sub), :]
        s = 1
        while s < sub:
            if reverse:
                b = b + jnp.where(row < sub - s, pltpu.roll(b, sub - s, 0), 0.0)
            else:
                b = b + jnp.where(row >= s, pltpu.roll(b, s, 0), 0.0)
            s *= 2
        edge = b[0:1, :] if reverse else b[sub - 1:sub, :]
        st = st_ref[...]
        qd = (q * jnp.exp(b)).astype(BF16)
        o = lax.dot_general(qd, st.astype(BF16), (((1,), (1,)), ((), ())),
                            preferred_element_type=F32)
        parts = []
        for j in range(sub):
            seen = (row <= j) if reverse else (row >= j)
            dec = jnp.exp(jnp.where(seen, b - b[j:j + 1, :], NEG_BIG))
            parts.append((dec * q * k[j:j + 1, :]).astype(BF16))
        r = _dot(jnp.concatenate(parts, axis=0), ex)
        for j in range(sub):
            o = o + r[j * sub:(j + 1) * sub, :] * v[j:j + 1, :]
        kd = (k * jnp.exp(edge - b)).astype(BF16)
        kv = lax.dot_general(v.astype(BF16), kd, (((0,), (0,)), ((), ())),
                             preferred_element_type=F32)
        st_ref[...] = st * jnp.exp(edge) + kv * mk
        ob_ref[pl.ds(r0, sub), :] = o
        return carry

    lax.fori_loop(0, n_chunks, body, 0, unroll=2)


def _gla_tile_map(n_batch, n_lat, n_ctx, reverse):
    def index_map(b, s):
        is_ctx = s < n_ctx
        sc = jnp.where(is_ctx, s, 0)
        sl = jnp.where(is_ctx, 0, s - n_ctx)
        if reverse:
            sc = n_ctx - 1 - sc
            sl = n_lat - 1 - sl
        return (jnp.where(is_ctx, n_batch * n_lat + b * n_ctx + sc, b * n_lat + sl), 0)
    return index_map


def _gla_mix(gq, gk, gv, gf, gb, og, ex, mk, gn, e256, *, n_batch, seq, ctx):
    t_rows = gq.shape[0]
    n_lat, n_ctx = seq // TP, ctx // TP
    grid = (n_batch, n_lat + n_ctx)
    const = lambda shape: pl.BlockSpec(shape, lambda b, s: (0,) * len(shape))

    def run(reverse, finalize, gate, extra):
        tmap = _gla_tile_map(n_batch, n_lat, n_ctx, reverse)
        rows = lambda w: pl.BlockSpec((TP, w), tmap)
        in_specs = [rows(GLA_KW), rows(GLA_KW), rows(GLA_W), rows(GLA_KW), const(ex.shape), const(mk.shape),
                    const(e256.shape)]
        args = [gq, gk, gv, gate, ex, mk, e256]
        scratch = [pltpu.VMEM((GLA_W, GLA_KW), F32)]
        if finalize:
            in_specs += [rows(GLA_W), rows(GLA_W), const((1, GLA_W)), const(e256.shape)]
            args += extra
            scratch.append(pltpu.VMEM((TP, GLA_W), F32))
        return pl.pallas_call(
            functools.partial(_gla_kernel, reverse=reverse, finalize=finalize),
            out_shape=jax.ShapeDtypeStruct((t_rows, GLA_W), F32),
            grid=grid,
            in_specs=in_specs,
            out_specs=rows(GLA_W),
            scratch_shapes=scratch,
            compiler_params=_cparams(("arbitrary", "arbitrary")),
            name="gla_bwd" if reverse else "gla_fwd",
        )(*args)

    o_f = run(False, False, gf, None)
    return run(True, True, gb, [o_f, og, gn, e256])


def _att_query_groups(q_t):
    tq = q_t.shape[1]
    zero = jnp.zeros((ATT_HD, tq), BF16)
    groups = []
    for g in range(ATT_KVH):
        halves = []
        for h in (2 * g, 2 * g + 1):
            qh = q_t[h * ATT_HD:(h + 1) * ATT_HD, :]
            halves.append(jnp.concatenate([qh, zero] if g == 0 else [zero, qh], axis=0))
        groups.append(jnp.concatenate(halves, axis=1))
    return groups


def _att_store(o_ref, weighted, denom, tq):
    heads = []
    for g in range(ATT_KVH):
        og = weighted[g] / denom[g]
        heads += [og[:, :tq], og[:, tq:]]
    o_ref[...] = jnp.concatenate(heads, axis=0).T.astype(o_ref.dtype)


def _att_kernel(*refs, n_main, has_ctx, bounded):
    refs = list(refs)
    q_ref, km_ref, vm_ref = refs.pop(0), refs.pop(0), refs.pop(0)
    kc_ref, vc_ref = (refs.pop(0), refs.pop(0)) if has_ctx else (None, None)
    o_ref, s0_ref, s1_ref = refs
    q_t = q_ref[...]
    tq = q_t.shape[1]
    tk = vm_ref.shape[3]
    groups = _att_query_groups(q_t)

    def scores(kblk, buf):
        for g in range(ATT_KVH):
            buf[g, 0:kblk.shape[0], :] = _dot(kblk, groups[g])

    def consume(buf, rows, v_of, carry):
        out = []
        for g in range(ATT_KVH):
            m, l, acc = carry[g]
            s = buf[g, 0:rows, :]
            if bounded:
                pv = _dot(v_of(g), jnp.exp2(s).astype(BF16))
                out.append((m, l + pv[ATT_HD:ATT_HD + 1, :], acc + pv[:ATT_HD, :]))
                continue
            m_new = jnp.maximum(m, jnp.max(s, axis=0, keepdims=True))
            alpha = jnp.exp2(m - m_new)
            p = jnp.exp2(s - m_new).astype(BF16)
            pv = _dot(v_of(g), p)
            acc = alpha * acc + pv[:ATT_HD, :]
            l = alpha * l + pv[ATT_HD:ATT_HD + 1, :]
            out.append((m_new, l, acc))
        return tuple(out)

    def k_main(j):
        return km_ref[pl.ds(pl.multiple_of(j * tk, tk), tk), :]

    def v_main(j):
        return lambda g: vm_ref[j, g]

    def pair(j, carry, issue_next):
        scores(k_main(j + 1), s1_ref)
        carry = consume(s0_ref, tk, v_main(j), carry)
        issue_next(j + 2)
        return consume(s1_ref, tk, v_main(j + 1), carry)

    carry = tuple((jnp.full((1, 2 * tq), -jnp.inf, F32), jnp.zeros((1, 2 * tq), F32),
                   jnp.zeros((ATT_HD, 2 * tq), F32)) for _ in range(ATT_KVH))
    next_main = lambda j: scores(k_main(j), s0_ref)
    if has_ctx:
        after_main = lambda j: scores(kc_ref[...], s0_ref)
    else:
        after_main = lambda j: None

    scores(k_main(0), s0_ref)
    if n_main > 1:
        pairs_per_trip = 2 if n_main % 4 == 0 else 1

        def body(i, carry):
            for u in range(pairs_per_trip):
                carry = pair(2 * (pairs_per_trip * i + u), carry, next_main)
            return carry

        n_trips = n_main // (2 * pairs_per_trip)
        carry = lax.fori_loop(0, n_trips - 1, body, carry)
        for u in range(pairs_per_trip):
            last = u == pairs_per_trip - 1
            carry = pair(2 * (pairs_per_trip * (n_trips - 1) + u), carry, after_main if last else next_main)
    else:
        assert not has_ctx
        carry = consume(s0_ref, tk, v_main(0), carry)
    if has_ctx:
        carry = consume(s0_ref, kc_ref.shape[0], lambda g: vc_ref[g], carry)
    _att_store(o_ref, [c[2] for c in carry], [c[1] for c in carry], tq)


def _attention(aq, ak, av, *, n_batch, seq, ctx, ctx_queries, bounded=False):
    lat_tiles = seq // TM
    ctx_tile = n_batch * lat_tiles
    ctx_cols = lambda b, i: (ctx_tile + (b * ctx) // TM, 0, ((b * ctx) % TM) // ctx)
    ctx_vcols = lambda b, i: (ctx_tile + (b * ctx) // TM, 0, 0, ((b * ctx) % TM) // ctx)
    scratch = []
    if not ctx_queries:
        sub = TM // TQ
        grid = (n_batch, seq // TQ)
        q_map = lambda b, i: (b * lat_tiles + i // sub, 0, i % sub)
        in_specs = [
            pl.BlockSpec((None, ATT_W, TQ), q_map),
            pl.BlockSpec((seq, ATT_KW), lambda b, i: (b, 0)),
            pl.BlockSpec((lat_tiles, ATT_KVH, ATT_VROWS, TM), lambda b, i: (b, 0, 0, 0)),
            pl.BlockSpec((ctx, ATT_KW), lambda b, i: (n_batch * seq // ctx + b, 0)),
            pl.BlockSpec((None, ATT_KVH, ATT_VROWS, ctx), ctx_vcols),
        ]
        out_spec = pl.BlockSpec((TQ, ATT_W), lambda b, i: (b * (seq // TQ) + i, 0))
        assert lat_tiles % 2 == 0
        out_rows = n_batch * seq
        args = (aq, ak, av, ak, av)
        kern = functools.partial(_att_kernel, n_main=lat_tiles, has_ctx=True, bounded=bounded)
        s_shape = (ATT_KVH, TM, 2 * TQ)
    else:
        grid = (n_batch, 1)
        in_specs = [
            pl.BlockSpec((None, ATT_W, ctx), ctx_cols),
            pl.BlockSpec((ctx, ATT_KW), lambda b, i: (n_batch * seq // ctx + b, 0)),
            pl.BlockSpec((1, ATT_KVH, ATT_VROWS, ctx), ctx_vcols),
        ]
        out_spec = pl.BlockSpec((ctx, ATT_W), lambda b, i: (b, 0))
        kern = functools.partial(_att_kernel, n_main=1, has_ctx=False, bounded=False)
        out_rows = n_batch * ctx
        args = (aq, ak, av)
        s_shape = (ATT_KVH, ctx, 2 * ctx)
    return pl.pallas_call(
        kern,
        out_shape=jax.ShapeDtypeStruct((out_rows, ATT_W), BF16),
        grid=grid,
        in_specs=in_specs,
        out_specs=out_spec,
        scratch_shapes=[pltpu.VMEM(s_shape, F32), pltpu.VMEM(s_shape, F32)],
        compiler_params=_cparams(("arbitrary", "arbitrary")),
        name="attention_ctx" if ctx_queries else ("attention_bounded" if bounded else "attention"),
    )(*args)


def _fft_a_kernel(x_ref, dh_ref, dl_ref, tr_ref, ti_ref, yr_ref, yi_ref):
    rb, w = x_ref.shape[1], x_ref.shape[2]
    x = jnp.concatenate([x_ref[:, r, :] for r in range(rb)], axis=1)
    x_hi, x_lo = _split(x)
    y = _dot3(dh_ref[...], dl_ref[...], x_hi, x_lo)
    tr, ti = tr_ref[...], ti_ref[...]
    for r in range(rb):
        yr, yi = y[:FFT_N1, r * w:(r + 1) * w], y[FFT_N1:, r * w:(r + 1) * w]
        c, s = tr[:, r:r + 1], ti[:, r:r + 1]
        yr_ref[:, r, :] = yr * c - yi * s
        yi_ref[:, r, :] = yr * s + yi * c


def _channel_mix(xr, xi, ch_ref, cl_ref, wf_ref, norm):
    xc = jnp.concatenate([xr, xi], axis=1)
    x_hi, x_lo = _split(xc)
    f = _dot3(x_hi, x_lo, ch_ref[...], cl_ref[...]) * norm
    return _dot(f.astype(BF16), wf_ref[...])


def _fft_c_kernel(yr_ref, yi_ref, mh_ref, ml_ref, ch_ref, cl_ref, wf_ref, o_ref, *, n2, norm):
    w = yr_ref.shape[1]
    blk = jnp.concatenate(
        [jnp.concatenate([yr_ref[j * n2:(j + 1) * n2, :] for j in range(8)], axis=1),
         jnp.concatenate([yi_ref[j * n2:(j + 1) * n2, :] for j in range(8)], axis=1)], axis=0)
    b_hi, b_lo = _split(blk)
    x = _dot3(mh_ref[...], ml_ref[...], b_hi, b_lo)
    xr = jnp.concatenate([x[:n2, j * w:(j + 1) * w] for j in range(8)], axis=0)
    xi = jnp.concatenate([x[n2:, j * w:(j + 1) * w] for j in range(8)], axis=0)
    y = _channel_mix(xr, xi, ch_ref, cl_ref, wf_ref, norm)
    for j in range(8):
        o_ref[:, j, :] = y[j * n2:(j + 1) * n2, :]


def _fft_dense_kernel(x_ref, dh_ref, dl_ref, ch_ref, cl_ref, wf_ref, o_ref, *, n, norm):
    x_hi, x_lo = _split(x_ref[...])
    x = _dot3(dh_ref[...], dl_ref[...], x_hi, x_lo)
    o_ref[...] = _channel_mix(x[:n, :], x[n:, :], ch_ref, cl_ref, wf_ref, norm)


def _dft_parts(n):
    idx = np.arange(n)
    ang = 2.0 * np.pi * ((idx[:, None] * idx[None, :]) % n) / n
    return np.cos(ang), np.sin(ang)


def _np_split(a):
    a = jnp.asarray(a, F32)
    return _split(a)


def _fnet_consts(seq):
    n1, n2 = FFT_N1, seq // FFT_N1
    c1, s1 = _dft_parts(n1)
    da = np.concatenate([c1, -s1], axis=0)
    k1 = np.arange(n1)[:, None]
    m2 = np.arange(n2)[None, :]
    ang = 2.0 * np.pi * ((k1 * m2) % seq) / seq
    tw_r, tw_i = np.cos(ang), -np.sin(ang)
    c2, s2 = _dft_parts(n2)
    mc = np.block([[c2, s2], [-s2, c2]])
    return da, tw_r, tw_i, mc


def _channel_consts():
    cc, sc = _dft_parts(FNET_HD)
    eye = np.eye(FNET_HEADS)
    return np.concatenate([np.kron(eye, cc), np.kron(eye, sc)], axis=0)


def _fnet_latent(uf3, wf, *, layer, n_batch, seq):
    n1, n2 = FFT_N1, seq // FFT_N1
    da, tw_r, tw_i, mc = _fnet_consts(seq)
    dh, dl = _np_split(da)
    mh, ml = _np_split(mc)
    ch, cl = _np_split(_channel_consts())
    rb = min(n2, 32)
    blocked = lambda tw: jnp.asarray(tw.reshape(n1, n2 // rb, rb).transpose(1, 0, 2), F32)
    const2 = lambda shape: pl.BlockSpec(shape, lambda b, j: (0,) * len(shape))
    slab = pl.BlockSpec((n1, rb, FNET_W), lambda b, j: (b, j, 0))
    yr, yi = pl.pallas_call(
        _fft_a_kernel,
        out_shape=(jax.ShapeDtypeStruct((n_batch * n1, n2, FNET_W), F32),) * 2,
        grid=(n_batch, n2 // rb),
        in_specs=[
            slab,
            const2(dh.shape), const2(dl.shape),
            pl.BlockSpec((None, n1, rb), lambda b, j: (j, 0, 0)),
            pl.BlockSpec((None, n1, rb), lambda b, j: (j, 0, 0)),
        ],
        out_specs=(slab, slab),
        compiler_params=_cparams(("arbitrary", "arbitrary")),
        name="fft_stage_a",
    )(uf3, dh, dl, blocked(tw_r), blocked(tw_i))
    yr = yr.reshape(n_batch * n1 * n2, FNET_W)
    yi = yi.reshape(n_batch * n1 * n2, FNET_W)
    norm = 1.0 / math.sqrt(seq * FNET_HD)
    out = pl.pallas_call(
        functools.partial(_fft_c_kernel, n2=n2, norm=norm),
        out_shape=jax.ShapeDtypeStruct((n_batch, n2, n1, FNET_W), F32),
        grid=(n_batch, n1 // 8),
        in_specs=[
            pl.BlockSpec((8 * n2, FNET_W), lambda b, i: (b * (n1 // 8) + i, 0)),
            pl.BlockSpec((8 * n2, FNET_W), lambda b, i: (b * (n1 // 8) + i, 0)),
            const2(mh.shape), const2(ml.shape), const2(ch.shape), const2(cl.shape),
            pl.BlockSpec((None,) + wf.shape[1:], lambda b, i: (layer, 0, 0)),
        ],
        out_specs=pl.BlockSpec((None, n2, 8, FNET_W), lambda b, i: (b, 0, i, 0)),
        compiler_params=_cparams(("arbitrary", "arbitrary")),
        name="fft_stage_c",
    )(yr, yi, mh, ml, ch, cl, wf)
    return out.reshape(n_batch * seq, FNET_W)


def _fnet_context(uf, wf, *, layer, n_batch, ctx, first_block):
    c, s = _dft_parts(ctx)
    dh, dl = _np_split(np.concatenate([c, -s], axis=0))
    ch, cl = _np_split(_channel_consts())
    const = lambda shape: pl.BlockSpec(shape, lambda b: (0,) * len(shape))
    return pl.pallas_call(
        functools.partial(_fft_dense_kernel, n=ctx, norm=1.0 / math.sqrt(ctx * FNET_HD)),
        out_shape=jax.ShapeDtypeStruct((n_batch * ctx, FNET_W), F32),
        grid=(n_batch,),
        in_specs=[pl.BlockSpec((ctx, FNET_W), lambda b: (first_block + b, 0)),
                  const(dh.shape), const(dl.shape), const(ch.shape), const(cl.shape),
                  pl.BlockSpec((None,) + wf.shape[1:], lambda b: (layer, 0, 0))],
        out_specs=pl.BlockSpec((ctx, FNET_W), lambda b: (b, 0)),
        compiler_params=_cparams(("arbitrary",)),
        name="fft_context",
    )(uf, dh, dl, ch, cl, wf)


def _rope_tables(seq):
    freqs = ROPE_THETA ** (-jnp.arange(ROPE_FREQS, dtype=F32) / ROPE_FREQS)
    row_ang = jnp.arange(seq // GRID_W, dtype=F32)[:, None] * freqs
    col_ang = jnp.arange(GRID_W, dtype=F32)[:, None] * freqs
    lanes = np.arange(ATT_HD)
    is_row = jnp.asarray(np.tile(lanes < 2 * ROPE_FREQS, ATT_QH)[None, :], F32)
    first_half = jnp.asarray(np.tile((lanes // ROPE_FREQS) % 2 == 0, ATT_QH)[None, :], F32)
    spread = lambda a: jnp.tile(a, (1, ATT_W // ROPE_FREQS))

    def tables(ang, mask, ident_lines):
        cos, sin = spread(jnp.cos(ang)) * mask, spread(jnp.sin(ang)) * mask
        ident = jnp.zeros((ident_lines, ATT_W), F32)
        return (jnp.concatenate([cos, ident + mask], axis=0),
                jnp.concatenate([-sin * first_half, ident], axis=0),
                jnp.concatenate([sin * (1.0 - first_half), ident], axis=0))

    rows = tables(row_ang, is_row, TM // GRID_W)
    cols = tuple(t.reshape(2, GRID_W, ATT_W) for t in tables(col_ang, 1.0 - is_row, GRID_W))
    return rows, cols


def _block_ones(width, group):
    return jnp.asarray(np.kron(np.eye(width // group), np.ones((group, group))), BF16)


def kernel(x, c, ctx, c_ctx, w_mod, b_mod, norm_g, ffn_wg, ffn_wu, ffn_wd, w_in, w_out,
           pool_w, pool_scale, gla_wa, gla_ba, gla_norm, att_qnorm, att_knorm, fnet_w, final_norm):
    n_batch, seq, d = x.shape
    n_ctx = ctx.shape[1]
    assert d == D_MODEL and seq % TM == 0 and (n_batch * n_ctx) % TM == 0 and n_ctx % TP == 0
    assert seq % (8 * FFT_N1) == 0 and n_ctx <= TM and TM % n_ctx == 0 and n_batch + 1 <= 8
    lat_rows = n_batch * seq
    lat_tiles = lat_rows // TM
    all_tiles = lat_tiles + (n_batch * n_ctx) // TM
    tiles_per_batch = seq // TM
    line_map = lambda t: (jnp.where(t < lat_tiles, t % tiles_per_batch, tiles_per_batch), 0)
    col_map = lambda t: (jnp.where(t < lat_tiles, 0, 1), 0, 0)

    cvec = jnp.concatenate([c, c_ctx[None, :], jnp.zeros((8 - n_batch - 1, d), F32)], axis=0)
    mods = _modulation(cvec, w_mod, b_mod).reshape(DEPTH, 8, N_MOD, d)

    rope_lines, rope_cols = _rope_tables(seq)
    e256 = _block_ones(ATT_W, ATT_HD)
    gla_ex = _block_ones(GLA_W, GLA_DV)[::2, :]
    gla_mk = jnp.asarray(np.kron(np.eye(GLA_HEADS), np.ones((GLA_DV, GLA_DK))), F32)

    wg, wu, wd = ffn_wg.astype(BF16), ffn_wu.astype(BF16), ffn_wd.astype(BF16)
    wi = jnp.concatenate([w_in[..., :768], w_in[..., 800:D_IN], w_in[..., 768:800],
                          jnp.zeros((DEPTH, d, D_IN_PAD - D_IN), F32)], axis=-1).astype(BF16)
    wa_blk = jnp.zeros((DEPTH, 128, 2 * GLA_KW), F32)
    wa_blk = wa_blk.at[:, :GLA_RANK, :GLA_KW].set(gla_wa[:, 0]).at[:, GLA_RANK:2 * GLA_RANK, GLA_KW:].set(gla_wa[:, 1])
    wa_blk = wa_blk.astype(BF16)
    ba_blk = gla_ba.reshape(DEPTH, 1, 2 * GLA_KW)
    qg = jnp.tile(att_qnorm, (1, ATT_QH))[:, None, :]
    kg = jnp.tile(att_knorm, (1, ATT_KVH))[:, None, :]
    pool_blk = jnp.zeros((DEPTH, POOL_W, POOL_W), F32)
    for gi in range(len(POOL_WINDOWS)):
        sl = slice(gi * POOL_GW, (gi + 1) * POOL_GW)
        pool_blk = pool_blk.at[:, sl, sl].set(pool_w[:, gi])
    pool_blk = pool_blk.astype(BF16)
    pool_s = pool_scale[:, None, :]
    wf = fnet_w.astype(BF16)
    wo = w_out.astype(BF16)
    norm4 = norm_g[:, :, None, :]
    n2 = seq // FFT_N1
    inv_lat = _pool_inverse_counts(seq)

    xs = x.reshape(lat_rows, d)
    ctx_rows = ctx.reshape(n_batch * n_ctx, d)
    for i in range(DEPTH):
        ctx_out = i < DEPTH - 1
        mod_map = _mod_spec(i, lat_tiles, tiles_per_batch, n_batch, d)
        gn = jnp.tile(gla_norm[i], GLA_HEADS)[None, :]

        xs = _half_ffn(xs, mods, norm4, wg, wu, wd, layer=i, n_tiles=all_tiles,
                       mod_map=mod_map, ctx_rows=ctx_rows if i == 0 else None)
        (u_pool, gq, gk, gv, og, gf, gb, aq, ak, av, uf) = _in_projection(
            xs, mods, norm4, wi, wa_blk, ba_blk, e256, qg, kg, rope_lines, rope_cols,
            layer=i, n_tiles=all_tiles, mod_map=mod_map, line_map=line_map, col_map=col_map)

        y_pool = _pool_mix(u_pool, inv_lat, pool_blk, pool_s, layer=i, seq=seq, n_seq=n_batch, first_block=0)
        y_gla = _gla_mix(gq, gk, gv, gf, gb, og, gla_ex, gla_mk, gn, e256, n_batch=n_batch, seq=seq, ctx=n_ctx)
        score_bound = (ATT_HD * ATT_HD ** -0.5 * math.log2(math.e) * ATT_BOUND_SLACK
                       * jnp.max(jnp.abs(att_qnorm[i])) * jnp.max(jnp.abs(att_knorm[i])))
        attend = functools.partial(_attention, aq, ak, av, n_batch=n_batch, seq=seq, ctx=n_ctx, ctx_queries=False)
        y_att = lax.cond(score_bound <= ATT_BOUND_MAX, lambda: attend(bounded=True), lambda: attend())
        y_fnet = _fnet_latent(uf.reshape(uf.shape[0] // n2, n2, FNET_W), wf, layer=i, n_batch=n_batch, seq=seq)
        n_tiles, ctx_parts = lat_tiles, None
        if ctx_out:
            first_ctx = lat_rows // n_ctx
            ctx_parts = (
                _pool_mix(u_pool, _pool_inverse_counts(n_ctx), pool_blk, pool_s, layer=i, seq=n_ctx, n_seq=n_batch,
                          first_block=first_ctx),
                _attention(aq, ak, av, n_batch=n_batch, seq=seq, ctx=n_ctx, ctx_queries=True),
                _fnet_context(uf, wf, layer=i, n_batch=n_batch, ctx=n_ctx, first_block=first_ctx),
            )
            n_tiles = all_tiles
        xs = _mix_ffn(xs, mods, y_pool, y_gla, y_att, y_fnet, wo, norm4, wg, wu, wd, final_norm,
                      layer=i, n_tiles=n_tiles, mod_map=mod_map, final=not ctx_out, ctx_parts=ctx_parts)
    return xs.reshape(n_batch, seq, d)
```

```python
import functools
import math

import jax
import jax.numpy as jnp
import numpy as np
from jax import lax
from jax.experimental import pallas as pl
from jax.experimental.pallas import tpu as pltpu

F32 = jnp.float32
BF16 = jnp.bfloat16

D_MODEL = 1024
DEPTH = 2
GRID_W = 64
EPS = 1e-6
N_MOD = 9
D_FF = 2816

POOL_W = 256
POOL_WINDOWS = (2, 4, 8, 16)
POOL_GW = 64
POOL_HALO = 8

GLA_HEADS = 4
GLA_W = 256
GLA_DV = 64
GLA_DK = 32
GLA_RANK = 16
GLA_TAU = 16.0
GLA_KW = GLA_HEADS * GLA_DK
GLA_SUB = 16
GLA_CHUNK = 64
GLA_FAST_RANGE = 150.0

ATT_W = 256
ATT_HD = 64
ATT_QH = 4
ATT_KVH = 2
ATT_KW = ATT_KVH * ATT_HD
ATT_VROWS = ATT_HD + 16
ATT_BOUND_SLACK = 1.02
ATT_BOUND_MAX = 50.0
ROPE_FREQS = 16
ROPE_THETA = 10000.0

FNET_W = 256
FNET_HEADS = 4
FNET_HD = 64
FFT_N1 = 64

D_IN = 1824
D_IN_PAD = 1920

O_POOL, O_GQ, O_GK, O_GV, O_OG, O_AQ, O_AK, O_AV, O_FN, O_R = 0, 256, 384, 512, 768, 1024, 1280, 1408, 1536, 1792

TM = 512
TQ = 256
TP = 256
MXU_TILE = 256
FF_CHUNKS = ((0, 6 * MXU_TILE), (6 * MXU_TILE, D_FF))
NEG_BIG = -1e30

VMEM_LIMIT = 56 * 1024 * 1024


def _cparams(sem):
    return pltpu.CompilerParams(dimension_semantics=sem, vmem_limit_bytes=VMEM_LIMIT)


def _dot(a, b):
    return jnp.dot(a, b, preferred_element_type=F32)


def _rms(x):
    return x * lax.rsqrt(jnp.mean(x * x, axis=-1, keepdims=True) + EPS)


def _silu(x):
    return x * jax.nn.sigmoid(x)


def _group_rms(x, e, width):
    ss = _dot((x * x).astype(BF16), e)
    return x * lax.rsqrt(ss * (1.0 / width) + EPS)


def _mod_kernel(c_ref, w_ref, b_ref, o_ref):
    s = _silu(c_ref[...]).astype(BF16)
    o_ref[...] = _dot(s, w_ref[...].astype(BF16)) + b_ref[...]


def _modulation(cvec, w_mod, b_mod):
    depth, d, nd = w_mod.shape
    tn = nd // 8
    return pl.pallas_call(
        _mod_kernel,
        out_shape=jax.ShapeDtypeStruct((depth, 8, nd), F32),
        grid=(depth, nd // tn),
        in_specs=[
            pl.BlockSpec((8, d), lambda l, j: (0, 0)),
            pl.BlockSpec((None, d, tn), lambda l, j: (l, 0, j)),
            pl.BlockSpec((None, 1, tn), lambda l, j: (l, 0, j)),
        ],
        out_specs=pl.BlockSpec((None, 8, tn), lambda l, j: (l, 0, j)),
        compiler_params=_cparams(("arbitrary", "arbitrary")),
        name="modulation",
    )(cvec, w_mod, b_mod.reshape(depth, 1, nd))


def _half_ffn_rows(x, m_ref, g_ref, wg_ref, wu_ref, wd_ref, mod_base):
    shift = m_ref[mod_base:mod_base + 1, :]
    scale = m_ref[mod_base + 1:mod_base + 2, :]
    gate = m_ref[mod_base + 2:mod_base + 3, :]
    h = ((_rms(x) * g_ref[...]) * (1.0 + scale) + shift).astype(BF16)
    y = jnp.zeros(x.shape, F32)
    for lo, hi in FF_CHUNKS:
        sl = slice(lo, hi)
        a = _dot(h, wg_ref[:, sl])
        u = _dot(h, wu_ref[:, sl])
        y = y + _dot((_silu(a) * u).astype(BF16), wd_ref[sl, :])
    return x + (0.5 * gate) * y


def _ffn_kernel(*refs, n_lat_tiles):
    if n_lat_tiles is None:
        x_ref, m_ref, g_ref, wg_ref, wu_ref, wd_ref, o_ref = refs
        x = x_ref[...]
    else:
        x_ref, c_ref, m_ref, g_ref, wg_ref, wu_ref, wd_ref, o_ref = refs
        x = jnp.where(pl.program_id(0) < n_lat_tiles, x_ref[...], c_ref[...])
    o_ref[...] = _half_ffn_rows(x, m_ref, g_ref, wg_ref, wu_ref, wd_ref, 0)


def _mix_ffn_kernel(*refs, n_lat_tiles, final):
    if n_lat_tiles is None:
        (x_ref, m_ref, yp_ref, yg_ref, ya_ref, yf_ref, wo_ref,
         g_ref, wg_ref, wu_ref, wd_ref, fg_ref, o_ref) = refs
        yp, ya, yf = yp_ref[...], ya_ref[...], yf_ref[...]
    else:
        (x_ref, m_ref, yp_ref, yg_ref, ya_ref, yf_ref, cp_ref, ca_ref, cf_ref, wo_ref,
         g_ref, wg_ref, wu_ref, wd_ref, fg_ref, o_ref) = refs
        lat = pl.program_id(0) < n_lat_tiles
        yp = jnp.where(lat, yp_ref[...], cp_ref[...])
        ya = jnp.where(lat, ya_ref[...], ca_ref[...])
        yf = jnp.where(lat, yf_ref[...], cf_ref[...])
    acc = _dot(yp.astype(BF16), wo_ref[0:256, :])
    acc = acc + _dot(yg_ref[...].astype(BF16), wo_ref[256:512, :])
    acc = acc + _dot(ya.astype(BF16), wo_ref[512:768, :])
    acc = acc + _dot(yf.astype(BF16), wo_ref[768:1024, :])
    x = x_ref[...] + m_ref[5:6, :] * acc
    out = _half_ffn_rows(x, m_ref, g_ref, wg_ref, wu_ref, wd_ref, 6)
    if final:
        out = _rms(out) * fg_ref[...]
    o_ref[...] = out


def _mod_spec(layer, n_lat_tiles, tiles_per_batch, n_batch, d):
    def index_map(t):
        return (layer, jnp.where(t < n_lat_tiles, t // tiles_per_batch, n_batch), 0, 0)
    return pl.BlockSpec((None, None, N_MOD, d), index_map)


def _resident(arr, lead):
    block = (None,) * len(lead) + arr.shape[len(lead):]
    index = tuple(lead) + (0,) * (arr.ndim - len(lead))
    return pl.BlockSpec(block, lambda t: index, pipeline_mode=pl.Buffered(1))


def _ffn_weight_specs(g, wg, wu, wd, layer, half):
    d = g.shape[-1]
    lead = (layer, half)
    return [pl.BlockSpec((None, None, 1, d), lambda t: (layer, 2 * half, 0, 0)),
            _resident(wg, lead), _resident(wu, lead), _resident(wd, lead)]


def _half_ffn(x, mods, g, wg, wu, wd, *, layer, n_tiles, mod_map, ctx_rows=None):
    d = x.shape[1]
    n_lat = None if ctx_rows is None else x.shape[0] // TM
    if ctx_rows is None:
        rows, row_specs = [x], [pl.BlockSpec((TM, d), lambda t: (t, 0))]
    else:
        rows = [x, ctx_rows]
        row_specs = [pl.BlockSpec((TM, d), lambda t: (jnp.minimum(t, n_lat - 1), 0)),
                     pl.BlockSpec((TM, d), lambda t: (jnp.maximum(t - n_lat, 0), 0))]
    return pl.pallas_call(
        functools.partial(_ffn_kernel, n_lat_tiles=n_lat),
        out_shape=jax.ShapeDtypeStruct((n_tiles * TM, d), F32),
        grid=(n_tiles,),
        in_specs=row_specs + [mod_map] + _ffn_weight_specs(g, wg, wu, wd, layer, 0),
        out_specs=pl.BlockSpec((TM, d), lambda t: (t, 0)),
        compiler_params=_cparams(("arbitrary",)),
        name="half_ffn",
    )(*rows, mods, g, wg, wu, wd)


def _mix_ffn(x, mods, y_pool, y_gla, y_att, y_fnet, w_out, g, wg, wu, wd, fg,
             *, layer, n_tiles, mod_map, final, ctx_parts=None):
    d = x.shape[1]
    part = pl.BlockSpec((TM, 256), lambda t: (t, 0))
    n_lat = None
    parts, part_specs = [y_pool, y_gla, y_att, y_fnet], [part, part, part, part]
    if ctx_parts is not None:
        n_lat = y_pool.shape[0] // TM
        lat_part = pl.BlockSpec((TM, 256), lambda t: (jnp.minimum(t, n_lat - 1), 0))
        ctx_part = pl.BlockSpec((TM, 256), lambda t: (jnp.maximum(t - n_lat, 0), 0))
        parts += list(ctx_parts)
        part_specs = [lat_part, part, lat_part, lat_part, ctx_part, ctx_part, ctx_part]
    return pl.pallas_call(
        functools.partial(_mix_ffn_kernel, n_lat_tiles=n_lat, final=final),
        out_shape=jax.ShapeDtypeStruct((n_tiles * TM, d), F32),
        grid=(n_tiles,),
        in_specs=[pl.BlockSpec((TM, d), lambda t: (t, 0)), mod_map] + part_specs
                 + [_resident(w_out, (layer,))] + _ffn_weight_specs(g, wg, wu, wd, layer, 1)
                 + [pl.BlockSpec((1, d), lambda t: (0, 0))],
        out_specs=pl.BlockSpec((TM, d), lambda t: (t, 0)),
        compiler_params=_cparams(("arbitrary",)),
        name="mix_ffn",
    )(x, mods, *parts, w_out, g, wg, wu, wd, fg.reshape(1, d))


def _rope(x, c, sa, sb):
    w = x.shape[1]
    return x * c + pltpu.roll(x, w - ROPE_FREQS, 1) * sa + pltpu.roll(x, ROPE_FREQS, 1) * sb


def _tile_rope_table(lines, col):
    return jnp.concatenate([col + lines[r:r + 1, :] for r in range(lines.shape[0])], axis=0)


def _inproj_kernel(x_ref, m_ref, g_ref, w_ref, wa_ref, ba_ref, e_ref, qg_ref, kg_ref,
                   lc_ref, lsa_ref, lsb_ref, cc_ref, csa_ref, csb_ref,
                   up_ref, gq_ref, gk_ref, gv_ref, og_ref, gf_ref, gb_ref,
                   aq_ref, ak_ref, av_ref, uf_ref):
    shift = m_ref[3:4, :]
    scale = m_ref[4:5, :]
    e = e_ref[...]
    n_parts = 2
    rp = TM // n_parts
    lp = rp // GRID_W
    for part in range(n_parts):
        rows = slice(part * rp, (part + 1) * rp)
        lines = slice(part * lp, (part + 1) * lp)
        h = ((_rms(x_ref[rows, :]) * g_ref[...]) * (1.0 + scale) + shift).astype(BF16)
        p = _dot(h, w_ref[...])

        up_ref[rows, :] = p[:, O_POOL:O_POOL + POOL_W]
        uf_ref[rows, :] = p[:, O_FN:O_FN + FNET_W]

        gq_ref[rows, :] = p[:, O_GQ:O_GQ + GLA_KW] * (GLA_DK ** -0.5)
        gk_ref[rows, :] = p[:, O_GK:O_GK + GLA_KW]
        gv_ref[rows, :] = p[:, O_GV:O_GV + GLA_W]
        og_ref[rows, :] = p[:, O_OG:O_OG + GLA_W]
        z = _dot(p[:, O_R:O_R + 128].astype(BF16), wa_ref[...]) + ba_ref[...]
        logsig = jnp.minimum(z, 0.0) - jnp.log(1.0 + jnp.exp(-jnp.abs(z)))
        gdec = logsig * (1.0 / GLA_TAU)
        gf_ref[rows, :] = gdec[:, :GLA_KW]
        gb_ref[rows, :] = gdec[:, GLA_KW:]

        rc = _tile_rope_table(lc_ref[lines, :], cc_ref[...])
        rsa = _tile_rope_table(lsa_ref[lines, :], csa_ref[...])
        rsb = _tile_rope_table(lsb_ref[lines, :], csb_ref[...])
        q = _group_rms(p[:, O_AQ:O_AQ + ATT_W], e, ATT_HD) * qg_ref[...]
        q = _rope(q, rc, rsa, rsb) * (ATT_HD ** -0.5 * math.log2(math.e))
        aq_ref[:, rows] = q.T.astype(BF16)
        k = _group_rms(p[:, O_AK:O_AK + ATT_KW], e[:ATT_KW, :ATT_KW], ATT_HD) * kg_ref[...]
        k = _rope(k, rc[:, :ATT_KW], rsa[:, :ATT_KW], rsb[:, :ATT_KW])
        ak_ref[rows, :] = k.astype(BF16)
        v_t = p[:, O_AV:O_AV + ATT_KW].T
        ones = jnp.ones((ATT_VROWS - ATT_HD, rp), F32)
        for kvh in range(ATT_KVH):
            av_ref[kvh, :, rows] = jnp.concatenate(
                [v_t[kvh * ATT_HD:(kvh + 1) * ATT_HD, :], ones], axis=0).astype(BF16)


def _in_projection(x, mods, g, w_in, wa_blk, ba_blk, e256, qg, kg, rope_lines, rope_cols,
                   *, layer, n_tiles, mod_map, line_map, col_map):
    t_rows, d = x.shape
    row = lambda w: pl.BlockSpec((TM, w), lambda t: (t, 0))
    const = lambda shape: pl.BlockSpec(shape, lambda t: (0,) * len(shape))
    per_layer = lambda arr: pl.BlockSpec((None,) + arr.shape[1:], lambda t: (layer,) + (0,) * (arr.ndim - 1))
    out_shapes = (
        jax.ShapeDtypeStruct((t_rows, POOL_W), F32),
        jax.ShapeDtypeStruct((t_rows, GLA_KW), F32),
        jax.ShapeDtypeStruct((t_rows, GLA_KW), F32),
        jax.ShapeDtypeStruct((t_rows, GLA_W), F32),
        jax.ShapeDtypeStruct((t_rows, GLA_W), F32),
        jax.ShapeDtypeStruct((t_rows, GLA_KW), F32),
        jax.ShapeDtypeStruct((t_rows, GLA_KW), F32),
        jax.ShapeDtypeStruct((n_tiles, ATT_W, TM), BF16),
        jax.ShapeDtypeStruct((t_rows, ATT_KW), BF16),
        jax.ShapeDtypeStruct((n_tiles, ATT_KVH, ATT_VROWS, TM), BF16),
        jax.ShapeDtypeStruct((t_rows, FNET_W), F32),
    )
    out_specs = (
        row(POOL_W), row(GLA_KW), row(GLA_KW), row(GLA_W), row(GLA_W), row(GLA_KW), row(GLA_KW),
        pl.BlockSpec((None, ATT_W, TM), lambda t: (t, 0, 0)),
        row(ATT_KW),
        pl.BlockSpec((None, ATT_KVH, ATT_VROWS, TM), lambda t: (t, 0, 0, 0)),
        row(FNET_W),
    )
    return pl.pallas_call(
        _inproj_kernel,
        out_shape=out_shapes,
        grid=(n_tiles,),
        in_specs=[
            pl.BlockSpec((TM, d), lambda t: (t, 0)),
            mod_map,
            pl.BlockSpec((None, None, 1, d), lambda t: (layer, 1, 0, 0)),
            per_layer(w_in),
            per_layer(wa_blk),
            per_layer(ba_blk),
            const(e256.shape),
            per_layer(qg),
            per_layer(kg),
        ] + [pl.BlockSpec((TM // GRID_W, ATT_W), line_map)] * 3
          + [pl.BlockSpec((None, GRID_W, ATT_W), col_map)] * 3,
        out_specs=out_specs,
        compiler_params=_cparams(("arbitrary",)),
        name="in_projection",
    )(x, mods, g, w_in, wa_blk, ba_blk, e256, qg, kg, *rope_lines, *rope_cols)


def _pool_kernel(u_ref, inv_ref, w_ref, s_ref, o_ref, pad_ref, *, seq, chunk):
    halo = POOL_HALO
    pad_ref[0:halo, :] = jnp.zeros((halo, POOL_W), F32)
    pad_ref[halo + seq:halo + seq + halo, :] = jnp.zeros((halo, POOL_W), F32)
    pad_ref[halo:halo + seq, :] = u_ref[...]
    rows = chunk + 2 * halo
    lane = lax.broadcasted_iota(jnp.int32, (chunk, POOL_W), 1)

    def body(ci, carry):
        c0 = pl.multiple_of(ci * chunk, chunk)
        xp = pad_ref[pl.ds(c0, rows), :]
        u = xp[halo:halo + chunk, :]
        acc = xp
        wsum = None
        for gi, w in enumerate(POOL_WINDOWS):
            acc = acc + pltpu.roll(acc, w // 2, 0)
            lead = w // 2 - 1
            win = acc if lead == 0 else pltpu.roll(acc, rows - lead, 0)
            win = win[halo:halo + chunk, :]
            wsum = win if wsum is None else jnp.where(lane >= gi * POOL_GW, win, wsum)
        m = wsum * inv_ref[pl.ds(c0, chunk), :] - u
        y = _dot(m.astype(BF16), w_ref[...]) * s_ref[...]
        o_ref[pl.ds(c0, chunk), :] = y
        return carry

    lax.fori_loop(0, seq // chunk, body, 0)


def _pool_inverse_counts(seq):
    t = jnp.arange(seq, dtype=jnp.int32)
    cols = []
    for w in POOL_WINDOWS:
        cnt = jnp.minimum(t + w // 2, seq) - jnp.maximum(t - w // 2, 0)
        cols.append(jnp.broadcast_to((1.0 / cnt.astype(F32))[:, None], (seq, POOL_GW)))
    return jnp.concatenate(cols, axis=1)


def _pool_mix(u, inv_cnt, w_blk, s_pool, *, layer, seq, n_seq, first_block):
    chunk = min(256, seq)
    return pl.pallas_call(
        functools.partial(_pool_kernel, seq=seq, chunk=chunk),
        out_shape=jax.ShapeDtypeStruct((n_seq * seq, POOL_W), F32),
        grid=(n_seq,),
        in_specs=[
            pl.BlockSpec((seq, POOL_W), lambda b: (first_block + b, 0)),
            pl.BlockSpec((seq, POOL_W), lambda b: (0, 0)),
            pl.BlockSpec((None, POOL_W, POOL_W), lambda b: (layer, 0, 0)),
            pl.BlockSpec((None, 1, POOL_W), lambda b: (layer, 0, 0)),
        ],
        out_specs=pl.BlockSpec((seq, POOL_W), lambda b: (b, 0)),
        scratch_shapes=[pltpu.VMEM((seq + 2 * POOL_HALO, POOL_W), F32)],
        compiler_params=_cparams(("arbitrary",)),
        name="pool_mix",
    )(u, inv_cnt, w_blk, s_pool)


def _gla_kernel(*refs, reverse, finalize):
    if finalize:
        (q_ref, k_ref, v_ref, g_ref, ex_ref, mk_ref, bm_ref, of_ref, og_ref, gn_ref, e_ref,
         o_ref, st_ref, ob_ref) = refs
    else:
        q_ref, k_ref, v_ref, g_ref, ex_ref, mk_ref, bm_ref, o_ref, st_ref = refs
        ob_ref = o_ref

    @pl.when(pl.program_id(1) == 0)
    def _():
        st_ref[...] = jnp.zeros(st_ref.shape, F32)

    g = g_ref[...]
    trow = lax.broadcasted_iota(jnp.int32, (TP, GLA_KW), 0) % GLA_CHUNK
    pre, suf = g, g
    s = 1
    while s < GLA_CHUNK:
        pre = pre + jnp.where(trow >= s, pltpu.roll(pre, s, 0), 0.0)
        suf = suf + jnp.where(trow < GLA_CHUNK - s, pltpu.roll(suf, TP - s, 0), 0.0)
        s *= 2
    tot = pre + suf - g
    in_range = jnp.max(jnp.abs(tot)) <= GLA_FAST_RANGE

    @pl.when(in_range)
    def _():
        _gla_fast_tile(q_ref, k_ref, v_ref, mk_ref, bm_ref, st_ref, ob_ref,
                       suf if reverse else pre, tot, reverse)

    @pl.when(jnp.logical_not(in_range))
    def _():
        _gla_exact_tile(q_ref, k_ref, v_ref, g_ref, ex_ref, mk_ref, st_ref, ob_ref, reverse)

    if finalize:
        both = ob_ref[...] + of_ref[...]
        y = _group_rms(both, e_ref[...], GLA_DV) * gn_ref[...]
        o_ref[...] = y * _silu(og_ref[...])


def _gla_fast_tile(q_ref, k_ref, v_ref, mk_ref, bm_ref, st_ref, ob_ref, b, tot, reverse):
    ch = GLA_CHUNK
    half = 0.5 * tot
    q, k = q_ref[...], k_ref[...]
    q_in = (q * jnp.exp(b - half)).astype(BF16)
    k_in = (k * jnp.exp(half - b)).astype(BF16)
    q_st = (q * jnp.exp(b)).astype(BF16)
    k_st = (k * jnp.exp(tot - b)).astype(BF16)
    dec = jnp.exp(tot)
    mk = mk_ref[...]
    mk16 = mk.astype(BF16)
    bm16 = bm_ref[...]
    irow = lax.broadcasted_iota(jnp.int32, (ch, GLA_HEADS * ch), 0)
    jcol = lax.broadcasted_iota(jnp.int32, (ch, GLA_HEADS * ch), 1) % ch
    seen = (jcol >= irow) if reverse else (jcol <= irow)
    nt = (((1,), (1,)), ((), ()))
    st = st_ref[...]
    n_chunks = TP // ch
    for ci in range(n_chunks):
        c = (n_chunks - 1 - ci) if reverse else ci
        rows = slice(c * ch, (c + 1) * ch)
        kb = jnp.concatenate([k_in[rows]] * GLA_HEADS, axis=0) * mk16
        a = lax.dot_general(q_in[rows], kb, nt, preferred_element_type=F32)
        a = jnp.where(seen, a, 0.0).astype(BF16)
        v = v_ref[rows, :]
        vb = jnp.concatenate([v.astype(BF16)] * GLA_HEADS, axis=0) * bm16
        o = _dot(a, vb) + lax.dot_general(q_st[rows], st.astype(BF16), nt, preferred_element_type=F32)
        kv = lax.dot_general(v.astype(BF16), k_st[rows], (((0,), (0,)), ((), ())),
                             preferred_element_type=F32)
        st = st * dec[c * ch:c * ch + 1, :] + kv * mk
        ob_ref[rows, :] = o
    st_ref[...] = st


def _gla_exact_tile(q_ref, k_ref, v_ref, g_ref, ex_ref, mk_ref, st_ref, ob_ref, reverse):
    sub = GLA_SUB
    n_chunks = TP // sub
    row = lax.broadcasted_iota(jnp.int32, (sub, GLA_KW), 0)
    ex = ex_ref[...]
    mk = mk_ref[...]

    def body(ci, carry):
        c = (n_chunks - 1 - ci) if reverse else ci
        r0 = pl.multiple_of(c * sub, sub)
        q = q_ref[pl.ds(r0, sub), :]
        k = k_ref[pl.ds(r0, sub), :]
        v = v_ref[pl.ds(r0, sub), :]
        b = g_ref[pl.ds(r0, sub), :]
        s = 1
        while s < sub:
            if reverse:
                b = b + jnp.where(row < sub - s, pltpu.roll(b, sub - s, 0), 0.0)
            else:
                b = b + jnp.where(row >= s, pltpu.roll(b, s, 0), 0.0)
            s *= 2
        edge = b[0:1, :] if reverse else b[sub - 1:sub, :]
        st = st_ref[...]
        qd = (q * jnp.exp(b)).astype(BF16)
        o = lax.dot_general(qd, st.astype(BF16), (((1,), (1,)), ((), ())),
                            preferred_element_type=F32)
        parts = []
        for j in range(sub):
            seen = (row <= j) if reverse else (row >= j)
            dec = jnp.exp(jnp.where(seen, b - b[j:j + 1, :], NEG_BIG))
            parts.append((dec * q * k[j:j + 1, :]).astype(BF16))
        r = _dot(jnp.concatenate(parts, axis=0), ex)
        for j in range(sub):
            o = o + r[j * sub:(j + 1) * sub, :] * v[j:j + 1, :]
        kd = (k * jnp.exp(edge - b)).astype(BF16)
        kv = lax.dot_general(v.astype(BF16), kd, (((0,), (0,)), ((), ())),
                             preferred_element_type=F32)
        st_ref[...] = st * jnp.exp(edge) + kv * mk
        ob_ref[pl.ds(r0, sub), :] = o
        return carry

    lax.fori_loop(0, n_chunks, body, 0, unroll=2)


def _gla_tile_map(n_batch, n_lat, n_ctx, reverse):
    def index_map(b, s):
        is_ctx = s < n_ctx
        sc = jnp.where(is_ctx, s, 0)
        sl = jnp.where(is_ctx, 0, s - n_ctx)
        if reverse:
            sc = n_ctx - 1 - sc
            sl = n_lat - 1 - sl
        return (jnp.where(is_ctx, n_batch * n_lat + b * n_ctx + sc, b * n_lat + sl), 0)
    return index_map


def _gla_mix(gq, gk, gv, gf, gb, og, ex, mk, gn, e256, *, n_batch, seq, ctx):
    t_rows = gq.shape[0]
    n_lat, n_ctx = seq // TP, ctx // TP
    grid = (n_batch, n_lat + n_ctx)
    const = lambda shape: pl.BlockSpec(shape, lambda b, s: (0,) * len(shape))

    def run(reverse, finalize, gate, extra):
        tmap = _gla_tile_map(n_batch, n_lat, n_ctx, reverse)
        rows = lambda w: pl.BlockSpec((TP, w), tmap)
        in_specs = [rows(GLA_KW), rows(GLA_KW), rows(GLA_W), rows(GLA_KW), const(ex.shape), const(mk.shape),
                    const(e256.shape)]
        args = [gq, gk, gv, gate, ex, mk, e256]
        scratch = [pltpu.VMEM((GLA_W, GLA_KW), F32)]
        if finalize:
            in_specs += [rows(GLA_W), rows(GLA_W), const((1, GLA_W)), const(e256.shape)]
            args += extra
            scratch.append(pltpu.VMEM((TP, GLA_W), F32))
        return pl.pallas_call(
            functools.partial(_gla_kernel, reverse=reverse, finalize=finalize),
            out_shape=jax.ShapeDtypeStruct((t_rows, GLA_W), F32),
            grid=grid,
            in_specs=in_specs,
            out_specs=rows(GLA_W),
            scratch_shapes=scratch,
            compiler_params=_cparams(("arbitrary", "arbitrary")),
            name="gla_bwd" if reverse else "gla_fwd",
        )(*args)

    o_f = run(False, False, gf, None)
    return run(True, True, gb, [o_f, og, gn, e256])


def _att_query_groups(q_t):
    tq = q_t.shape[1]
    zero = jnp.zeros((ATT_HD, tq), BF16)
    groups = []
    for g in range(ATT_KVH):
        halves = []
        for h in (2 * g, 2 * g + 1):
            qh = q_t[h * ATT_HD:(h + 1) * ATT_HD, :]
            halves.append(jnp.concatenate([qh, zero] if g == 0 else [zero, qh], axis=0))
        groups.append(jnp.concatenate(halves, axis=1))
    return groups


def _att_store(o_ref, weighted, denom, tq):
    heads = []
    for g in range(ATT_KVH):
        og = weighted[g] / denom[g]
        heads += [og[:, :tq], og[:, tq:]]
    o_ref[...] = jnp.concatenate(heads, axis=0).T.astype(o_ref.dtype)


def _att_kernel(*refs, n_main, has_ctx, bounded):
    refs = list(refs)
    q_ref, km_ref, vm_ref = refs.pop(0), refs.pop(0), refs.pop(0)
    kc_ref, vc_ref = (refs.pop(0), refs.pop(0)) if has_ctx else (None, None)
    o_ref, s0_ref, s1_ref = refs
    q_t = q_ref[...]
    tq = q_t.shape[1]
    tk = vm_ref.shape[3]
    groups = _att_query_groups(q_t)

    def scores(kblk, buf):
        for g in range(ATT_KVH):
            buf[g, 0:kblk.shape[0], :] = _dot(kblk, groups[g])

    def consume(buf, rows, v_of, carry):
        out = []
        for g in range(ATT_KVH):
            m, l, acc = carry[g]
            s = buf[g, 0:rows, :]
            if bounded:
                pv = _dot(v_of(g), jnp.exp2(s).astype(BF16))
                out.append((m, l + pv[ATT_HD:ATT_HD + 1, :], acc + pv[:ATT_HD, :]))
                continue
            m_new = jnp.maximum(m, jnp.max(s, axis=0, keepdims=True))
            alpha = jnp.exp2(m - m_new)
            p = jnp.exp2(s - m_new).astype(BF16)
            pv = _dot(v_of(g), p)
            acc = alpha * acc + pv[:ATT_HD, :]
            l = alpha * l + pv[ATT_HD:ATT_HD + 1, :]
            out.append((m_new, l, acc))
        return tuple(out)

    def k_main(j):
        return km_ref[pl.ds(pl.multiple_of(j * tk, tk), tk), :]

    def v_main(j):
        return lambda g: vm_ref[j, g]

    def pair(j, carry, issue_next):
        scores(k_main(j + 1), s1_ref)
        carry = consume(s0_ref, tk, v_main(j), carry)
        issue_next(j + 2)
        return consume(s1_ref, tk, v_main(j + 1), carry)

    carry = tuple((jnp.full((1, 2 * tq), -jnp.inf, F32), jnp.zeros((1, 2 * tq), F32),
                   jnp.zeros((ATT_HD, 2 * tq), F32)) for _ in range(ATT_KVH))
    next_main = lambda j: scores(k_main(j), s0_ref)
    if has_ctx:
        after_main = lambda j: scores(kc_ref[...], s0_ref)
    else:
        after_main = lambda j: None

    scores(k_main(0), s0_ref)
    if n_main > 1:
        pairs_per_trip = 2 if n_main % 4 == 0 else 1

        def body(i, carry):
            for u in range(pairs_per_trip):
                carry = pair(2 * (pairs_per_trip * i + u), carry, next_main)
            return carry

        n_trips = n_main // (2 * pairs_per_trip)
        carry = lax.fori_loop(0, n_trips - 1, body, carry)
        for u in range(pairs_per_trip):
            last = u == pairs_per_trip - 1
            carry = pair(2 * (pairs_per_trip * (n_trips - 1) + u), carry, after_main if last else next_main)
    else:
        assert not has_ctx
        carry = consume(s0_ref, tk, v_main(0), carry)
    if has_ctx:
        carry = consume(s0_ref, kc_ref.shape[0], lambda g: vc_ref[g], carry)
    _att_store(o_ref, [c[2] for c in carry], [c[1] for c in carry], tq)


def _attention(aq, ak, av, *, n_batch, seq, ctx, ctx_queries, bounded=False):
    lat_tiles = seq // TM
    ctx_tile = n_batch * lat_tiles
    ctx_cols = lambda b, i: (ctx_tile + (b * ctx) // TM, 0, ((b * ctx) % TM) // ctx)
    ctx_vcols = lambda b, i: (ctx_tile + (b * ctx) // TM, 0, 0, ((b * ctx) % TM) // ctx)
    if not ctx_queries:
        sub = TM // TQ
        grid = (n_batch, seq // TQ)
        in_specs = [
            pl.BlockSpec((None, ATT_W, TQ), lambda b, i: (b * lat_tiles + i // sub, 0, i % sub)),
            pl.BlockSpec((seq, ATT_KW), lambda b, i: (b, 0)),
            pl.BlockSpec((lat_tiles, ATT_KVH, ATT_VROWS, TM), lambda b, i: (b, 0, 0, 0)),
            pl.BlockSpec((ctx, ATT_KW), lambda b, i: (n_batch * seq // ctx + b, 0)),
            pl.BlockSpec((None, ATT_KVH, ATT_VROWS, ctx), ctx_vcols),
        ]
        out_spec = pl.BlockSpec((TQ, ATT_W), lambda b, i: (b * (seq // TQ) + i, 0))
        assert lat_tiles % 2 == 0
        out_rows = n_batch * seq
        args = (aq, ak, av, ak, av)
        kern = functools.partial(_att_kernel, n_main=lat_tiles, has_ctx=True, bounded=bounded)
        s_shape = (ATT_KVH, TM, 2 * TQ)
    else:
        grid = (n_batch, 1)
        in_specs = [
            pl.BlockSpec((None, ATT_W, ctx), ctx_cols),
            pl.BlockSpec((ctx, ATT_KW), lambda b, i: (n_batch * seq // ctx + b, 0)),
            pl.BlockSpec((1, ATT_KVH, ATT_VROWS, ctx), ctx_vcols),
        ]
        out_spec = pl.BlockSpec((ctx, ATT_W), lambda b, i: (b, 0))
        kern = functools.partial(_att_kernel, n_main=1, has_ctx=False, bounded=False)
        out_rows = n_batch * ctx
        args = (aq, ak, av)
        s_shape = (ATT_KVH, ctx, 2 * ctx)
    return pl.pallas_call(
        kern,
        out_shape=jax.ShapeDtypeStruct((out_rows, ATT_W), BF16),
        grid=grid,
        in_specs=in_specs,
        out_specs=out_spec,
        scratch_shapes=[pltpu.VMEM(s_shape, F32), pltpu.VMEM(s_shape, F32)],
        compiler_params=_cparams(("arbitrary", "arbitrary")),
        name="attention_ctx" if ctx_queries else ("attention_bounded" if bounded else "attention"),
    )(*args)


def _fft_a_kernel(x_ref, d_ref, tr_ref, ti_ref, yr_ref, yi_ref):
    rb, w = x_ref.shape[1], x_ref.shape[2]
    x = jnp.concatenate([x_ref[:, r, :] for r in range(rb)], axis=1)
    y = _dot(d_ref[...], x.astype(BF16))
    tr, ti = tr_ref[...], ti_ref[...]
    for r in range(rb):
        yr, yi = y[:FFT_N1, r * w:(r + 1) * w], y[FFT_N1:, r * w:(r + 1) * w]
        c, s = tr[:, r:r + 1], ti[:, r:r + 1]
        yr_ref[:, r, :] = yr * c - yi * s
        yi_ref[:, r, :] = yr * s + yi * c


def _channel_mix(xr, xi, c_ref, wf_ref, norm):
    xc = jnp.concatenate([xr, xi], axis=1).astype(BF16)
    f = _dot(xc, c_ref[...]) * norm
    return _dot(f.astype(BF16), wf_ref[...])


def _fft_c_kernel(yr_ref, yi_ref, m_ref, c_ref, wf_ref, o_ref, *, n2, norm):
    w = yr_ref.shape[1]
    blk = jnp.concatenate(
        [jnp.concatenate([yr_ref[j * n2:(j + 1) * n2, :] for j in range(8)], axis=1),
         jnp.concatenate([yi_ref[j * n2:(j + 1) * n2, :] for j in range(8)], axis=1)], axis=0)
    x = _dot(m_ref[...], blk.astype(BF16))
    xr = jnp.concatenate([x[:n2, j * w:(j + 1) * w] for j in range(8)], axis=0)
    xi = jnp.concatenate([x[n2:, j * w:(j + 1) * w] for j in range(8)], axis=0)
    y = _channel_mix(xr, xi, c_ref, wf_ref, norm)
    for j in range(8):
        o_ref[:, j, :] = y[j * n2:(j + 1) * n2, :]


def _fft_dense_kernel(x_ref, d_ref, c_ref, wf_ref, o_ref, *, n, norm):
    x = _dot(d_ref[...], x_ref[...].astype(BF16))
    o_ref[...] = _channel_mix(x[:n, :], x[n:, :], c_ref, wf_ref, norm)


def _dft_parts(n):
    idx = np.arange(n)
    ang = 2.0 * np.pi * ((idx[:, None] * idx[None, :]) % n) / n
    return np.cos(ang), np.sin(ang)


def _mxu_const(a):
    return jnp.asarray(a, F32).astype(BF16)


def _fnet_consts(seq):
    n1, n2 = FFT_N1, seq // FFT_N1
    c1, s1 = _dft_parts(n1)
    da = np.concatenate([c1, -s1], axis=0)
    k1 = np.arange(n1)[:, None]
    m2 = np.arange(n2)[None, :]
    ang = 2.0 * np.pi * ((k1 * m2) % seq) / seq
    tw_r, tw_i = np.cos(ang), -np.sin(ang)
    c2, s2 = _dft_parts(n2)
    mc = np.block([[c2, s2], [-s2, c2]])
    return da, tw_r, tw_i, mc


def _channel_consts():
    cc, sc = _dft_parts(FNET_HD)
    eye = np.eye(FNET_HEADS)
    return np.concatenate([np.kron(eye, cc), np.kron(eye, sc)], axis=0)


def _fnet_latent(uf3, wf, *, layer, n_batch, seq):
    n1, n2 = FFT_N1, seq // FFT_N1
    da, tw_r, tw_i, mc = _fnet_consts(seq)
    da, mc, cc = _mxu_const(da), _mxu_const(mc), _mxu_const(_channel_consts())
    rb = min(n2, 32)
    blocked = lambda tw: jnp.asarray(tw.reshape(n1, n2 // rb, rb).transpose(1, 0, 2), F32)
    const2 = lambda shape: pl.BlockSpec(shape, lambda b, j: (0,) * len(shape))
    slab = pl.BlockSpec((n1, rb, FNET_W), lambda b, j: (b, j, 0))
    yr, yi = pl.pallas_call(
        _fft_a_kernel,
        out_shape=(jax.ShapeDtypeStruct((n_batch * n1, n2, FNET_W), F32),) * 2,
        grid=(n_batch, n2 // rb),
        in_specs=[
            slab,
            const2(da.shape),
            pl.BlockSpec((None, n1, rb), lambda b, j: (j, 0, 0)),
            pl.BlockSpec((None, n1, rb), lambda b, j: (j, 0, 0)),
        ],
        out_specs=(slab, slab),
        compiler_params=_cparams(("arbitrary", "arbitrary")),
        name="fft_stage_a",
    )(uf3, da, blocked(tw_r), blocked(tw_i))
    yr = yr.reshape(n_batch * n1 * n2, FNET_W)
    yi = yi.reshape(n_batch * n1 * n2, FNET_W)
    norm = 1.0 / math.sqrt(seq * FNET_HD)
    out = pl.pallas_call(
        functools.partial(_fft_c_kernel, n2=n2, norm=norm),
        out_shape=jax.ShapeDtypeStruct((n_batch, n2, n1, FNET_W), F32),
        grid=(n_batch, n1 // 8),
        in_specs=[
            pl.BlockSpec((8 * n2, FNET_W), lambda b, i: (b * (n1 // 8) + i, 0)),
            pl.BlockSpec((8 * n2, FNET_W), lambda b, i: (b * (n1 // 8) + i, 0)),
            const2(mc.shape), const2(cc.shape),
            pl.BlockSpec((None,) + wf.shape[1:], lambda b, i: (layer, 0, 0)),
        ],
        out_specs=pl.BlockSpec((None, n2, 8, FNET_W), lambda b, i: (b, 0, i, 0)),
        compiler_params=_cparams(("arbitrary", "arbitrary")),
        name="fft_stage_c",
    )(yr, yi, mc, cc, wf)
    return out.reshape(n_batch * seq, FNET_W)


def _fnet_context(uf, wf, *, layer, n_batch, ctx, first_block):
    c, s = _dft_parts(ctx)
    dd, cc = _mxu_const(np.concatenate([c, -s], axis=0)), _mxu_const(_channel_consts())
    const = lambda shape: pl.BlockSpec(shape, lambda b: (0,) * len(shape))
    return pl.pallas_call(
        functools.partial(_fft_dense_kernel, n=ctx, norm=1.0 / math.sqrt(ctx * FNET_HD)),
        out_shape=jax.ShapeDtypeStruct((n_batch * ctx, FNET_W), F32),
        grid=(n_batch,),
        in_specs=[pl.BlockSpec((ctx, FNET_W), lambda b: (first_block + b, 0)),
                  const(dd.shape), const(cc.shape),
                  pl.BlockSpec((None,) + wf.shape[1:], lambda b: (layer, 0, 0))],
        out_specs=pl.BlockSpec((ctx, FNET_W), lambda b: (b, 0)),
        compiler_params=_cparams(("arbitrary",)),
        name="fft_context",
    )(uf, dd, cc, wf)


def _rope_tables(seq):
    freqs = ROPE_THETA ** (-jnp.arange(ROPE_FREQS, dtype=F32) / ROPE_FREQS)
    row_ang = jnp.arange(seq // GRID_W, dtype=F32)[:, None] * freqs
    col_ang = jnp.arange(GRID_W, dtype=F32)[:, None] * freqs
    lanes = np.arange(ATT_HD)
    is_row = jnp.asarray(np.tile(lanes < 2 * ROPE_FREQS, ATT_QH)[None, :], F32)
    first_half = jnp.asarray(np.tile((lanes // ROPE_FREQS) % 2 == 0, ATT_QH)[None, :], F32)
    spread = lambda a: jnp.tile(a, (1, ATT_W // ROPE_FREQS))

    def tables(ang, mask, ident_lines):
        cos, sin = spread(jnp.cos(ang)) * mask, spread(jnp.sin(ang)) * mask
        ident = jnp.zeros((ident_lines, ATT_W), F32)
        return (jnp.concatenate([cos, ident + mask], axis=0),
                jnp.concatenate([-sin * first_half, ident], axis=0),
                jnp.concatenate([sin * (1.0 - first_half), ident], axis=0))

    rows = tables(row_ang, is_row, TM // GRID_W)
    cols = tuple(t.reshape(2, GRID_W, ATT_W) for t in tables(col_ang, 1.0 - is_row, GRID_W))
    return rows, cols


def _block_ones(width, group):
    return jnp.asarray(np.kron(np.eye(width // group), np.ones((group, group))), BF16)


def kernel(x, c, ctx, c_ctx, w_mod, b_mod, norm_g, ffn_wg, ffn_wu, ffn_wd, w_in, w_out,
           pool_w, pool_scale, gla_wa, gla_ba, gla_norm, att_qnorm, att_knorm, fnet_w, final_norm):
    n_batch, seq, d = x.shape
    n_ctx = ctx.shape[1]
    assert d == D_MODEL and seq % TM == 0 and (n_batch * n_ctx) % TM == 0 and n_ctx % TP == 0
    assert seq % (8 * FFT_N1) == 0 and n_ctx <= TM and TM % n_ctx == 0 and n_batch + 1 <= 8
    lat_rows = n_batch * seq
    lat_tiles = lat_rows // TM
    all_tiles = lat_tiles + (n_batch * n_ctx) // TM
    tiles_per_batch = seq // TM
    line_map = lambda t: (jnp.where(t < lat_tiles, t % tiles_per_batch, tiles_per_batch), 0)
    col_map = lambda t: (jnp.where(t < lat_tiles, 0, 1), 0, 0)

    cvec = jnp.concatenate([c, c_ctx[None, :], jnp.zeros((8 - n_batch - 1, d), F32)], axis=0)
    mods = _modulation(cvec, w_mod, b_mod).reshape(DEPTH, 8, N_MOD, d)

    rope_lines, rope_cols = _rope_tables(seq)
    e256 = _block_ones(ATT_W, ATT_HD)
    gla_ex = _block_ones(GLA_W, GLA_DV)[::2, :]
    gla_mk = jnp.asarray(np.kron(np.eye(GLA_HEADS), np.ones((GLA_DV, GLA_DK))), F32)

    wg, wu, wd = ffn_wg.astype(BF16), ffn_wu.astype(BF16), ffn_wd.astype(BF16)
    wi = jnp.concatenate([w_in[..., :768], w_in[..., 800:D_IN], w_in[..., 768:800],
                          jnp.zeros((DEPTH, d, D_IN_PAD - D_IN), F32)], axis=-1).astype(BF16)
    wa_blk = jnp.zeros((DEPTH, 128, 2 * GLA_KW), F32)
    wa_blk = wa_blk.at[:, :GLA_RANK, :GLA_KW].set(gla_wa[:, 0]).at[:, GLA_RANK:2 * GLA_RANK, GLA_KW:].set(gla_wa[:, 1])
    wa_blk = wa_blk.astype(BF16)
    ba_blk = gla_ba.reshape(DEPTH, 1, 2 * GLA_KW)
    qg = jnp.tile(att_qnorm, (1, ATT_QH))[:, None, :]
    kg = jnp.tile(att_knorm, (1, ATT_KVH))[:, None, :]
    pool_blk = jnp.zeros((DEPTH, POOL_W, POOL_W), F32)
    for gi in range(len(POOL_WINDOWS)):
        sl = slice(gi * POOL_GW, (gi + 1) * POOL_GW)
        pool_blk = pool_blk.at[:, sl, sl].set(pool_w[:, gi])
    pool_blk = pool_blk.astype(BF16)
    pool_s = pool_scale[:, None, :]
    wf = fnet_w.astype(BF16)
    wo = w_out.astype(BF16)
    norm4 = norm_g[:, :, None, :]
    n2 = seq // FFT_N1
    inv_lat = _pool_inverse_counts(seq)

    xs = x.reshape(lat_rows, d)
    ctx_rows = ctx.reshape(n_batch * n_ctx, d)
    for i in range(DEPTH):
        ctx_out = i < DEPTH - 1
        mod_map = _mod_spec(i, lat_tiles, tiles_per_batch, n_batch, d)
        gn = jnp.tile(gla_norm[i], GLA_HEADS)[None, :]

        xs = _half_ffn(xs, mods, norm4, wg, wu, wd, layer=i, n_tiles=all_tiles,
                       mod_map=mod_map, ctx_rows=ctx_rows if i == 0 else None)
        (u_pool, gq, gk, gv, og, gf, gb, aq, ak, av, uf) = _in_projection(
            xs, mods, norm4, wi, wa_blk, ba_blk, e256, qg, kg, rope_lines, rope_cols,
            layer=i, n_tiles=all_tiles, mod_map=mod_map, line_map=line_map, col_map=col_map)

        y_pool = _pool_mix(u_pool, inv_lat, pool_blk, pool_s, layer=i, seq=seq, n_seq=n_batch, first_block=0)
        y_gla = _gla_mix(gq, gk, gv, gf, gb, og, gla_ex, gla_mk, gn, e256, n_batch=n_batch, seq=seq, ctx=n_ctx)
        score_bound = (ATT_HD * ATT_HD ** -0.5 * math.log2(math.e) * ATT_BOUND_SLACK
                       * jnp.max(jnp.abs(att_qnorm[i])) * jnp.max(jnp.abs(att_knorm[i])))
        attend = functools.partial(_attention, aq, ak, av, n_batch=n_batch, seq=seq, ctx=n_ctx, ctx_queries=False)
        y_att = lax.cond(score_bound <= ATT_BOUND_MAX, lambda: attend(bounded=True), lambda: attend())
        y_fnet = _fnet_latent(uf.reshape(uf.shape[0] // n2, n2, FNET_W), wf, layer=i, n_batch=n_batch, seq=seq)
        n_tiles, ctx_parts = lat_tiles, None
        if ctx_out:
            first_ctx = lat_rows // n_ctx
            ctx_parts = (
                _pool_mix(u_pool, _pool_inverse_counts(n_ctx), pool_blk, pool_s, layer=i, seq=n_ctx, n_seq=n_batch,
                          first_block=first_ctx),
                _attention(aq, ak, av, n_batch=n_batch, seq=seq, ctx=n_ctx, ctx_queries=True),
                _fnet_context(uf, wf, layer=i, n_batch=n_batch, ctx=n_ctx, first_block=first_ctx),
            )
            n_tiles = all_tiles
        xs = _mix_ffn(xs, mods, y_pool, y_gla, y_att, y_fnet, wo, norm4, wg, wu, wd, final_norm,
                      layer=i, n_tiles=n_tiles, mod_map=mod_map, final=not ctx_out, ctx_parts=ctx_parts)
    return xs.reshape(n_batch, seq, d)
```

```python
import functools
import math

import jax
import jax.numpy as jnp
import numpy as np
from jax import lax
from jax.experimental import pallas as pl
from jax.experimental.pallas import tpu as pltpu

F32 = jnp.float32
BF16 = jnp.bfloat16

D_MODEL = 1024
DEPTH = 2
GRID_W = 64
EPS = 1e-6
N_MOD = 9
D_FF = 2816

POOL_W = 256
POOL_WINDOWS = (2, 4, 8, 16)
POOL_GW = 64
POOL_HALO = 8

GLA_HEADS = 4
GLA_W = 256
GLA_DV = 64
GLA_DK = 32
GLA_RANK = 16
GLA_TAU = 16.0
GLA_KW = GLA_HEADS * GLA_DK
GLA_SUB = 16
GLA_CHUNK = 64
GLA_FAST_RANGE = 150.0

ATT_W = 256
ATT_HD = 64
ATT_QH = 4
ATT_KVH = 2
ATT_KW = ATT_KVH * ATT_HD
ATT_VROWS = ATT_HD + 16
ATT_BOUND_SLACK = 1.02
ATT_BOUND_MAX = 50.0
ROPE_FREQS = 16
ROPE_THETA = 10000.0

FNET_W = 256
FNET_HEADS = 4
FNET_HD = 64
FFT_N1 = 64

D_IN = 1824
D_IN_PAD = 1920

O_POOL, O_GQ, O_GK, O_GV, O_OG, O_AQ, O_AK, O_AV, O_FN, O_R = 0, 256, 384, 512, 768, 1024, 1280, 1408, 1536, 1792

TM = 512
TQ = 256
TP = 256
MXU_TILE = 256
FF_CHUNKS = ((0, 6 * MXU_TILE), (6 * MXU_TILE, D_FF))
NEG_BIG = -1e30

VMEM_LIMIT = 56 * 1024 * 1024


def _cparams(sem):
    return pltpu.CompilerParams(dimension_semantics=sem, vmem_limit_bytes=VMEM_LIMIT)


def _dot(a, b):
    return jnp.dot(a, b, preferred_element_type=F32)


def _rms(x):
    return x * lax.rsqrt(jnp.mean(x * x, axis=-1, keepdims=True) + EPS)


def _silu(x):
    return x * jax.nn.sigmoid(x)


def _group_rms(x, e, width):
    ss = _dot((x * x).astype(BF16), e)
    return x * lax.rsqrt(ss * (1.0 / width) + EPS)


def _mod_kernel(c_ref, w_ref, b_ref, o_ref):
    s = _silu(c_ref[...]).astype(BF16)
    o_ref[...] = _dot(s, w_ref[...].astype(BF16)) + b_ref[...]


def _modulation(cvec, w_mod, b_mod):
    depth, d, nd = w_mod.shape
    tn = nd // 8
    return pl.pallas_call(
        _mod_kernel,
        out_shape=jax.ShapeDtypeStruct((depth, 8, nd), F32),
        grid=(depth, nd // tn),
        in_specs=[
            pl.BlockSpec((8, d), lambda l, j: (0, 0)),
            pl.BlockSpec((None, d, tn), lambda l, j: (l, 0, j)),
            pl.BlockSpec((None, 1, tn), lambda l, j: (l, 0, j)),
        ],
        out_specs=pl.BlockSpec((None, 8, tn), lambda l, j: (l, 0, j)),
        compiler_params=_cparams(("arbitrary", "arbitrary")),
        name="modulation",
    )(cvec, w_mod, b_mod.reshape(depth, 1, nd))


def _half_ffn_rows(x, m_ref, g_ref, wg_ref, wu_ref, wd_ref, mod_base):
    shift = m_ref[mod_base:mod_base + 1, :]
    scale = m_ref[mod_base + 1:mod_base + 2, :]
    gate = m_ref[mod_base + 2:mod_base + 3, :]
    h = ((_rms(x) * g_ref[...]) * (1.0 + scale) + shift).astype(BF16)
    y = jnp.zeros(x.shape, F32)
    for lo, hi in FF_CHUNKS:
        sl = slice(lo, hi)
        a = _dot(h, wg_ref[:, sl])
        u = _dot(h, wu_ref[:, sl])
        y = y + _dot((_silu(a) * u).astype(BF16), wd_ref[sl, :])
    return x + (0.5 * gate) * y


def _ffn_kernel(*refs, n_lat_tiles):
    if n_lat_tiles is None:
        x_ref, m_ref, g_ref, wg_ref, wu_ref, wd_ref, o_ref = refs
        x = x_ref[...]
    else:
        x_ref, c_ref, m_ref, g_ref, wg_ref, wu_ref, wd_ref, o_ref = refs
        x = jnp.where(pl.program_id(0) < n_lat_tiles, x_ref[...], c_ref[...])
    o_ref[...] = _half_ffn_rows(x, m_ref, g_ref, wg_ref, wu_ref, wd_ref, 0)


def _mix_ffn_kernel(*refs, n_lat_tiles, final):
    if n_lat_tiles is None:
        (x_ref, m_ref, yp_ref, yg_ref, ya_ref, yf_ref, wo_ref,
         g_ref, wg_ref, wu_ref, wd_ref, fg_ref, o_ref) = refs
        yp, yg, ya, yf = yp_ref[...], yg_ref[...], ya_ref[...], yf_ref[...]
    else:
        (x_ref, m_ref, yp_ref, yg_ref, ya_ref, yf_ref, cp_ref, cg_ref, ca_ref, cf_ref, wo_ref,
         g_ref, wg_ref, wu_ref, wd_ref, fg_ref, o_ref) = refs
        lat = pl.program_id(0) < n_lat_tiles
        yp = jnp.where(lat, yp_ref[...], cp_ref[...])
        yg = jnp.where(lat, yg_ref[...], cg_ref[...].reshape(yg_ref.shape))
        ya = jnp.where(lat, ya_ref[...], ca_ref[...])
        yf = jnp.where(lat, yf_ref[...], cf_ref[...])
    acc = _dot(yp.astype(BF16), wo_ref[0:256, :])
    acc = acc + _dot(yg.astype(BF16), wo_ref[256:512, :])
    acc = acc + _dot(ya.astype(BF16), wo_ref[512:768, :])
    acc = acc + _dot(yf.astype(BF16), wo_ref[768:1024, :])
    x = x_ref[...] + m_ref[5:6, :] * acc
    out = _half_ffn_rows(x, m_ref, g_ref, wg_ref, wu_ref, wd_ref, 6)
    if final:
        out = _rms(out) * fg_ref[...]
    o_ref[...] = out


def _mod_spec(layer, n_lat_tiles, tiles_per_batch, n_batch, d):
    def index_map(t):
        return (layer, jnp.where(t < n_lat_tiles, t // tiles_per_batch, n_batch), 0, 0)
    return pl.BlockSpec((None, None, N_MOD, d), index_map)


def _resident(arr, lead):
    block = (None,) * len(lead) + arr.shape[len(lead):]
    index = tuple(lead) + (0,) * (arr.ndim - len(lead))
    return pl.BlockSpec(block, lambda t: index, pipeline_mode=pl.Buffered(1))


def _ffn_weight_specs(g, wg, wu, wd, layer, half):
    d = g.shape[-1]
    lead = (layer, half)
    return [pl.BlockSpec((None, None, 1, d), lambda t: (layer, 2 * half, 0, 0)),
            _resident(wg, lead), _resident(wu, lead), _resident(wd, lead)]


def _half_ffn(x, mods, g, wg, wu, wd, *, layer, n_tiles, mod_map, ctx_rows=None):
    d = x.shape[1]
    n_lat = None if ctx_rows is None else x.shape[0] // TM
    if ctx_rows is None:
        rows, row_specs = [x], [pl.BlockSpec((TM, d), lambda t: (t, 0))]
    else:
        rows = [x, ctx_rows]
        row_specs = [pl.BlockSpec((TM, d), lambda t: (jnp.minimum(t, n_lat - 1), 0)),
                     pl.BlockSpec((TM, d), lambda t: (jnp.maximum(t - n_lat, 0), 0))]
    return pl.pallas_call(
        functools.partial(_ffn_kernel, n_lat_tiles=n_lat),
        out_shape=jax.ShapeDtypeStruct((n_tiles * TM, d), F32),
        grid=(n_tiles,),
        in_specs=row_specs + [mod_map] + _ffn_weight_specs(g, wg, wu, wd, layer, 0),
        out_specs=pl.BlockSpec((TM, d), lambda t: (t, 0)),
        compiler_params=_cparams(("arbitrary",)),
        name="half_ffn",
    )(*rows, mods, g, wg, wu, wd)


def _mix_ffn(x, mods, y_pool, y_gla, y_att, y_fnet, w_out, g, wg, wu, wd, fg,
             *, layer, n_tiles, mod_map, final, ctx_parts=None):
    d = x.shape[1]
    n_batch, gla_rows = y_gla.shape[0], y_gla.shape[1]
    lat_tiles = y_pool.shape[0] // TM
    tpb = lat_tiles // n_batch
    clamp = lambda t: jnp.minimum(t, lat_tiles - 1)
    lat_part = pl.BlockSpec((TM, 256), lambda t: (clamp(t), 0))
    gla_part = pl.BlockSpec((None, TM, 256), lambda t: (clamp(t) // tpb, clamp(t) % tpb, 0))
    parts, part_specs = [y_pool, y_gla, y_att, y_fnet], [lat_part, gla_part, lat_part, lat_part]
    if ctx_parts is not None:
        ctx_rows = gla_rows - tpb * TM
        assert n_batch * ctx_rows == TM and (tpb * TM) % ctx_rows == 0
        ctx_part = pl.BlockSpec((TM, 256), lambda t: (jnp.maximum(t - lat_tiles, 0), 0))
        gla_ctx = pl.BlockSpec((n_batch, ctx_rows, 256), lambda t: (0, tpb * TM // ctx_rows, 0))
        c_pool, c_att, c_fnet = ctx_parts
        parts += [c_pool, y_gla, c_att, c_fnet]
        part_specs += [ctx_part, gla_ctx, ctx_part, ctx_part]
    return pl.pallas_call(
        functools.partial(_mix_ffn_kernel, n_lat_tiles=None if ctx_parts is None else lat_tiles, final=final),
        out_shape=jax.ShapeDtypeStruct((n_tiles * TM, d), F32),
        grid=(n_tiles,),
        in_specs=[pl.BlockSpec((TM, d), lambda t: (t, 0)), mod_map] + part_specs
                 + [_resident(w_out, (layer,))] + _ffn_weight_specs(g, wg, wu, wd, layer, 1)
                 + [pl.BlockSpec((1, d), lambda t: (0, 0))],
        out_specs=pl.BlockSpec((TM, d), lambda t: (t, 0)),
        compiler_params=_cparams(("arbitrary",)),
        name="mix_ffn",
    )(x, mods, *parts, w_out, g, wg, wu, wd, fg.reshape(1, d))


def _rope(x, c, sa, sb):
    w = x.shape[1]
    return x * c + pltpu.roll(x, w - ROPE_FREQS, 1) * sa + pltpu.roll(x, ROPE_FREQS, 1) * sb


def _tile_rope_table(lines, col):
    return jnp.concatenate([col + lines[r:r + 1, :] for r in range(lines.shape[0])], axis=0)


def _inproj_kernel(x_ref, m_ref, g_ref, w_ref, wa_ref, ba_ref, e_ref, qg_ref, kg_ref,
                   lc_ref, lsa_ref, lsb_ref, cc_ref, csa_ref, csb_ref,
                   up_ref, gq_ref, gk_ref, gv_ref, og_ref, gf_ref, gb_ref,
                   aq_ref, ak_ref, av_ref, uf_ref):
    shift = m_ref[3:4, :]
    scale = m_ref[4:5, :]
    e = e_ref[...]
    n_parts = 2
    rp = TM // n_parts
    lp = rp // GRID_W
    for part in range(n_parts):
        rows = slice(part * rp, (part + 1) * rp)
        lines = slice(part * lp, (part + 1) * lp)
        h = ((_rms(x_ref[rows, :]) * g_ref[...]) * (1.0 + scale) + shift).astype(BF16)
        p = _dot(h, w_ref[...])

        up_ref[rows, :] = p[:, O_POOL:O_POOL + POOL_W]
        uf_ref[rows, :] = p[:, O_FN:O_FN + FNET_W]

        gq_ref[rows, :] = p[:, O_GQ:O_GQ + GLA_KW] * (GLA_DK ** -0.5)
        gk_ref[rows, :] = p[:, O_GK:O_GK + GLA_KW]
        gv_ref[rows, :] = p[:, O_GV:O_GV + GLA_W]
        og_ref[rows, :] = p[:, O_OG:O_OG + GLA_W]
        z = _dot(p[:, O_R:O_R + 128].astype(BF16), wa_ref[...]) + ba_ref[...]
        logsig = jnp.minimum(z, 0.0) - jnp.log(1.0 + jnp.exp(-jnp.abs(z)))
        gdec = logsig * (1.0 / GLA_TAU)
        gf_ref[rows, :] = gdec[:, :GLA_KW]
        gb_ref[rows, :] = gdec[:, GLA_KW:]

        rc = _tile_rope_table(lc_ref[lines, :], cc_ref[...])
        rsa = _tile_rope_table(lsa_ref[lines, :], csa_ref[...])
        rsb = _tile_rope_table(lsb_ref[lines, :], csb_ref[...])
        q = _group_rms(p[:, O_AQ:O_AQ + ATT_W], e, ATT_HD) * qg_ref[...]
        q = _rope(q, rc, rsa, rsb) * (ATT_HD ** -0.5 * math.log2(math.e))
        aq_ref[:, rows] = q.T.astype(BF16)
        k = _group_rms(p[:, O_AK:O_AK + ATT_KW], e[:ATT_KW, :ATT_KW], ATT_HD) * kg_ref[...]
        k = _rope(k, rc[:, :ATT_KW], rsa[:, :ATT_KW], rsb[:, :ATT_KW])
        ak_ref[rows, :] = k.astype(BF16)
        v_t = p[:, O_AV:O_AV + ATT_KW].T
        ones = jnp.ones((ATT_VROWS - ATT_HD, rp), F32)
        for kvh in range(ATT_KVH):
            av_ref[kvh, :, rows] = jnp.concatenate(
                [v_t[kvh * ATT_HD:(kvh + 1) * ATT_HD, :], ones], axis=0).astype(BF16)


def _in_projection(x, mods, g, w_in, wa_blk, ba_blk, e256, qg, kg, rope_lines, rope_cols,
                   *, layer, n_tiles, mod_map, line_map, col_map):
    t_rows, d = x.shape
    row = lambda w: pl.BlockSpec((TM, w), lambda t: (t, 0))
    const = lambda shape: pl.BlockSpec(shape, lambda t: (0,) * len(shape))
    per_layer = lambda arr: pl.BlockSpec((None,) + arr.shape[1:], lambda t: (layer,) + (0,) * (arr.ndim - 1))
    out_shapes = (
        jax.ShapeDtypeStruct((t_rows, POOL_W), F32),
        jax.ShapeDtypeStruct((t_rows, GLA_KW), F32),
        jax.ShapeDtypeStruct((t_rows, GLA_KW), F32),
        jax.ShapeDtypeStruct((t_rows, GLA_W), F32),
        jax.ShapeDtypeStruct((t_rows, GLA_W), F32),
        jax.ShapeDtypeStruct((t_rows, GLA_KW), F32),
        jax.ShapeDtypeStruct((t_rows, GLA_KW), F32),
        jax.ShapeDtypeStruct((n_tiles, ATT_W, TM), BF16),
        jax.ShapeDtypeStruct((t_rows, ATT_KW), BF16),
        jax.ShapeDtypeStruct((n_tiles, ATT_KVH, ATT_VROWS, TM), BF16),
        jax.ShapeDtypeStruct((t_rows, FNET_W), F32),
    )
    out_specs = (
        row(POOL_W), row(GLA_KW), row(GLA_KW), row(GLA_W), row(GLA_W), row(GLA_KW), row(GLA_KW),
        pl.BlockSpec((None, ATT_W, TM), lambda t: (t, 0, 0)),
        row(ATT_KW),
        pl.BlockSpec((None, ATT_KVH, ATT_VROWS, TM), lambda t: (t, 0, 0, 0)),
        row(FNET_W),
    )
    return pl.pallas_call(
        _inproj_kernel,
        out_shape=out_shapes,
        grid=(n_tiles,),
        in_specs=[
            pl.BlockSpec((TM, d), lambda t: (t, 0)),
            mod_map,
            pl.BlockSpec((None, None, 1, d), lambda t: (layer, 1, 0, 0)),
            per_layer(w_in),
            per_layer(wa_blk),
            per_layer(ba_blk),
            const(e256.shape),
            per_layer(qg),
            per_layer(kg),
        ] + [pl.BlockSpec((TM // GRID_W, ATT_W), line_map)] * 3
          + [pl.BlockSpec((None, GRID_W, ATT_W), col_map)] * 3,
        out_specs=out_specs,
        compiler_params=_cparams(("arbitrary",)),
        name="in_projection",
    )(x, mods, g, w_in, wa_blk, ba_blk, e256, qg, kg, *rope_lines, *rope_cols)


def _pool_kernel(u_ref, inv_ref, w_ref, s_ref, o_ref, pad_ref, *, seq, chunk):
    halo = POOL_HALO
    pad_ref[0:halo, :] = jnp.zeros((halo, POOL_W), F32)
    pad_ref[halo + seq:halo + seq + halo, :] = jnp.zeros((halo, POOL_W), F32)
    pad_ref[halo:halo + seq, :] = u_ref[...]
    rows = chunk + 2 * halo
    lane = lax.broadcasted_iota(jnp.int32, (chunk, POOL_W), 1)

    def body(ci, carry):
        c0 = pl.multiple_of(ci * chunk, chunk)
        xp = pad_ref[pl.ds(c0, rows), :]
        u = xp[halo:halo + chunk, :]
        acc = xp
        wsum = None
        for gi, w in enumerate(POOL_WINDOWS):
            acc = acc + pltpu.roll(acc, w // 2, 0)
            lead = w // 2 - 1
            win = acc if lead == 0 else pltpu.roll(acc, rows - lead, 0)
            win = win[halo:halo + chunk, :]
            wsum = win if wsum is None else jnp.where(lane >= gi * POOL_GW, win, wsum)
        m = wsum * inv_ref[pl.ds(c0, chunk), :] - u
        y = _dot(m.astype(BF16), w_ref[...]) * s_ref[...]
        o_ref[pl.ds(c0, chunk), :] = y
        return carry

    lax.fori_loop(0, seq // chunk, body, 0)


def _pool_inverse_counts(seq):
    t = jnp.arange(seq, dtype=jnp.int32)
    cols = []
    for w in POOL_WINDOWS:
        cnt = jnp.minimum(t + w // 2, seq) - jnp.maximum(t - w // 2, 0)
        cols.append(jnp.broadcast_to((1.0 / cnt.astype(F32))[:, None], (seq, POOL_GW)))
    return jnp.concatenate(cols, axis=1)


def _pool_mix(u, inv_cnt, w_blk, s_pool, *, layer, seq, n_seq, first_block):
    chunk = min(256, seq)
    return pl.pallas_call(
        functools.partial(_pool_kernel, seq=seq, chunk=chunk),
        out_shape=jax.ShapeDtypeStruct((n_seq * seq, POOL_W), F32),
        grid=(n_seq,),
        in_specs=[
            pl.BlockSpec((seq, POOL_W), lambda b: (first_block + b, 0)),
            pl.BlockSpec((seq, POOL_W), lambda b: (0, 0)),
            pl.BlockSpec((None, POOL_W, POOL_W), lambda b: (layer, 0, 0)),
            pl.BlockSpec((None, 1, POOL_W), lambda b: (layer, 0, 0)),
        ],
        out_specs=pl.BlockSpec((seq, POOL_W), lambda b: (b, 0)),
        scratch_shapes=[pltpu.VMEM((seq + 2 * POOL_HALO, POOL_W), F32)],
        compiler_params=_cparams(("arbitrary",)),
        name="pool_mix",
    )(u, inv_cnt, w_blk, s_pool)


def _chunk_cumsums(g):
    sl = 8
    row = lax.broadcasted_iota(jnp.int32, (sl, g.shape[1]), 0)
    pre, suf = [], []
    for i in range(g.shape[0] // sl):
        p = q = g[i * sl:(i + 1) * sl, :]
        s = 1
        while s < sl:
            p = p + jnp.where(row >= s, pltpu.roll(p, s, 0), 0.0)
            q = q + jnp.where(row < sl - s, pltpu.roll(q, sl - s, 0), 0.0)
            s *= 2
        pre.append(p)
        suf.append(q)
    per_chunk = GLA_CHUNK // sl
    for c in range(len(pre) // per_chunk):
        for v in range(1, per_chunk):
            i = c * per_chunk + v
            pre[i] = pre[i] + jnp.broadcast_to(pre[i - 1][sl - 1:sl, :], pre[i].shape)
            j = (c + 1) * per_chunk - 1 - v
            suf[j] = suf[j] + jnp.broadcast_to(suf[j + 1][0:1, :], suf[j].shape)
    return jnp.concatenate(pre, axis=0), jnp.concatenate(suf, axis=0)


def _gla_kernel(*refs, n_batch, reverse, finalize):
    refs = list(refs)
    take = lambda: [refs.pop(0) for _ in range(n_batch)]
    q_refs, k_refs, v_refs, g_refs = take(), take(), take(), take()
    ex_ref, mk_ref, bm_ref = refs.pop(0), refs.pop(0), refs.pop(0)
    if finalize:
        of_ref = refs.pop(0)
        og_refs = take()
        gn_ref, e_ref = refs.pop(0), refs.pop(0)
        o_ref, st_ref, ob_ref = refs
    else:
        o_ref, st_ref = refs
        ob_ref = o_ref

    @pl.when(pl.program_id(0) == 0)
    def _():
        st_ref[...] = jnp.zeros(st_ref.shape, F32)

    sides, tots = [], []
    for b in range(n_batch):
        g = g_refs[b][...]
        pre, suf = _chunk_cumsums(g)
        sides.append(suf if reverse else pre)
        tots.append(pre + suf - g)
    worst = jnp.max(jnp.abs(tots[0]))
    for b in range(1, n_batch):
        worst = jnp.maximum(worst, jnp.max(jnp.abs(tots[b])))
    in_range = worst <= GLA_FAST_RANGE

    @pl.when(in_range)
    def _():
        for b in range(n_batch):
            _gla_fast_tile(q_refs[b], k_refs[b], v_refs[b], mk_ref, bm_ref, st_ref.at[b], ob_ref.at[b],
                           sides[b], tots[b], reverse)

    @pl.when(jnp.logical_not(in_range))
    def _():
        for b in range(n_batch):
            _gla_exact_tile(q_refs[b], k_refs[b], v_refs[b], g_refs[b], ex_ref, mk_ref, st_ref.at[b],
                            ob_ref.at[b], reverse)

    if finalize:
        for b in range(n_batch):
            both = ob_ref[b] + of_ref[b]
            y = _group_rms(both, e_ref[...], GLA_DV) * gn_ref[...]
            o_ref[b] = y * _silu(og_refs[b][...])


def _gla_fast_tile(q_ref, k_ref, v_ref, mk_ref, bm_ref, st_ref, ob_ref, b, tot, reverse):
    ch = GLA_CHUNK
    half = 0.5 * tot
    q, k = q_ref[...], k_ref[...]
    q_in = (q * jnp.exp(b - half)).astype(BF16)
    k_in = (k * jnp.exp(half - b)).astype(BF16)
    q_st = (q * jnp.exp(b)).astype(BF16)
    k_st = (k * jnp.exp(tot - b)).astype(BF16)
    dec = jnp.exp(tot)
    mk = mk_ref[...]
    mk16 = mk.astype(BF16)
    bm16 = bm_ref[...]
    irow = lax.broadcasted_iota(jnp.int32, (ch, GLA_HEADS * ch), 0)
    jcol = lax.broadcasted_iota(jnp.int32, (ch, GLA_HEADS * ch), 1) % ch
    seen = (jcol >= irow) if reverse else (jcol <= irow)
    nt = (((1,), (1,)), ((), ()))
    st = st_ref[...]
    n_chunks = TP // ch
    for ci in range(n_chunks):
        c = (n_chunks - 1 - ci) if reverse else ci
        rows = slice(c * ch, (c + 1) * ch)
        kb = jnp.concatenate([k_in[rows]] * GLA_HEADS, axis=0) * mk16
        a = lax.dot_general(q_in[rows], kb, nt, preferred_element_type=F32)
        a = jnp.where(seen, a, 0.0).astype(BF16)
        v = v_ref[rows, :]
        vb = jnp.concatenate([v.astype(BF16)] * GLA_HEADS, axis=0) * bm16
        o = _dot(a, vb) + lax.dot_general(q_st[rows], st.astype(BF16), nt, preferred_element_type=F32)
        kv = lax.dot_general(v.astype(BF16), k_st[rows], (((0,), (0,)), ((), ())),
                             preferred_element_type=F32)
        st = st * dec[c * ch:c * ch + 1, :] + kv * mk
        ob_ref[rows, :] = o
    st_ref[...] = st


def _gla_exact_tile(q_ref, k_ref, v_ref, g_ref, ex_ref, mk_ref, st_ref, ob_ref, reverse):
    sub = GLA_SUB
    n_chunks = TP // sub
    row = lax.broadcasted_iota(jnp.int32, (sub, GLA_KW), 0)
    ex = ex_ref[...]
    mk = mk_ref[...]

    def body(ci, carry):
        c = (n_chunks - 1 - ci) if reverse else ci
        r0 = pl.multiple_of(c * sub, sub)
        q = q_ref[pl.ds(r0, sub), :]
        k = k_ref[pl.ds(r0, sub), :]
        v = v_ref[pl.ds(r0, sub), :]
        b = g_ref[pl.ds(r0, sub), :]
        s = 1
        while s < sub:
            if reverse:
                b = b + jnp.where(row < sub - s, pltpu.roll(b, sub - s, 0), 0.0)
            else:
                b = b + jnp.where(row >= s, pltpu.roll(b, s, 0), 0.0)
            s *= 2
        edge = b[0:1, :] if reverse else b[sub - 1:sub, :]
        st = st_ref[...]
        qd = (q * jnp.exp(b)).astype(BF16)
        o = lax.dot_general(qd, st.astype(BF16), (((1,), (1,)), ((), ())),
                            preferred_element_type=F32)
        parts = []
        for j in range(sub):
            seen = (row <= j) if reverse else (row >= j)
            dec = jnp.exp(jnp.where(seen, b - b[j:j + 1, :], NEG_BIG))
            parts.append((dec * q * k[j:j + 1, :]).astype(BF16))
        r = _dot(jnp.concatenate(parts, axis=0), ex)
        for j in range(sub):
            o = o + r[j * sub:(j + 1) * sub, :] * v[j:j + 1, :]
        kd = (k * jnp.exp(edge - b)).astype(BF16)
        kv = lax.dot_general(v.astype(BF16), kd, (((0,), (0,)), ((), ())),
                             preferred_element_type=F32)
        st_ref[...] = st * jnp.exp(edge) + kv * mk
        ob_ref[pl.ds(r0, sub), :] = o
        return carry

    lax.fori_loop(0, n_chunks, body, 0, unroll=2)


def _gla_step_tile(n_lat, n_ctx, reverse):
    def tile(s):
        is_ctx = s < n_ctx
        sc = jnp.where(is_ctx, s, 0)
        sl = jnp.where(is_ctx, 0, s - n_ctx)
        if reverse:
            sc = n_ctx - 1 - sc
            sl = n_lat - 1 - sl
        return is_ctx, sc, sl
    return tile


def _gla_mix(gq, gk, gv, gf, gb, og, ex, mk, gn, e256, *, n_batch, seq, ctx):
    n_lat, n_ctx = seq // TP, ctx // TP
    const = lambda shape: pl.BlockSpec(shape, lambda s: (0,) * len(shape))

    def run(reverse, finalize, gate, extra):
        tile = _gla_step_tile(n_lat, n_ctx, reverse)

        def rows(w, b):
            def index_map(s):
                is_ctx, sc, sl = tile(s)
                return (jnp.where(is_ctx, n_batch * n_lat + b * n_ctx + sc, b * n_lat + sl), 0)
            return pl.BlockSpec((TP, w), index_map)

        def per_batch(arr, w):
            return [arr] * n_batch, [rows(w, b) for b in range(n_batch)]

        def out_map(s):
            is_ctx, sc, sl = tile(s)
            return (0, jnp.where(is_ctx, n_lat + sc, sl), 0)

        stacked = pl.BlockSpec((n_batch, TP, GLA_W), out_map)
        args, in_specs = [], []
        for arr, w in ((gq, GLA_KW), (gk, GLA_KW), (gv, GLA_W), (gate, GLA_KW)):
            a, sp = per_batch(arr, w)
            args += a
            in_specs += sp
        args += [ex, mk, e256]
        in_specs += [const(ex.shape), const(mk.shape), const(e256.shape)]
        scratch = [pltpu.VMEM((n_batch, GLA_W, GLA_KW), F32)]
        if finalize:
            o_f, og_rows, gn_row, e_blk = extra
            a, sp = per_batch(og_rows, GLA_W)
            args += [o_f] + a + [gn_row, e_blk]
            in_specs += [stacked] + sp + [const((1, GLA_W)), const(e_blk.shape)]
            scratch.append(pltpu.VMEM((n_batch, TP, GLA_W), F32))
        return pl.pallas_call(
            functools.partial(_gla_kernel, n_batch=n_batch, reverse=reverse, finalize=finalize),
            out_shape=jax.ShapeDtypeStruct((n_batch, seq + ctx, GLA_W), F32),
            grid=(n_lat + n_ctx,),
            in_specs=in_specs,
            out_specs=stacked,
            scratch_shapes=scratch,
            compiler_params=_cparams(("arbitrary",)),
            name="gla_bwd" if reverse else "gla_fwd",
        )(*args)

    o_f = run(False, False, gf, None)
    return run(True, True, gb, [o_f, og, gn, e256])


def _att_query_groups(q_t):
    tq = q_t.shape[1]
    zero = jnp.zeros((ATT_HD, tq), BF16)
    groups = []
    for g in range(ATT_KVH):
        halves = []
        for h in (2 * g, 2 * g + 1):
            qh = q_t[h * ATT_HD:(h + 1) * ATT_HD, :]
            halves.append(jnp.concatenate([qh, zero] if g == 0 else [zero, qh], axis=0))
        groups.append(jnp.concatenate(halves, axis=1))
    return groups


def _att_store(o_ref, weighted, denom, tq):
    heads = []
    for g in range(ATT_KVH):
        og = weighted[g] / denom[g]
        heads += [og[:, :tq], og[:, tq:]]
    o_ref[...] = jnp.concatenate(heads, axis=0).T.astype(o_ref.dtype)


def _att_kernel(*refs, n_main, has_ctx, bounded):
    refs = list(refs)
    q_ref, km_ref, vm_ref = refs.pop(0), refs.pop(0), refs.pop(0)
    kc_ref, vc_ref = (refs.pop(0), refs.pop(0)) if has_ctx else (None, None)
    o_ref, s0_ref, s1_ref = refs
    q_t = q_ref[...]
    tq = q_t.shape[1]
    tk = vm_ref.shape[3]
    groups = _att_query_groups(q_t)

    def scores(kblk, buf):
        for g in range(ATT_KVH):
            buf[g, 0:kblk.shape[0], :] = _dot(kblk, groups[g])

    def consume(buf, rows, v_of, carry):
        out = []
        for g in range(ATT_KVH):
            m, l, acc = carry[g]
            s = buf[g, 0:rows, :]
            if bounded:
                pv = _dot(v_of(g), jnp.exp2(s).astype(BF16))
                out.append((m, l + pv[ATT_HD:ATT_HD + 1, :], acc + pv[:ATT_HD, :]))
                continue
            m_new = jnp.maximum(m, jnp.max(s, axis=0, keepdims=True))
            alpha = jnp.exp2(m - m_new)
            p = jnp.exp2(s - m_new).astype(BF16)
            pv = _dot(v_of(g), p)
            acc = alpha * acc + pv[:ATT_HD, :]
            l = alpha * l + pv[ATT_HD:ATT_HD + 1, :]
            out.append((m_new, l, acc))
        return tuple(out)

    def k_main(j):
        return km_ref[pl.ds(pl.multiple_of(j * tk, tk), tk), :]

    def v_main(j):
        return lambda g: vm_ref[j, g]

    def pair(j, carry, issue_next):
        scores(k_main(j + 1), s1_ref)
        carry = consume(s0_ref, tk, v_main(j), carry)
        issue_next(j + 2)
        return consume(s1_ref, tk, v_main(j + 1), carry)

    carry = tuple((jnp.full((1, 2 * tq), -jnp.inf, F32), jnp.zeros((1, 2 * tq), F32),
                   jnp.zeros((ATT_HD, 2 * tq), F32)) for _ in range(ATT_KVH))
    next_main = lambda j: scores(k_main(j), s0_ref)
    if has_ctx:
        after_main = lambda j: scores(kc_ref[...], s0_ref)
    else:
        after_main = lambda j: None

    scores(k_main(0), s0_ref)
    if n_main > 1:
        pairs_per_trip = 2 if n_main % 4 == 0 else 1

        def body(i, carry):
            for u in range(pairs_per_trip):
                carry = pair(2 * (pairs_per_trip * i + u), carry, next_main)
            return carry

        n_trips = n_main // (2 * pairs_per_trip)
        carry = lax.fori_loop(0, n_trips - 1, body, carry)
        for u in range(pairs_per_trip):
            last = u == pairs_per_trip - 1
            carry = pair(2 * (pairs_per_trip * (n_trips - 1) + u), carry, after_main if last else next_main)
    else:
        assert not has_ctx
        carry = consume(s0_ref, tk, v_main(0), carry)
    if has_ctx:
        carry = consume(s0_ref, kc_ref.shape[0], lambda g: vc_ref[g], carry)
    _att_store(o_ref, [c[2] for c in carry], [c[1] for c in carry], tq)


def _attention(aq, ak, av, *, n_batch, seq, ctx, ctx_queries, bounded=False):
    lat_tiles = seq // TM
    ctx_tile = n_batch * lat_tiles
    ctx_cols = lambda b, i: (ctx_tile + (b * ctx) // TM, 0, ((b * ctx) % TM) // ctx)
    ctx_vcols = lambda b, i: (ctx_tile + (b * ctx) // TM, 0, 0, ((b * ctx) % TM) // ctx)
    if not ctx_queries:
        sub = TM // TQ
        grid = (n_batch, seq // TQ)
        in_specs = [
            pl.BlockSpec((None, ATT_W, TQ), lambda b, i: (b * lat_tiles + i // sub, 0, i % sub)),
            pl.BlockSpec((seq, ATT_KW), lambda b, i: (b, 0)),
            pl.BlockSpec((lat_tiles, ATT_KVH, ATT_VROWS, TM), lambda b, i: (b, 0, 0, 0)),
            pl.BlockSpec((ctx, ATT_KW), lambda b, i: (n_batch * seq // ctx + b, 0)),
            pl.BlockSpec((None, ATT_KVH, ATT_VROWS, ctx), ctx_vcols),
        ]
        out_spec = pl.BlockSpec((TQ, ATT_W), lambda b, i: (b * (seq // TQ) + i, 0))
        assert lat_tiles % 2 == 0
        out_rows = n_batch * seq
        args = (aq, ak, av, ak, av)
        kern = functools.partial(_att_kernel, n_main=lat_tiles, has_ctx=True, bounded=bounded)
        s_shape = (ATT_KVH, TM, 2 * TQ)
    else:
        grid = (n_batch, 1)
        in_specs = [
            pl.BlockSpec((None, ATT_W, ctx), ctx_cols),
            pl.BlockSpec((ctx, ATT_KW), lambda b, i: (n_batch * seq // ctx + b, 0)),
            pl.BlockSpec((1, ATT_KVH, ATT_VROWS, ctx), ctx_vcols),
        ]
        out_spec = pl.BlockSpec((ctx, ATT_W), lambda b, i: (b, 0))
        kern = functools.partial(_att_kernel, n_main=1, has_ctx=False, bounded=False)
        out_rows = n_batch * ctx
        args = (aq, ak, av)
        s_shape = (ATT_KVH, ctx, 2 * ctx)
    return pl.pallas_call(
        kern,
        out_shape=jax.ShapeDtypeStruct((out_rows, ATT_W), BF16),
        grid=grid,
        in_specs=in_specs,
        out_specs=out_spec,
        scratch_shapes=[pltpu.VMEM(s_shape, F32), pltpu.VMEM(s_shape, F32)],
        compiler_params=_cparams(("arbitrary", "arbitrary")),
        name="attention_ctx" if ctx_queries else ("attention_bounded" if bounded else "attention"),
    )(*args)


def _fft_a_kernel(x_ref, d_ref, tr_ref, ti_ref, yr_ref, yi_ref):
    rb, w = x_ref.shape[1], x_ref.shape[2]
    x = jnp.concatenate([x_ref[:, r, :] for r in range(rb)], axis=1)
    y = _dot(d_ref[...], x.astype(BF16))
    tr, ti = tr_ref[...], ti_ref[...]
    for r in range(rb):
        yr, yi = y[:FFT_N1, r * w:(r + 1) * w], y[FFT_N1:, r * w:(r + 1) * w]
        c, s = tr[:, r:r + 1], ti[:, r:r + 1]
        yr_ref[:, r, :] = yr * c - yi * s
        yi_ref[:, r, :] = yr * s + yi * c


def _channel_mix(xr, xi, c_ref, wf_ref, norm):
    xc = jnp.concatenate([xr, xi], axis=1).astype(BF16)
    f = _dot(xc, c_ref[...]) * norm
    return _dot(f.astype(BF16), wf_ref[...])


def _fft_c_kernel(yr_ref, yi_ref, m_ref, c_ref, wf_ref, o_ref, *, n2, norm):
    w = yr_ref.shape[1]
    blk = jnp.concatenate(
        [jnp.concatenate([yr_ref[j * n2:(j + 1) * n2, :] for j in range(8)], axis=1),
         jnp.concatenate([yi_ref[j * n2:(j + 1) * n2, :] for j in range(8)], axis=1)], axis=0)
    x = _dot(m_ref[...], blk.astype(BF16))
    xr = jnp.concatenate([x[:n2, j * w:(j + 1) * w] for j in range(8)], axis=0)
    xi = jnp.concatenate([x[n2:, j * w:(j + 1) * w] for j in range(8)], axis=0)
    y = _channel_mix(xr, xi, c_ref, wf_ref, norm)
    for j in range(8):
        o_ref[:, j, :] = y[j * n2:(j + 1) * n2, :]


def _fft_dense_kernel(x_ref, d_ref, c_ref, wf_ref, o_ref, *, n, norm):
    x = _dot(d_ref[...], x_ref[...].astype(BF16))
    o_ref[...] = _channel_mix(x[:n, :], x[n:, :], c_ref, wf_ref, norm)


def _dft_parts(n):
    idx = np.arange(n)
    ang = 2.0 * np.pi * ((idx[:, None] * idx[None, :]) % n) / n
    return np.cos(ang), np.sin(ang)


def _mxu_const(a):
    return jnp.asarray(a, F32).astype(BF16)


def _fnet_consts(seq):
    n1, n2 = FFT_N1, seq // FFT_N1
    c1, s1 = _dft_parts(n1)
    da = np.concatenate([c1, -s1], axis=0)
    k1 = np.arange(n1)[:, None]
    m2 = np.arange(n2)[None, :]
    ang = 2.0 * np.pi * ((k1 * m2) % seq) / seq
    tw_r, tw_i = np.cos(ang), -np.sin(ang)
    c2, s2 = _dft_parts(n2)
    mc = np.block([[c2, s2], [-s2, c2]])
    return da, tw_r, tw_i, mc


def _channel_consts():
    cc, sc = _dft_parts(FNET_HD)
    eye = np.eye(FNET_HEADS)
    return np.concatenate([np.kron(eye, cc), np.kron(eye, sc)], axis=0)


def _fnet_latent(uf3, wf, *, layer, n_batch, seq):
    n1, n2 = FFT_N1, seq // FFT_N1
    da, tw_r, tw_i, mc = _fnet_consts(seq)
    da, mc, cc = _mxu_const(da), _mxu_const(mc), _mxu_const(_channel_consts())
    rb = min(n2, 32)
    blocked = lambda tw: jnp.asarray(tw.reshape(n1, n2 // rb, rb).transpose(1, 0, 2), F32)
    const2 = lambda shape: pl.BlockSpec(shape, lambda b, j: (0,) * len(shape))
    slab = pl.BlockSpec((n1, rb, FNET_W), lambda b, j: (b, j, 0))
    yr, yi = pl.pallas_call(
        _fft_a_kernel,
        out_shape=(jax.ShapeDtypeStruct((n_batch * n1, n2, FNET_W), F32),) * 2,
        grid=(n_batch, n2 // rb),
        in_specs=[
            slab,
            const2(da.shape),
            pl.BlockSpec((None, n1, rb), lambda b, j: (j, 0, 0)),
            pl.BlockSpec((None, n1, rb), lambda b, j: (j, 0, 0)),
        ],
        out_specs=(slab, slab),
        compiler_params=_cparams(("arbitrary", "arbitrary")),
        name="fft_stage_a",
    )(uf3, da, blocked(tw_r), blocked(tw_i))
    yr = yr.reshape(n_batch * n1 * n2, FNET_W)
    yi = yi.reshape(n_batch * n1 * n2, FNET_W)
    norm = 1.0 / math.sqrt(seq * FNET_HD)
    out = pl.pallas_call(
        functools.partial(_fft_c_kernel, n2=n2, norm=norm),
        out_shape=jax.ShapeDtypeStruct((n_batch, n2, n1, FNET_W), F32),
        grid=(n_batch, n1 // 8),
        in_specs=[
            pl.BlockSpec((8 * n2, FNET_W), lambda b, i: (b * (n1 // 8) + i, 0)),
            pl.BlockSpec((8 * n2, FNET_W), lambda b, i: (b * (n1 // 8) + i, 0)),
            const2(mc.shape), const2(cc.shape),
            pl.BlockSpec((None,) + wf.shape[1:], lambda b, i: (layer, 0, 0)),
        ],
        out_specs=pl.BlockSpec((None, n2, 8, FNET_W), lambda b, i: (b, 0, i, 0)),
        compiler_params=_cparams(("arbitrary", "arbitrary")),
        name="fft_stage_c",
    )(yr, yi, mc, cc, wf)
    return out.reshape(n_batch * seq, FNET_W)


def _fnet_context(uf, wf, *, layer, n_batch, ctx, first_block):
    c, s = _dft_parts(ctx)
    dd, cc = _mxu_const(np.concatenate([c, -s], axis=0)), _mxu_const(_channel_consts())
    const = lambda shape: pl.BlockSpec(shape, lambda b: (0,) * len(shape))
    return pl.pallas_call(
        functools.partial(_fft_dense_kernel, n=ctx, norm=1.0 / math.sqrt(ctx * FNET_HD)),
        out_shape=jax.ShapeDtypeStruct((n_batch * ctx, FNET_W), F32),
        grid=(n_batch,),
        in_specs=[pl.BlockSpec((ctx, FNET_W), lambda b: (first_block + b, 0)),
                  const(dd.shape), const(cc.shape),
                  pl.BlockSpec((None,) + wf.shape[1:], lambda b: (layer, 0, 0))],
        out_specs=pl.BlockSpec((ctx, FNET_W), lambda b: (b, 0)),
        compiler_params=_cparams(("arbitrary",)),
        name="fft_context",
    )(uf, dd, cc, wf)


def _rope_tables(seq):
    freqs = ROPE_THETA ** (-jnp.arange(ROPE_FREQS, dtype=F32) / ROPE_FREQS)
    row_ang = jnp.arange(seq // GRID_W, dtype=F32)[:, None] * freqs
    col_ang = jnp.arange(GRID_W, dtype=F32)[:, None] * freqs
    lanes = np.arange(ATT_HD)
    is_row = jnp.asarray(np.tile(lanes < 2 * ROPE_FREQS, ATT_QH)[None, :], F32)
    first_half = jnp.asarray(np.tile((lanes // ROPE_FREQS) % 2 == 0, ATT_QH)[None, :], F32)
    spread = lambda a: jnp.tile(a, (1, ATT_W // ROPE_FREQS))

    def tables(ang, mask, ident_lines):
        cos, sin = spread(jnp.cos(ang)) * mask, spread(jnp.sin(ang)) * mask
        ident = jnp.zeros((ident_lines, ATT_W), F32)
        return (jnp.concatenate([cos, ident + mask], axis=0),
                jnp.concatenate([-sin * first_half, ident], axis=0),
                jnp.concatenate([sin * (1.0 - first_half), ident], axis=0))

    rows = tables(row_ang, is_row, TM // GRID_W)
    cols = tuple(t.reshape(2, GRID_W, ATT_W) for t in tables(col_ang, 1.0 - is_row, GRID_W))
    return rows, cols


def _block_ones(width, group):
    return jnp.asarray(np.kron(np.eye(width // group), np.ones((group, group))), BF16)


def kernel(x, c, ctx, c_ctx, w_mod, b_mod, norm_g, ffn_wg, ffn_wu, ffn_wd, w_in, w_out,
           pool_w, pool_scale, gla_wa, gla_ba, gla_norm, att_qnorm, att_knorm, fnet_w, final_norm):
    n_batch, seq, d = x.shape
    n_ctx = ctx.shape[1]
    assert d == D_MODEL and seq % TM == 0 and (n_batch * n_ctx) % TM == 0 and n_ctx % TP == 0
    assert seq % (8 * FFT_N1) == 0 and n_ctx <= TM and TM % n_ctx == 0 and n_batch + 1 <= 8
    lat_rows = n_batch * seq
    lat_tiles = lat_rows // TM
    all_tiles = lat_tiles + (n_batch * n_ctx) // TM
    tiles_per_batch = seq // TM
    line_map = lambda t: (jnp.where(t < lat_tiles, t % tiles_per_batch, tiles_per_batch), 0)
    col_map = lambda t: (jnp.where(t < lat_tiles, 0, 1), 0, 0)

    cvec = jnp.concatenate([c, c_ctx[None, :], jnp.zeros((8 - n_batch - 1, d), F32)], axis=0)
    mods = _modulation(cvec, w_mod, b_mod).reshape(DEPTH, 8, N_MOD, d)

    rope_lines, rope_cols = _rope_tables(seq)
    e256 = _block_ones(ATT_W, ATT_HD)
    gla_ex = _block_ones(GLA_W, GLA_DV)[::2, :]
    gla_mk = jnp.asarray(np.kron(np.eye(GLA_HEADS), np.ones((GLA_DV, GLA_DK))), F32)

    wg, wu, wd = ffn_wg.astype(BF16), ffn_wu.astype(BF16), ffn_wd.astype(BF16)
    wi = jnp.concatenate([w_in[..., :768], w_in[..., 800:D_IN], w_in[..., 768:800],
                          jnp.zeros((DEPTH, d, D_IN_PAD - D_IN), F32)], axis=-1).astype(BF16)
    wa_blk = jnp.zeros((DEPTH, 128, 2 * GLA_KW), F32)
    wa_blk = wa_blk.at[:, :GLA_RANK, :GLA_KW].set(gla_wa[:, 0]).at[:, GLA_RANK:2 * GLA_RANK, GLA_KW:].set(gla_wa[:, 1])
    wa_blk = wa_blk.astype(BF16)
    ba_blk = gla_ba.reshape(DEPTH, 1, 2 * GLA_KW)
    qg = jnp.tile(att_qnorm, (1, ATT_QH))[:, None, :]
    kg = jnp.tile(att_knorm, (1, ATT_KVH))[:, None, :]
    pool_blk = jnp.zeros((DEPTH, POOL_W, POOL_W), F32)
    for gi in range(len(POOL_WINDOWS)):
        sl = slice(gi * POOL_GW, (gi + 1) * POOL_GW)
        pool_blk = pool_blk.at[:, sl, sl].set(pool_w[:, gi])
    pool_blk = pool_blk.astype(BF16)
    pool_s = pool_scale[:, None, :]
    wf = fnet_w.astype(BF16)
    wo = w_out.astype(BF16)
    norm4 = norm_g[:, :, None, :]
    n2 = seq // FFT_N1
    inv_lat = _pool_inverse_counts(seq)

    xs = x.reshape(lat_rows, d)
    ctx_rows = ctx.reshape(n_batch * n_ctx, d)
    for i in range(DEPTH):
        ctx_out = i < DEPTH - 1
        mod_map = _mod_spec(i, lat_tiles, tiles_per_batch, n_batch, d)
        gn = jnp.tile(gla_norm[i], GLA_HEADS)[None, :]

        xs = _half_ffn(xs, mods, norm4, wg, wu, wd, layer=i, n_tiles=all_tiles,
                       mod_map=mod_map, ctx_rows=ctx_rows if i == 0 else None)
        (u_pool, gq, gk, gv, og, gf, gb, aq, ak, av, uf) = _in_projection(
            xs, mods, norm4, wi, wa_blk, ba_blk, e256, qg, kg, rope_lines, rope_cols,
            layer=i, n_tiles=all_tiles, mod_map=mod_map, line_map=line_map, col_map=col_map)

        y_pool = _pool_mix(u_pool, inv_lat, pool_blk, pool_s, layer=i, seq=seq, n_seq=n_batch, first_block=0)
        y_gla = _gla_mix(gq, gk, gv, gf, gb, og, gla_ex, gla_mk, gn, e256, n_batch=n_batch, seq=seq, ctx=n_ctx)
        score_bound = (ATT_HD * ATT_HD ** -0.5 * math.log2(math.e) * ATT_BOUND_SLACK
                       * jnp.max(jnp.abs(att_qnorm[i])) * jnp.max(jnp.abs(att_knorm[i])))
        attend = functools.partial(_attention, aq, ak, av, n_batch=n_batch, seq=seq, ctx=n_ctx, ctx_queries=False)
        y_att = lax.cond(score_bound <= ATT_BOUND_MAX, lambda: attend(bounded=True), lambda: attend())
        y_fnet = _fnet_latent(uf.reshape(uf.shape[0] // n2, n2, FNET_W), wf, layer=i, n_batch=n_batch, seq=seq)
        n_tiles, ctx_parts = lat_tiles, None
        if ctx_out:
            first_ctx = lat_rows // n_ctx
            ctx_parts = (
                _pool_mix(u_pool, _pool_inverse_counts(n_ctx), pool_blk, pool_s, layer=i, seq=n_ctx, n_seq=n_batch,
                          first_block=first_ctx),
                _attention(aq, ak, av, n_batch=n_batch, seq=seq, ctx=n_ctx, ctx_queries=True),
                _fnet_context(uf, wf, layer=i, n_batch=n_batch, ctx=n_ctx, first_block=first_ctx),
            )
            n_tiles = all_tiles
        xs = _mix_ffn(xs, mods, y_pool, y_gla, y_att, y_fnet, wo, norm4, wg, wu, wd, final_norm,
                      layer=i, n_tiles=n_tiles, mod_map=mod_map, final=not ctx_out, ctx_parts=ctx_parts)
    return xs.reshape(n_batch, seq, d)
```

```python
import functools
import math

import jax
import jax.numpy as jnp
import numpy as np
from jax import lax
from jax.experimental import pallas as pl
from jax.experimental.pallas import tpu as pltpu

F32 = jnp.float32
BF16 = jnp.bfloat16

D_MODEL = 1024
DEPTH = 2
GRID_W = 64
EPS = 1e-6
N_MOD = 9
D_FF = 2816

POOL_W = 256
POOL_WINDOWS = (2, 4, 8, 16)
POOL_GW = 64
POOL_HALO = 8

GLA_HEADS = 4
GLA_W = 256
GLA_DV = 64
GLA_DK = 32
GLA_RANK = 16
GLA_TAU = 16.0
GLA_KW = GLA_HEADS * GLA_DK
GLA_SUB = 16
GLA_CHUNK = 64
GLA_FAST_RANGE = 150.0

ATT_W = 256
ATT_HD = 64
ATT_QH = 4
ATT_KVH = 2
ATT_KW = ATT_KVH * ATT_HD
ATT_VROWS = ATT_HD + 16
ATT_BOUND_SLACK = 1.02
ATT_BOUND_MAX = 50.0
ROPE_FREQS = 16
ROPE_THETA = 10000.0

FNET_W = 256
FNET_HEADS = 4
FNET_HD = 64
FFT_N1 = 64

D_IN = 1824
D_IN_PAD = 2048

O_POOL, O_GQ, O_GK, O_GV, O_OG, O_AQ, O_AK, O_AV, O_FN, O_Z = 0, 256, 384, 512, 768, 1024, 1280, 1408, 1536, 1792

TM = 512
TQ = 256
TP = 256
MXU_TILE = 256
FF_CHUNKS = ((0, 6 * MXU_TILE), (6 * MXU_TILE, D_FF))
NEG_BIG = -1e30

VMEM_LIMIT = 56 * 1024 * 1024


def _cparams(sem):
    return pltpu.CompilerParams(dimension_semantics=sem, vmem_limit_bytes=VMEM_LIMIT)


def _dot(a, b):
    return jnp.dot(a, b, preferred_element_type=F32)


def _rms(x):
    return x * lax.rsqrt(jnp.mean(x * x, axis=-1, keepdims=True) + EPS)


def _silu(x):
    return x * jax.nn.sigmoid(x)


def _group_rms(x, e, width):
    ss = _dot((x * x).astype(BF16), e)
    return x * lax.rsqrt(ss * (1.0 / width) + EPS)


def _mod_kernel(c_ref, w_ref, b_ref, o_ref):
    s = _silu(c_ref[...]).astype(BF16)
    o_ref[...] = _dot(s, w_ref[...].astype(BF16)) + b_ref[...]


def _modulation(cvec, w_mod, b_mod):
    depth, d, nd = w_mod.shape
    tn = nd // 8
    return pl.pallas_call(
        _mod_kernel,
        out_shape=jax.ShapeDtypeStruct((depth, 8, nd), F32),
        grid=(depth, nd // tn),
        in_specs=[
            pl.BlockSpec((8, d), lambda l, j: (0, 0)),
            pl.BlockSpec((None, d, tn), lambda l, j: (l, 0, j)),
            pl.BlockSpec((None, 1, tn), lambda l, j: (l, 0, j)),
        ],
        out_specs=pl.BlockSpec((None, 8, tn), lambda l, j: (l, 0, j)),
        compiler_params=_cparams(("arbitrary", "arbitrary")),
        name="modulation",
    )(cvec, w_mod, b_mod.reshape(depth, 1, nd))


def _half_ffn_rows(x, m_ref, g_ref, wg_ref, wu_ref, wd_ref, mod_base):
    shift = m_ref[mod_base:mod_base + 1, :]
    scale = m_ref[mod_base + 1:mod_base + 2, :]
    gate = m_ref[mod_base + 2:mod_base + 3, :]
    h = ((_rms(x) * g_ref[...]) * (1.0 + scale) + shift).astype(BF16)
    y = jnp.zeros(x.shape, F32)
    for lo, hi in FF_CHUNKS:
        sl = slice(lo, hi)
        a = _dot(h, wg_ref[:, sl])
        u = _dot(h, wu_ref[:, sl])
        y = y + _dot((_silu(a) * u).astype(BF16), wd_ref[sl, :])
    return x + (0.5 * gate) * y


def _ffn_kernel(*refs, n_lat_tiles):
    if n_lat_tiles is None:
        x_ref, m_ref, g_ref, wg_ref, wu_ref, wd_ref, o_ref = refs
        x = x_ref[...]
    else:
        x_ref, c_ref, m_ref, g_ref, wg_ref, wu_ref, wd_ref, o_ref = refs
        x = jnp.where(pl.program_id(0) < n_lat_tiles, x_ref[...], c_ref[...])
    o_ref[...] = _half_ffn_rows(x, m_ref, g_ref, wg_ref, wu_ref, wd_ref, 0)


def _mix_ffn_kernel(*refs, n_lat_tiles, final):
    if n_lat_tiles is None:
        (x_ref, m_ref, yp_ref, yg_ref, ya_ref, yf_ref, wo_ref,
         g_ref, wg_ref, wu_ref, wd_ref, fg_ref, o_ref) = refs
        yp, yg, ya, yf = yp_ref[...], yg_ref[...], ya_ref[...], yf_ref[...]
    else:
        (x_ref, m_ref, yp_ref, yg_ref, ya_ref, yf_ref, cp_ref, cg_ref, ca_ref, cf_ref, wo_ref,
         g_ref, wg_ref, wu_ref, wd_ref, fg_ref, o_ref) = refs
        lat = pl.program_id(0) < n_lat_tiles
        yp = jnp.where(lat, yp_ref[...], cp_ref[...])
        yg = jnp.where(lat, yg_ref[...], cg_ref[...].reshape(yg_ref.shape))
        ya = jnp.where(lat, ya_ref[...], ca_ref[...])
        yf = jnp.where(lat, yf_ref[...], cf_ref[...])
    acc = _dot(yp.astype(BF16), wo_ref[0:256, :])
    acc = acc + _dot(yg.astype(BF16), wo_ref[256:512, :])
    acc = acc + _dot(ya.astype(BF16), wo_ref[512:768, :])
    acc = acc + _dot(yf.astype(BF16), wo_ref[768:1024, :])
    x = x_ref[...] + m_ref[5:6, :] * acc
    out = _half_ffn_rows(x, m_ref, g_ref, wg_ref, wu_ref, wd_ref, 6)
    if final:
        out = _rms(out) * fg_ref[...]
    o_ref[...] = out


def _mod_spec(layer, n_lat_tiles, tiles_per_batch, n_batch, d):
    def index_map(t):
        return (layer, jnp.where(t < n_lat_tiles, t // tiles_per_batch, n_batch), 0, 0)
    return pl.BlockSpec((None, None, N_MOD, d), index_map)


def _resident(arr, lead):
    block = (None,) * len(lead) + arr.shape[len(lead):]
    index = tuple(lead) + (0,) * (arr.ndim - len(lead))
    return pl.BlockSpec(block, lambda t: index, pipeline_mode=pl.Buffered(1))


def _ffn_weight_specs(g, wg, wu, wd, layer, half):
    d = g.shape[-1]
    lead = (layer, half)
    return [pl.BlockSpec((None, None, 1, d), lambda t: (layer, 2 * half, 0, 0)),
            _resident(wg, lead), _resident(wu, lead), _resident(wd, lead)]


def _half_ffn(x, mods, g, wg, wu, wd, *, layer, n_tiles, mod_map, ctx_rows=None):
    d = x.shape[1]
    n_lat = None if ctx_rows is None else x.shape[0] // TM
    if ctx_rows is None:
        rows, row_specs = [x], [pl.BlockSpec((TM, d), lambda t: (t, 0))]
    else:
        rows = [x, ctx_rows]
        row_specs = [pl.BlockSpec((TM, d), lambda t: (jnp.minimum(t, n_lat - 1), 0)),
                     pl.BlockSpec((TM, d), lambda t: (jnp.maximum(t - n_lat, 0), 0))]
    return pl.pallas_call(
        functools.partial(_ffn_kernel, n_lat_tiles=n_lat),
        out_shape=jax.ShapeDtypeStruct((n_tiles * TM, d), F32),
        grid=(n_tiles,),
        in_specs=row_specs + [mod_map] + _ffn_weight_specs(g, wg, wu, wd, layer, 0),
        out_specs=pl.BlockSpec((TM, d), lambda t: (t, 0)),
        compiler_params=_cparams(("arbitrary",)),
        name="half_ffn",
    )(*rows, mods, g, wg, wu, wd)


def _mix_ffn(x, mods, y_pool, y_gla, y_att, y_fnet, w_out, g, wg, wu, wd, fg,
             *, layer, n_tiles, mod_map, final, ctx_parts=None):
    d = x.shape[1]
    n_batch, gla_rows = y_gla.shape[0], y_gla.shape[1]
    lat_tiles = y_pool.shape[0] // TM
    tpb = lat_tiles // n_batch
    clamp = lambda t: jnp.minimum(t, lat_tiles - 1)
    lat_part = pl.BlockSpec((TM, 256), lambda t: (clamp(t), 0))
    gla_part = pl.BlockSpec((None, TM, 256), lambda t: (clamp(t) // tpb, clamp(t) % tpb, 0))
    parts, part_specs = [y_pool, y_gla, y_att, y_fnet], [lat_part, gla_part, lat_part, lat_part]
    if ctx_parts is not None:
        ctx_rows = gla_rows - tpb * TM
        assert n_batch * ctx_rows == TM and (tpb * TM) % ctx_rows == 0
        ctx_part = pl.BlockSpec((TM, 256), lambda t: (jnp.maximum(t - lat_tiles, 0), 0))
        gla_ctx = pl.BlockSpec((n_batch, ctx_rows, 256), lambda t: (0, tpb * TM // ctx_rows, 0))
        c_pool, c_att, c_fnet = ctx_parts
        parts += [c_pool, y_gla, c_att, c_fnet]
        part_specs += [ctx_part, gla_ctx, ctx_part, ctx_part]
    return pl.pallas_call(
        functools.partial(_mix_ffn_kernel, n_lat_tiles=None if ctx_parts is None else lat_tiles, final=final),
        out_shape=jax.ShapeDtypeStruct((n_tiles * TM, d), F32),
        grid=(n_tiles,),
        in_specs=[pl.BlockSpec((TM, d), lambda t: (t, 0)), mod_map] + part_specs
                 + [_resident(w_out, (layer,))] + _ffn_weight_specs(g, wg, wu, wd, layer, 1)
                 + [pl.BlockSpec((1, d), lambda t: (0, 0))],
        out_specs=pl.BlockSpec((TM, d), lambda t: (t, 0)),
        compiler_params=_cparams(("arbitrary",)),
        name="mix_ffn",
    )(x, mods, *parts, w_out, g, wg, wu, wd, fg.reshape(1, d))


def _fold_gate_kernel(wr_ref, wa_ref, o_ref):
    a, b = wr_ref[...], wa_ref[...]
    a_hi, b_hi = a.astype(BF16), b.astype(BF16)
    a_lo, b_lo = (a - a_hi.astype(F32)).astype(BF16), (b - b_hi.astype(F32)).astype(BF16)
    o_ref[...] = _dot(a_hi, b_hi) + (_dot(a_hi, b_lo) + _dot(a_lo, b_hi))


def _fold_gates(w_rank, wa_blk):
    depth, d, kr = w_rank.shape
    return pl.pallas_call(
        _fold_gate_kernel,
        out_shape=jax.ShapeDtypeStruct((depth, d, wa_blk.shape[2]), F32),
        grid=(depth,),
        in_specs=[pl.BlockSpec((None, d, kr), lambda l: (l, 0, 0)),
                  pl.BlockSpec((None,) + wa_blk.shape[1:], lambda l: (l, 0, 0))],
        out_specs=pl.BlockSpec((None, d, wa_blk.shape[2]), lambda l: (l, 0, 0)),
        compiler_params=_cparams(("arbitrary",)),
        name="fold_gates",
    )(w_rank, wa_blk)


def _rope(x, c, sa, sb):
    w = x.shape[1]
    return x * c + pltpu.roll(x, w - ROPE_FREQS, 1) * sa + pltpu.roll(x, ROPE_FREQS, 1) * sb


def _tile_rope_table(lines, col):
    return jnp.concatenate([col + lines[r:r + 1, :] for r in range(lines.shape[0])], axis=0)


def _inproj_kernel(x_ref, m_ref, g_ref, w_ref, ba_ref, e_ref, qg_ref, kg_ref,
                   lc_ref, lsa_ref, lsb_ref, cc_ref, csa_ref, csb_ref,
                   up_ref, gq_ref, gk_ref, gv_ref, og_ref, gf_ref, gb_ref,
                   aq_ref, ak_ref, av_ref, uf_ref):
    shift = m_ref[3:4, :]
    scale = m_ref[4:5, :]
    e = e_ref[...]
    n_parts = 2
    rp = TM // n_parts
    lp = rp // GRID_W
    for part in range(n_parts):
        rows = slice(part * rp, (part + 1) * rp)
        lines = slice(part * lp, (part + 1) * lp)
        h = ((_rms(x_ref[rows, :]) * g_ref[...]) * (1.0 + scale) + shift).astype(BF16)
        p = _dot(h, w_ref[...])

        up_ref[rows, :] = p[:, O_POOL:O_POOL + POOL_W]
        uf_ref[rows, :] = p[:, O_FN:O_FN + FNET_W]

        gq_ref[rows, :] = p[:, O_GQ:O_GQ + GLA_KW] * (GLA_DK ** -0.5)
        gk_ref[rows, :] = p[:, O_GK:O_GK + GLA_KW]
        gv_ref[rows, :] = p[:, O_GV:O_GV + GLA_W]
        og_ref[rows, :] = p[:, O_OG:O_OG + GLA_W]
        z = p[:, O_Z:O_Z + 2 * GLA_KW] + ba_ref[...]
        logsig = jnp.minimum(z, 0.0) - jnp.log(1.0 + jnp.exp(-jnp.abs(z)))
        gdec = logsig * (1.0 / GLA_TAU)
        gf_ref[rows, :] = gdec[:, :GLA_KW]
        gb_ref[rows, :] = gdec[:, GLA_KW:]

        rc = _tile_rope_table(lc_ref[lines, :], cc_ref[...])
        rsa = _tile_rope_table(lsa_ref[lines, :], csa_ref[...])
        rsb = _tile_rope_table(lsb_ref[lines, :], csb_ref[...])
        q = _group_rms(p[:, O_AQ:O_AQ + ATT_W], e, ATT_HD) * qg_ref[...]
        q = _rope(q, rc, rsa, rsb) * (ATT_HD ** -0.5 * math.log2(math.e))
        aq_ref[:, rows] = q.T.astype(BF16)
        k = _group_rms(p[:, O_AK:O_AK + ATT_KW], e[:ATT_KW, :ATT_KW], ATT_HD) * kg_ref[...]
        k = _rope(k, rc[:, :ATT_KW], rsa[:, :ATT_KW], rsb[:, :ATT_KW])
        ak_ref[rows, :] = k.astype(BF16)
        v_t = p[:, O_AV:O_AV + ATT_KW].T
        ones = jnp.ones((ATT_VROWS - ATT_HD, rp), F32)
        for kvh in range(ATT_KVH):
            av_ref[kvh, :, rows] = jnp.concatenate(
                [v_t[kvh * ATT_HD:(kvh + 1) * ATT_HD, :], ones], axis=0).astype(BF16)


def _in_projection(x, mods, g, w_in, ba_blk, e256, qg, kg, rope_lines, rope_cols,
                   *, layer, n_tiles, mod_map, line_map, col_map):
    t_rows, d = x.shape
    row = lambda w: pl.BlockSpec((TM, w), lambda t: (t, 0))
    const = lambda shape: pl.BlockSpec(shape, lambda t: (0,) * len(shape))
    per_layer = lambda arr: pl.BlockSpec((None,) + arr.shape[1:], lambda t: (layer,) + (0,) * (arr.ndim - 1))
    out_shapes = (
        jax.ShapeDtypeStruct((t_rows, POOL_W), F32),
        jax.ShapeDtypeStruct((t_rows, GLA_KW), F32),
        jax.ShapeDtypeStruct((t_rows, GLA_KW), F32),
        jax.ShapeDtypeStruct((t_rows, GLA_W), F32),
        jax.ShapeDtypeStruct((t_rows, GLA_W), F32),
        jax.ShapeDtypeStruct((t_rows, GLA_KW), F32),
        jax.ShapeDtypeStruct((t_rows, GLA_KW), F32),
        jax.ShapeDtypeStruct((n_tiles, ATT_W, TM), BF16),
        jax.ShapeDtypeStruct((t_rows, ATT_KW), BF16),
        jax.ShapeDtypeStruct((n_tiles, ATT_KVH, ATT_VROWS, TM), BF16),
        jax.ShapeDtypeStruct((t_rows, FNET_W), F32),
    )
    out_specs = (
        row(POOL_W), row(GLA_KW), row(GLA_KW), row(GLA_W), row(GLA_W), row(GLA_KW), row(GLA_KW),
        pl.BlockSpec((None, ATT_W, TM), lambda t: (t, 0, 0)),
        row(ATT_KW),
        pl.BlockSpec((None, ATT_KVH, ATT_VROWS, TM), lambda t: (t, 0, 0, 0)),
        row(FNET_W),
    )
    return pl.pallas_call(
        _inproj_kernel,
        out_shape=out_shapes,
        grid=(n_tiles,),
        in_specs=[
            pl.BlockSpec((TM, d), lambda t: (t, 0)),
            mod_map,
            pl.BlockSpec((None, None, 1, d), lambda t: (layer, 1, 0, 0)),
            per_layer(w_in),
            per_layer(ba_blk),
            const(e256.shape),
            per_layer(qg),
            per_layer(kg),
        ] + [pl.BlockSpec((TM // GRID_W, ATT_W), line_map)] * 3
          + [pl.BlockSpec((None, GRID_W, ATT_W), col_map)] * 3,
        out_specs=out_specs,
        compiler_params=_cparams(("arbitrary",)),
        name="in_projection",
    )(x, mods, g, w_in, ba_blk, e256, qg, kg, *rope_lines, *rope_cols)


def _pool_kernel(u_ref, inv_ref, w_ref, s_ref, o_ref, pad_ref, *, seq, chunk):
    halo = POOL_HALO
    pad_ref[0:halo, :] = jnp.zeros((halo, POOL_W), F32)
    pad_ref[halo + seq:halo + seq + halo, :] = jnp.zeros((halo, POOL_W), F32)
    pad_ref[halo:halo + seq, :] = u_ref[...]
    rows = chunk + 2 * halo
    lane = lax.broadcasted_iota(jnp.int32, (chunk, POOL_W), 1)

    def body(ci, carry):
        c0 = pl.multiple_of(ci * chunk, chunk)
        xp = pad_ref[pl.ds(c0, rows), :]
        u = xp[halo:halo + chunk, :]
        acc = xp
        wsum = None
        for gi, w in enumerate(POOL_WINDOWS):
            acc = acc + pltpu.roll(acc, w // 2, 0)
            lead = w // 2 - 1
            win = acc if lead == 0 else pltpu.roll(acc, rows - lead, 0)
            win = win[halo:halo + chunk, :]
            wsum = win if wsum is None else jnp.where(lane >= gi * POOL_GW, win, wsum)
        m = wsum * inv_ref[pl.ds(c0, chunk), :] - u
        y = _dot(m.astype(BF16), w_ref[...]) * s_ref[...]
        o_ref[pl.ds(c0, chunk), :] = y
        return carry

    lax.fori_loop(0, seq // chunk, body, 0)


def _pool_inverse_counts(seq):
    t = jnp.arange(seq, dtype=jnp.int32)
    cols = []
    for w in POOL_WINDOWS:
        cnt = jnp.minimum(t + w // 2, seq) - jnp.maximum(t - w // 2, 0)
        cols.append(jnp.broadcast_to((1.0 / cnt.astype(F32))[:, None], (seq, POOL_GW)))
    return jnp.concatenate(cols, axis=1)


def _pool_mix(u, inv_cnt, w_blk, s_pool, *, layer, seq, n_seq, first_block):
    chunk = min(256, seq)
    return pl.pallas_call(
        functools.partial(_pool_kernel, seq=seq, chunk=chunk),
        out_shape=jax.ShapeDtypeStruct((n_seq * seq, POOL_W), F32),
        grid=(n_seq,),
        in_specs=[
            pl.BlockSpec((seq, POOL_W), lambda b: (first_block + b, 0)),
            pl.BlockSpec((seq, POOL_W), lambda b: (0, 0)),
            pl.BlockSpec((None, POOL_W, POOL_W), lambda b: (layer, 0, 0)),
            pl.BlockSpec((None, 1, POOL_W), lambda b: (layer, 0, 0)),
        ],
        out_specs=pl.BlockSpec((seq, POOL_W), lambda b: (b, 0)),
        scratch_shapes=[pltpu.VMEM((seq + 2 * POOL_HALO, POOL_W), F32)],
        compiler_params=_cparams(("arbitrary",)),
        name="pool_mix",
    )(u, inv_cnt, w_blk, s_pool)


def _chunk_cumsums(g):
    sl = 8
    row = lax.broadcasted_iota(jnp.int32, (sl, g.shape[1]), 0)
    pre, suf = [], []
    for i in range(g.shape[0] // sl):
        p = q = g[i * sl:(i + 1) * sl, :]
        s = 1
        while s < sl:
            p = p + jnp.where(row >= s, pltpu.roll(p, s, 0), 0.0)
            q = q + jnp.where(row < sl - s, pltpu.roll(q, sl - s, 0), 0.0)
            s *= 2
        pre.append(p)
        suf.append(q)
    per_chunk = GLA_CHUNK // sl
    for c in range(len(pre) // per_chunk):
        for v in range(1, per_chunk):
            i = c * per_chunk + v
            pre[i] = pre[i] + jnp.broadcast_to(pre[i - 1][sl - 1:sl, :], pre[i].shape)
            j = (c + 1) * per_chunk - 1 - v
            suf[j] = suf[j] + jnp.broadcast_to(suf[j + 1][0:1, :], suf[j].shape)
    return jnp.concatenate(pre, axis=0), jnp.concatenate(suf, axis=0)


def _gla_kernel(*refs, n_batch, reverse, finalize):
    refs = list(refs)
    take = lambda: [refs.pop(0) for _ in range(n_batch)]
    q_refs, k_refs, v_refs, g_refs = take(), take(), take(), take()
    ex_ref, mk_ref, bm_ref = refs.pop(0), refs.pop(0), refs.pop(0)
    if finalize:
        of_ref = refs.pop(0)
        og_refs = take()
        gn_ref, e_ref = refs.pop(0), refs.pop(0)
        o_ref, st_ref, ob_ref = refs
    else:
        o_ref, st_ref = refs
        ob_ref = o_ref

    @pl.when(pl.program_id(0) == 0)
    def _():
        st_ref[...] = jnp.zeros(st_ref.shape, F32)

    sides, tots = [], []
    for b in range(n_batch):
        g = g_refs[b][...]
        pre, suf = _chunk_cumsums(g)
        sides.append(suf if reverse else pre)
        tots.append(pre + suf - g)
    worst = jnp.max(jnp.abs(tots[0]))
    for b in range(1, n_batch):
        worst = jnp.maximum(worst, jnp.max(jnp.abs(tots[b])))
    in_range = worst <= GLA_FAST_RANGE

    @pl.when(in_range)
    def _():
        for b in range(n_batch):
            _gla_fast_tile(q_refs[b], k_refs[b], v_refs[b], mk_ref, bm_ref, st_ref.at[b], ob_ref.at[b],
                           sides[b], tots[b], reverse)

    @pl.when(jnp.logical_not(in_range))
    def _():
        for b in range(n_batch):
            _gla_exact_tile(q_refs[b], k_refs[b], v_refs[b], g_refs[b], ex_ref, mk_ref, st_ref.at[b],
                            ob_ref.at[b], reverse)

    if finalize:
        for b in range(n_batch):
            both = ob_ref[b] + of_ref[b]
            y = _group_rms(both, e_ref[...], GLA_DV) * gn_ref[...]
            o_ref[b] = y * _silu(og_refs[b][...])


def _gla_fast_tile(q_ref, k_ref, v_ref, mk_ref, bm_ref, st_ref, ob_ref, b, tot, reverse):
    ch = GLA_CHUNK
    half = 0.5 * tot
    q, k = q_ref[...], k_ref[...]
    q_in = (q * jnp.exp(b - half)).astype(BF16)
    k_in = (k * jnp.exp(half - b)).astype(BF16)
    q_st = (q * jnp.exp(b)).astype(BF16)
    k_st = (k * jnp.exp(tot - b)).astype(BF16)
    dec = jnp.exp(tot)
    mk = mk_ref[...]
    mk16 = mk.astype(BF16)
    bm16 = bm_ref[...]
    irow = lax.broadcasted_iota(jnp.int32, (ch, GLA_HEADS * ch), 0)
    jcol = lax.broadcasted_iota(jnp.int32, (ch, GLA_HEADS * ch), 1) % ch
    seen = (jcol >= irow) if reverse else (jcol <= irow)
    nt = (((1,), (1,)), ((), ()))
    st = st_ref[...]
    n_chunks = TP // ch
    for ci in range(n_chunks):
        c = (n_chunks - 1 - ci) if reverse else ci
        rows = slice(c * ch, (c + 1) * ch)
        kb = jnp.concatenate([k_in[rows]] * GLA_HEADS, axis=0) * mk16
        a = lax.dot_general(q_in[rows], kb, nt, preferred_element_type=F32)
        a = jnp.where(seen, a, 0.0).astype(BF16)
        v = v_ref[rows, :]
        vb = jnp.concatenate([v.astype(BF16)] * GLA_HEADS, axis=0) * bm16
        o = _dot(a, vb) + lax.dot_general(q_st[rows], st.astype(BF16), nt, preferred_element_type=F32)
        kv = lax.dot_general(v.astype(BF16), k_st[rows], (((0,), (0,)), ((), ())),
                             preferred_element_type=F32)
        st = st * dec[c * ch:c * ch + 1, :] + kv * mk
        ob_ref[rows, :] = o
    st_ref[...] = st


def _gla_exact_tile(q_ref, k_ref, v_ref, g_ref, ex_ref, mk_ref, st_ref, ob_ref, reverse):
    sub = GLA_SUB
    n_chunks = TP // sub
    row = lax.broadcasted_iota(jnp.int32, (sub, GLA_KW), 0)
    ex = ex_ref[...]
    mk = mk_ref[...]

    def body(ci, carry):
        c = (n_chunks - 1 - ci) if reverse else ci
        r0 = pl.multiple_of(c * sub, sub)
        q = q_ref[pl.ds(r0, sub), :]
        k = k_ref[pl.ds(r0, sub), :]
        v = v_ref[pl.ds(r0, sub), :]
        b = g_ref[pl.ds(r0, sub), :]
        s = 1
        while s < sub:
            if reverse:
                b = b + jnp.where(row < sub - s, pltpu.roll(b, sub - s, 0), 0.0)
            else:
                b = b + jnp.where(row >= s, pltpu.roll(b, s, 0), 0.0)
            s *= 2
        edge = b[0:1, :] if reverse else b[sub - 1:sub, :]
        st = st_ref[...]
        qd = (q * jnp.exp(b)).astype(BF16)
        o = lax.dot_general(qd, st.astype(BF16), (((1,), (1,)), ((), ())),
                            preferred_element_type=F32)
        parts = []
        for j in range(sub):
            seen = (row <= j) if reverse else (row >= j)
            dec = jnp.exp(jnp.where(seen, b - b[j:j + 1, :], NEG_BIG))
            parts.append((dec * q * k[j:j + 1, :]).astype(BF16))
        r = _dot(jnp.concatenate(parts, axis=0), ex)
        for j in range(sub):
            o = o + r[j * sub:(j + 1) * sub, :] * v[j:j + 1, :]
        kd = (k * jnp.exp(edge - b)).astype(BF16)
        kv = lax.dot_general(v.astype(BF16), kd, (((0,), (0,)), ((), ())),
                             preferred_element_type=F32)
        st_ref[...] = st * jnp.exp(edge) + kv * mk
        ob_ref[pl.ds(r0, sub), :] = o
        return carry

    lax.fori_loop(0, n_chunks, body, 0, unroll=2)


def _gla_step_tile(n_lat, n_ctx, reverse):
    def tile(s):
        is_ctx = s < n_ctx
        sc = jnp.where(is_ctx, s, 0)
        sl = jnp.where(is_ctx, 0, s - n_ctx)
        if reverse:
            sc = n_ctx - 1 - sc
            sl = n_lat - 1 - sl
        return is_ctx, sc, sl
    return tile


def _gla_mix(gq, gk, gv, gf, gb, og, ex, mk, gn, e256, *, n_batch, seq, ctx):
    n_lat, n_ctx = seq // TP, ctx // TP
    const = lambda shape: pl.BlockSpec(shape, lambda s: (0,) * len(shape))

    def run(reverse, finalize, gate, extra):
        tile = _gla_step_tile(n_lat, n_ctx, reverse)

        def rows(w, b):
            def index_map(s):
                is_ctx, sc, sl = tile(s)
                return (jnp.where(is_ctx, n_batch * n_lat + b * n_ctx + sc, b * n_lat + sl), 0)
            return pl.BlockSpec((TP, w), index_map)

        def per_batch(arr, w):
            return [arr] * n_batch, [rows(w, b) for b in range(n_batch)]

        def out_map(s):
            is_ctx, sc, sl = tile(s)
            return (0, jnp.where(is_ctx, n_lat + sc, sl), 0)

        stacked = pl.BlockSpec((n_batch, TP, GLA_W), out_map)
        args, in_specs = [], []
        for arr, w in ((gq, GLA_KW), (gk, GLA_KW), (gv, GLA_W), (gate, GLA_KW)):
            a, sp = per_batch(arr, w)
            args += a
            in_specs += sp
        args += [ex, mk, e256]
        in_specs += [const(ex.shape), const(mk.shape), const(e256.shape)]
        scratch = [pltpu.VMEM((n_batch, GLA_W, GLA_KW), F32)]
        if finalize:
            o_f, og_rows, gn_row, e_blk = extra
            a, sp = per_batch(og_rows, GLA_W)
            args += [o_f] + a + [gn_row, e_blk]
            in_specs += [stacked] + sp + [const((1, GLA_W)), const(e_blk.shape)]
            scratch.append(pltpu.VMEM((n_batch, TP, GLA_W), F32))
        return pl.pallas_call(
            functools.partial(_gla_kernel, n_batch=n_batch, reverse=reverse, finalize=finalize),
            out_shape=jax.ShapeDtypeStruct((n_batch, seq + ctx, GLA_W), F32),
            grid=(n_lat + n_ctx,),
            in_specs=in_specs,
            out_specs=stacked,
            scratch_shapes=scratch,
            compiler_params=_cparams(("arbitrary",)),
            name="gla_bwd" if reverse else "gla_fwd",
        )(*args)

    o_f = run(False, False, gf, None)
    return run(True, True, gb, [o_f, og, gn, e256])


def _att_query_groups(q_t):
    tq = q_t.shape[1]
    zero = jnp.zeros((ATT_HD, tq), BF16)
    groups = []
    for g in range(ATT_KVH):
        halves = []
        for h in (2 * g, 2 * g + 1):
            qh = q_t[h * ATT_HD:(h + 1) * ATT_HD, :]
            halves.append(jnp.concatenate([qh, zero] if g == 0 else [zero, qh], axis=0))
        groups.append(jnp.concatenate(halves, axis=1))
    return groups


def _att_store(o_ref, weighted, denom, tq):
    heads = []
    for g in range(ATT_KVH):
        og = weighted[g] / denom[g]
        heads += [og[:, :tq], og[:, tq:]]
    o_ref[...] = jnp.concatenate(heads, axis=0).T.astype(o_ref.dtype)


def _att_kernel(*refs, n_main, has_ctx, bounded):
    refs = list(refs)
    q_ref, km_ref, vm_ref = refs.pop(0), refs.pop(0), refs.pop(0)
    kc_ref, vc_ref = (refs.pop(0), refs.pop(0)) if has_ctx else (None, None)
    o_ref, s0_ref, s1_ref = refs
    q_t = q_ref[...]
    tq = q_t.shape[1]
    tk = vm_ref.shape[3]
    groups = _att_query_groups(q_t)

    def scores(kblk, buf):
        for g in range(ATT_KVH):
            buf[g, 0:kblk.shape[0], :] = _dot(kblk, groups[g])

    def consume(buf, rows, v_of, carry):
        out = []
        for g in range(ATT_KVH):
            m, l, acc = carry[g]
            s = buf[g, 0:rows, :]
            if bounded:
                pv = _dot(v_of(g), jnp.exp2(s).astype(BF16))
                out.append((m, l + pv[ATT_HD:ATT_HD + 1, :], acc + pv[:ATT_HD, :]))
                continue
            m_new = jnp.maximum(m, jnp.max(s, axis=0, keepdims=True))
            alpha = jnp.exp2(m - m_new)
            p = jnp.exp2(s - m_new).astype(BF16)
            pv = _dot(v_of(g), p)
            acc = alpha * acc + pv[:ATT_HD, :]
            l = alpha * l + pv[ATT_HD:ATT_HD + 1, :]
            out.append((m_new, l, acc))
        return tuple(out)

    def k_main(j):
        return km_ref[pl.ds(pl.multiple_of(j * tk, tk), tk), :]

    def v_main(j):
        return lambda g: vm_ref[j, g]

    def pair(j, carry, issue_next):
        scores(k_main(j + 1), s1_ref)
        carry = consume(s0_ref, tk, v_main(j), carry)
        issue_next(j + 2)
        return consume(s1_ref, tk, v_main(j + 1), carry)

    carry = tuple((jnp.full((1, 2 * tq), -jnp.inf, F32), jnp.zeros((1, 2 * tq), F32),
                   jnp.zeros((ATT_HD, 2 * tq), F32)) for _ in range(ATT_KVH))
    next_main = lambda j: scores(k_main(j), s0_ref)
    if has_ctx:
        after_main = lambda j: scores(kc_ref[...], s0_ref)
    else:
        after_main = lambda j: None

    scores(k_main(0), s0_ref)
    if n_main > 1:
        pairs_per_trip = 4 if n_main % 8 == 0 else (2 if n_main % 4 == 0 else 1)

        def body(i, carry):
            for u in range(pairs_per_trip):
                carry = pair(2 * (pairs_per_trip * i + u), carry, next_main)
            return carry

        n_trips = n_main // (2 * pairs_per_trip)
        carry = lax.fori_loop(0, n_trips - 1, body, carry)
        for u in range(pairs_per_trip):
            last = u == pairs_per_trip - 1
            carry = pair(2 * (pairs_per_trip * (n_trips - 1) + u), carry, after_main if last else next_main)
    else:
        assert not has_ctx
        carry = consume(s0_ref, tk, v_main(0), carry)
    if has_ctx:
        carry = consume(s0_ref, kc_ref.shape[0], lambda g: vc_ref[g], carry)
    _att_store(o_ref, [c[2] for c in carry], [c[1] for c in carry], tq)


def _attention(aq, ak, av, *, n_batch, seq, ctx, ctx_queries, bounded=False):
    lat_tiles = seq // TM
    ctx_tile = n_batch * lat_tiles
    ctx_cols = lambda b, i: (ctx_tile + (b * ctx) // TM, 0, ((b * ctx) % TM) // ctx)
    ctx_vcols = lambda b, i: (ctx_tile + (b * ctx) // TM, 0, 0, ((b * ctx) % TM) // ctx)
    if not ctx_queries:
        sub = TM // TQ
        grid = (n_batch, seq // TQ)
        in_specs = [
            pl.BlockSpec((None, ATT_W, TQ), lambda b, i: (b * lat_tiles + i // sub, 0, i % sub)),
            pl.BlockSpec((seq, ATT_KW), lambda b, i: (b, 0)),
            pl.BlockSpec((lat_tiles, ATT_KVH, ATT_VROWS, TM), lambda b, i: (b, 0, 0, 0)),
            pl.BlockSpec((ctx, ATT_KW), lambda b, i: (n_batch * seq // ctx + b, 0)),
            pl.BlockSpec((None, ATT_KVH, ATT_VROWS, ctx), ctx_vcols),
        ]
        out_spec = pl.BlockSpec((TQ, ATT_W), lambda b, i: (b * (seq // TQ) + i, 0))
        assert lat_tiles % 2 == 0
        out_rows = n_batch * seq
        args = (aq, ak, av, ak, av)
        kern = functools.partial(_att_kernel, n_main=lat_tiles, has_ctx=True, bounded=bounded)
        s_shape = (ATT_KVH, TM, 2 * TQ)
    else:
        grid = (n_batch, 1)
        in_specs = [
            pl.BlockSpec((None, ATT_W, ctx), ctx_cols),
            pl.BlockSpec((ctx, ATT_KW), lambda b, i: (n_batch * seq // ctx + b, 0)),
            pl.BlockSpec((1, ATT_KVH, ATT_VROWS, ctx), ctx_vcols),
        ]
        out_spec = pl.BlockSpec((ctx, ATT_W), lambda b, i: (b, 0))
        kern = functools.partial(_att_kernel, n_main=1, has_ctx=False, bounded=False)
        out_rows = n_batch * ctx
        args = (aq, ak, av)
        s_shape = (ATT_KVH, ctx, 2 * ctx)
    return pl.pallas_call(
        kern,
        out_shape=jax.ShapeDtypeStruct((out_rows, ATT_W), BF16),
        grid=grid,
        in_specs=in_specs,
        out_specs=out_spec,
        scratch_shapes=[pltpu.VMEM(s_shape, F32), pltpu.VMEM(s_shape, F32)],
        compiler_params=_cparams(("arbitrary", "arbitrary")),
        name="attention_ctx" if ctx_queries else ("attention_bounded" if bounded else "attention"),
    )(*args)


def _fft_a_kernel(x_ref, d_ref, tr_ref, ti_ref, yr_ref, yi_ref):
    rb, w = x_ref.shape[1], x_ref.shape[2]
    x = jnp.concatenate([x_ref[:, r, :] for r in range(rb)], axis=1)
    y = _dot(d_ref[...], x.astype(BF16))
    tr, ti = tr_ref[...], ti_ref[...]
    for r in range(rb):
        yr, yi = y[:FFT_N1, r * w:(r + 1) * w], y[FFT_N1:, r * w:(r + 1) * w]
        c, s = tr[:, r:r + 1], ti[:, r:r + 1]
        yr_ref[:, r, :] = yr * c - yi * s
        yi_ref[:, r, :] = yr * s + yi * c


def _channel_mix(xr, xi, c_ref, wf_ref, norm):
    xc = jnp.concatenate([xr, xi], axis=1).astype(BF16)
    f = _dot(xc, c_ref[...]) * norm
    return _dot(f.astype(BF16), wf_ref[...])


def _fft_c_kernel(yr_ref, yi_ref, m_ref, c_ref, wf_ref, o_ref, *, n2, norm):
    w = yr_ref.shape[1]
    blk = jnp.concatenate(
        [jnp.concatenate([yr_ref[j * n2:(j + 1) * n2, :] for j in range(8)], axis=1),
         jnp.concatenate([yi_ref[j * n2:(j + 1) * n2, :] for j in range(8)], axis=1)], axis=0)
    x = _dot(m_ref[...], blk.astype(BF16))
    xr = jnp.concatenate([x[:n2, j * w:(j + 1) * w] for j in range(8)], axis=0)
    xi = jnp.concatenate([x[n2:, j * w:(j + 1) * w] for j in range(8)], axis=0)
    y = _channel_mix(xr, xi, c_ref, wf_ref, norm)
    for j in range(8):
        o_ref[:, j, :] = y[j * n2:(j + 1) * n2, :]


def _fft_dense_kernel(x_ref, d_ref, c_ref, wf_ref, o_ref, *, n, norm):
    x = _dot(d_ref[...], x_ref[...].astype(BF16))
    o_ref[...] = _channel_mix(x[:n, :], x[n:, :], c_ref, wf_ref, norm)


def _dft_parts(n):
    idx = np.arange(n)
    ang = 2.0 * np.pi * ((idx[:, None] * idx[None, :]) % n) / n
    return np.cos(ang), np.sin(ang)


def _mxu_const(a):
    return jnp.asarray(a, F32).astype(BF16)


def _fnet_consts(seq):
    n1, n2 = FFT_N1, seq // FFT_N1
    c1, s1 = _dft_parts(n1)
    da = np.concatenate([c1, -s1], axis=0)
    k1 = np.arange(n1)[:, None]
    m2 = np.arange(n2)[None, :]
    ang = 2.0 * np.pi * ((k1 * m2) % seq) / seq
    tw_r, tw_i = np.cos(ang), -np.sin(ang)
    c2, s2 = _dft_parts(n2)
    mc = np.block([[c2, s2], [-s2, c2]])
    return da, tw_r, tw_i, mc


def _channel_consts():
    cc, sc = _dft_parts(FNET_HD)
    eye = np.eye(FNET_HEADS)
    return np.concatenate([np.kron(eye, cc), np.kron(eye, sc)], axis=0)


def _fnet_latent(uf3, wf, *, layer, n_batch, seq):
    n1, n2 = FFT_N1, seq // FFT_N1
    da, tw_r, tw_i, mc = _fnet_consts(seq)
    da, mc, cc = _mxu_const(da), _mxu_const(mc), _mxu_const(_channel_consts())
    rb = min(n2, 32)
    blocked = lambda tw: jnp.asarray(tw.reshape(n1, n2 // rb, rb).transpose(1, 0, 2), F32)
    const2 = lambda shape: pl.BlockSpec(shape, lambda b, j: (0,) * len(shape))
    slab = pl.BlockSpec((n1, rb, FNET_W), lambda b, j: (b, j, 0))
    yr, yi = pl.pallas_call(
        _fft_a_kernel,
        out_shape=(jax.ShapeDtypeStruct((n_batch * n1, n2, FNET_W), F32),) * 2,
        grid=(n_batch, n2 // rb),
        in_specs=[
            slab,
            const2(da.shape),
            pl.BlockSpec((None, n1, rb), lambda b, j: (j, 0, 0)),
            pl.BlockSpec((None, n1, rb), lambda b, j: (j, 0, 0)),
        ],
        out_specs=(slab, slab),
        compiler_params=_cparams(("arbitrary", "arbitrary")),
        name="fft_stage_a",
    )(uf3, da, blocked(tw_r), blocked(tw_i))
    yr = yr.reshape(n_batch * n1 * n2, FNET_W)
    yi = yi.reshape(n_batch * n1 * n2, FNET_W)
    norm = 1.0 / math.sqrt(seq * FNET_HD)
    out = pl.pallas_call(
        functools.partial(_fft_c_kernel, n2=n2, norm=norm),
        out_shape=jax.ShapeDtypeStruct((n_batch, n2, n1, FNET_W), F32),
        grid=(n_batch, n1 // 8),
        in_specs=[
            pl.BlockSpec((8 * n2, FNET_W), lambda b, i: (b * (n1 // 8) + i, 0)),
            pl.BlockSpec((8 * n2, FNET_W), lambda b, i: (b * (n1 // 8) + i, 0)),
            const2(mc.shape), const2(cc.shape),
            pl.BlockSpec((None,) + wf.shape[1:], lambda b, i: (layer, 0, 0)),
        ],
        out_specs=pl.BlockSpec((None, n2, 8, FNET_W), lambda b, i: (b, 0, i, 0)),
        compiler_params=_cparams(("arbitrary", "arbitrary")),
        name="fft_stage_c",
    )(yr, yi, mc, cc, wf)
    return out.reshape(n_batch * seq, FNET_W)


def _fnet_context(uf, wf, *, layer, n_batch, ctx, first_block):
    c, s = _dft_parts(ctx)
    dd, cc = _mxu_const(np.concatenate([c, -s], axis=0)), _mxu_const(_channel_consts())
    const = lambda shape: pl.BlockSpec(shape, lambda b: (0,) * len(shape))
    return pl.pallas_call(
        functools.partial(_fft_dense_kernel, n=ctx, norm=1.0 / math.sqrt(ctx * FNET_HD)),
        out_shape=jax.ShapeDtypeStruct((n_batch * ctx, FNET_W), F32),
        grid=(n_batch,),
        in_specs=[pl.BlockSpec((ctx, FNET_W), lambda b: (first_block + b, 0)),
                  const(dd.shape), const(cc.shape),
                  pl.BlockSpec((None,) + wf.shape[1:], lambda b: (layer, 0, 0))],
        out_specs=pl.BlockSpec((ctx, FNET_W), lambda b: (b, 0)),
        compiler_params=_cparams(("arbitrary",)),
        name="fft_context",
    )(uf, dd, cc, wf)


def _rope_tables(seq):
    freqs = ROPE_THETA ** (-jnp.arange(ROPE_FREQS, dtype=F32) / ROPE_FREQS)
    row_ang = jnp.arange(seq // GRID_W, dtype=F32)[:, None] * freqs
    col_ang = jnp.arange(GRID_W, dtype=F32)[:, None] * freqs
    lanes = np.arange(ATT_HD)
    is_row = jnp.asarray(np.tile(lanes < 2 * ROPE_FREQS, ATT_QH)[None, :], F32)
    first_half = jnp.asarray(np.tile((lanes // ROPE_FREQS) % 2 == 0, ATT_QH)[None, :], F32)
    spread = lambda a: jnp.tile(a, (1, ATT_W // ROPE_FREQS))

    def tables(ang, mask, ident_lines):
        cos, sin = spread(jnp.cos(ang)) * mask, spread(jnp.sin(ang)) * mask
        ident = jnp.zeros((ident_lines, ATT_W), F32)
        return (jnp.concatenate([cos, ident + mask], axis=0),
                jnp.concatenate([-sin * first_half, ident], axis=0),
                jnp.concatenate([sin * (1.0 - first_half), ident], axis=0))

    rows = tables(row_ang, is_row, TM // GRID_W)
    cols = tuple(t.reshape(2, GRID_W, ATT_W) for t in tables(col_ang, 1.0 - is_row, GRID_W))
    return rows, cols


def _block_ones(width, group):
    return jnp.asarray(np.kron(np.eye(width // group), np.ones((group, group))), BF16)


def kernel(x, c, ctx, c_ctx, w_mod, b_mod, norm_g, ffn_wg, ffn_wu, ffn_wd, w_in, w_out,
           pool_w, pool_scale, gla_wa, gla_ba, gla_norm, att_qnorm, att_knorm, fnet_w, final_norm):
    n_batch, seq, d = x.shape
    n_ctx = ctx.shape[1]
    assert d == D_MODEL and seq % TM == 0 and (n_batch * n_ctx) % TM == 0 and n_ctx % TP == 0
    assert seq % (8 * FFT_N1) == 0 and n_ctx <= TM and TM % n_ctx == 0 and n_batch + 1 <= 8
    lat_rows = n_batch * seq
    lat_tiles = lat_rows // TM
    all_tiles = lat_tiles + (n_batch * n_ctx) // TM
    tiles_per_batch = seq // TM
    line_map = lambda t: (jnp.where(t < lat_tiles, t % tiles_per_batch, tiles_per_batch), 0)
    col_map = lambda t: (jnp.where(t < lat_tiles, 0, 1), 0, 0)

    cvec = jnp.concatenate([c, c_ctx[None, :], jnp.zeros((8 - n_batch - 1, d), F32)], axis=0)
    mods = _modulation(cvec, w_mod, b_mod).reshape(DEPTH, 8, N_MOD, d)

    rope_lines, rope_cols = _rope_tables(seq)
    e256 = _block_ones(ATT_W, ATT_HD)
    gla_ex = _block_ones(GLA_W, GLA_DV)[::2, :]
    gla_mk = jnp.asarray(np.kron(np.eye(GLA_HEADS), np.ones((GLA_DV, GLA_DK))), F32)

    wg, wu, wd = ffn_wg.astype(BF16), ffn_wu.astype(BF16), ffn_wd.astype(BF16)
    w_rank = jnp.concatenate([w_in[..., 768:800], jnp.zeros((DEPTH, d, 128 - 2 * GLA_RANK), F32)], axis=-1)
    wa_blk = jnp.zeros((DEPTH, 128, 2 * GLA_KW), F32)
    wa_blk = wa_blk.at[:, :GLA_RANK, :GLA_KW].set(gla_wa[:, 0]).at[:, GLA_RANK:2 * GLA_RANK, GLA_KW:].set(gla_wa[:, 1])
    wi = jnp.concatenate([w_in[..., :768], w_in[..., 800:D_IN], _fold_gates(w_rank, wa_blk)], axis=-1).astype(BF16)
    ba_blk = gla_ba.reshape(DEPTH, 1, 2 * GLA_KW)
    qg = jnp.tile(att_qnorm, (1, ATT_QH))[:, None, :]
    kg = jnp.tile(att_knorm, (1, ATT_KVH))[:, None, :]
    pool_blk = jnp.zeros((DEPTH, POOL_W, POOL_W), F32)
    for gi in range(len(POOL_WINDOWS)):
        sl = slice(gi * POOL_GW, (gi + 1) * POOL_GW)
        pool_blk = pool_blk.at[:, sl, sl].set(pool_w[:, gi])
    pool_blk = pool_blk.astype(BF16)
    pool_s = pool_scale[:, None, :]
    wf = fnet_w.astype(BF16)
    wo = w_out.astype(BF16)
    norm4 = norm_g[:, :, None, :]
    n2 = seq // FFT_N1
    inv_lat = _pool_inverse_counts(seq)

    xs = x.reshape(lat_rows, d)
    ctx_rows = ctx.reshape(n_batch * n_ctx, d)
    for i in range(DEPTH):
        ctx_out = i < DEPTH - 1
        mod_map = _mod_spec(i, lat_tiles, tiles_per_batch, n_batch, d)
        gn = jnp.tile(gla_norm[i], GLA_HEADS)[None, :]

        xs = _half_ffn(xs, mods, norm4, wg, wu, wd, layer=i, n_tiles=all_tiles,
                       mod_map=mod_map, ctx_rows=ctx_rows if i == 0 else None)
        (u_pool, gq, gk, gv, og, gf, gb, aq, ak, av, uf) = _in_projection(
            xs, mods, norm4, wi, ba_blk, e256, qg, kg, rope_lines, rope_cols,
            layer=i, n_tiles=all_tiles, mod_map=mod_map, line_map=line_map, col_map=col_map)

        y_pool = _pool_mix(u_pool, inv_lat, pool_blk, pool_s, layer=i, seq=seq, n_seq=n_batch, first_block=0)
        y_gla = _gla_mix(gq, gk, gv, gf, gb, og, gla_ex, gla_mk, gn, e256, n_batch=n_batch, seq=seq, ctx=n_ctx)
        score_bound = (ATT_HD * ATT_HD ** -0.5 * math.log2(math.e) * ATT_BOUND_SLACK
                       * jnp.max(jnp.abs(att_qnorm[i])) * jnp.max(jnp.abs(att_knorm[i])))
        attend = functools.partial(_attention, aq, ak, av, n_batch=n_batch, seq=seq, ctx=n_ctx, ctx_queries=False)
        y_att = lax.cond(score_bound <= ATT_BOUND_MAX, lambda: attend(bounded=True), lambda: attend())
        y_fnet = _fnet_latent(uf.reshape(uf.shape[0] // n2, n2, FNET_W), wf, layer=i, n_batch=n_batch, seq=seq)
        n_tiles, ctx_parts = lat_tiles, None
        if ctx_out:
            first_ctx = lat_rows // n_ctx
            ctx_parts = (
                _pool_mix(u_pool, _pool_inverse_counts(n_ctx), pool_blk, pool_s, layer=i, seq=n_ctx, n_seq=n_batch,
                          first_block=first_ctx),
                _attention(aq, ak, av, n_batch=n_batch, seq=seq, ctx=n_ctx, ctx_queries=True),
                _fnet_context(uf, wf, layer=i, n_batch=n_batch, ctx=n_ctx, first_block=first_ctx),
            )
            n_tiles = all_tiles
        xs = _mix_ffn(xs, mods, y_pool, y_gla, y_att, y_fnet, wo, norm4, wg, wu, wd, final_norm,
                      layer=i, n_tiles=n_tiles, mod_map=mod_map, final=not ctx_out, ctx_parts=ctx_parts)
    return xs.reshape(n_batch, seq, d)
```

```python
import functools
import math

import jax
import jax.numpy as jnp
import numpy as np
from jax import lax
from jax.experimental import pallas as pl
from jax.experimental.pallas import tpu as pltpu

F32 = jnp.float32
BF16 = jnp.bfloat16

D_MODEL = 1024
DEPTH = 2
GRID_W = 64
EPS = 1e-6
N_MOD = 9
D_FF = 2816

POOL_W = 256
POOL_WINDOWS = (2, 4, 8, 16)
POOL_GW = 64
POOL_HALO = 8

GLA_HEADS = 4
GLA_W = 256
GLA_DV = 64
GLA_DK = 32
GLA_RANK = 16
GLA_TAU = 16.0
GLA_KW = GLA_HEADS * GLA_DK
GLA_SUB = 16
GLA_CHUNK = 64
GLA_FAST_RANGE = 150.0

ATT_W = 256
ATT_HD = 64
ATT_QH = 4
ATT_KVH = 2
ATT_KW = ATT_KVH * ATT_HD
ATT_VROWS = ATT_HD + 16
ATT_BOUND_SLACK = 1.02
ATT_BOUND_MAX = 50.0
ROPE_FREQS = 16
ROPE_THETA = 10000.0

FNET_W = 256
FNET_HEADS = 4
FNET_HD = 64
FFT_N1 = 64

D_IN = 1824
D_IN_PAD = 2048

O_POOL, O_GQ, O_GK, O_GV, O_OG, O_AQ, O_AK, O_AV, O_FN, O_Z = 0, 256, 384, 512, 768, 1024, 1280, 1408, 1536, 1792

TM = 512
TQ = 512
TP = 256
MXU_TILE = 256
FF_CHUNKS = ((0, 6 * MXU_TILE), (6 * MXU_TILE, D_FF))
NEG_BIG = -1e30

VMEM_LIMIT = 56 * 1024 * 1024


def _cparams(sem):
    return pltpu.CompilerParams(dimension_semantics=sem, vmem_limit_bytes=VMEM_LIMIT)


def _dot(a, b):
    return jnp.dot(a, b, preferred_element_type=F32)


def _rms(x):
    return x * lax.rsqrt(jnp.mean(x * x, axis=-1, keepdims=True) + EPS)


def _silu(x):
    return x * jax.nn.sigmoid(x)


def _group_rms(x, e, width):
    ss = _dot((x * x).astype(BF16), e)
    return x * lax.rsqrt(ss * (1.0 / width) + EPS)


def _mod_kernel(c_ref, w_ref, b_ref, o_ref):
    s = _silu(c_ref[...]).astype(BF16)
    o_ref[...] = _dot(s, w_ref[...].astype(BF16)) + b_ref[...]


def _modulation(cvec, w_mod, b_mod):
    depth, d, nd = w_mod.shape
    tn = nd // 8
    return pl.pallas_call(
        _mod_kernel,
        out_shape=jax.ShapeDtypeStruct((depth, 8, nd), F32),
        grid=(depth, nd // tn),
        in_specs=[
            pl.BlockSpec((8, d), lambda l, j: (0, 0)),
            pl.BlockSpec((None, d, tn), lambda l, j: (l, 0, j)),
            pl.BlockSpec((None, 1, tn), lambda l, j: (l, 0, j)),
        ],
        out_specs=pl.BlockSpec((None, 8, tn), lambda l, j: (l, 0, j)),
        compiler_params=_cparams(("arbitrary", "arbitrary")),
        name="modulation",
    )(cvec, w_mod, b_mod.reshape(depth, 1, nd))


def _half_ffn_rows(x, m_ref, g_ref, wg_ref, wu_ref, wd_ref, mod_base):
    shift = m_ref[mod_base:mod_base + 1, :]
    scale = m_ref[mod_base + 1:mod_base + 2, :]
    gate = m_ref[mod_base + 2:mod_base + 3, :]
    h = ((_rms(x) * g_ref[...]) * (1.0 + scale) + shift).astype(BF16)
    y = jnp.zeros(x.shape, F32)
    for lo, hi in FF_CHUNKS:
        sl = slice(lo, hi)
        a = _dot(h, wg_ref[:, sl])
        u = _dot(h, wu_ref[:, sl])
        y = y + _dot((_silu(a) * u).astype(BF16), wd_ref[sl, :])
    return x + (0.5 * gate) * y


def _ffn_kernel(*refs, n_lat_tiles):
    if n_lat_tiles is None:
        x_ref, m_ref, g_ref, wg_ref, wu_ref, wd_ref, o_ref = refs
        x = x_ref[...]
    else:
        x_ref, c_ref, m_ref, g_ref, wg_ref, wu_ref, wd_ref, o_ref = refs
        x = jnp.where(pl.program_id(0) < n_lat_tiles, x_ref[...], c_ref[...])
    o_ref[...] = _half_ffn_rows(x, m_ref, g_ref, wg_ref, wu_ref, wd_ref, 0)


def _mix_ffn_kernel(*refs, n_lat_tiles, final):
    if n_lat_tiles is None:
        (x_ref, m_ref, yp_ref, yg_ref, ya_ref, yf_ref, wo_ref,
         g_ref, wg_ref, wu_ref, wd_ref, fg_ref, o_ref) = refs
        yp, yg, ya, yf = yp_ref[...], yg_ref[...], ya_ref[...], yf_ref[...]
    else:
        (x_ref, m_ref, yp_ref, yg_ref, ya_ref, yf_ref, cp_ref, cg_ref, ca_ref, cf_ref, wo_ref,
         g_ref, wg_ref, wu_ref, wd_ref, fg_ref, o_ref) = refs
        lat = pl.program_id(0) < n_lat_tiles
        yp = jnp.where(lat, yp_ref[...], cp_ref[...])
        yg = jnp.where(lat, yg_ref[...], cg_ref[...].reshape(yg_ref.shape))
        ya = jnp.where(lat, ya_ref[...], ca_ref[...])
        yf = jnp.where(lat, yf_ref[...], cf_ref[...])
    acc = _dot(yp.astype(BF16), wo_ref[0:256, :])
    acc = acc + _dot(yg.astype(BF16), wo_ref[256:512, :])
    acc = acc + _dot(ya.astype(BF16), wo_ref[512:768, :])
    acc = acc + _dot(yf.astype(BF16), wo_ref[768:1024, :])
    x = x_ref[...] + m_ref[5:6, :] * acc
    out = _half_ffn_rows(x, m_ref, g_ref, wg_ref, wu_ref, wd_ref, 6)
    if final:
        out = _rms(out) * fg_ref[...]
    o_ref[...] = out


def _mod_spec(layer, n_lat_tiles, tiles_per_batch, n_batch, d):
    def index_map(t):
        return (layer, jnp.where(t < n_lat_tiles, t // tiles_per_batch, n_batch), 0, 0)
    return pl.BlockSpec((None, None, N_MOD, d), index_map)


def _resident(arr, lead):
    block = (None,) * len(lead) + arr.shape[len(lead):]
    index = tuple(lead) + (0,) * (arr.ndim - len(lead))
    return pl.BlockSpec(block, lambda t: index, pipeline_mode=pl.Buffered(1))


def _ffn_weight_specs(g, wg, wu, wd, layer, half):
    d = g.shape[-1]
    lead = (layer, half)
    return [pl.BlockSpec((None, None, 1, d), lambda t: (layer, 2 * half, 0, 0)),
            _resident(wg, lead), _resident(wu, lead), _resident(wd, lead)]


def _half_ffn(x, mods, g, wg, wu, wd, *, layer, n_tiles, mod_map, ctx_rows=None):
    d = x.shape[1]
    n_lat = None if ctx_rows is None else x.shape[0] // TM
    if ctx_rows is None:
        rows, row_specs = [x], [pl.BlockSpec((TM, d), lambda t: (t, 0))]
    else:
        rows = [x, ctx_rows]
        row_specs = [pl.BlockSpec((TM, d), lambda t: (jnp.minimum(t, n_lat - 1), 0)),
                     pl.BlockSpec((TM, d), lambda t: (jnp.maximum(t - n_lat, 0), 0))]
    return pl.pallas_call(
        functools.partial(_ffn_kernel, n_lat_tiles=n_lat),
        out_shape=jax.ShapeDtypeStruct((n_tiles * TM, d), F32),
        grid=(n_tiles,),
        in_specs=row_specs + [mod_map] + _ffn_weight_specs(g, wg, wu, wd, layer, 0),
        out_specs=pl.BlockSpec((TM, d), lambda t: (t, 0)),
        compiler_params=_cparams(("arbitrary",)),
        name="half_ffn",
    )(*rows, mods, g, wg, wu, wd)


def _mix_ffn(x, mods, y_pool, y_gla, y_att, y_fnet, w_out, g, wg, wu, wd, fg,
             *, layer, n_tiles, mod_map, final, ctx_parts=None):
    d = x.shape[1]
    n_batch, gla_rows = y_gla.shape[0], y_gla.shape[1]
    lat_tiles = y_pool.shape[0] // TM
    tpb = lat_tiles // n_batch
    clamp = lambda t: jnp.minimum(t, lat_tiles - 1)
    lat_part = pl.BlockSpec((TM, 256), lambda t: (clamp(t), 0))
    gla_part = pl.BlockSpec((None, TM, 256), lambda t: (clamp(t) // tpb, clamp(t) % tpb, 0))
    parts, part_specs = [y_pool, y_gla, y_att, y_fnet], [lat_part, gla_part, lat_part, lat_part]
    if ctx_parts is not None:
        ctx_rows = gla_rows - tpb * TM
        assert n_batch * ctx_rows == TM and (tpb * TM) % ctx_rows == 0
        ctx_part = pl.BlockSpec((TM, 256), lambda t: (jnp.maximum(t - lat_tiles, 0), 0))
        gla_ctx = pl.BlockSpec((n_batch, ctx_rows, 256), lambda t: (0, tpb * TM // ctx_rows, 0))
        c_pool, c_att, c_fnet = ctx_parts
        parts += [c_pool, y_gla, c_att, c_fnet]
        part_specs += [ctx_part, gla_ctx, ctx_part, ctx_part]
    return pl.pallas_call(
        functools.partial(_mix_ffn_kernel, n_lat_tiles=None if ctx_parts is None else lat_tiles, final=final),
        out_shape=jax.ShapeDtypeStruct((n_tiles * TM, d), F32),
        grid=(n_tiles,),
        in_specs=[pl.BlockSpec((TM, d), lambda t: (t, 0)), mod_map] + part_specs
                 + [_resident(w_out, (layer,))] + _ffn_weight_specs(g, wg, wu, wd, layer, 1)
                 + [pl.BlockSpec((1, d), lambda t: (0, 0))],
        out_specs=pl.BlockSpec((TM, d), lambda t: (t, 0)),
        compiler_params=_cparams(("arbitrary",)),
        name="mix_ffn",
    )(x, mods, *parts, w_out, g, wg, wu, wd, fg.reshape(1, d))


def _fold_gate_kernel(wr_ref, wa_ref, o_ref):
    a, b = wr_ref[...], wa_ref[...]
    a_hi, b_hi = a.astype(BF16), b.astype(BF16)
    a_lo, b_lo = (a - a_hi.astype(F32)).astype(BF16), (b - b_hi.astype(F32)).astype(BF16)
    o_ref[...] = _dot(a_hi, b_hi) + (_dot(a_hi, b_lo) + _dot(a_lo, b_hi))


def _fold_gates(w_rank, wa_blk):
    depth, d, kr = w_rank.shape
    return pl.pallas_call(
        _fold_gate_kernel,
        out_shape=jax.ShapeDtypeStruct((depth, d, wa_blk.shape[2]), F32),
        grid=(depth,),
        in_specs=[pl.BlockSpec((None, d, kr), lambda l: (l, 0, 0)),
                  pl.BlockSpec((None,) + wa_blk.shape[1:], lambda l: (l, 0, 0))],
        out_specs=pl.BlockSpec((None, d, wa_blk.shape[2]), lambda l: (l, 0, 0)),
        compiler_params=_cparams(("arbitrary",)),
        name="fold_gates",
    )(w_rank, wa_blk)


def _rope(x, c, sa, sb):
    w = x.shape[1]
    return x * c + pltpu.roll(x, w - ROPE_FREQS, 1) * sa + pltpu.roll(x, ROPE_FREQS, 1) * sb


def _tile_rope_table(lines, col):
    return jnp.concatenate([col + lines[r:r + 1, :] for r in range(lines.shape[0])], axis=0)


def _inproj_kernel(x_ref, m_ref, g_ref, w_ref, ba_ref, e_ref, qg_ref, kg_ref,
                   lc_ref, lsa_ref, lsb_ref, cc_ref, csa_ref, csb_ref,
                   up_ref, gq_ref, gk_ref, gv_ref, og_ref, gf_ref, gb_ref,
                   aq_ref, ak_ref, av_ref, uf_ref):
    shift = m_ref[3:4, :]
    scale = m_ref[4:5, :]
    e = e_ref[...]
    n_parts = 2
    rp = TM // n_parts
    lp = rp // GRID_W
    for part in range(n_parts):
        rows = slice(part * rp, (part + 1) * rp)
        lines = slice(part * lp, (part + 1) * lp)
        h = ((_rms(x_ref[rows, :]) * g_ref[...]) * (1.0 + scale) + shift).astype(BF16)
        p = _dot(h, w_ref[...])

        up_ref[rows, :] = p[:, O_POOL:O_POOL + POOL_W]
        uf_ref[rows, :] = p[:, O_FN:O_FN + FNET_W]

        gq_ref[rows, :] = p[:, O_GQ:O_GQ + GLA_KW] * (GLA_DK ** -0.5)
        gk_ref[rows, :] = p[:, O_GK:O_GK + GLA_KW]
        gv_ref[rows, :] = p[:, O_GV:O_GV + GLA_W]
        og_ref[rows, :] = p[:, O_OG:O_OG + GLA_W]
        z = p[:, O_Z:O_Z + 2 * GLA_KW] + ba_ref[...]
        logsig = jnp.minimum(z, 0.0) - jnp.log(1.0 + jnp.exp(-jnp.abs(z)))
        gdec = logsig * (1.0 / GLA_TAU)
        gf_ref[rows, :] = gdec[:, :GLA_KW]
        gb_ref[rows, :] = gdec[:, GLA_KW:]

        rc = _tile_rope_table(lc_ref[lines, :], cc_ref[...])
        rsa = _tile_rope_table(lsa_ref[lines, :], csa_ref[...])
        rsb = _tile_rope_table(lsb_ref[lines, :], csb_ref[...])
        q = _group_rms(p[:, O_AQ:O_AQ + ATT_W], e, ATT_HD) * qg_ref[...]
        q = _rope(q, rc, rsa, rsb) * (ATT_HD ** -0.5 * math.log2(math.e))
        aq_ref[:, rows] = q.T.astype(BF16)
        k = _group_rms(p[:, O_AK:O_AK + ATT_KW], e[:ATT_KW, :ATT_KW], ATT_HD) * kg_ref[...]
        k = _rope(k, rc[:, :ATT_KW], rsa[:, :ATT_KW], rsb[:, :ATT_KW])
        ak_ref[rows, :] = k.astype(BF16)
        v_t = p[:, O_AV:O_AV + ATT_KW].T
        ones = jnp.ones((ATT_VROWS - ATT_HD, rp), F32)
        for kvh in range(ATT_KVH):
            av_ref[kvh, :, rows] = jnp.concatenate(
                [v_t[kvh * ATT_HD:(kvh + 1) * ATT_HD, :], ones], axis=0).astype(BF16)


def _in_projection(x, mods, g, w_in, ba_blk, e256, qg, kg, rope_lines, rope_cols,
                   *, layer, n_tiles, mod_map, line_map, col_map):
    t_rows, d = x.shape
    row = lambda w: pl.BlockSpec((TM, w), lambda t: (t, 0))
    const = lambda shape: pl.BlockSpec(shape, lambda t: (0,) * len(shape))
    per_layer = lambda arr: pl.BlockSpec((None,) + arr.shape[1:], lambda t: (layer,) + (0,) * (arr.ndim - 1))
    out_shapes = (
        jax.ShapeDtypeStruct((t_rows, POOL_W), F32),
        jax.ShapeDtypeStruct((t_rows, GLA_KW), F32),
        jax.ShapeDtypeStruct((t_rows, GLA_KW), F32),
        jax.ShapeDtypeStruct((t_rows, GLA_W), F32),
        jax.ShapeDtypeStruct((t_rows, GLA_W), F32),
        jax.ShapeDtypeStruct((t_rows, GLA_KW), F32),
        jax.ShapeDtypeStruct((t_rows, GLA_KW), F32),
        jax.ShapeDtypeStruct((n_tiles, ATT_W, TM), BF16),
        jax.ShapeDtypeStruct((t_rows, ATT_KW), BF16),
        jax.ShapeDtypeStruct((n_tiles, ATT_KVH, ATT_VROWS, TM), BF16),
        jax.ShapeDtypeStruct((t_rows, FNET_W), F32),
    )
    out_specs = (
        row(POOL_W), row(GLA_KW), row(GLA_KW), row(GLA_W), row(GLA_W), row(GLA_KW), row(GLA_KW),
        pl.BlockSpec((None, ATT_W, TM), lambda t: (t, 0, 0)),
        row(ATT_KW),
        pl.BlockSpec((None, ATT_KVH, ATT_VROWS, TM), lambda t: (t, 0, 0, 0)),
        row(FNET_W),
    )
    return pl.pallas_call(
        _inproj_kernel,
        out_shape=out_shapes,
        grid=(n_tiles,),
        in_specs=[
            pl.BlockSpec((TM, d), lambda t: (t, 0)),
            mod_map,
            pl.BlockSpec((None, None, 1, d), lambda t: (layer, 1, 0, 0)),
            per_layer(w_in),
            per_layer(ba_blk),
            const(e256.shape),
            per_layer(qg),
            per_layer(kg),
        ] + [pl.BlockSpec((TM // GRID_W, ATT_W), line_map)] * 3
          + [pl.BlockSpec((None, GRID_W, ATT_W), col_map)] * 3,
        out_specs=out_specs,
        compiler_params=_cparams(("arbitrary",)),
        name="in_projection",
    )(x, mods, g, w_in, ba_blk, e256, qg, kg, *rope_lines, *rope_cols)


def _pool_kernel(u_ref, inv_ref, w_ref, s_ref, o_ref, pad_ref, *, seq, chunk):
    halo = POOL_HALO
    pad_ref[0:halo, :] = jnp.zeros((halo, POOL_W), F32)
    pad_ref[halo + seq:halo + seq + halo, :] = jnp.zeros((halo, POOL_W), F32)
    pad_ref[halo:halo + seq, :] = u_ref[...]
    rows = chunk + 2 * halo
    lane = lax.broadcasted_iota(jnp.int32, (chunk, POOL_W), 1)

    def body(ci, carry):
        c0 = pl.multiple_of(ci * chunk, chunk)
        xp = pad_ref[pl.ds(c0, rows), :]
        u = xp[halo:halo + chunk, :]
        acc = xp
        wsum = None
        for gi, w in enumerate(POOL_WINDOWS):
            acc = acc + pltpu.roll(acc, w // 2, 0)
            lead = w // 2 - 1
            win = acc if lead == 0 else pltpu.roll(acc, rows - lead, 0)
            win = win[halo:halo + chunk, :]
            wsum = win if wsum is None else jnp.where(lane >= gi * POOL_GW, win, wsum)
        m = wsum * inv_ref[pl.ds(c0, chunk), :] - u
        y = _dot(m.astype(BF16), w_ref[...]) * s_ref[...]
        o_ref[pl.ds(c0, chunk), :] = y
        return carry

    lax.fori_loop(0, seq // chunk, body, 0)


def _pool_inverse_counts(seq):
    t = jnp.arange(seq, dtype=jnp.int32)
    cols = []
    for w in POOL_WINDOWS:
        cnt = jnp.minimum(t + w // 2, seq) - jnp.maximum(t - w // 2, 0)
        cols.append(jnp.broadcast_to((1.0 / cnt.astype(F32))[:, None], (seq, POOL_GW)))
    return jnp.concatenate(cols, axis=1)


def _pool_mix(u, inv_cnt, w_blk, s_pool, *, layer, seq, n_seq, first_block):
    chunk = min(256, seq)
    return pl.pallas_call(
        functools.partial(_pool_kernel, seq=seq, chunk=chunk),
        out_shape=jax.ShapeDtypeStruct((n_seq * seq, POOL_W), F32),
        grid=(n_seq,),
        in_specs=[
            pl.BlockSpec((seq, POOL_W), lambda b: (first_block + b, 0)),
            pl.BlockSpec((seq, POOL_W), lambda b: (0, 0)),
            pl.BlockSpec((None, POOL_W, POOL_W), lambda b: (layer, 0, 0)),
            pl.BlockSpec((None, 1, POOL_W), lambda b: (layer, 0, 0)),
        ],
        out_specs=pl.BlockSpec((seq, POOL_W), lambda b: (b, 0)),
        scratch_shapes=[pltpu.VMEM((seq + 2 * POOL_HALO, POOL_W), F32)],
        compiler_params=_cparams(("arbitrary",)),
        name="pool_mix",
    )(u, inv_cnt, w_blk, s_pool)


def _chunk_cumsums(g):
    sl = 8
    row = lax.broadcasted_iota(jnp.int32, (sl, g.shape[1]), 0)
    pre, suf = [], []
    for i in range(g.shape[0] // sl):
        p = q = g[i * sl:(i + 1) * sl, :]
        s = 1
        while s < sl:
            p = p + jnp.where(row >= s, pltpu.roll(p, s, 0), 0.0)
            q = q + jnp.where(row < sl - s, pltpu.roll(q, sl - s, 0), 0.0)
            s *= 2
        pre.append(p)
        suf.append(q)
    per_chunk = GLA_CHUNK // sl
    for c in range(len(pre) // per_chunk):
        for v in range(1, per_chunk):
            i = c * per_chunk + v
            pre[i] = pre[i] + jnp.broadcast_to(pre[i - 1][sl - 1:sl, :], pre[i].shape)
            j = (c + 1) * per_chunk - 1 - v
            suf[j] = suf[j] + jnp.broadcast_to(suf[j + 1][0:1, :], suf[j].shape)
    return jnp.concatenate(pre, axis=0), jnp.concatenate(suf, axis=0)


def _gla_kernel(*refs, n_batch, reverse, finalize):
    refs = list(refs)
    take = lambda: [refs.pop(0) for _ in range(n_batch)]
    q_refs, k_refs, v_refs, g_refs = take(), take(), take(), take()
    ex_ref, mk_ref, bm_ref = refs.pop(0), refs.pop(0), refs.pop(0)
    if finalize:
        of_ref = refs.pop(0)
        og_refs = take()
        gn_ref, e_ref = refs.pop(0), refs.pop(0)
        o_ref, st_ref, ob_ref = refs
    else:
        o_ref, st_ref = refs
        ob_ref = o_ref

    @pl.when(pl.program_id(0) == 0)
    def _():
        st_ref[...] = jnp.zeros(st_ref.shape, F32)

    sides, tots = [], []
    for b in range(n_batch):
        g = g_refs[b][...]
        pre, suf = _chunk_cumsums(g)
        sides.append(suf if reverse else pre)
        tots.append(pre + suf - g)
    worst = jnp.max(jnp.abs(tots[0]))
    for b in range(1, n_batch):
        worst = jnp.maximum(worst, jnp.max(jnp.abs(tots[b])))
    in_range = worst <= GLA_FAST_RANGE

    @pl.when(in_range)
    def _():
        for b in range(n_batch):
            _gla_fast_tile(q_refs[b], k_refs[b], v_refs[b], mk_ref, bm_ref, st_ref.at[b], ob_ref.at[b],
                           sides[b], tots[b], reverse)

    @pl.when(jnp.logical_not(in_range))
    def _():
        for b in range(n_batch):
            _gla_exact_tile(q_refs[b], k_refs[b], v_refs[b], g_refs[b], ex_ref, mk_ref, st_ref.at[b],
                            ob_ref.at[b], reverse)

    if finalize:
        for b in range(n_batch):
            both = ob_ref[b] + of_ref[b]
            y = _group_rms(both, e_ref[...], GLA_DV) * gn_ref[...]
            o_ref[b] = y * _silu(og_refs[b][...])


def _gla_fast_tile(q_ref, k_ref, v_ref, mk_ref, bm_ref, st_ref, ob_ref, b, tot, reverse):
    ch = GLA_CHUNK
    half = 0.5 * tot
    q, k = q_ref[...], k_ref[...]
    q_in = (q * jnp.exp(b - half)).astype(BF16)
    k_in = (k * jnp.exp(half - b)).astype(BF16)
    q_st = (q * jnp.exp(b)).astype(BF16)
    k_st = (k * jnp.exp(tot - b)).astype(BF16)
    dec = jnp.exp(tot)
    mk = mk_ref[...]
    mk16 = mk.astype(BF16)
    bm16 = bm_ref[...]
    irow = lax.broadcasted_iota(jnp.int32, (ch, GLA_HEADS * ch), 0)
    jcol = lax.broadcasted_iota(jnp.int32, (ch, GLA_HEADS * ch), 1) % ch
    seen = (jcol >= irow) if reverse else (jcol <= irow)
    nt = (((1,), (1,)), ((), ()))
    st = st_ref[...]
    n_chunks = TP // ch
    for ci in range(n_chunks):
        c = (n_chunks - 1 - ci) if reverse else ci
        rows = slice(c * ch, (c + 1) * ch)
        kb = jnp.concatenate([k_in[rows]] * GLA_HEADS, axis=0) * mk16
        a = lax.dot_general(q_in[rows], kb, nt, preferred_element_type=F32)
        a = jnp.where(seen, a, 0.0).astype(BF16)
        v = v_ref[rows, :]
        vb = jnp.concatenate([v.astype(BF16)] * GLA_HEADS, axis=0) * bm16
        o = _dot(a, vb) + lax.dot_general(q_st[rows], st.astype(BF16), nt, preferred_element_type=F32)
        kv = lax.dot_general(v.astype(BF16), k_st[rows], (((0,), (0,)), ((), ())),
                             preferred_element_type=F32)
        st = st * dec[c * ch:c * ch + 1, :] + kv * mk
        ob_ref[rows, :] = o
    st_ref[...] = st


def _gla_exact_tile(q_ref, k_ref, v_ref, g_ref, ex_ref, mk_ref, st_ref, ob_ref, reverse):
    sub = GLA_SUB
    n_chunks = TP // sub
    row = lax.broadcasted_iota(jnp.int32, (sub, GLA_KW), 0)
    ex = ex_ref[...]
    mk = mk_ref[...]

    def body(ci, carry):
        c = (n_chunks - 1 - ci) if reverse else ci
        r0 = pl.multiple_of(c * sub, sub)
        q = q_ref[pl.ds(r0, sub), :]
        k = k_ref[pl.ds(r0, sub), :]
        v = v_ref[pl.ds(r0, sub), :]
        b = g_ref[pl.ds(r0, sub), :]
        s = 1
        while s < sub:
            if reverse:
                b = b + jnp.where(row < sub - s, pltpu.roll(b, sub - s, 0), 0.0)
            else:
                b = b + jnp.where(row >= s, pltpu.roll(b, s, 0), 0.0)
            s *= 2
        edge = b[0:1, :] if reverse else b[sub - 1:sub, :]
        st = st_ref[...]
        qd = (q * jnp.exp(b)).astype(BF16)
        o = lax.dot_general(qd, st.astype(BF16), (((1,), (1,)), ((), ())),
                            preferred_element_type=F32)
        parts = []
        for j in range(sub):
            seen = (row <= j) if reverse else (row >= j)
            dec = jnp.exp(jnp.where(seen, b - b[j:j + 1, :], NEG_BIG))
            parts.append((dec * q * k[j:j + 1, :]).astype(BF16))
        r = _dot(jnp.concatenate(parts, axis=0), ex)
        for j in range(sub):
            o = o + r[j * sub:(j + 1) * sub, :] * v[j:j + 1, :]
        kd = (k * jnp.exp(edge - b)).astype(BF16)
        kv = lax.dot_general(v.astype(BF16), kd, (((0,), (0,)), ((), ())),
                             preferred_element_type=F32)
        st_ref[...] = st * jnp.exp(edge) + kv * mk
        ob_ref[pl.ds(r0, sub), :] = o
        return carry

    lax.fori_loop(0, n_chunks, body, 0, unroll=2)


def _gla_step_tile(n_lat, n_ctx, reverse):
    def tile(s):
        is_ctx = s < n_ctx
        sc = jnp.where(is_ctx, s, 0)
        sl = jnp.where(is_ctx, 0, s - n_ctx)
        if reverse:
            sc = n_ctx - 1 - sc
            sl = n_lat - 1 - sl
        return is_ctx, sc, sl
    return tile


def _gla_mix(gq, gk, gv, gf, gb, og, ex, mk, gn, e256, *, n_batch, seq, ctx):
    n_lat, n_ctx = seq // TP, ctx // TP
    const = lambda shape: pl.BlockSpec(shape, lambda s: (0,) * len(shape))

    def run(reverse, finalize, gate, extra):
        tile = _gla_step_tile(n_lat, n_ctx, reverse)

        def rows(w, b):
            def index_map(s):
                is_ctx, sc, sl = tile(s)
                return (jnp.where(is_ctx, n_batch * n_lat + b * n_ctx + sc, b * n_lat + sl), 0)
            return pl.BlockSpec((TP, w), index_map)

        def per_batch(arr, w):
            return [arr] * n_batch, [rows(w, b) for b in range(n_batch)]

        def out_map(s):
            is_ctx, sc, sl = tile(s)
            return (0, jnp.where(is_ctx, n_lat + sc, sl), 0)

        stacked = pl.BlockSpec((n_batch, TP, GLA_W), out_map)
        args, in_specs = [], []
        for arr, w in ((gq, GLA_KW), (gk, GLA_KW), (gv, GLA_W), (gate, GLA_KW)):
            a, sp = per_batch(arr, w)
            args += a
            in_specs += sp
        args += [ex, mk, e256]
        in_specs += [const(ex.shape), const(mk.shape), const(e256.shape)]
        scratch = [pltpu.VMEM((n_batch, GLA_W, GLA_KW), F32)]
        if finalize:
            o_f, og_rows, gn_row, e_blk = extra
            a, sp = per_batch(og_rows, GLA_W)
            args += [o_f] + a + [gn_row, e_blk]
            in_specs += [stacked] + sp + [const((1, GLA_W)), const(e_blk.shape)]
            scratch.append(pltpu.VMEM((n_batch, TP, GLA_W), F32))
        return pl.pallas_call(
            functools.partial(_gla_kernel, n_batch=n_batch, reverse=reverse, finalize=finalize),
            out_shape=jax.ShapeDtypeStruct((n_batch, seq + ctx, GLA_W), F32),
            grid=(n_lat + n_ctx,),
            in_specs=in_specs,
            out_specs=stacked,
            scratch_shapes=scratch,
            compiler_params=_cparams(("arbitrary",)),
            name="gla_bwd" if reverse else "gla_fwd",
        )(*args)

    o_f = run(False, False, gf, None)
    return run(True, True, gb, [o_f, og, gn, e256])


def _att_query_groups(q_t):
    tq = q_t.shape[1]
    zero = jnp.zeros((ATT_HD, tq), BF16)
    groups = []
    for g in range(ATT_KVH):
        halves = []
        for h in (2 * g, 2 * g + 1):
            qh = q_t[h * ATT_HD:(h + 1) * ATT_HD, :]
            halves.append(jnp.concatenate([qh, zero] if g == 0 else [zero, qh], axis=0))
        groups.append(jnp.concatenate(halves, axis=1))
    return groups


def _att_store(o_ref, weighted, denom, tq):
    heads = []
    for g in range(ATT_KVH):
        og = weighted[g] / denom[g]
        heads += [og[:, :tq], og[:, tq:]]
    o_ref[...] = jnp.concatenate(heads, axis=0).T.astype(o_ref.dtype)


def _att_kernel(*refs, n_main, has_ctx, bounded):
    refs = list(refs)
    q_ref, km_ref, vm_ref = refs.pop(0), refs.pop(0), refs.pop(0)
    kc_ref, vc_ref = (refs.pop(0), refs.pop(0)) if has_ctx else (None, None)
    o_ref, s0_ref, s1_ref = refs
    q_t = q_ref[...]
    tq = q_t.shape[1]
    tk = vm_ref.shape[3]
    groups = _att_query_groups(q_t)

    def scores(kblk, buf):
        for g in range(ATT_KVH):
            s = _dot(kblk, groups[g])
            buf[g, 0:kblk.shape[0], :] = jnp.exp2(s).astype(BF16) if bounded else s

    def consume(buf, rows, v_of, carry):
        out = []
        for g in range(ATT_KVH):
            m, l, acc = carry[g]
            s = buf[g, 0:rows, :]
            if bounded:
                pv = _dot(v_of(g), s)
                out.append((m, l + pv[ATT_HD:ATT_HD + 1, :], acc + pv[:ATT_HD, :]))
                continue
            m_new = jnp.maximum(m, jnp.max(s, axis=0, keepdims=True))
            alpha = jnp.exp2(m - m_new)
            p = jnp.exp2(s - m_new).astype(BF16)
            pv = _dot(v_of(g), p)
            acc = alpha * acc + pv[:ATT_HD, :]
            l = alpha * l + pv[ATT_HD:ATT_HD + 1, :]
            out.append((m_new, l, acc))
        return tuple(out)

    def k_main(j):
        return km_ref[pl.ds(pl.multiple_of(j * tk, tk), tk), :]

    def v_main(j):
        return lambda g: vm_ref[j, g]

    def pair(j, carry, issue_next):
        scores(k_main(j + 1), s1_ref)
        carry = consume(s0_ref, tk, v_main(j), carry)
        issue_next(j + 2)
        return consume(s1_ref, tk, v_main(j + 1), carry)

    carry = tuple((jnp.full((1, 2 * tq), -jnp.inf, F32), jnp.zeros((1, 2 * tq), F32),
                   jnp.zeros((ATT_HD, 2 * tq), F32)) for _ in range(ATT_KVH))
    next_main = lambda j: scores(k_main(j), s0_ref)
    if has_ctx:
        after_main = lambda j: scores(kc_ref[...], s0_ref)
    else:
        after_main = lambda j: None

    scores(k_main(0), s0_ref)
    if n_main > 1:
        pairs_per_trip = 4 if n_main % 8 == 0 else (2 if n_main % 4 == 0 else 1)

        def body(i, carry):
            for u in range(pairs_per_trip):
                carry = pair(2 * (pairs_per_trip * i + u), carry, next_main)
            return carry

        n_trips = n_main // (2 * pairs_per_trip)
        carry = lax.fori_loop(0, n_trips - 1, body, carry)
        for u in range(pairs_per_trip):
            last = u == pairs_per_trip - 1
            carry = pair(2 * (pairs_per_trip * (n_trips - 1) + u), carry, after_main if last else next_main)
    else:
        assert not has_ctx
        carry = consume(s0_ref, tk, v_main(0), carry)
    if has_ctx:
        carry = consume(s0_ref, kc_ref.shape[0], lambda g: vc_ref[g], carry)
    _att_store(o_ref, [c[2] for c in carry], [c[1] for c in carry], tq)


def _attention(aq, ak, av, *, n_batch, seq, ctx, ctx_queries, bounded=False):
    lat_tiles = seq // TM
    ctx_tile = n_batch * lat_tiles
    ctx_cols = lambda b, i: (ctx_tile + (b * ctx) // TM, 0, ((b * ctx) % TM) // ctx)
    ctx_vcols = lambda b, i: (ctx_tile + (b * ctx) // TM, 0, 0, ((b * ctx) % TM) // ctx)
    if not ctx_queries:
        sub = TM // TQ
        grid = (n_batch, seq // TQ)
        in_specs = [
            pl.BlockSpec((None, ATT_W, TQ), lambda b, i: (b * lat_tiles + i // sub, 0, i % sub)),
            pl.BlockSpec((seq, ATT_KW), lambda b, i: (b, 0)),
            pl.BlockSpec((lat_tiles, ATT_KVH, ATT_VROWS, TM), lambda b, i: (b, 0, 0, 0)),
            pl.BlockSpec((ctx, ATT_KW), lambda b, i: (n_batch * seq // ctx + b, 0)),
            pl.BlockSpec((None, ATT_KVH, ATT_VROWS, ctx), ctx_vcols),
        ]
        out_spec = pl.BlockSpec((TQ, ATT_W), lambda b, i: (b * (seq // TQ) + i, 0))
        assert lat_tiles % 2 == 0
        out_rows = n_batch * seq
        args = (aq, ak, av, ak, av)
        kern = functools.partial(_att_kernel, n_main=lat_tiles, has_ctx=True, bounded=bounded)
        s_shape = (ATT_KVH, TM, 2 * TQ)
    else:
        grid = (n_batch, 1)
        in_specs = [
            pl.BlockSpec((None, ATT_W, ctx), ctx_cols),
            pl.BlockSpec((ctx, ATT_KW), lambda b, i: (n_batch * seq // ctx + b, 0)),
            pl.BlockSpec((1, ATT_KVH, ATT_VROWS, ctx), ctx_vcols),
        ]
        out_spec = pl.BlockSpec((ctx, ATT_W), lambda b, i: (b, 0))
        kern = functools.partial(_att_kernel, n_main=1, has_ctx=False, bounded=False)
        out_rows = n_batch * ctx
        args = (aq, ak, av)
        s_shape = (ATT_KVH, ctx, 2 * ctx)
    return pl.pallas_call(
        kern,
        out_shape=jax.ShapeDtypeStruct((out_rows, ATT_W), BF16),
        grid=grid,
        in_specs=in_specs,
        out_specs=out_spec,
        scratch_shapes=[pltpu.VMEM(s_shape, BF16 if bounded else F32)] * 2,
        compiler_params=_cparams(("arbitrary", "arbitrary")),
        name="attention_ctx" if ctx_queries else ("attention_bounded" if bounded else "attention"),
    )(*args)


def _fft_a_kernel(x_ref, d_ref, tr_ref, ti_ref, yr_ref, yi_ref):
    rb, w = x_ref.shape[1], x_ref.shape[2]
    x = jnp.concatenate([x_ref[:, r, :] for r in range(rb)], axis=1)
    y = _dot(d_ref[...], x.astype(BF16))
    tr, ti = tr_ref[...], ti_ref[...]
    for r in range(rb):
        yr, yi = y[:FFT_N1, r * w:(r + 1) * w], y[FFT_N1:, r * w:(r + 1) * w]
        c, s = tr[:, r:r + 1], ti[:, r:r + 1]
        yr_ref[:, r, :] = yr * c - yi * s
        yi_ref[:, r, :] = yr * s + yi * c


def _channel_mix(xr, xi, c_ref, wf_ref, norm):
    xc = jnp.concatenate([xr, xi], axis=1).astype(BF16)
    f = _dot(xc, c_ref[...]) * norm
    return _dot(f.astype(BF16), wf_ref[...])


def _fft_c_kernel(yr_ref, yi_ref, m_ref, c_ref, wf_ref, o_ref, *, n2, norm):
    w = yr_ref.shape[1]
    blk = jnp.concatenate(
        [jnp.concatenate([yr_ref[j * n2:(j + 1) * n2, :] for j in range(8)], axis=1),
         jnp.concatenate([yi_ref[j * n2:(j + 1) * n2, :] for j in range(8)], axis=1)], axis=0)
    x = _dot(m_ref[...], blk.astype(BF16))
    xr = jnp.concatenate([x[:n2, j * w:(j + 1) * w] for j in range(8)], axis=0)
    xi = jnp.concatenate([x[n2:, j * w:(j + 1) * w] for j in range(8)], axis=0)
    y = _channel_mix(xr, xi, c_ref, wf_ref, norm)
    for j in range(8):
        o_ref[:, j, :] = y[j * n2:(j + 1) * n2, :]


def _fft_dense_kernel(x_ref, d_ref, c_ref, wf_ref, o_ref, *, n, norm):
    x = _dot(d_ref[...], x_ref[...].astype(BF16))
    o_ref[...] = _channel_mix(x[:n, :], x[n:, :], c_ref, wf_ref, norm)


def _dft_parts(n):
    idx = np.arange(n)
    ang = 2.0 * np.pi * ((idx[:, None] * idx[None, :]) % n) / n
    return np.cos(ang), np.sin(ang)


def _mxu_const(a):
    return jnp.asarray(a, F32).astype(BF16)


def _fnet_consts(seq):
    n1, n2 = FFT_N1, seq // FFT_N1
    c1, s1 = _dft_parts(n1)
    da = np.concatenate([c1, -s1], axis=0)
    k1 = np.arange(n1)[:, None]
    m2 = np.arange(n2)[None, :]
    ang = 2.0 * np.pi * ((k1 * m2) % seq) / seq
    tw_r, tw_i = np.cos(ang), -np.sin(ang)
    c2, s2 = _dft_parts(n2)
    mc = np.block([[c2, s2], [-s2, c2]])
    return da, tw_r, tw_i, mc


def _channel_consts():
    cc, sc = _dft_parts(FNET_HD)
    eye = np.eye(FNET_HEADS)
    return np.concatenate([np.kron(eye, cc), np.kron(eye, sc)], axis=0)


def _fnet_latent(uf3, wf, *, layer, n_batch, seq):
    n1, n2 = FFT_N1, seq // FFT_N1
    da, tw_r, tw_i, mc = _fnet_consts(seq)
    da, mc, cc = _mxu_const(da), _mxu_const(mc), _mxu_const(_channel_consts())
    rb = min(n2, 32)
    blocked = lambda tw: jnp.asarray(tw.reshape(n1, n2 // rb, rb).transpose(1, 0, 2), F32)
    const2 = lambda shape: pl.BlockSpec(shape, lambda b, j: (0,) * len(shape))
    slab = pl.BlockSpec((n1, rb, FNET_W), lambda b, j: (b, j, 0))
    yr, yi = pl.pallas_call(
        _fft_a_kernel,
        out_shape=(jax.ShapeDtypeStruct((n_batch * n1, n2, FNET_W), F32),) * 2,
        grid=(n_batch, n2 // rb),
        in_specs=[
            slab,
            const2(da.shape),
            pl.BlockSpec((None, n1, rb), lambda b, j: (j, 0, 0)),
            pl.BlockSpec((None, n1, rb), lambda b, j: (j, 0, 0)),
        ],
        out_specs=(slab, slab),
        compiler_params=_cparams(("arbitrary", "arbitrary")),
        name="fft_stage_a",
    )(uf3, da, blocked(tw_r), blocked(tw_i))
    yr = yr.reshape(n_batch * n1 * n2, FNET_W)
    yi = yi.reshape(n_batch * n1 * n2, FNET_W)
    norm = 1.0 / math.sqrt(seq * FNET_HD)
    out = pl.pallas_call(
        functools.partial(_fft_c_kernel, n2=n2, norm=norm),
        out_shape=jax.ShapeDtypeStruct((n_batch, n2, n1, FNET_W), F32),
        grid=(n_batch, n1 // 8),
        in_specs=[
            pl.BlockSpec((8 * n2, FNET_W), lambda b, i: (b * (n1 // 8) + i, 0)),
            pl.BlockSpec((8 * n2, FNET_W), lambda b, i: (b * (n1 // 8) + i, 0)),
            const2(mc.shape), const2(cc.shape),
            pl.BlockSpec((None,) + wf.shape[1:], lambda b, i: (layer, 0, 0)),
        ],
        out_specs=pl.BlockSpec((None, n2, 8, FNET_W), lambda b, i: (b, 0, i, 0)),
        compiler_params=_cparams(("arbitrary", "arbitrary")),
        name="fft_stage_c",
    )(yr, yi, mc, cc, wf)
    return out.reshape(n_batch * seq, FNET_W)


def _fnet_context(uf, wf, *, layer, n_batch, ctx, first_block):
    c, s = _dft_parts(ctx)
    dd, cc = _mxu_const(np.concatenate([c, -s], axis=0)), _mxu_const(_channel_consts())
    const = lambda shape: pl.BlockSpec(shape, lambda b: (0,) * len(shape))
    return pl.pallas_call(
        functools.partial(_fft_dense_kernel, n=ctx, norm=1.0 / math.sqrt(ctx * FNET_HD)),
        out_shape=jax.ShapeDtypeStruct((n_batch * ctx, FNET_W), F32),
        grid=(n_batch,),
        in_specs=[pl.BlockSpec((ctx, FNET_W), lambda b: (first_block + b, 0)),
                  const(dd.shape), const(cc.shape),
                  pl.BlockSpec((None,) + wf.shape[1:], lambda b: (layer, 0, 0))],
        out_specs=pl.BlockSpec((ctx, FNET_W), lambda b: (b, 0)),
        compiler_params=_cparams(("arbitrary",)),
        name="fft_context",
    )(uf, dd, cc, wf)


def _rope_tables(seq):
    freqs = ROPE_THETA ** (-jnp.arange(ROPE_FREQS, dtype=F32) / ROPE_FREQS)
    row_ang = jnp.arange(seq // GRID_W, dtype=F32)[:, None] * freqs
    col_ang = jnp.arange(GRID_W, dtype=F32)[:, None] * freqs
    lanes = np.arange(ATT_HD)
    is_row = jnp.asarray(np.tile(lanes < 2 * ROPE_FREQS, ATT_QH)[None, :], F32)
    first_half = jnp.asarray(np.tile((lanes // ROPE_FREQS) % 2 == 0, ATT_QH)[None, :], F32)
    spread = lambda a: jnp.tile(a, (1, ATT_W // ROPE_FREQS))

    def tables(ang, mask, ident_lines):
        cos, sin = spread(jnp.cos(ang)) * mask, spread(jnp.sin(ang)) * mask
        ident = jnp.zeros((ident_lines, ATT_W), F32)
        return (jnp.concatenate([cos, ident + mask], axis=0),
                jnp.concatenate([-sin * first_half, ident], axis=0),
                jnp.concatenate([sin * (1.0 - first_half), ident], axis=0))

    rows = tables(row_ang, is_row, TM // GRID_W)
    cols = tuple(t.reshape(2, GRID_W, ATT_W) for t in tables(col_ang, 1.0 - is_row, GRID_W))
    return rows, cols


def _block_ones(width, group):
    return jnp.asarray(np.kron(np.eye(width // group), np.ones((group, group))), BF16)


def kernel(x, c, ctx, c_ctx, w_mod, b_mod, norm_g, ffn_wg, ffn_wu, ffn_wd, w_in, w_out,
           pool_w, pool_scale, gla_wa, gla_ba, gla_norm, att_qnorm, att_knorm, fnet_w, final_norm):
    n_batch, seq, d = x.shape
    n_ctx = ctx.shape[1]
    assert d == D_MODEL and seq % TM == 0 and (n_batch * n_ctx) % TM == 0 and n_ctx % TP == 0
    assert seq % (8 * FFT_N1) == 0 and n_ctx <= TM and TM % n_ctx == 0 and n_batch + 1 <= 8
    lat_rows = n_batch * seq
    lat_tiles = lat_rows // TM
    all_tiles = lat_tiles + (n_batch * n_ctx) // TM
    tiles_per_batch = seq // TM
    line_map = lambda t: (jnp.where(t < lat_tiles, t % tiles_per_batch, tiles_per_batch), 0)
    col_map = lambda t: (jnp.where(t < lat_tiles, 0, 1), 0, 0)

    cvec = jnp.concatenate([c, c_ctx[None, :], jnp.zeros((8 - n_batch - 1, d), F32)], axis=0)
    mods = _modulation(cvec, w_mod, b_mod).reshape(DEPTH, 8, N_MOD, d)

    rope_lines, rope_cols = _rope_tables(seq)
    e256 = _block_ones(ATT_W, ATT_HD)
    gla_ex = _block_ones(GLA_W, GLA_DV)[::2, :]
    gla_mk = jnp.asarray(np.kron(np.eye(GLA_HEADS), np.ones((GLA_DV, GLA_DK))), F32)

    wg, wu, wd = ffn_wg.astype(BF16), ffn_wu.astype(BF16), ffn_wd.astype(BF16)
    w_rank = jnp.concatenate([w_in[..., 768:800], jnp.zeros((DEPTH, d, 128 - 2 * GLA_RANK), F32)], axis=-1)
    wa_blk = jnp.zeros((DEPTH, 128, 2 * GLA_KW), F32)
    wa_blk = wa_blk.at[:, :GLA_RANK, :GLA_KW].set(gla_wa[:, 0]).at[:, GLA_RANK:2 * GLA_RANK, GLA_KW:].set(gla_wa[:, 1])
    wi = jnp.concatenate([w_in[..., :768], w_in[..., 800:D_IN], _fold_gates(w_rank, wa_blk)], axis=-1).astype(BF16)
    ba_blk = gla_ba.reshape(DEPTH, 1, 2 * GLA_KW)
    qg = jnp.tile(att_qnorm, (1, ATT_QH))[:, None, :]
    kg = jnp.tile(att_knorm, (1, ATT_KVH))[:, None, :]
    pool_blk = jnp.zeros((DEPTH, POOL_W, POOL_W), F32)
    for gi in range(len(POOL_WINDOWS)):
        sl = slice(gi * POOL_GW, (gi + 1) * POOL_GW)
        pool_blk = pool_blk.at[:, sl, sl].set(pool_w[:, gi])
    pool_blk = pool_blk.astype(BF16)
    pool_s = pool_scale[:, None, :]
    wf = fnet_w.astype(BF16)
    wo = w_out.astype(BF16)
    norm4 = norm_g[:, :, None, :]
    n2 = seq // FFT_N1
    inv_lat = _pool_inverse_counts(seq)

    xs = x.reshape(lat_rows, d)
    ctx_rows = ctx.reshape(n_batch * n_ctx, d)
    for i in range(DEPTH):
        ctx_out = i < DEPTH - 1
        mod_map = _mod_spec(i, lat_tiles, tiles_per_batch, n_batch, d)
        gn = jnp.tile(gla_norm[i], GLA_HEADS)[None, :]

        xs = _half_ffn(xs, mods, norm4, wg, wu, wd, layer=i, n_tiles=all_tiles,
                       mod_map=mod_map, ctx_rows=ctx_rows if i == 0 else None)
        (u_pool, gq, gk, gv, og, gf, gb, aq, ak, av, uf) = _in_projection(
            xs, mods, norm4, wi, ba_blk, e256, qg, kg, rope_lines, rope_cols,
            layer=i, n_tiles=all_tiles, mod_map=mod_map, line_map=line_map, col_map=col_map)

        y_pool = _pool_mix(u_pool, inv_lat, pool_blk, pool_s, layer=i, seq=seq, n_seq=n_batch, first_block=0)
        y_gla = _gla_mix(gq, gk, gv, gf, gb, og, gla_ex, gla_mk, gn, e256, n_batch=n_batch, seq=seq, ctx=n_ctx)
        score_bound = (ATT_HD * ATT_HD ** -0.5 * math.log2(math.e) * ATT_BOUND_SLACK
                       * jnp.max(jnp.abs(att_qnorm[i])) * jnp.max(jnp.abs(att_knorm[i])))
        attend = functools.partial(_attention, aq, ak, av, n_batch=n_batch, seq=seq, ctx=n_ctx, ctx_queries=False)
        y_att = lax.cond(score_bound <= ATT_BOUND_MAX, lambda: attend(bounded=True), lambda: attend())
        y_fnet = _fnet_latent(uf.reshape(uf.shape[0] // n2, n2, FNET_W), wf, layer=i, n_batch=n_batch, seq=seq)
        n_tiles, ctx_parts = lat_tiles, None
        if ctx_out:
            first_ctx = lat_rows // n_ctx
            ctx_parts = (
                _pool_mix(u_pool, _pool_inverse_counts(n_ctx), pool_blk, pool_s, layer=i, seq=n_ctx, n_seq=n_batch,
                          first_block=first_ctx),
                _attention(aq, ak, av, n_batch=n_batch, seq=seq, ctx=n_ctx, ctx_queries=True),
                _fnet_context(uf, wf, layer=i, n_batch=n_batch, ctx=n_ctx, first_block=first_ctx),
            )
            n_tiles = all_tiles
        xs = _mix_ffn(xs, mods, y_pool, y_gla, y_att, y_fnet, wo, norm4, wg, wu, wd, final_norm,
                      layer=i, n_tiles=n_tiles, mod_map=mod_map, final=not ctx_out, ctx_parts=ctx_parts)
    return xs.reshape(n_batch, seq, d)
```

```python
import functools
import math

import jax
import jax.numpy as jnp
import numpy as np
from jax import lax
from jax.experimental import pallas as pl
from jax.experimental.pallas import tpu as pltpu

F32 = jnp.float32
BF16 = jnp.bfloat16

D_MODEL = 1024
DEPTH = 2
GRID_W = 64
EPS = 1e-6
N_MOD = 9
D_FF = 2816

POOL_W = 256
POOL_WINDOWS = (2, 4, 8, 16)
POOL_GW = 64
POOL_HALO = 8

GLA_HEADS = 4
GLA_W = 256
GLA_DV = 64
GLA_DK = 32
GLA_RANK = 16
GLA_TAU = 16.0
GLA_KW = GLA_HEADS * GLA_DK
GLA_SUB = 16
GLA_CHUNK = 64
GLA_FAST_RANGE = 150.0

ATT_W = 256
ATT_HD = 64
ATT_QH = 4
ATT_KVH = 2
ATT_KW = ATT_KVH * ATT_HD
ATT_VROWS = ATT_HD + 16
ATT_BOUND_SLACK = 1.02
ATT_BOUND_MAX = 50.0
ROPE_FREQS = 16
ROPE_THETA = 10000.0

FNET_W = 256
FNET_HEADS = 4
FNET_HD = 64
FFT_N1 = 64

D_IN = 1824
D_IN_PAD = 2048

O_POOL, O_GQ, O_GK, O_GV, O_OG, O_AQ, O_AK, O_AV, O_FN, O_Z = 0, 256, 384, 512, 768, 1024, 1280, 1408, 1536, 1792

TM = 512
TQ = 512
TP = 256
MXU_TILE = 256
FF_CHUNKS = ((0, 6 * MXU_TILE), (6 * MXU_TILE, D_FF))
NEG_BIG = -1e30

VMEM_LIMIT = 56 * 1024 * 1024


def _cparams(sem):
    return pltpu.CompilerParams(dimension_semantics=sem, vmem_limit_bytes=VMEM_LIMIT)


def _dot(a, b):
    return jnp.dot(a, b, preferred_element_type=F32)


def _rms(x):
    return x * lax.rsqrt(jnp.mean(x * x, axis=-1, keepdims=True) + EPS)


def _silu(x):
    return x * jax.nn.sigmoid(x)


def _group_rms(x, e, width):
    ss = _dot((x * x).astype(BF16), e)
    return x * lax.rsqrt(ss * (1.0 / width) + EPS)


def _mod_kernel(c_ref, w_ref, b_ref, o_ref):
    s = _silu(c_ref[...]).astype(BF16)
    o_ref[...] = _dot(s, w_ref[...].astype(BF16)) + b_ref[...]


def _modulation(cvec, w_mod, b_mod):
    depth, d, nd = w_mod.shape
    tn = nd // 8
    return pl.pallas_call(
        _mod_kernel,
        out_shape=jax.ShapeDtypeStruct((depth, 8, nd), F32),
        grid=(depth, nd // tn),
        in_specs=[
            pl.BlockSpec((8, d), lambda l, j: (0, 0)),
            pl.BlockSpec((None, d, tn), lambda l, j: (l, 0, j)),
            pl.BlockSpec((None, 1, tn), lambda l, j: (l, 0, j)),
        ],
        out_specs=pl.BlockSpec((None, 8, tn), lambda l, j: (l, 0, j)),
        compiler_params=_cparams(("arbitrary", "arbitrary")),
        name="modulation",
    )(cvec, w_mod, b_mod.reshape(depth, 1, nd))


def _half_ffn_rows(x, m_ref, g_ref, wg_ref, wu_ref, wd_ref, mod_base):
    shift = m_ref[mod_base:mod_base + 1, :]
    scale = m_ref[mod_base + 1:mod_base + 2, :]
    gate = m_ref[mod_base + 2:mod_base + 3, :]
    h = ((_rms(x) * g_ref[...]) * (1.0 + scale) + shift).astype(BF16)
    y = jnp.zeros(x.shape, F32)
    for lo, hi in FF_CHUNKS:
        sl = slice(lo, hi)
        a = _dot(h, wg_ref[:, sl])
        u = _dot(h, wu_ref[:, sl])
        y = y + _dot((_silu(a) * u).astype(BF16), wd_ref[sl, :])
    return x + (0.5 * gate) * y


def _ffn_kernel(*refs, n_lat_tiles):
    if n_lat_tiles is None:
        x_ref, m_ref, g_ref, wg_ref, wu_ref, wd_ref, o_ref = refs
        x = x_ref[...]
    else:
        x_ref, c_ref, m_ref, g_ref, wg_ref, wu_ref, wd_ref, o_ref = refs
        x = jnp.where(pl.program_id(0) < n_lat_tiles, x_ref[...], c_ref[...])
    o_ref[...] = _half_ffn_rows(x, m_ref, g_ref, wg_ref, wu_ref, wd_ref, 0)


def _mix_ffn_kernel(*refs, n_lat_tiles, final):
    if n_lat_tiles is None:
        (x_ref, m_ref, yp_ref, yg_ref, ya_ref, yf_ref, wo_ref,
         g_ref, wg_ref, wu_ref, wd_ref, fg_ref, o_ref) = refs
        yp, yg, ya, yf = yp_ref[...], yg_ref[...], ya_ref[...], yf_ref[...]
    else:
        (x_ref, m_ref, yp_ref, yg_ref, ya_ref, yf_ref, cp_ref, cg_ref, ca_ref, cf_ref, wo_ref,
         g_ref, wg_ref, wu_ref, wd_ref, fg_ref, o_ref) = refs
        lat = pl.program_id(0) < n_lat_tiles
        yp = jnp.where(lat, yp_ref[...], cp_ref[...])
        yg = jnp.where(lat, yg_ref[...], cg_ref[...].reshape(yg_ref.shape))
        ya = jnp.where(lat, ya_ref[...], ca_ref[...])
        yf = jnp.where(lat, yf_ref[...], cf_ref[...])
    y = jnp.concatenate([yp.astype(BF16), yg.astype(BF16), ya.astype(BF16), yf.astype(BF16)], axis=1)
    x = x_ref[...] + m_ref[5:6, :] * _dot(y, wo_ref[...])
    out = _half_ffn_rows(x, m_ref, g_ref, wg_ref, wu_ref, wd_ref, 6)
    if final:
        out = _rms(out) * fg_ref[...]
    o_ref[...] = out


def _mod_spec(layer, n_lat_tiles, tiles_per_batch, n_batch, d):
    def index_map(t):
        return (layer, jnp.where(t < n_lat_tiles, t // tiles_per_batch, n_batch), 0, 0)
    return pl.BlockSpec((None, None, N_MOD, d), index_map)


def _resident(arr, lead):
    block = (None,) * len(lead) + arr.shape[len(lead):]
    index = tuple(lead) + (0,) * (arr.ndim - len(lead))
    return pl.BlockSpec(block, lambda t: index, pipeline_mode=pl.Buffered(1))


def _ffn_weight_specs(g, wg, wu, wd, layer, half):
    d = g.shape[-1]
    lead = (layer, half)
    return [pl.BlockSpec((None, None, 1, d), lambda t: (layer, 2 * half, 0, 0)),
            _resident(wg, lead), _resident(wu, lead), _resident(wd, lead)]


def _half_ffn(x, mods, g, wg, wu, wd, *, layer, n_tiles, mod_map, ctx_rows=None):
    d = x.shape[1]
    n_lat = None if ctx_rows is None else x.shape[0] // TM
    if ctx_rows is None:
        rows, row_specs = [x], [pl.BlockSpec((TM, d), lambda t: (t, 0))]
    else:
        rows = [x, ctx_rows]
        row_specs = [pl.BlockSpec((TM, d), lambda t: (jnp.minimum(t, n_lat - 1), 0)),
                     pl.BlockSpec((TM, d), lambda t: (jnp.maximum(t - n_lat, 0), 0))]
    return pl.pallas_call(
        functools.partial(_ffn_kernel, n_lat_tiles=n_lat),
        out_shape=jax.ShapeDtypeStruct((n_tiles * TM, d), F32),
        grid=(n_tiles,),
        in_specs=row_specs + [mod_map] + _ffn_weight_specs(g, wg, wu, wd, layer, 0),
        out_specs=pl.BlockSpec((TM, d), lambda t: (t, 0)),
        compiler_params=_cparams(("arbitrary",)),
        name="half_ffn",
    )(*rows, mods, g, wg, wu, wd)


def _mix_ffn(x, mods, y_pool, y_gla, y_att, y_fnet, w_out, g, wg, wu, wd, fg,
             *, layer, n_tiles, mod_map, final, ctx_parts=None):
    d = x.shape[1]
    n_batch, gla_rows = y_gla.shape[0], y_gla.shape[1]
    lat_tiles = y_pool.shape[0] // TM
    tpb = lat_tiles // n_batch
    clamp = lambda t: jnp.minimum(t, lat_tiles - 1)
    lat_part = pl.BlockSpec((TM, 256), lambda t: (clamp(t), 0))
    gla_part = pl.BlockSpec((None, TM, 256), lambda t: (clamp(t) // tpb, clamp(t) % tpb, 0))
    parts, part_specs = [y_pool, y_gla, y_att, y_fnet], [lat_part, gla_part, lat_part, lat_part]
    if ctx_parts is not None:
        ctx_rows = gla_rows - tpb * TM
        assert n_batch * ctx_rows == TM and (tpb * TM) % ctx_rows == 0
        ctx_part = pl.BlockSpec((TM, 256), lambda t: (jnp.maximum(t - lat_tiles, 0), 0))
        gla_ctx = pl.BlockSpec((n_batch, ctx_rows, 256), lambda t: (0, tpb * TM // ctx_rows, 0))
        c_pool, c_att, c_fnet = ctx_parts
        parts += [c_pool, y_gla, c_att, c_fnet]
        part_specs += [ctx_part, gla_ctx, ctx_part, ctx_part]
    return pl.pallas_call(
        functools.partial(_mix_ffn_kernel, n_lat_tiles=None if ctx_parts is None else lat_tiles, final=final),
        out_shape=jax.ShapeDtypeStruct((n_tiles * TM, d), F32),
        grid=(n_tiles,),
        in_specs=[pl.BlockSpec((TM, d), lambda t: (t, 0)), mod_map] + part_specs
                 + [_resident(w_out, (layer,))] + _ffn_weight_specs(g, wg, wu, wd, layer, 1)
                 + [pl.BlockSpec((1, d), lambda t: (0, 0))],
        out_specs=pl.BlockSpec((TM, d), lambda t: (t, 0)),
        compiler_params=_cparams(("arbitrary",)),
        name="mix_ffn",
    )(x, mods, *parts, w_out, g, wg, wu, wd, fg.reshape(1, d))


def _fold_gate_kernel(wr_ref, wa_ref, o_ref):
    a, b = wr_ref[...], wa_ref[...]
    a_hi, b_hi = a.astype(BF16), b.astype(BF16)
    a_lo, b_lo = (a - a_hi.astype(F32)).astype(BF16), (b - b_hi.astype(F32)).astype(BF16)
    o_ref[...] = _dot(a_hi, b_hi) + (_dot(a_hi, b_lo) + _dot(a_lo, b_hi))


def _fold_gates(w_rank, wa_blk):
    depth, d, kr = w_rank.shape
    return pl.pallas_call(
        _fold_gate_kernel,
        out_shape=jax.ShapeDtypeStruct((depth, d, wa_blk.shape[2]), F32),
        grid=(depth,),
        in_specs=[pl.BlockSpec((None, d, kr), lambda l: (l, 0, 0)),
                  pl.BlockSpec((None,) + wa_blk.shape[1:], lambda l: (l, 0, 0))],
        out_specs=pl.BlockSpec((None, d, wa_blk.shape[2]), lambda l: (l, 0, 0)),
        compiler_params=_cparams(("arbitrary",)),
        name="fold_gates",
    )(w_rank, wa_blk)


def _rope(x, c, sa, sb):
    w = x.shape[1]
    return x * c + pltpu.roll(x, w - ROPE_FREQS, 1) * sa + pltpu.roll(x, ROPE_FREQS, 1) * sb


def _tile_rope_table(lines, col):
    return jnp.concatenate([col + lines[r:r + 1, :] for r in range(lines.shape[0])], axis=0)


def _inproj_kernel(x_ref, m_ref, g_ref, w_ref, ba_ref, e_ref, qg_ref, kg_ref,
                   lc_ref, lsa_ref, lsb_ref, cc_ref, csa_ref, csb_ref,
                   up_ref, gq_ref, gk_ref, gv_ref, og_ref, gf_ref, gb_ref,
                   aq_ref, ak_ref, av_ref, uf_ref):
    shift = m_ref[3:4, :]
    scale = m_ref[4:5, :]
    e = e_ref[...]
    n_parts = 2
    rp = TM // n_parts
    lp = rp // GRID_W
    for part in range(n_parts):
        rows = slice(part * rp, (part + 1) * rp)
        lines = slice(part * lp, (part + 1) * lp)
        h = ((_rms(x_ref[rows, :]) * g_ref[...]) * (1.0 + scale) + shift).astype(BF16)
        p = _dot(h, w_ref[...])

        up_ref[rows, :] = p[:, O_POOL:O_POOL + POOL_W]
        uf_ref[rows, :] = p[:, O_FN:O_FN + FNET_W]

        gq_ref[rows, :] = p[:, O_GQ:O_GQ + GLA_KW] * (GLA_DK ** -0.5)
        gk_ref[rows, :] = p[:, O_GK:O_GK + GLA_KW]
        gv_ref[rows, :] = p[:, O_GV:O_GV + GLA_W]
        og_ref[rows, :] = p[:, O_OG:O_OG + GLA_W]
        z = p[:, O_Z:O_Z + 2 * GLA_KW] + ba_ref[...]
        logsig = jnp.minimum(z, 0.0) - jnp.log(1.0 + jnp.exp(-jnp.abs(z)))
        gdec = logsig * (1.0 / GLA_TAU)
        gf_ref[rows, :] = gdec[:, :GLA_KW]
        gb_ref[rows, :] = gdec[:, GLA_KW:]

        rc = _tile_rope_table(lc_ref[lines, :], cc_ref[...])
        rsa = _tile_rope_table(lsa_ref[lines, :], csa_ref[...])
        rsb = _tile_rope_table(lsb_ref[lines, :], csb_ref[...])
        q = _group_rms(p[:, O_AQ:O_AQ + ATT_W], e, ATT_HD) * qg_ref[...]
        q = _rope(q, rc, rsa, rsb) * (ATT_HD ** -0.5 * math.log2(math.e))
        aq_ref[:, rows] = q.T.astype(BF16)
        k = _group_rms(p[:, O_AK:O_AK + ATT_KW], e[:ATT_KW, :ATT_KW], ATT_HD) * kg_ref[...]
        k = _rope(k, rc[:, :ATT_KW], rsa[:, :ATT_KW], rsb[:, :ATT_KW])
        ak_ref[rows, :] = k.astype(BF16)
        v_t = p[:, O_AV:O_AV + ATT_KW].T
        ones = jnp.ones((ATT_VROWS - ATT_HD, rp), F32)
        for kvh in range(ATT_KVH):
            av_ref[kvh, :, rows] = jnp.concatenate(
                [v_t[kvh * ATT_HD:(kvh + 1) * ATT_HD, :], ones], axis=0).astype(BF16)


def _in_projection(x, mods, g, w_in, ba_blk, e256, qg, kg, rope_lines, rope_cols,
                   *, layer, n_tiles, mod_map, line_map, col_map):
    t_rows, d = x.shape
    row = lambda w: pl.BlockSpec((TM, w), lambda t: (t, 0))
    const = lambda shape: pl.BlockSpec(shape, lambda t: (0,) * len(shape))
    per_layer = lambda arr: pl.BlockSpec((None,) + arr.shape[1:], lambda t: (layer,) + (0,) * (arr.ndim - 1))
    out_shapes = (
        jax.ShapeDtypeStruct((t_rows, POOL_W), F32),
        jax.ShapeDtypeStruct((t_rows, GLA_KW), F32),
        jax.ShapeDtypeStruct((t_rows, GLA_KW), F32),
        jax.ShapeDtypeStruct((t_rows, GLA_W), F32),
        jax.ShapeDtypeStruct((t_rows, GLA_W), F32),
        jax.ShapeDtypeStruct((t_rows, GLA_KW), F32),
        jax.ShapeDtypeStruct((t_rows, GLA_KW), F32),
        jax.ShapeDtypeStruct((n_tiles, ATT_W, TM), BF16),
        jax.ShapeDtypeStruct((t_rows, ATT_KW), BF16),
        jax.ShapeDtypeStruct((n_tiles, ATT_KVH, ATT_VROWS, TM), BF16),
        jax.ShapeDtypeStruct((t_rows, FNET_W), F32),
    )
    out_specs = (
        row(POOL_W), row(GLA_KW), row(GLA_KW), row(GLA_W), row(GLA_W), row(GLA_KW), row(GLA_KW),
        pl.BlockSpec((None, ATT_W, TM), lambda t: (t, 0, 0)),
        row(ATT_KW),
        pl.BlockSpec((None, ATT_KVH, ATT_VROWS, TM), lambda t: (t, 0, 0, 0)),
        row(FNET_W),
    )
    return pl.pallas_call(
        _inproj_kernel,
        out_shape=out_shapes,
        grid=(n_tiles,),
        in_specs=[
            pl.BlockSpec((TM, d), lambda t: (t, 0)),
            mod_map,
            pl.BlockSpec((None, None, 1, d), lambda t: (layer, 1, 0, 0)),
            per_layer(w_in),
            per_layer(ba_blk),
            const(e256.shape),
            per_layer(qg),
            per_layer(kg),
        ] + [pl.BlockSpec((TM // GRID_W, ATT_W), line_map)] * 3
          + [pl.BlockSpec((None, GRID_W, ATT_W), col_map)] * 3,
        out_specs=out_specs,
        compiler_params=_cparams(("arbitrary",)),
        name="in_projection",
    )(x, mods, g, w_in, ba_blk, e256, qg, kg, *rope_lines, *rope_cols)


def _pool_kernel(u_ref, inv_ref, w_ref, s_ref, o_ref, pad_ref, *, seq, chunk):
    halo = POOL_HALO
    pad_ref[0:halo, :] = jnp.zeros((halo, POOL_W), F32)
    pad_ref[halo + seq:halo + seq + halo, :] = jnp.zeros((halo, POOL_W), F32)
    pad_ref[halo:halo + seq, :] = u_ref[...]
    rows = chunk + 2 * halo
    lane = lax.broadcasted_iota(jnp.int32, (chunk, POOL_W), 1)

    def body(ci, carry):
        c0 = pl.multiple_of(ci * chunk, chunk)
        xp = pad_ref[pl.ds(c0, rows), :]
        u = xp[halo:halo + chunk, :]
        acc = xp
        wsum = None
        for gi, w in enumerate(POOL_WINDOWS):
            acc = acc + pltpu.roll(acc, w // 2, 0)
            lead = w // 2 - 1
            win = acc if lead == 0 else pltpu.roll(acc, rows - lead, 0)
            win = win[halo:halo + chunk, :]
            wsum = win if wsum is None else jnp.where(lane >= gi * POOL_GW, win, wsum)
        m = wsum * inv_ref[pl.ds(c0, chunk), :] - u
        y = _dot(m.astype(BF16), w_ref[...]) * s_ref[...]
        o_ref[pl.ds(c0, chunk), :] = y
        return carry

    lax.fori_loop(0, seq // chunk, body, 0)


def _pool_inverse_counts(seq):
    t = jnp.arange(seq, dtype=jnp.int32)
    cols = []
    for w in POOL_WINDOWS:
        cnt = jnp.minimum(t + w // 2, seq) - jnp.maximum(t - w // 2, 0)
        cols.append(jnp.broadcast_to((1.0 / cnt.astype(F32))[:, None], (seq, POOL_GW)))
    return jnp.concatenate(cols, axis=1)


def _pool_mix(u, inv_cnt, w_blk, s_pool, *, layer, seq, n_seq, first_block):
    chunk = min(256, seq)
    return pl.pallas_call(
        functools.partial(_pool_kernel, seq=seq, chunk=chunk),
        out_shape=jax.ShapeDtypeStruct((n_seq * seq, POOL_W), F32),
        grid=(n_seq,),
        in_specs=[
            pl.BlockSpec((seq, POOL_W), lambda b: (first_block + b, 0)),
            pl.BlockSpec((seq, POOL_W), lambda b: (0, 0)),
            pl.BlockSpec((None, POOL_W, POOL_W), lambda b: (layer, 0, 0)),
            pl.BlockSpec((None, 1, POOL_W), lambda b: (layer, 0, 0)),
        ],
        out_specs=pl.BlockSpec((seq, POOL_W), lambda b: (b, 0)),
        scratch_shapes=[pltpu.VMEM((seq + 2 * POOL_HALO, POOL_W), F32)],
        compiler_params=_cparams(("arbitrary",)),
        name="pool_mix",
    )(u, inv_cnt, w_blk, s_pool)


def _chunk_cumsums(g):
    sl = 8
    row = lax.broadcasted_iota(jnp.int32, (sl, g.shape[1]), 0)
    pre, suf = [], []
    for i in range(g.shape[0] // sl):
        p = q = g[i * sl:(i + 1) * sl, :]
        s = 1
        while s < sl:
            p = p + jnp.where(row >= s, pltpu.roll(p, s, 0), 0.0)
            q = q + jnp.where(row < sl - s, pltpu.roll(q, sl - s, 0), 0.0)
            s *= 2
        pre.append(p)
        suf.append(q)
    per_chunk = GLA_CHUNK // sl
    for c in range(len(pre) // per_chunk):
        for v in range(1, per_chunk):
            i = c * per_chunk + v
            pre[i] = pre[i] + jnp.broadcast_to(pre[i - 1][sl - 1:sl, :], pre[i].shape)
            j = (c + 1) * per_chunk - 1 - v
            suf[j] = suf[j] + jnp.broadcast_to(suf[j + 1][0:1, :], suf[j].shape)
    return jnp.concatenate(pre, axis=0), jnp.concatenate(suf, axis=0)


def _gla_kernel(*refs, n_batch, reverse, finalize):
    refs = list(refs)
    take = lambda: [refs.pop(0) for _ in range(n_batch)]
    q_refs, k_refs, v_refs, g_refs = take(), take(), take(), take()
    ex_ref, mk_ref, bm_ref = refs.pop(0), refs.pop(0), refs.pop(0)
    if finalize:
        of_ref = refs.pop(0)
        og_refs = take()
        gn_ref, e_ref = refs.pop(0), refs.pop(0)
        o_ref, st_ref, ob_ref = refs
    else:
        o_ref, st_ref = refs
        ob_ref = o_ref

    @pl.when(pl.program_id(0) == 0)
    def _():
        st_ref[...] = jnp.zeros(st_ref.shape, F32)

    sides, tots = [], []
    for b in range(n_batch):
        g = g_refs[b][...]
        pre, suf = _chunk_cumsums(g)
        sides.append(suf if reverse else pre)
        tots.append(pre + suf - g)
    worst = jnp.max(jnp.abs(tots[0]))
    for b in range(1, n_batch):
        worst = jnp.maximum(worst, jnp.max(jnp.abs(tots[b])))
    in_range = worst <= GLA_FAST_RANGE

    @pl.when(in_range)
    def _():
        for b in range(n_batch):
            _gla_fast_tile(q_refs[b], k_refs[b], v_refs[b], mk_ref, bm_ref, st_ref.at[b], ob_ref.at[b],
                           sides[b], tots[b], reverse)

    @pl.when(jnp.logical_not(in_range))
    def _():
        for b in range(n_batch):
            _gla_exact_tile(q_refs[b], k_refs[b], v_refs[b], g_refs[b], ex_ref, mk_ref, st_ref.at[b],
                            ob_ref.at[b], reverse)

    if finalize:
        for b in range(n_batch):
            both = ob_ref[b] + of_ref[b]
            y = _group_rms(both, e_ref[...], GLA_DV) * gn_ref[...]
            o_ref[b] = y * _silu(og_refs[b][...])


def _gla_fast_tile(q_ref, k_ref, v_ref, mk_ref, bm_ref, st_ref, ob_ref, b, tot, reverse):
    ch = GLA_CHUNK
    half = 0.5 * tot
    q, k = q_ref[...], k_ref[...]
    q_in = (q * jnp.exp(b - half)).astype(BF16)
    k_in = (k * jnp.exp(half - b)).astype(BF16)
    q_st = (q * jnp.exp(b)).astype(BF16)
    k_st = (k * jnp.exp(tot - b)).astype(BF16)
    dec = jnp.exp(tot)
    mk = mk_ref[...]
    mk16 = mk.astype(BF16)
    bm16 = bm_ref[...]
    irow = lax.broadcasted_iota(jnp.int32, (ch, GLA_HEADS * ch), 0)
    jcol = lax.broadcasted_iota(jnp.int32, (ch, GLA_HEADS * ch), 1) % ch
    seen = (jcol >= irow) if reverse else (jcol <= irow)
    nt = (((1,), (1,)), ((), ()))
    st = st_ref[...]
    n_chunks = TP // ch
    for ci in range(n_chunks):
        c = (n_chunks - 1 - ci) if reverse else ci
        rows = slice(c * ch, (c + 1) * ch)
        kb = jnp.concatenate([k_in[rows]] * GLA_HEADS, axis=0) * mk16
        a = lax.dot_general(q_in[rows], kb, nt, preferred_element_type=F32)
        a = jnp.where(seen, a, 0.0).astype(BF16)
        v = v_ref[rows, :]
        vb = jnp.concatenate([v.astype(BF16)] * GLA_HEADS, axis=0) * bm16
        o = _dot(a, vb) + lax.dot_general(q_st[rows], st.astype(BF16), nt, preferred_element_type=F32)
        kv = lax.dot_general(v.astype(BF16), k_st[rows], (((0,), (0,)), ((), ())),
                             preferred_element_type=F32)
        st = st * dec[c * ch:c * ch + 1, :] + kv * mk
        ob_ref[rows, :] = o
    st_ref[...] = st


def _gla_exact_tile(q_ref, k_ref, v_ref, g_ref, ex_ref, mk_ref, st_ref, ob_ref, reverse):
    sub = GLA_SUB
    n_chunks = TP // sub
    row = lax.broadcasted_iota(jnp.int32, (sub, GLA_KW), 0)
    ex = ex_ref[...]
    mk = mk_ref[...]

    def body(ci, carry):
        c = (n_chunks - 1 - ci) if reverse else ci
        r0 = pl.multiple_of(c * sub, sub)
        q = q_ref[pl.ds(r0, sub), :]
        k = k_ref[pl.ds(r0, sub), :]
        v = v_ref[pl.ds(r0, sub), :]
        b = g_ref[pl.ds(r0, sub), :]
        s = 1
        while s < sub:
            if reverse:
                b = b + jnp.where(row < sub - s, pltpu.roll(b, sub - s, 0), 0.0)
            else:
                b = b + jnp.where(row >= s, pltpu.roll(b, s, 0), 0.0)
            s *= 2
        edge = b[0:1, :] if reverse else b[sub - 1:sub, :]
        st = st_ref[...]
        qd = (q * jnp.exp(b)).astype(BF16)
        o = lax.dot_general(qd, st.astype(BF16), (((1,), (1,)), ((), ())),
                            preferred_element_type=F32)
        parts = []
        for j in range(sub):
            seen = (row <= j) if reverse else (row >= j)
            dec = jnp.exp(jnp.where(seen, b - b[j:j + 1, :], NEG_BIG))
            parts.append((dec * q * k[j:j + 1, :]).astype(BF16))
        r = _dot(jnp.concatenate(parts, axis=0), ex)
        for j in range(sub):
            o = o + r[j * sub:(j + 1) * sub, :] * v[j:j + 1, :]
        kd = (k * jnp.exp(edge - b)).astype(BF16)
        kv = lax.dot_general(v.astype(BF16), kd, (((0,), (0,)), ((), ())),
                             preferred_element_type=F32)
        st_ref[...] = st * jnp.exp(edge) + kv * mk
        ob_ref[pl.ds(r0, sub), :] = o
        return carry

    lax.fori_loop(0, n_chunks, body, 0, unroll=2)


def _gla_step_tile(n_lat, n_ctx, reverse):
    def tile(s):
        is_ctx = s < n_ctx
        sc = jnp.where(is_ctx, s, 0)
        sl = jnp.where(is_ctx, 0, s - n_ctx)
        if reverse:
            sc = n_ctx - 1 - sc
            sl = n_lat - 1 - sl
        return is_ctx, sc, sl
    return tile


def _gla_mix(gq, gk, gv, gf, gb, og, ex, mk, gn, e256, *, n_batch, seq, ctx):
    n_lat, n_ctx = seq // TP, ctx // TP
    const = lambda shape: pl.BlockSpec(shape, lambda s: (0,) * len(shape))

    def run(reverse, finalize, gate, extra):
        tile = _gla_step_tile(n_lat, n_ctx, reverse)

        def rows(w, b):
            def index_map(s):
                is_ctx, sc, sl = tile(s)
                return (jnp.where(is_ctx, n_batch * n_lat + b * n_ctx + sc, b * n_lat + sl), 0)
            return pl.BlockSpec((TP, w), index_map)

        def per_batch(arr, w):
            return [arr] * n_batch, [rows(w, b) for b in range(n_batch)]

        def out_map(s):
            is_ctx, sc, sl = tile(s)
            return (0, jnp.where(is_ctx, n_lat + sc, sl), 0)

        stacked = pl.BlockSpec((n_batch, TP, GLA_W), out_map)
        args, in_specs = [], []
        for arr, w in ((gq, GLA_KW), (gk, GLA_KW), (gv, GLA_W), (gate, GLA_KW)):
            a, sp = per_batch(arr, w)
            args += a
            in_specs += sp
        args += [ex, mk, e256]
        in_specs += [const(ex.shape), const(mk.shape), const(e256.shape)]
        scratch = [pltpu.VMEM((n_batch, GLA_W, GLA_KW), F32)]
        if finalize:
            o_f, og_rows, gn_row, e_blk = extra
            a, sp = per_batch(og_rows, GLA_W)
            args += [o_f] + a + [gn_row, e_blk]
            in_specs += [stacked] + sp + [const((1, GLA_W)), const(e_blk.shape)]
            scratch.append(pltpu.VMEM((n_batch, TP, GLA_W), F32))
        return pl.pallas_call(
            functools.partial(_gla_kernel, n_batch=n_batch, reverse=reverse, finalize=finalize),
            out_shape=jax.ShapeDtypeStruct((n_batch, seq + ctx, GLA_W), F32),
            grid=(n_lat + n_ctx,),
            in_specs=in_specs,
            out_specs=stacked,
            scratch_shapes=scratch,
            compiler_params=_cparams(("arbitrary",)),
            name="gla_bwd" if reverse else "gla_fwd",
        )(*args)

    o_f = run(False, False, gf, None)
    return run(True, True, gb, [o_f, og, gn, e256])


def _att_query_groups(q_t):
    tq = q_t.shape[1]
    zero = jnp.zeros((ATT_HD, tq), BF16)
    groups = []
    for g in range(ATT_KVH):
        halves = []
        for h in (2 * g, 2 * g + 1):
            qh = q_t[h * ATT_HD:(h + 1) * ATT_HD, :]
            halves.append(jnp.concatenate([qh, zero] if g == 0 else [zero, qh], axis=0))
        groups.append(jnp.concatenate(halves, axis=1))
    return groups


def _att_store(o_ref, weighted, denom, tq):
    heads = []
    for g in range(ATT_KVH):
        og = weighted[g] / denom[g]
        heads += [og[:, :tq], og[:, tq:]]
    o_ref[...] = jnp.concatenate(heads, axis=0).T.astype(o_ref.dtype)


def _att_kernel(*refs, n_main, has_ctx, bounded):
    refs = list(refs)
    q_ref, km_ref, vm_ref = refs.pop(0), refs.pop(0), refs.pop(0)
    kc_ref, vc_ref = (refs.pop(0), refs.pop(0)) if has_ctx else (None, None)
    o_ref, s0_ref, s1_ref = refs
    q_t = q_ref[...]
    tq = q_t.shape[1]
    tk = vm_ref.shape[3]
    groups = _att_query_groups(q_t)

    def scores(kblk, buf):
        for g in range(ATT_KVH):
            s = _dot(kblk, groups[g])
            buf[g, 0:kblk.shape[0], :] = jnp.exp2(s).astype(BF16) if bounded else s

    def consume(buf, rows, v_of, carry):
        out = []
        for g in range(ATT_KVH):
            m, l, acc = carry[g]
            s = buf[g, 0:rows, :]
            if bounded:
                pv = _dot(v_of(g), s)
                out.append((m, l + pv[ATT_HD:ATT_HD + 1, :], acc + pv[:ATT_HD, :]))
                continue
            m_new = jnp.maximum(m, jnp.max(s, axis=0, keepdims=True))
            alpha = jnp.exp2(m - m_new)
            p = jnp.exp2(s - m_new).astype(BF16)
            pv = _dot(v_of(g), p)
            acc = alpha * acc + pv[:ATT_HD, :]
            l = alpha * l + pv[ATT_HD:ATT_HD + 1, :]
            out.append((m_new, l, acc))
        return tuple(out)

    def k_main(j):
        return km_ref[pl.ds(pl.multiple_of(j * tk, tk), tk), :]

    def v_main(j):
        return lambda g: vm_ref[j, g]

    def pair(j, carry, issue_next):
        scores(k_main(j + 1), s1_ref)
        carry = consume(s0_ref, tk, v_main(j), carry)
        issue_next(j + 2)
        return consume(s1_ref, tk, v_main(j + 1), carry)

    carry = tuple((jnp.full((1, 2 * tq), -jnp.inf, F32), jnp.zeros((1, 2 * tq), F32),
                   jnp.zeros((ATT_HD, 2 * tq), F32)) for _ in range(ATT_KVH))
    next_main = lambda j: scores(k_main(j), s0_ref)
    if has_ctx:
        after_main = lambda j: scores(kc_ref[...], s0_ref)
    else:
        after_main = lambda j: None

    scores(k_main(0), s0_ref)
    if n_main > 1:
        pairs_per_trip = 4 if n_main % 8 == 0 else (2 if n_main % 4 == 0 else 1)

        def body(i, carry):
            for u in range(pairs_per_trip):
                carry = pair(2 * (pairs_per_trip * i + u), carry, next_main)
            return carry

        n_trips = n_main // (2 * pairs_per_trip)
        carry = lax.fori_loop(0, n_trips - 1, body, carry)
        for u in range(pairs_per_trip):
            last = u == pairs_per_trip - 1
            carry = pair(2 * (pairs_per_trip * (n_trips - 1) + u), carry, after_main if last else next_main)
    else:
        assert not has_ctx
        carry = consume(s0_ref, tk, v_main(0), carry)
    if has_ctx:
        carry = consume(s0_ref, kc_ref.shape[0], lambda g: vc_ref[g], carry)
    _att_store(o_ref, [c[2] for c in carry], [c[1] for c in carry], tq)


def _attention(aq, ak, av, *, n_batch, seq, ctx, ctx_queries, bounded=False):
    lat_tiles = seq // TM
    ctx_tile = n_batch * lat_tiles
    ctx_cols = lambda b, i: (ctx_tile + (b * ctx) // TM, 0, ((b * ctx) % TM) // ctx)
    ctx_vcols = lambda b, i: (ctx_tile + (b * ctx) // TM, 0, 0, ((b * ctx) % TM) // ctx)
    if not ctx_queries:
        sub = TM // TQ
        grid = (n_batch, seq // TQ)
        in_specs = [
            pl.BlockSpec((None, ATT_W, TQ), lambda b, i: (b * lat_tiles + i // sub, 0, i % sub)),
            pl.BlockSpec((seq, ATT_KW), lambda b, i: (b, 0)),
            pl.BlockSpec((lat_tiles, ATT_KVH, ATT_VROWS, TM), lambda b, i: (b, 0, 0, 0)),
            pl.BlockSpec((ctx, ATT_KW), lambda b, i: (n_batch * seq // ctx + b, 0)),
            pl.BlockSpec((None, ATT_KVH, ATT_VROWS, ctx), ctx_vcols),
        ]
        out_spec = pl.BlockSpec((TQ, ATT_W), lambda b, i: (b * (seq // TQ) + i, 0))
        assert lat_tiles % 2 == 0
        out_rows = n_batch * seq
        args = (aq, ak, av, ak, av)
        kern = functools.partial(_att_kernel, n_main=lat_tiles, has_ctx=True, bounded=bounded)
        s_shape = (ATT_KVH, TM, 2 * TQ)
    else:
        grid = (n_batch, 1)
        in_specs = [
            pl.BlockSpec((None, ATT_W, ctx), ctx_cols),
            pl.BlockSpec((ctx, ATT_KW), lambda b, i: (n_batch * seq // ctx + b, 0)),
            pl.BlockSpec((1, ATT_KVH, ATT_VROWS, ctx), ctx_vcols),
        ]
        out_spec = pl.BlockSpec((ctx, ATT_W), lambda b, i: (b, 0))
        kern = functools.partial(_att_kernel, n_main=1, has_ctx=False, bounded=False)
        out_rows = n_batch * ctx
        args = (aq, ak, av)
        s_shape = (ATT_KVH, ctx, 2 * ctx)
    return pl.pallas_call(
        kern,
        out_shape=jax.ShapeDtypeStruct((out_rows, ATT_W), BF16),
        grid=grid,
        in_specs=in_specs,
        out_specs=out_spec,
        scratch_shapes=[pltpu.VMEM(s_shape, BF16 if bounded else F32)] * 2,
        compiler_params=_cparams(("arbitrary", "arbitrary")),
        name="attention_ctx" if ctx_queries else ("attention_bounded" if bounded else "attention"),
    )(*args)


def _fft_a_kernel(x_ref, d_ref, tr_ref, ti_ref, yr_ref, yi_ref):
    rb, w = x_ref.shape[1], x_ref.shape[2]
    x = jnp.concatenate([x_ref[:, r, :] for r in range(rb)], axis=1)
    y = _dot(d_ref[...], x.astype(BF16))
    tr, ti = tr_ref[...], ti_ref[...]
    for r in range(rb):
        yr, yi = y[:FFT_N1, r * w:(r + 1) * w], y[FFT_N1:, r * w:(r + 1) * w]
        c, s = tr[:, r:r + 1], ti[:, r:r + 1]
        yr_ref[:, r, :] = yr * c - yi * s
        yi_ref[:, r, :] = yr * s + yi * c


def _channel_mix(xr, xi, c_ref, wf_ref, norm):
    xc = jnp.concatenate([xr, xi], axis=1).astype(BF16)
    f = _dot(xc, c_ref[...]) * norm
    return _dot(f.astype(BF16), wf_ref[...])


def _fft_c_kernel(yr_ref, yi_ref, m_ref, c_ref, wf_ref, o_ref, *, n2, norm):
    w = yr_ref.shape[1]
    blk = jnp.concatenate(
        [jnp.concatenate([yr_ref[j * n2:(j + 1) * n2, :] for j in range(8)], axis=1),
         jnp.concatenate([yi_ref[j * n2:(j + 1) * n2, :] for j in range(8)], axis=1)], axis=0)
    x = _dot(m_ref[...], blk.astype(BF16))
    xr = jnp.concatenate([x[:n2, j * w:(j + 1) * w] for j in range(8)], axis=0)
    xi = jnp.concatenate([x[n2:, j * w:(j + 1) * w] for j in range(8)], axis=0)
    y = _channel_mix(xr, xi, c_ref, wf_ref, norm)
    for j in range(8):
        o_ref[:, j, :] = y[j * n2:(j + 1) * n2, :]


def _fft_dense_kernel(x_ref, d_ref, c_ref, wf_ref, o_ref, *, n, norm):
    x = _dot(d_ref[...], x_ref[...].astype(BF16))
    o_ref[...] = _channel_mix(x[:n, :], x[n:, :], c_ref, wf_ref, norm)


def _dft_parts(n):
    idx = np.arange(n)
    ang = 2.0 * np.pi * ((idx[:, None] * idx[None, :]) % n) / n
    return np.cos(ang), np.sin(ang)


def _mxu_const(a):
    return jnp.asarray(a, F32).astype(BF16)


def _fnet_consts(seq):
    n1, n2 = FFT_N1, seq // FFT_N1
    c1, s1 = _dft_parts(n1)
    da = np.concatenate([c1, -s1], axis=0)
    k1 = np.arange(n1)[:, None]
    m2 = np.arange(n2)[None, :]
    ang = 2.0 * np.pi * ((k1 * m2) % seq) / seq
    tw_r, tw_i = np.cos(ang), -np.sin(ang)
    c2, s2 = _dft_parts(n2)
    mc = np.block([[c2, s2], [-s2, c2]])
    return da, tw_r, tw_i, mc


def _channel_consts():
    cc, sc = _dft_parts(FNET_HD)
    eye = np.eye(FNET_HEADS)
    return np.concatenate([np.kron(eye, cc), np.kron(eye, sc)], axis=0)


def _fnet_latent(uf3, wf, *, layer, n_batch, seq):
    n1, n2 = FFT_N1, seq // FFT_N1
    da, tw_r, tw_i, mc = _fnet_consts(seq)
    da, mc, cc = _mxu_const(da), _mxu_const(mc), _mxu_const(_channel_consts())
    rb = min(n2, 32)
    blocked = lambda tw: jnp.asarray(tw.reshape(n1, n2 // rb, rb).transpose(1, 0, 2), F32)
    const2 = lambda shape: pl.BlockSpec(shape, lambda b, j: (0,) * len(shape))
    slab = pl.BlockSpec((n1, rb, FNET_W), lambda b, j: (b, j, 0))
    yr, yi = pl.pallas_call(
        _fft_a_kernel,
        out_shape=(jax.ShapeDtypeStruct((n_batch * n1, n2, FNET_W), F32),) * 2,
        grid=(n_batch, n2 // rb),
        in_specs=[
            slab,
            const2(da.shape),
            pl.BlockSpec((None, n1, rb), lambda b, j: (j, 0, 0)),
            pl.BlockSpec((None, n1, rb), lambda b, j: (j, 0, 0)),
        ],
        out_specs=(slab, slab),
        compiler_params=_cparams(("arbitrary", "arbitrary")),
        name="fft_stage_a",
    )(uf3, da, blocked(tw_r), blocked(tw_i))
    yr = yr.reshape(n_batch * n1 * n2, FNET_W)
    yi = yi.reshape(n_batch * n1 * n2, FNET_W)
    norm = 1.0 / math.sqrt(seq * FNET_HD)
    out = pl.pallas_call(
        functools.partial(_fft_c_kernel, n2=n2, norm=norm),
        out_shape=jax.ShapeDtypeStruct((n_batch, n2, n1, FNET_W), F32),
        grid=(n_batch, n1 // 8),
        in_specs=[
            pl.BlockSpec((8 * n2, FNET_W), lambda b, i: (b * (n1 // 8) + i, 0)),
            pl.BlockSpec((8 * n2, FNET_W), lambda b, i: (b * (n1 // 8) + i, 0)),
            const2(mc.shape), const2(cc.shape),
            pl.BlockSpec((None,) + wf.shape[1:], lambda b, i: (layer, 0, 0)),
        ],
        out_specs=pl.BlockSpec((None, n2, 8, FNET_W), lambda b, i: (b, 0, i, 0)),
        compiler_params=_cparams(("arbitrary", "arbitrary")),
        name="fft_stage_c",
    )(yr, yi, mc, cc, wf)
    return out.reshape(n_batch * seq, FNET_W)


def _fnet_context(uf, wf, *, layer, n_batch, ctx, first_block):
    c, s = _dft_parts(ctx)
    dd, cc = _mxu_const(np.concatenate([c, -s], axis=0)), _mxu_const(_channel_consts())
    const = lambda shape: pl.BlockSpec(shape, lambda b: (0,) * len(shape))
    return pl.pallas_call(
        functools.partial(_fft_dense_kernel, n=ctx, norm=1.0 / math.sqrt(ctx * FNET_HD)),
        out_shape=jax.ShapeDtypeStruct((n_batch * ctx, FNET_W), F32),
        grid=(n_batch,),
        in_specs=[pl.BlockSpec((ctx, FNET_W), lambda b: (first_block + b, 0)),
                  const(dd.shape), const(cc.shape),
                  pl.BlockSpec((None,) + wf.shape[1:], lambda b: (layer, 0, 0))],
        out_specs=pl.BlockSpec((ctx, FNET_W), lambda b: (b, 0)),
        compiler_params=_cparams(("arbitrary",)),
        name="fft_context",
    )(uf, dd, cc, wf)


def _rope_tables(seq):
    freqs = ROPE_THETA ** (-jnp.arange(ROPE_FREQS, dtype=F32) / ROPE_FREQS)
    row_ang = jnp.arange(seq // GRID_W, dtype=F32)[:, None] * freqs
    col_ang = jnp.arange(GRID_W, dtype=F32)[:, None] * freqs
    lanes = np.arange(ATT_HD)
    is_row = jnp.asarray(np.tile(lanes < 2 * ROPE_FREQS, ATT_QH)[None, :], F32)
    first_half = jnp.asarray(np.tile((lanes // ROPE_FREQS) % 2 == 0, ATT_QH)[None, :], F32)
    spread = lambda a: jnp.tile(a, (1, ATT_W // ROPE_FREQS))

    def tables(ang, mask, ident_lines):
        cos, sin = spread(jnp.cos(ang)) * mask, spread(jnp.sin(ang)) * mask
        ident = jnp.zeros((ident_lines, ATT_W), F32)
        return (jnp.concatenate([cos, ident + mask], axis=0),
                jnp.concatenate([-sin * first_half, ident], axis=0),
                jnp.concatenate([sin * (1.0 - first_half), ident], axis=0))

    rows = tables(row_ang, is_row, TM // GRID_W)
    cols = tuple(t.reshape(2, GRID_W, ATT_W) for t in tables(col_ang, 1.0 - is_row, GRID_W))
    return rows, cols


def _block_ones(width, group):
    return jnp.asarray(np.kron(np.eye(width // group), np.ones((group, group))), BF16)


def kernel(x, c, ctx, c_ctx, w_mod, b_mod, norm_g, ffn_wg, ffn_wu, ffn_wd, w_in, w_out,
           pool_w, pool_scale, gla_wa, gla_ba, gla_norm, att_qnorm, att_knorm, fnet_w, final_norm):
    n_batch, seq, d = x.shape
    n_ctx = ctx.shape[1]
    assert d == D_MODEL and seq % TM == 0 and (n_batch * n_ctx) % TM == 0 and n_ctx % TP == 0
    assert seq % (8 * FFT_N1) == 0 and n_ctx <= TM and TM % n_ctx == 0 and n_batch + 1 <= 8
    lat_rows = n_batch * seq
    lat_tiles = lat_rows // TM
    all_tiles = lat_tiles + (n_batch * n_ctx) // TM
    tiles_per_batch = seq // TM
    line_map = lambda t: (jnp.where(t < lat_tiles, t % tiles_per_batch, tiles_per_batch), 0)
    col_map = lambda t: (jnp.where(t < lat_tiles, 0, 1), 0, 0)

    cvec = jnp.concatenate([c, c_ctx[None, :], jnp.zeros((8 - n_batch - 1, d), F32)], axis=0)
    mods = _modulation(cvec, w_mod, b_mod).reshape(DEPTH, 8, N_MOD, d)

    rope_lines, rope_cols = _rope_tables(seq)
    e256 = _block_ones(ATT_W, ATT_HD)
    gla_ex = _block_ones(GLA_W, GLA_DV)[::2, :]
    gla_mk = jnp.asarray(np.kron(np.eye(GLA_HEADS), np.ones((GLA_DV, GLA_DK))), F32)

    wg, wu, wd = ffn_wg.astype(BF16), ffn_wu.astype(BF16), ffn_wd.astype(BF16)
    w_rank = jnp.concatenate([w_in[..., 768:800], jnp.zeros((DEPTH, d, 128 - 2 * GLA_RANK), F32)], axis=-1)
    wa_blk = jnp.zeros((DEPTH, 128, 2 * GLA_KW), F32)
    wa_blk = wa_blk.at[:, :GLA_RANK, :GLA_KW].set(gla_wa[:, 0]).at[:, GLA_RANK:2 * GLA_RANK, GLA_KW:].set(gla_wa[:, 1])
    wi = jnp.concatenate([w_in[..., :768], w_in[..., 800:D_IN], _fold_gates(w_rank, wa_blk)], axis=-1).astype(BF16)
    ba_blk = gla_ba.reshape(DEPTH, 1, 2 * GLA_KW)
    qg = jnp.tile(att_qnorm, (1, ATT_QH))[:, None, :]
    kg = jnp.tile(att_knorm, (1, ATT_KVH))[:, None, :]
    pool_blk = jnp.zeros((DEPTH, POOL_W, POOL_W), F32)
    for gi in range(len(POOL_WINDOWS)):
        sl = slice(gi * POOL_GW, (gi + 1) * POOL_GW)
        pool_blk = pool_blk.at[:, sl, sl].set(pool_w[:, gi])
    pool_blk = pool_blk.astype(BF16)
    pool_s = pool_scale[:, None, :]
    wf = fnet_w.astype(BF16)
    wo = w_out.astype(BF16)
    norm4 = norm_g[:, :, None, :]
    n2 = seq // FFT_N1
    inv_lat = _pool_inverse_counts(seq)

    xs = x.reshape(lat_rows, d)
    ctx_rows = ctx.reshape(n_batch * n_ctx, d)
    for i in range(DEPTH):
        ctx_out = i < DEPTH - 1
        mod_map = _mod_spec(i, lat_tiles, tiles_per_batch, n_batch, d)
        gn = jnp.tile(gla_norm[i], GLA_HEADS)[None, :]

        xs = _half_ffn(xs, mods, norm4, wg, wu, wd, layer=i, n_tiles=all_tiles,
                       mod_map=mod_map, ctx_rows=ctx_rows if i == 0 else None)
        (u_pool, gq, gk, gv, og, gf, gb, aq, ak, av, uf) = _in_projection(
            xs, mods, norm4, wi, ba_blk, e256, qg, kg, rope_lines, rope_cols,
            layer=i, n_tiles=all_tiles, mod_map=mod_map, line_map=line_map, col_map=col_map)

        y_pool = _pool_mix(u_pool, inv_lat, pool_blk, pool_s, layer=i, seq=seq, n_seq=n_batch, first_block=0)
        y_gla = _gla_mix(gq, gk, gv, gf, gb, og, gla_ex, gla_mk, gn, e256, n_batch=n_batch, seq=seq, ctx=n_ctx)
        score_bound = (ATT_HD * ATT_HD ** -0.5 * math.log2(math.e) * ATT_BOUND_SLACK
                       * jnp.max(jnp.abs(att_qnorm[i])) * jnp.max(jnp.abs(att_knorm[i])))
        attend = functools.partial(_attention, aq, ak, av, n_batch=n_batch, seq=seq, ctx=n_ctx, ctx_queries=False)
        y_att = lax.cond(score_bound <= ATT_BOUND_MAX, lambda: attend(bounded=True), lambda: attend())
        y_fnet = _fnet_latent(uf.reshape(uf.shape[0] // n2, n2, FNET_W), wf, layer=i, n_batch=n_batch, seq=seq)
        n_tiles, ctx_parts = lat_tiles, None
        if ctx_out:
            first_ctx = lat_rows // n_ctx
            ctx_parts = (
                _pool_mix(u_pool, _pool_inverse_counts(n_ctx), pool_blk, pool_s, layer=i, seq=n_ctx, n_seq=n_batch,
                          first_block=first_ctx),
                _attention(aq, ak, av, n_batch=n_batch, seq=seq, ctx=n_ctx, ctx_queries=True),
                _fnet_context(uf, wf, layer=i, n_batch=n_batch, ctx=n_ctx, first_block=first_ctx),
            )
            n_tiles = all_tiles
        xs = _mix_ffn(xs, mods, y_pool, y_gla, y_att, y_fnet, wo, norm4, wg, wu, wd, final_norm,
                      layer=i, n_tiles=n_tiles, mod_map=mod_map, final=not ctx_out, ctx_parts=ctx_parts)
    return xs.reshape(n_batch, seq, d)
```

```python
import functools
import math

import jax
import jax.numpy as jnp
import numpy as np
from jax import lax
from jax.experimental import pallas as pl
from jax.experimental.pallas import tpu as pltpu

F32 = jnp.float32
BF16 = jnp.bfloat16

D_MODEL = 1024
DEPTH = 2
GRID_W = 64
EPS = 1e-6
N_MOD = 9
D_FF = 2816

POOL_W = 256
POOL_WINDOWS = (2, 4, 8, 16)
POOL_GW = 64
POOL_HALO = 8

GLA_HEADS = 4
GLA_W = 256
GLA_DV = 64
GLA_DK = 32
GLA_RANK = 16
GLA_TAU = 16.0
GLA_KW = GLA_HEADS * GLA_DK
GLA_SUB = 16
GLA_CHUNK = 64
GLA_FAST_RANGE = 150.0

ATT_W = 256
ATT_HD = 64
ATT_QH = 4
ATT_KVH = 2
ATT_KW = ATT_KVH * ATT_HD
ATT_VROWS = ATT_HD + 16
ATT_BOUND_SLACK = 1.02
ATT_BOUND_MAX = 50.0
ROPE_FREQS = 16
ROPE_THETA = 10000.0

FNET_W = 256
FNET_HEADS = 4
FNET_HD = 64
FFT_N1 = 64

D_IN = 1824
D_IN_PAD = 2048

O_POOL, O_GQ, O_GK, O_GV, O_OG, O_AQ, O_AK, O_AV, O_FN, O_Z = 0, 256, 384, 512, 768, 1024, 1280, 1408, 1536, 1792

TM = 512
TQ = 512
TP = 256
MXU_TILE = 256
FF_CHUNKS = ((0, 6 * MXU_TILE), (6 * MXU_TILE, D_FF))
NEG_BIG = -1e30

VMEM_LIMIT = 56 * 1024 * 1024


def _cparams(sem):
    return pltpu.CompilerParams(dimension_semantics=sem, vmem_limit_bytes=VMEM_LIMIT)


def _dot(a, b):
    return jnp.dot(a, b, preferred_element_type=F32)


def _rms(x):
    return x * lax.rsqrt(jnp.mean(x * x, axis=-1, keepdims=True) + EPS)


def _silu(x):
    return x * jax.nn.sigmoid(x)


def _group_rms(x, e, width):
    ss = _dot((x * x).astype(BF16), e)
    return x * lax.rsqrt(ss * (1.0 / width) + EPS)


def _mod_kernel(c_ref, w_ref, b_ref, o_ref):
    s = _silu(c_ref[...]).astype(BF16)
    o_ref[...] = _dot(s, w_ref[...].astype(BF16)) + b_ref[...]


def _modulation(cvec, w_mod, b_mod):
    depth, d, nd = w_mod.shape
    tn = nd // 8
    return pl.pallas_call(
        _mod_kernel,
        out_shape=jax.ShapeDtypeStruct((depth, 8, nd), F32),
        grid=(depth, nd // tn),
        in_specs=[
            pl.BlockSpec((8, d), lambda l, j: (0, 0)),
            pl.BlockSpec((None, d, tn), lambda l, j: (l, 0, j)),
            pl.BlockSpec((None, 1, tn), lambda l, j: (l, 0, j)),
        ],
        out_specs=pl.BlockSpec((None, 8, tn), lambda l, j: (l, 0, j)),
        compiler_params=_cparams(("arbitrary", "arbitrary")),
        name="modulation",
    )(cvec, w_mod, b_mod.reshape(depth, 1, nd))


def _half_ffn_rows(x, m_ref, g_ref, wg_ref, wu_ref, wd_ref, mod_base):
    shift = m_ref[mod_base:mod_base + 1, :]
    scale = m_ref[mod_base + 1:mod_base + 2, :]
    gate = m_ref[mod_base + 2:mod_base + 3, :]
    h = ((_rms(x) * g_ref[...]) * (1.0 + scale) + shift).astype(BF16)
    y = jnp.zeros(x.shape, F32)
    for lo, hi in FF_CHUNKS:
        sl = slice(lo, hi)
        a = _dot(h, wg_ref[:, sl])
        u = _dot(h, wu_ref[:, sl])
        y = y + _dot((_silu(a) * u).astype(BF16), wd_ref[sl, :])
    return x + (0.5 * gate) * y


def _ffn_kernel(*refs, n_lat_tiles):
    if n_lat_tiles is None:
        x_ref, m_ref, g_ref, wg_ref, wu_ref, wd_ref, o_ref = refs
        x = x_ref[...]
    else:
        x_ref, c_ref, m_ref, g_ref, wg_ref, wu_ref, wd_ref, o_ref = refs
        x = jnp.where(pl.program_id(0) < n_lat_tiles, x_ref[...], c_ref[...])
    o_ref[...] = _half_ffn_rows(x, m_ref, g_ref, wg_ref, wu_ref, wd_ref, 0)


def _mix_ffn_kernel(*refs, n_lat_tiles, final):
    if n_lat_tiles is None:
        (x_ref, m_ref, yp_ref, yg_ref, ya_ref, yf_ref, og_ref, gn_ref, e_ref, wo_ref,
         g_ref, wg_ref, wu_ref, wd_ref, fg_ref, o_ref) = refs
        yp, yg, ya, yf = yp_ref[...], yg_ref[...], ya_ref[...], yf_ref[...]
    else:
        (x_ref, m_ref, yp_ref, yg_ref, ya_ref, yf_ref, cp_ref, cg_ref, ca_ref, cf_ref, og_ref, gn_ref, e_ref,
         wo_ref, g_ref, wg_ref, wu_ref, wd_ref, fg_ref, o_ref) = refs
        lat = pl.program_id(0) < n_lat_tiles
        yp = jnp.where(lat, yp_ref[...], cp_ref[...])
        yg = jnp.where(lat, yg_ref[...], cg_ref[...].reshape(yg_ref.shape))
        ya = jnp.where(lat, ya_ref[...], ca_ref[...])
        yf = jnp.where(lat, yf_ref[...], cf_ref[...])
    yg = _group_rms(yg, e_ref[...], GLA_DV) * gn_ref[...] * _silu(og_ref[...])
    y = jnp.concatenate([yp.astype(BF16), yg.astype(BF16), ya.astype(BF16), yf.astype(BF16)], axis=1)
    x = x_ref[...] + m_ref[5:6, :] * _dot(y, wo_ref[...])
    out = _half_ffn_rows(x, m_ref, g_ref, wg_ref, wu_ref, wd_ref, 6)
    if final:
        out = _rms(out) * fg_ref[...]
    o_ref[...] = out


def _mod_spec(layer, n_lat_tiles, tiles_per_batch, n_batch, d):
    def index_map(t):
        return (layer, jnp.where(t < n_lat_tiles, t // tiles_per_batch, n_batch), 0, 0)
    return pl.BlockSpec((None, None, N_MOD, d), index_map)


def _resident(arr, lead):
    block = (None,) * len(lead) + arr.shape[len(lead):]
    index = tuple(lead) + (0,) * (arr.ndim - len(lead))
    return pl.BlockSpec(block, lambda t: index, pipeline_mode=pl.Buffered(1))


def _ffn_weight_specs(g, wg, wu, wd, layer, half):
    d = g.shape[-1]
    lead = (layer, half)
    return [pl.BlockSpec((None, None, 1, d), lambda t: (layer, 2 * half, 0, 0)),
            _resident(wg, lead), _resident(wu, lead), _resident(wd, lead)]


def _half_ffn(x, mods, g, wg, wu, wd, *, layer, n_tiles, mod_map, ctx_rows=None):
    d = x.shape[1]
    n_lat = None if ctx_rows is None else x.shape[0] // TM
    if ctx_rows is None:
        rows, row_specs = [x], [pl.BlockSpec((TM, d), lambda t: (t, 0))]
    else:
        rows = [x, ctx_rows]
        row_specs = [pl.BlockSpec((TM, d), lambda t: (jnp.minimum(t, n_lat - 1), 0)),
                     pl.BlockSpec((TM, d), lambda t: (jnp.maximum(t - n_lat, 0), 0))]
    return pl.pallas_call(
        functools.partial(_ffn_kernel, n_lat_tiles=n_lat),
        out_shape=jax.ShapeDtypeStruct((n_tiles * TM, d), F32),
        grid=(n_tiles,),
        in_specs=row_specs + [mod_map] + _ffn_weight_specs(g, wg, wu, wd, layer, 0),
        out_specs=pl.BlockSpec((TM, d), lambda t: (t, 0)),
        compiler_params=_cparams(("arbitrary",)),
        name="half_ffn",
    )(*rows, mods, g, wg, wu, wd)


def _mix_ffn(x, mods, y_pool, y_gla, y_att, y_fnet, og, gn, e_blk, w_out, g, wg, wu, wd, fg,
             *, layer, n_tiles, mod_map, final, ctx_parts=None):
    d = x.shape[1]
    n_batch, gla_rows = y_gla.shape[0], y_gla.shape[1]
    lat_tiles = y_pool.shape[0] // TM
    tpb = lat_tiles // n_batch
    clamp = lambda t: jnp.minimum(t, lat_tiles - 1)
    lat_part = pl.BlockSpec((TM, 256), lambda t: (clamp(t), 0))
    gla_part = pl.BlockSpec((None, TM, 256), lambda t: (clamp(t) // tpb, clamp(t) % tpb, 0))
    parts, part_specs = [y_pool, y_gla, y_att, y_fnet], [lat_part, gla_part, lat_part, lat_part]
    if ctx_parts is not None:
        ctx_rows = gla_rows - tpb * TM
        assert n_batch * ctx_rows == TM and (tpb * TM) % ctx_rows == 0
        ctx_part = pl.BlockSpec((TM, 256), lambda t: (jnp.maximum(t - lat_tiles, 0), 0))
        gla_ctx = pl.BlockSpec((n_batch, ctx_rows, 256), lambda t: (0, tpb * TM // ctx_rows, 0))
        c_pool, c_att, c_fnet = ctx_parts
        parts += [c_pool, y_gla, c_att, c_fnet]
        part_specs += [ctx_part, gla_ctx, ctx_part, ctx_part]
    return pl.pallas_call(
        functools.partial(_mix_ffn_kernel, n_lat_tiles=None if ctx_parts is None else lat_tiles, final=final),
        out_shape=jax.ShapeDtypeStruct((n_tiles * TM, d), F32),
        grid=(n_tiles,),
        in_specs=[pl.BlockSpec((TM, d), lambda t: (t, 0)), mod_map] + part_specs
                 + [pl.BlockSpec((TM, GLA_W), lambda t: (t, 0)), pl.BlockSpec((1, GLA_W), lambda t: (0, 0)),
                    pl.BlockSpec(e_blk.shape, lambda t: (0, 0))]
                 + [_resident(w_out, (layer,))] + _ffn_weight_specs(g, wg, wu, wd, layer, 1)
                 + [pl.BlockSpec((1, d), lambda t: (0, 0))],
        out_specs=pl.BlockSpec((TM, d), lambda t: (t, 0)),
        compiler_params=_cparams(("arbitrary",)),
        name="mix_ffn",
    )(x, mods, *parts, og, gn, e_blk, w_out, g, wg, wu, wd, fg.reshape(1, d))


def _fold_gate_kernel(wr_ref, wa_ref, o_ref):
    a, b = wr_ref[...], wa_ref[...]
    a_hi, b_hi = a.astype(BF16), b.astype(BF16)
    a_lo, b_lo = (a - a_hi.astype(F32)).astype(BF16), (b - b_hi.astype(F32)).astype(BF16)
    o_ref[...] = _dot(a_hi, b_hi) + (_dot(a_hi, b_lo) + _dot(a_lo, b_hi))


def _fold_gates(w_rank, wa_blk):
    depth, d, kr = w_rank.shape
    return pl.pallas_call(
        _fold_gate_kernel,
        out_shape=jax.ShapeDtypeStruct((depth, d, wa_blk.shape[2]), F32),
        grid=(depth,),
        in_specs=[pl.BlockSpec((None, d, kr), lambda l: (l, 0, 0)),
                  pl.BlockSpec((None,) + wa_blk.shape[1:], lambda l: (l, 0, 0))],
        out_specs=pl.BlockSpec((None, d, wa_blk.shape[2]), lambda l: (l, 0, 0)),
        compiler_params=_cparams(("arbitrary",)),
        name="fold_gates",
    )(w_rank, wa_blk)


def _rope(x, c, sa, sb):
    w = x.shape[1]
    return x * c + pltpu.roll(x, w - ROPE_FREQS, 1) * sa + pltpu.roll(x, ROPE_FREQS, 1) * sb


def _tile_rope_table(lines, col):
    return jnp.concatenate([col + lines[r:r + 1, :] for r in range(lines.shape[0])], axis=0)


def _inproj_kernel(x_ref, m_ref, g_ref, w_ref, ba_ref, e_ref, qg_ref, kg_ref,
                   lc_ref, lsa_ref, lsb_ref, cc_ref, csa_ref, csb_ref,
                   up_ref, gq_ref, gk_ref, gv_ref, og_ref, gf_ref, gb_ref,
                   aq_ref, ak_ref, av_ref, uf_ref):
    shift = m_ref[3:4, :]
    scale = m_ref[4:5, :]
    e = e_ref[...]
    n_parts = 2
    rp = TM // n_parts
    lp = rp // GRID_W
    for part in range(n_parts):
        rows = slice(part * rp, (part + 1) * rp)
        lines = slice(part * lp, (part + 1) * lp)
        h = ((_rms(x_ref[rows, :]) * g_ref[...]) * (1.0 + scale) + shift).astype(BF16)
        p = _dot(h, w_ref[...])

        up_ref[rows, :] = p[:, O_POOL:O_POOL + POOL_W]
        uf_ref[rows, :] = p[:, O_FN:O_FN + FNET_W]

        gq_ref[rows, :] = p[:, O_GQ:O_GQ + GLA_KW] * (GLA_DK ** -0.5)
        gk_ref[rows, :] = p[:, O_GK:O_GK + GLA_KW]
        gv_ref[rows, :] = p[:, O_GV:O_GV + GLA_W]
        og_ref[rows, :] = p[:, O_OG:O_OG + GLA_W]
        z = p[:, O_Z:O_Z + 2 * GLA_KW] + ba_ref[...]
        logsig = jnp.minimum(z, 0.0) - jnp.log(1.0 + jnp.exp(-jnp.abs(z)))
        gdec = logsig * (1.0 / GLA_TAU)
        gf_ref[rows, :] = gdec[:, :GLA_KW]
        gb_ref[rows, :] = gdec[:, GLA_KW:]

        rc = _tile_rope_table(lc_ref[lines, :], cc_ref[...])
        rsa = _tile_rope_table(lsa_ref[lines, :], csa_ref[...])
        rsb = _tile_rope_table(lsb_ref[lines, :], csb_ref[...])
        q = _group_rms(p[:, O_AQ:O_AQ + ATT_W], e, ATT_HD) * qg_ref[...]
        q = _rope(q, rc, rsa, rsb) * (ATT_HD ** -0.5 * math.log2(math.e))
        aq_ref[:, rows] = q.T.astype(BF16)
        k = _group_rms(p[:, O_AK:O_AK + ATT_KW], e[:ATT_KW, :ATT_KW], ATT_HD) * kg_ref[...]
        k = _rope(k, rc[:, :ATT_KW], rsa[:, :ATT_KW], rsb[:, :ATT_KW])
        ak_ref[rows, :] = k.astype(BF16)
        v_t = p[:, O_AV:O_AV + ATT_KW].T
        ones = jnp.ones((ATT_VROWS - ATT_HD, rp), F32)
        for kvh in range(ATT_KVH):
            av_ref[kvh, :, rows] = jnp.concatenate(
                [v_t[kvh * ATT_HD:(kvh + 1) * ATT_HD, :], ones], axis=0).astype(BF16)


def _in_projection(x, mods, g, w_in, ba_blk, e256, qg, kg, rope_lines, rope_cols,
                   *, layer, n_tiles, mod_map, line_map, col_map):
    t_rows, d = x.shape
    row = lambda w: pl.BlockSpec((TM, w), lambda t: (t, 0))
    const = lambda shape: pl.BlockSpec(shape, lambda t: (0,) * len(shape))
    per_layer = lambda arr: pl.BlockSpec((None,) + arr.shape[1:], lambda t: (layer,) + (0,) * (arr.ndim - 1))
    out_shapes = (
        jax.ShapeDtypeStruct((t_rows, POOL_W), F32),
        jax.ShapeDtypeStruct((t_rows, GLA_KW), F32),
        jax.ShapeDtypeStruct((t_rows, GLA_KW), F32),
        jax.ShapeDtypeStruct((t_rows, GLA_W), F32),
        jax.ShapeDtypeStruct((t_rows, GLA_W), F32),
        jax.ShapeDtypeStruct((t_rows, GLA_KW), F32),
        jax.ShapeDtypeStruct((t_rows, GLA_KW), F32),
        jax.ShapeDtypeStruct((n_tiles, ATT_W, TM), BF16),
        jax.ShapeDtypeStruct((t_rows, ATT_KW), BF16),
        jax.ShapeDtypeStruct((n_tiles, ATT_KVH, ATT_VROWS, TM), BF16),
        jax.ShapeDtypeStruct((t_rows, FNET_W), F32),
    )
    out_specs = (
        row(POOL_W), row(GLA_KW), row(GLA_KW), row(GLA_W), row(GLA_W), row(GLA_KW), row(GLA_KW),
        pl.BlockSpec((None, ATT_W, TM), lambda t: (t, 0, 0)),
        row(ATT_KW),
        pl.BlockSpec((None, ATT_KVH, ATT_VROWS, TM), lambda t: (t, 0, 0, 0)),
        row(FNET_W),
    )
    return pl.pallas_call(
        _inproj_kernel,
        out_shape=out_shapes,
        grid=(n_tiles,),
        in_specs=[
            pl.BlockSpec((TM, d), lambda t: (t, 0)),
            mod_map,
            pl.BlockSpec((None, None, 1, d), lambda t: (layer, 1, 0, 0)),
            per_layer(w_in),
            per_layer(ba_blk),
            const(e256.shape),
            per_layer(qg),
            per_layer(kg),
        ] + [pl.BlockSpec((TM // GRID_W, ATT_W), line_map)] * 3
          + [pl.BlockSpec((None, GRID_W, ATT_W), col_map)] * 3,
        out_specs=out_specs,
        compiler_params=_cparams(("arbitrary",)),
        name="in_projection",
    )(x, mods, g, w_in, ba_blk, e256, qg, kg, *rope_lines, *rope_cols)


def _pool_kernel(u_ref, inv_ref, w_ref, s_ref, o_ref, pad_ref, *, seq, chunk):
    halo = POOL_HALO
    pad_ref[0:halo, :] = jnp.zeros((halo, POOL_W), F32)
    pad_ref[halo + seq:halo + seq + halo, :] = jnp.zeros((halo, POOL_W), F32)
    pad_ref[halo:halo + seq, :] = u_ref[...]
    rows = chunk + 2 * halo
    lane = lax.broadcasted_iota(jnp.int32, (chunk, POOL_W), 1)

    def body(ci, carry):
        c0 = pl.multiple_of(ci * chunk, chunk)
        xp = pad_ref[pl.ds(c0, rows), :]
        u = xp[halo:halo + chunk, :]
        acc = xp
        wsum = None
        for gi, w in enumerate(POOL_WINDOWS):
            acc = acc + pltpu.roll(acc, w // 2, 0)
            lead = w // 2 - 1
            win = acc if lead == 0 else pltpu.roll(acc, rows - lead, 0)
            win = win[halo:halo + chunk, :]
            wsum = win if wsum is None else jnp.where(lane >= gi * POOL_GW, win, wsum)
        m = wsum * inv_ref[pl.ds(c0, chunk), :] - u
        y = _dot(m.astype(BF16), w_ref[...]) * s_ref[...]
        o_ref[pl.ds(c0, chunk), :] = y
        return carry

    lax.fori_loop(0, seq // chunk, body, 0)


def _pool_inverse_counts(seq):
    t = jnp.arange(seq, dtype=jnp.int32)
    cols = []
    for w in POOL_WINDOWS:
        cnt = jnp.minimum(t + w // 2, seq) - jnp.maximum(t - w // 2, 0)
        cols.append(jnp.broadcast_to((1.0 / cnt.astype(F32))[:, None], (seq, POOL_GW)))
    return jnp.concatenate(cols, axis=1)


def _pool_mix(u, inv_cnt, w_blk, s_pool, *, layer, seq, n_seq, first_block):
    chunk = min(256, seq)
    return pl.pallas_call(
        functools.partial(_pool_kernel, seq=seq, chunk=chunk),
        out_shape=jax.ShapeDtypeStruct((n_seq * seq, POOL_W), F32),
        grid=(n_seq,),
        in_specs=[
            pl.BlockSpec((seq, POOL_W), lambda b: (first_block + b, 0)),
            pl.BlockSpec((seq, POOL_W), lambda b: (0, 0)),
            pl.BlockSpec((None, POOL_W, POOL_W), lambda b: (layer, 0, 0)),
            pl.BlockSpec((None, 1, POOL_W), lambda b: (layer, 0, 0)),
        ],
        out_specs=pl.BlockSpec((seq, POOL_W), lambda b: (b, 0)),
        scratch_shapes=[pltpu.VMEM((seq + 2 * POOL_HALO, POOL_W), F32)],
        compiler_params=_cparams(("arbitrary",)),
        name="pool_mix",
    )(u, inv_cnt, w_blk, s_pool)


def _chunk_cumsums(g):
    sl = 8
    row = lax.broadcasted_iota(jnp.int32, (sl, g.shape[1]), 0)
    pre, suf = [], []
    for i in range(g.shape[0] // sl):
        p = q = g[i * sl:(i + 1) * sl, :]
        s = 1
        while s < sl:
            p = p + jnp.where(row >= s, pltpu.roll(p, s, 0), 0.0)
            q = q + jnp.where(row < sl - s, pltpu.roll(q, sl - s, 0), 0.0)
            s *= 2
        pre.append(p)
        suf.append(q)
    per_chunk = GLA_CHUNK // sl
    for c in range(len(pre) // per_chunk):
        for v in range(1, per_chunk):
            i = c * per_chunk + v
            pre[i] = pre[i] + jnp.broadcast_to(pre[i - 1][sl - 1:sl, :], pre[i].shape)
            j = (c + 1) * per_chunk - 1 - v
            suf[j] = suf[j] + jnp.broadcast_to(suf[j + 1][0:1, :], suf[j].shape)
    return jnp.concatenate(pre, axis=0), jnp.concatenate(suf, axis=0)


def _gla_kernel(*refs, n_batch, reverse, finalize):
    refs = list(refs)
    take = lambda: [refs.pop(0) for _ in range(n_batch)]
    q_refs, k_refs, v_refs, g_refs = take(), take(), take(), take()
    ex_ref, mk_ref, bm_ref = refs.pop(0), refs.pop(0), refs.pop(0)
    if finalize:
        of_ref = refs.pop(0)
        o_ref, st_ref, ob_ref = refs
    else:
        o_ref, st_ref = refs
        ob_ref = o_ref

    @pl.when(pl.program_id(0) == 0)
    def _():
        st_ref[...] = jnp.zeros(st_ref.shape, F32)

    sides, tots = [], []
    for b in range(n_batch):
        g = g_refs[b][...]
        pre, suf = _chunk_cumsums(g)
        sides.append(suf if reverse else pre)
        tots.append(pre + suf - g)
    worst = jnp.max(jnp.abs(tots[0]))
    for b in range(1, n_batch):
        worst = jnp.maximum(worst, jnp.max(jnp.abs(tots[b])))
    in_range = worst <= GLA_FAST_RANGE

    @pl.when(in_range)
    def _():
        for b in range(n_batch):
            _gla_fast_tile(q_refs[b], k_refs[b], v_refs[b], mk_ref, bm_ref, st_ref.at[b], ob_ref.at[b],
                           sides[b], tots[b], reverse)

    @pl.when(jnp.logical_not(in_range))
    def _():
        for b in range(n_batch):
            _gla_exact_tile(q_refs[b], k_refs[b], v_refs[b], g_refs[b], ex_ref, mk_ref, st_ref.at[b],
                            ob_ref.at[b], reverse)

    if finalize:
        o_ref[...] = ob_ref[...] + of_ref[...]


def _gla_fast_tile(q_ref, k_ref, v_ref, mk_ref, bm_ref, st_ref, ob_ref, b, tot, reverse):
    ch = GLA_CHUNK
    half = 0.5 * tot
    q, k = q_ref[...], k_ref[...]
    q_in = (q * jnp.exp(b - half)).astype(BF16)
    k_in = (k * jnp.exp(half - b)).astype(BF16)
    q_st = (q * jnp.exp(b)).astype(BF16)
    k_st = (k * jnp.exp(tot - b)).astype(BF16)
    dec = jnp.exp(tot)
    mk = mk_ref[...]
    mk16 = mk.astype(BF16)
    bm16 = bm_ref[...]
    irow = lax.broadcasted_iota(jnp.int32, (ch, GLA_HEADS * ch), 0)
    jcol = lax.broadcasted_iota(jnp.int32, (ch, GLA_HEADS * ch), 1) % ch
    seen = (jcol >= irow) if reverse else (jcol <= irow)
    nt = (((1,), (1,)), ((), ()))
    st = st_ref[...]
    n_chunks = TP // ch
    for ci in range(n_chunks):
        c = (n_chunks - 1 - ci) if reverse else ci
        rows = slice(c * ch, (c + 1) * ch)
        kb = jnp.concatenate([k_in[rows]] * GLA_HEADS, axis=0) * mk16
        a = lax.dot_general(q_in[rows], kb, nt, preferred_element_type=F32)
        a = jnp.where(seen, a, 0.0).astype(BF16)
        v = v_ref[rows, :]
        vb = jnp.concatenate([v.astype(BF16)] * GLA_HEADS, axis=0) * bm16
        o = _dot(a, vb) + lax.dot_general(q_st[rows], st.astype(BF16), nt, preferred_element_type=F32)
        kv = lax.dot_general(v.astype(BF16), k_st[rows], (((0,), (0,)), ((), ())),
                             preferred_element_type=F32)
        st = st * dec[c * ch:c * ch + 1, :] + kv * mk
        ob_ref[rows, :] = o
    st_ref[...] = st


def _gla_exact_tile(q_ref, k_ref, v_ref, g_ref, ex_ref, mk_ref, st_ref, ob_ref, reverse):
    sub = GLA_SUB
    n_chunks = TP // sub
    row = lax.broadcasted_iota(jnp.int32, (sub, GLA_KW), 0)
    ex = ex_ref[...]
    mk = mk_ref[...]

    def body(ci, carry):
        c = (n_chunks - 1 - ci) if reverse else ci
        r0 = pl.multiple_of(c * sub, sub)
        q = q_ref[pl.ds(r0, sub), :]
        k = k_ref[pl.ds(r0, sub), :]
        v = v_ref[pl.ds(r0, sub), :]
        b = g_ref[pl.ds(r0, sub), :]
        s = 1
        while s < sub:
            if reverse:
                b = b + jnp.where(row < sub - s, pltpu.roll(b, sub - s, 0), 0.0)
            else:
                b = b + jnp.where(row >= s, pltpu.roll(b, s, 0), 0.0)
            s *= 2
        edge = b[0:1, :] if reverse else b[sub - 1:sub, :]
        st = st_ref[...]
        qd = (q * jnp.exp(b)).astype(BF16)
        o = lax.dot_general(qd, st.astype(BF16), (((1,), (1,)), ((), ())),
                            preferred_element_type=F32)
        parts = []
        for j in range(sub):
            seen = (row <= j) if reverse else (row >= j)
            dec = jnp.exp(jnp.where(seen, b - b[j:j + 1, :], NEG_BIG))
            parts.append((dec * q * k[j:j + 1, :]).astype(BF16))
        r = _dot(jnp.concatenate(parts, axis=0), ex)
        for j in range(sub):
            o = o + r[j * sub:(j + 1) * sub, :] * v[j:j + 1, :]
        kd = (k * jnp.exp(edge - b)).astype(BF16)
        kv = lax.dot_general(v.astype(BF16), kd, (((0,), (0,)), ((), ())),
                             preferred_element_type=F32)
        st_ref[...] = st * jnp.exp(edge) + kv * mk
        ob_ref[pl.ds(r0, sub), :] = o
        return carry

    lax.fori_loop(0, n_chunks, body, 0, unroll=2)


def _gla_step_tile(n_lat, n_ctx, reverse):
    def tile(s):
        is_ctx = s < n_ctx
        sc = jnp.where(is_ctx, s, 0)
        sl = jnp.where(is_ctx, 0, s - n_ctx)
        if reverse:
            sc = n_ctx - 1 - sc
            sl = n_lat - 1 - sl
        return is_ctx, sc, sl
    return tile


def _gla_mix(gq, gk, gv, gf, gb, ex, mk, e256, *, n_batch, seq, ctx):
    n_lat, n_ctx = seq // TP, ctx // TP
    const = lambda shape: pl.BlockSpec(shape, lambda s: (0,) * len(shape))

    def run(reverse, finalize, gate, extra):
        tile = _gla_step_tile(n_lat, n_ctx, reverse)

        def rows(w, b):
            def index_map(s):
                is_ctx, sc, sl = tile(s)
                return (jnp.where(is_ctx, n_batch * n_lat + b * n_ctx + sc, b * n_lat + sl), 0)
            return pl.BlockSpec((TP, w), index_map)

        def per_batch(arr, w):
            return [arr] * n_batch, [rows(w, b) for b in range(n_batch)]

        def out_map(s):
            is_ctx, sc, sl = tile(s)
            return (0, jnp.where(is_ctx, n_lat + sc, sl), 0)

        stacked = pl.BlockSpec((n_batch, TP, GLA_W), out_map)
        args, in_specs = [], []
        for arr, w in ((gq, GLA_KW), (gk, GLA_KW), (gv, GLA_W), (gate, GLA_KW)):
            a, sp = per_batch(arr, w)
            args += a
            in_specs += sp
        args += [ex, mk, e256]
        in_specs += [const(ex.shape), const(mk.shape), const(e256.shape)]
        scratch = [pltpu.VMEM((n_batch, GLA_W, GLA_KW), F32)]
        if finalize:
            args.append(extra)
            in_specs.append(stacked)
            scratch.append(pltpu.VMEM((n_batch, TP, GLA_W), F32))
        return pl.pallas_call(
            functools.partial(_gla_kernel, n_batch=n_batch, reverse=reverse, finalize=finalize),
            out_shape=jax.ShapeDtypeStruct((n_batch, seq + ctx, GLA_W), F32),
            grid=(n_lat + n_ctx,),
            in_specs=in_specs,
            out_specs=stacked,
            scratch_shapes=scratch,
            compiler_params=_cparams(("arbitrary",)),
            name="gla_bwd" if reverse else "gla_fwd",
        )(*args)

    o_f = run(False, False, gf, None)
    return run(True, True, gb, o_f)


def _att_query_groups(q_t):
    tq = q_t.shape[1]
    zero = jnp.zeros((ATT_HD, tq), BF16)
    groups = []
    for g in range(ATT_KVH):
        halves = []
        for h in (2 * g, 2 * g + 1):
            qh = q_t[h * ATT_HD:(h + 1) * ATT_HD, :]
            halves.append(jnp.concatenate([qh, zero] if g == 0 else [zero, qh], axis=0))
        groups.append(jnp.concatenate(halves, axis=1))
    return groups


def _att_store(o_ref, weighted, denom, tq):
    heads = []
    for g in range(ATT_KVH):
        og = weighted[g] / denom[g]
        heads += [og[:, :tq], og[:, tq:]]
    o_ref[...] = jnp.concatenate(heads, axis=0).T.astype(o_ref.dtype)


def _att_kernel(*refs, n_main, has_ctx, bounded):
    refs = list(refs)
    q_ref, km_ref, vm_ref = refs.pop(0), refs.pop(0), refs.pop(0)
    kc_ref, vc_ref = (refs.pop(0), refs.pop(0)) if has_ctx else (None, None)
    o_ref, s0_ref, s1_ref = refs
    q_t = q_ref[...]
    tq = q_t.shape[1]
    tk = vm_ref.shape[3]
    groups = _att_query_groups(q_t)

    def scores(kblk, buf):
        for g in range(ATT_KVH):
            s = _dot(kblk, groups[g])
            buf[g, 0:kblk.shape[0], :] = jnp.exp2(s).astype(BF16) if bounded else s

    def consume(buf, rows, v_of, carry):
        out = []
        for g in range(ATT_KVH):
            m, l, acc = carry[g]
            s = buf[g, 0:rows, :]
            if bounded:
                pv = _dot(v_of(g), s)
                out.append((m, l + pv[ATT_HD:ATT_HD + 1, :], acc + pv[:ATT_HD, :]))
                continue
            m_new = jnp.maximum(m, jnp.max(s, axis=0, keepdims=True))
            alpha = jnp.exp2(m - m_new)
            p = jnp.exp2(s - m_new).astype(BF16)
            pv = _dot(v_of(g), p)
            acc = alpha * acc + pv[:ATT_HD, :]
            l = alpha * l + pv[ATT_HD:ATT_HD + 1, :]
            out.append((m_new, l, acc))
        return tuple(out)

    def k_main(j):
        return km_ref[pl.ds(pl.multiple_of(j * tk, tk), tk), :]

    def v_main(j):
        return lambda g: vm_ref[j, g]

    def pair(j, carry, issue_next):
        scores(k_main(j + 1), s1_ref)
        carry = consume(s0_ref, tk, v_main(j), carry)
        issue_next(j + 2)
        return consume(s1_ref, tk, v_main(j + 1), carry)

    carry = tuple((jnp.full((1, 2 * tq), -jnp.inf, F32), jnp.zeros((1, 2 * tq), F32),
                   jnp.zeros((ATT_HD, 2 * tq), F32)) for _ in range(ATT_KVH))
    next_main = lambda j: scores(k_main(j), s0_ref)
    if has_ctx:
        after_main = lambda j: scores(kc_ref[...], s0_ref)
    else:
        after_main = lambda j: None

    scores(k_main(0), s0_ref)
    if n_main > 1:
        pairs_per_trip = 4 if n_main % 8 == 0 else (2 if n_main % 4 == 0 else 1)

        def body(i, carry):
            for u in range(pairs_per_trip):
                carry = pair(2 * (pairs_per_trip * i + u), carry, next_main)
            return carry

        n_trips = n_main // (2 * pairs_per_trip)
        carry = lax.fori_loop(0, n_trips - 1, body, carry)
        for u in range(pairs_per_trip):
            last = u == pairs_per_trip - 1
            carry = pair(2 * (pairs_per_trip * (n_trips - 1) + u), carry, after_main if last else next_main)
    else:
        assert not has_ctx
        carry = consume(s0_ref, tk, v_main(0), carry)
    if has_ctx:
        carry = consume(s0_ref, kc_ref.shape[0], lambda g: vc_ref[g], carry)
    _att_store(o_ref, [c[2] for c in carry], [c[1] for c in carry], tq)


def _attention(aq, ak, av, *, n_batch, seq, ctx, ctx_queries, bounded=False):
    lat_tiles = seq // TM
    ctx_tile = n_batch * lat_tiles
    ctx_cols = lambda b, i: (ctx_tile + (b * ctx) // TM, 0, ((b * ctx) % TM) // ctx)
    ctx_vcols = lambda b, i: (ctx_tile + (b * ctx) // TM, 0, 0, ((b * ctx) % TM) // ctx)
    if not ctx_queries:
        sub = TM // TQ
        grid = (n_batch, seq // TQ)
        in_specs = [
            pl.BlockSpec((None, ATT_W, TQ), lambda b, i: (b * lat_tiles + i // sub, 0, i % sub)),
            pl.BlockSpec((seq, ATT_KW), lambda b, i: (b, 0)),
            pl.BlockSpec((lat_tiles, ATT_KVH, ATT_VROWS, TM), lambda b, i: (b, 0, 0, 0)),
            pl.BlockSpec((ctx, ATT_KW), lambda b, i: (n_batch * seq // ctx + b, 0)),
            pl.BlockSpec((None, ATT_KVH, ATT_VROWS, ctx), ctx_vcols),
        ]
        out_spec = pl.BlockSpec((TQ, ATT_W), lambda b, i: (b * (seq // TQ) + i, 0))
        assert lat_tiles % 2 == 0
        out_rows = n_batch * seq
        args = (aq, ak, av, ak, av)
        kern = functools.partial(_att_kernel, n_main=lat_tiles, has_ctx=True, bounded=bounded)
        s_shape = (ATT_KVH, TM, 2 * TQ)
    else:
        grid = (n_batch, 1)
        in_specs = [
            pl.BlockSpec((None, ATT_W, ctx), ctx_cols),
            pl.BlockSpec((ctx, ATT_KW), lambda b, i: (n_batch * seq // ctx + b, 0)),
            pl.BlockSpec((1, ATT_KVH, ATT_VROWS, ctx), ctx_vcols),
        ]
        out_spec = pl.BlockSpec((ctx, ATT_W), lambda b, i: (b, 0))
        kern = functools.partial(_att_kernel, n_main=1, has_ctx=False, bounded=False)
        out_rows = n_batch * ctx
        args = (aq, ak, av)
        s_shape = (ATT_KVH, ctx, 2 * ctx)
    return pl.pallas_call(
        kern,
        out_shape=jax.ShapeDtypeStruct((out_rows, ATT_W), BF16),
        grid=grid,
        in_specs=in_specs,
        out_specs=out_spec,
        scratch_shapes=[pltpu.VMEM(s_shape, BF16 if bounded else F32)] * 2,
        compiler_params=_cparams(("arbitrary", "arbitrary")),
        name="attention_ctx" if ctx_queries else ("attention_bounded" if bounded else "attention"),
    )(*args)


def _fft_a_kernel(x_ref, d_ref, tr_ref, ti_ref, yr_ref, yi_ref):
    rb, w = x_ref.shape[1], x_ref.shape[2]
    x = jnp.concatenate([x_ref[:, r, :] for r in range(rb)], axis=1)
    y = _dot(d_ref[...], x.astype(BF16))
    tr, ti = tr_ref[...], ti_ref[...]
    for r in range(rb):
        yr, yi = y[:FFT_N1, r * w:(r + 1) * w], y[FFT_N1:, r * w:(r + 1) * w]
        c, s = tr[:, r:r + 1], ti[:, r:r + 1]
        yr_ref[:, r, :] = yr * c - yi * s
        yi_ref[:, r, :] = yr * s + yi * c


def _channel_mix(xr, xi, c_ref, wf_ref, norm):
    xc = jnp.concatenate([xr, xi], axis=1).astype(BF16)
    f = _dot(xc, c_ref[...]) * norm
    return _dot(f.astype(BF16), wf_ref[...])


def _fft_c_kernel(yr_ref, yi_ref, m_ref, c_ref, wf_ref, o_ref, *, n2, norm):
    w = yr_ref.shape[1]
    blk = jnp.concatenate(
        [jnp.concatenate([yr_ref[j * n2:(j + 1) * n2, :] for j in range(8)], axis=1),
         jnp.concatenate([yi_ref[j * n2:(j + 1) * n2, :] for j in range(8)], axis=1)], axis=0)
    x = _dot(m_ref[...], blk.astype(BF16))
    xr = jnp.concatenate([x[:n2, j * w:(j + 1) * w] for j in range(8)], axis=0)
    xi = jnp.concatenate([x[n2:, j * w:(j + 1) * w] for j in range(8)], axis=0)
    y = _channel_mix(xr, xi, c_ref, wf_ref, norm)
    for j in range(8):
        o_ref[:, j, :] = y[j * n2:(j + 1) * n2, :]


def _fft_dense_kernel(x_ref, d_ref, c_ref, wf_ref, o_ref, *, n, norm):
    x = _dot(d_ref[...], x_ref[...].astype(BF16))
    o_ref[...] = _channel_mix(x[:n, :], x[n:, :], c_ref, wf_ref, norm)


def _dft_parts(n):
    idx = np.arange(n)
    ang = 2.0 * np.pi * ((idx[:, None] * idx[None, :]) % n) / n
    return np.cos(ang), np.sin(ang)


def _mxu_const(a):
    return jnp.asarray(a, F32).astype(BF16)


def _fnet_consts(seq):
    n1, n2 = FFT_N1, seq // FFT_N1
    c1, s1 = _dft_parts(n1)
    da = np.concatenate([c1, -s1], axis=0)
    k1 = np.arange(n1)[:, None]
    m2 = np.arange(n2)[None, :]
    ang = 2.0 * np.pi * ((k1 * m2) % seq) / seq
    tw_r, tw_i = np.cos(ang), -np.sin(ang)
    c2, s2 = _dft_parts(n2)
    mc = np.block([[c2, s2], [-s2, c2]])
    return da, tw_r, tw_i, mc


def _channel_consts():
    cc, sc = _dft_parts(FNET_HD)
    eye = np.eye(FNET_HEADS)
    return np.concatenate([np.kron(eye, cc), np.kron(eye, sc)], axis=0)


def _fnet_latent(uf3, wf, *, layer, n_batch, seq):
    n1, n2 = FFT_N1, seq // FFT_N1
    da, tw_r, tw_i, mc = _fnet_consts(seq)
    da, mc, cc = _mxu_const(da), _mxu_const(mc), _mxu_const(_channel_consts())
    rb = min(n2, 32)
    blocked = lambda tw: jnp.asarray(tw.reshape(n1, n2 // rb, rb).transpose(1, 0, 2), F32)
    const2 = lambda shape: pl.BlockSpec(shape, lambda b, j: (0,) * len(shape))
    slab = pl.BlockSpec((n1, rb, FNET_W), lambda b, j: (b, j, 0))
    yr, yi = pl.pallas_call(
        _fft_a_kernel,
        out_shape=(jax.ShapeDtypeStruct((n_batch * n1, n2, FNET_W), F32),) * 2,
        grid=(n_batch, n2 // rb),
        in_specs=[
            slab,
            const2(da.shape),
            pl.BlockSpec((None, n1, rb), lambda b, j: (j, 0, 0)),
            pl.BlockSpec((None, n1, rb), lambda b, j: (j, 0, 0)),
        ],
        out_specs=(slab, slab),
        compiler_params=_cparams(("arbitrary", "arbitrary")),
        name="fft_stage_a",
    )(uf3, da, blocked(tw_r), blocked(tw_i))
    yr = yr.reshape(n_batch * n1 * n2, FNET_W)
    yi = yi.reshape(n_batch * n1 * n2, FNET_W)
    norm = 1.0 / math.sqrt(seq * FNET_HD)
    out = pl.pallas_call(
        functools.partial(_fft_c_kernel, n2=n2, norm=norm),
        out_shape=jax.ShapeDtypeStruct((n_batch, n2, n1, FNET_W), F32),
        grid=(n_batch, n1 // 8),
        in_specs=[
            pl.BlockSpec((8 * n2, FNET_W), lambda b, i: (b * (n1 // 8) + i, 0)),
            pl.BlockSpec((8 * n2, FNET_W), lambda b, i: (b * (n1 // 8) + i, 0)),
            const2(mc.shape), const2(cc.shape),
            pl.BlockSpec((None,) + wf.shape[1:], lambda b, i: (layer, 0, 0)),
        ],
        out_specs=pl.BlockSpec((None, n2, 8, FNET_W), lambda b, i: (b, 0, i, 0)),
        compiler_params=_cparams(("arbitrary", "arbitrary")),
        name="fft_stage_c",
    )(yr, yi, mc, cc, wf)
    return out.reshape(n_batch * seq, FNET_W)


def _fnet_context(uf, wf, *, layer, n_batch, ctx, first_block):
    c, s = _dft_parts(ctx)
    dd, cc = _mxu_const(np.concatenate([c, -s], axis=0)), _mxu_const(_channel_consts())
    const = lambda shape: pl.BlockSpec(shape, lambda b: (0,) * len(shape))
    return pl.pallas_call(
        functools.partial(_fft_dense_kernel, n=ctx, norm=1.0 / math.sqrt(ctx * FNET_HD)),
        out_shape=jax.ShapeDtypeStruct((n_batch * ctx, FNET_W), F32),
        grid=(n_batch,),
        in_specs=[pl.BlockSpec((ctx, FNET_W), lambda b: (first_block + b, 0)),
                  const(dd.shape), const(cc.shape),
                  pl.BlockSpec((None,) + wf.shape[1:], lambda b: (layer, 0, 0))],
        out_specs=pl.BlockSpec((ctx, FNET_W), lambda b: (b, 0)),
        compiler_params=_cparams(("arbitrary",)),
        name="fft_context",
    )(uf, dd, cc, wf)


def _rope_tables(seq):
    freqs = ROPE_THETA ** (-jnp.arange(ROPE_FREQS, dtype=F32) / ROPE_FREQS)
    row_ang = jnp.arange(seq // GRID_W, dtype=F32)[:, None] * freqs
    col_ang = jnp.arange(GRID_W, dtype=F32)[:, None] * freqs
    lanes = np.arange(ATT_HD)
    is_row = jnp.asarray(np.tile(lanes < 2 * ROPE_FREQS, ATT_QH)[None, :], F32)
    first_half = jnp.asarray(np.tile((lanes // ROPE_FREQS) % 2 == 0, ATT_QH)[None, :], F32)
    spread = lambda a: jnp.tile(a, (1, ATT_W // ROPE_FREQS))

    def tables(ang, mask, ident_lines):
        cos, sin = spread(jnp.cos(ang)) * mask, spread(jnp.sin(ang)) * mask
        ident = jnp.zeros((ident_lines, ATT_W), F32)
        return (jnp.concatenate([cos, ident + mask], axis=0),
                jnp.concatenate([-sin * first_half, ident], axis=0),
                jnp.concatenate([sin * (1.0 - first_half), ident], axis=0))

    rows = tables(row_ang, is_row, TM // GRID_W)
    cols = tuple(t.reshape(2, GRID_W, ATT_W) for t in tables(col_ang, 1.0 - is_row, GRID_W))
    return rows, cols


def _block_ones(width, group):
    return jnp.asarray(np.kron(np.eye(width // group), np.ones((group, group))), BF16)


def kernel(x, c, ctx, c_ctx, w_mod, b_mod, norm_g, ffn_wg, ffn_wu, ffn_wd, w_in, w_out,
           pool_w, pool_scale, gla_wa, gla_ba, gla_norm, att_qnorm, att_knorm, fnet_w, final_norm):
    n_batch, seq, d = x.shape
    n_ctx = ctx.shape[1]
    assert d == D_MODEL and seq % TM == 0 and (n_batch * n_ctx) % TM == 0 and n_ctx % TP == 0
    assert seq % (8 * FFT_N1) == 0 and n_ctx <= TM and TM % n_ctx == 0 and n_batch + 1 <= 8
    lat_rows = n_batch * seq
    lat_tiles = lat_rows // TM
    all_tiles = lat_tiles + (n_batch * n_ctx) // TM
    tiles_per_batch = seq // TM
    line_map = lambda t: (jnp.where(t < lat_tiles, t % tiles_per_batch, tiles_per_batch), 0)
    col_map = lambda t: (jnp.where(t < lat_tiles, 0, 1), 0, 0)

    cvec = jnp.concatenate([c, c_ctx[None, :], jnp.zeros((8 - n_batch - 1, d), F32)], axis=0)
    mods = _modulation(cvec, w_mod, b_mod).reshape(DEPTH, 8, N_MOD, d)

    rope_lines, rope_cols = _rope_tables(seq)
    e256 = _block_ones(ATT_W, ATT_HD)
    gla_ex = _block_ones(GLA_W, GLA_DV)[::2, :]
    gla_mk = jnp.asarray(np.kron(np.eye(GLA_HEADS), np.ones((GLA_DV, GLA_DK))), F32)

    wg, wu, wd = ffn_wg.astype(BF16), ffn_wu.astype(BF16), ffn_wd.astype(BF16)
    w_rank = jnp.concatenate([w_in[..., 768:800], jnp.zeros((DEPTH, d, 128 - 2 * GLA_RANK), F32)], axis=-1)
    wa_blk = jnp.zeros((DEPTH, 128, 2 * GLA_KW), F32)
    wa_blk = wa_blk.at[:, :GLA_RANK, :GLA_KW].set(gla_wa[:, 0]).at[:, GLA_RANK:2 * GLA_RANK, GLA_KW:].set(gla_wa[:, 1])
    wi = jnp.concatenate([w_in[..., :768], w_in[..., 800:D_IN], _fold_gates(w_rank, wa_blk)], axis=-1).astype(BF16)
    ba_blk = gla_ba.reshape(DEPTH, 1, 2 * GLA_KW)
    qg = jnp.tile(att_qnorm, (1, ATT_QH))[:, None, :]
    kg = jnp.tile(att_knorm, (1, ATT_KVH))[:, None, :]
    pool_blk = jnp.zeros((DEPTH, POOL_W, POOL_W), F32)
    for gi in range(len(POOL_WINDOWS)):
        sl = slice(gi * POOL_GW, (gi + 1) * POOL_GW)
        pool_blk = pool_blk.at[:, sl, sl].set(pool_w[:, gi])
    pool_blk = pool_blk.astype(BF16)
    pool_s = pool_scale[:, None, :]
    wf = fnet_w.astype(BF16)
    wo = w_out.astype(BF16)
    norm4 = norm_g[:, :, None, :]
    n2 = seq // FFT_N1
    inv_lat = _pool_inverse_counts(seq)

    xs = x.reshape(lat_rows, d)
    ctx_rows = ctx.reshape(n_batch * n_ctx, d)
    for i in range(DEPTH):
        ctx_out = i < DEPTH - 1
        mod_map = _mod_spec(i, lat_tiles, tiles_per_batch, n_batch, d)
        gn = jnp.tile(gla_norm[i], GLA_HEADS)[None, :]

        xs = _half_ffn(xs, mods, norm4, wg, wu, wd, layer=i, n_tiles=all_tiles,
                       mod_map=mod_map, ctx_rows=ctx_rows if i == 0 else None)
        (u_pool, gq, gk, gv, og, gf, gb, aq, ak, av, uf) = _in_projection(
            xs, mods, norm4, wi, ba_blk, e256, qg, kg, rope_lines, rope_cols,
            layer=i, n_tiles=all_tiles, mod_map=mod_map, line_map=line_map, col_map=col_map)

        y_pool = _pool_mix(u_pool, inv_lat, pool_blk, pool_s, layer=i, seq=seq, n_seq=n_batch, first_block=0)
        y_gla = _gla_mix(gq, gk, gv, gf, gb, gla_ex, gla_mk, e256, n_batch=n_batch, seq=seq, ctx=n_ctx)
        score_bound = (ATT_HD * ATT_HD ** -0.5 * math.log2(math.e) * ATT_BOUND_SLACK
                       * jnp.max(jnp.abs(att_qnorm[i])) * jnp.max(jnp.abs(att_knorm[i])))
        attend = functools.partial(_attention, aq, ak, av, n_batch=n_batch, seq=seq, ctx=n_ctx, ctx_queries=False)
        y_att = lax.cond(score_bound <= ATT_BOUND_MAX, lambda: attend(bounded=True), lambda: attend())
        y_fnet = _fnet_latent(uf.reshape(uf.shape[0] // n2, n2, FNET_W), wf, layer=i, n_batch=n_batch, seq=seq)
        n_tiles, ctx_parts = lat_tiles, None
        if ctx_out:
            first_ctx = lat_rows // n_ctx
            ctx_parts = (
                _pool_mix(u_pool, _pool_inverse_counts(n_ctx), pool_blk, pool_s, layer=i, seq=n_ctx, n_seq=n_batch,
                          first_block=first_ctx),
                _attention(aq, ak, av, n_batch=n_batch, seq=seq, ctx=n_ctx, ctx_queries=True),
                _fnet_context(uf, wf, layer=i, n_batch=n_batch, ctx=n_ctx, first_block=first_ctx),
            )
            n_tiles = all_tiles
        xs = _mix_ffn(xs, mods, y_pool, y_gla, y_att, y_fnet, og, gn, e256, wo, norm4, wg, wu, wd, final_norm,
                      layer=i, n_tiles=n_tiles, mod_map=mod_map, final=not ctx_out, ctx_parts=ctx_parts)
    return xs.reshape(n_batch, seq, d)
```

```python
import functools
import math

import jax
import jax.numpy as jnp
import numpy as np
from jax import lax
from jax.experimental import pallas as pl
from jax.experimental.pallas import tpu as pltpu

F32 = jnp.float32
BF16 = jnp.bfloat16

D_MODEL = 1024
DEPTH = 2
GRID_W = 64
EPS = 1e-6
N_MOD = 9
D_FF = 2816

POOL_W = 256
POOL_WINDOWS = (2, 4, 8, 16)
POOL_GW = 64
POOL_HALO = 8

GLA_HEADS = 4
GLA_W = 256
GLA_DV = 64
GLA_DK = 32
GLA_RANK = 16
GLA_TAU = 16.0
GLA_KW = GLA_HEADS * GLA_DK
GLA_SUB = 16
GLA_CHUNK = 64
GLA_FAST_RANGE = 150.0

ATT_W = 256
ATT_HD = 64
ATT_QH = 4
ATT_KVH = 2
ATT_KW = ATT_KVH * ATT_HD
ATT_VROWS = ATT_HD + 16
ATT_BOUND_SLACK = 1.02
ATT_BOUND_MAX = 50.0
ROPE_FREQS = 16
ROPE_THETA = 10000.0

FNET_W = 256
FNET_HEADS = 4
FNET_HD = 64
FFT_N1 = 64

D_IN = 1824
D_IN_PAD = 2048

O_POOL, O_GQ, O_GK, O_GV, O_OG, O_AQ, O_AK, O_AV, O_FN, O_Z = 0, 256, 384, 512, 768, 1024, 1280, 1408, 1536, 1792

TM = 512
TQ = 512
TP = 256
MXU_TILE = 256
FF_CHUNKS = ((0, 6 * MXU_TILE), (6 * MXU_TILE, D_FF))
NEG_BIG = -1e30

VMEM_LIMIT = 56 * 1024 * 1024


def _cparams(sem):
    return pltpu.CompilerParams(dimension_semantics=sem, vmem_limit_bytes=VMEM_LIMIT)


def _dot(a, b):
    return jnp.dot(a, b, preferred_element_type=F32)


def _rms(x):
    return x * lax.rsqrt(jnp.mean(x * x, axis=-1, keepdims=True) + EPS)


def _silu(x):
    return x * jax.nn.sigmoid(x)


def _group_rms(x, e, width):
    ss = _dot((x * x).astype(BF16), e)
    return x * lax.rsqrt(ss * (1.0 / width) + EPS)


def _mod_kernel(c_ref, w_ref, b_ref, o_ref):
    s = _silu(c_ref[...]).astype(BF16)
    o_ref[...] = _dot(s, w_ref[...].astype(BF16)) + b_ref[...]


def _modulation(cvec, w_mod, b_mod):
    depth, d, nd = w_mod.shape
    tn = nd // 8
    return pl.pallas_call(
        _mod_kernel,
        out_shape=jax.ShapeDtypeStruct((depth, 8, nd), F32),
        grid=(depth, nd // tn),
        in_specs=[
            pl.BlockSpec((8, d), lambda l, j: (0, 0)),
            pl.BlockSpec((None, d, tn), lambda l, j: (l, 0, j)),
            pl.BlockSpec((None, 1, tn), lambda l, j: (l, 0, j)),
        ],
        out_specs=pl.BlockSpec((None, 8, tn), lambda l, j: (l, 0, j)),
        compiler_params=_cparams(("arbitrary", "arbitrary")),
        name="modulation",
    )(cvec, w_mod, b_mod.reshape(depth, 1, nd))


def _half_ffn_rows(x, m_ref, g_ref, wg_ref, wu_ref, wd_ref, mod_base):
    shift = m_ref[mod_base:mod_base + 1, :]
    scale = m_ref[mod_base + 1:mod_base + 2, :]
    gate = m_ref[mod_base + 2:mod_base + 3, :]
    h = ((_rms(x) * g_ref[...]) * (1.0 + scale) + shift).astype(BF16)
    y = jnp.zeros(x.shape, F32)
    for lo, hi in FF_CHUNKS:
        sl = slice(lo, hi)
        a = _dot(h, wg_ref[:, sl])
        u = _dot(h, wu_ref[:, sl])
        y = y + _dot((_silu(a) * u).astype(BF16), wd_ref[sl, :])
    return x + (0.5 * gate) * y


def _ffn_kernel(*refs, n_lat_tiles):
    if n_lat_tiles is None:
        x_ref, m_ref, g_ref, wg_ref, wu_ref, wd_ref, o_ref = refs
        x = x_ref[...]
    else:
        x_ref, c_ref, m_ref, g_ref, wg_ref, wu_ref, wd_ref, o_ref = refs
        x = jnp.where(pl.program_id(0) < n_lat_tiles, x_ref[...], c_ref[...])
    o_ref[...] = _half_ffn_rows(x, m_ref, g_ref, wg_ref, wu_ref, wd_ref, 0)


def _mix_ffn_kernel(*refs, n_lat_tiles, final):
    if n_lat_tiles is None:
        (x_ref, m_ref, yp_ref, yg_ref, ya_ref, yf_ref, og_ref, gn_ref, e_ref, wo_ref,
         g_ref, wg_ref, wu_ref, wd_ref, fg_ref, o_ref) = refs
        yp, yg, ya, yf = yp_ref[...], yg_ref[...], ya_ref[...], yf_ref[...]
    else:
        (x_ref, m_ref, yp_ref, yg_ref, ya_ref, yf_ref, cp_ref, cg_ref, ca_ref, cf_ref, og_ref, gn_ref, e_ref,
         wo_ref, g_ref, wg_ref, wu_ref, wd_ref, fg_ref, o_ref) = refs
        lat = pl.program_id(0) < n_lat_tiles
        yp = jnp.where(lat, yp_ref[...], cp_ref[...])
        yg = jnp.where(lat, yg_ref[...], cg_ref[...].reshape(yg_ref.shape))
        ya = jnp.where(lat, ya_ref[...], ca_ref[...])
        yf = jnp.where(lat, yf_ref[...], cf_ref[...])
    yg = _group_rms(yg, e_ref[...], GLA_DV) * gn_ref[...] * _silu(og_ref[...])
    y = jnp.concatenate([yp.astype(BF16), yg.astype(BF16), ya.astype(BF16), yf.astype(BF16)], axis=1)
    x = x_ref[...] + m_ref[5:6, :] * _dot(y, wo_ref[...])
    out = _half_ffn_rows(x, m_ref, g_ref, wg_ref, wu_ref, wd_ref, 6)
    if final:
        out = _rms(out) * fg_ref[...]
    o_ref[...] = out


def _mod_spec(layer, n_lat_tiles, tiles_per_batch, n_batch, d):
    def index_map(t):
        return (layer, jnp.where(t < n_lat_tiles, t // tiles_per_batch, n_batch), 0, 0)
    return pl.BlockSpec((None, None, N_MOD, d), index_map)


def _resident(arr, lead):
    block = (None,) * len(lead) + arr.shape[len(lead):]
    index = tuple(lead) + (0,) * (arr.ndim - len(lead))
    return pl.BlockSpec(block, lambda t: index, pipeline_mode=pl.Buffered(1))


def _ffn_weight_specs(g, wg, wu, wd, layer, half):
    d = g.shape[-1]
    lead = (layer, half)
    return [pl.BlockSpec((None, None, 1, d), lambda t: (layer, 2 * half, 0, 0)),
            _resident(wg, lead), _resident(wu, lead), _resident(wd, lead)]


def _half_ffn(x, mods, g, wg, wu, wd, *, layer, n_tiles, mod_map, ctx_rows=None):
    d = x.shape[1]
    n_lat = None if ctx_rows is None else x.shape[0] // TM
    if ctx_rows is None:
        rows, row_specs = [x], [pl.BlockSpec((TM, d), lambda t: (t, 0))]
    else:
        rows = [x, ctx_rows]
        row_specs = [pl.BlockSpec((TM, d), lambda t: (jnp.minimum(t, n_lat - 1), 0)),
                     pl.BlockSpec((TM, d), lambda t: (jnp.maximum(t - n_lat, 0), 0))]
    return pl.pallas_call(
        functools.partial(_ffn_kernel, n_lat_tiles=n_lat),
        out_shape=jax.ShapeDtypeStruct((n_tiles * TM, d), F32),
        grid=(n_tiles,),
        in_specs=row_specs + [mod_map] + _ffn_weight_specs(g, wg, wu, wd, layer, 0),
        out_specs=pl.BlockSpec((TM, d), lambda t: (t, 0)),
        compiler_params=_cparams(("arbitrary",)),
        name="half_ffn",
    )(*rows, mods, g, wg, wu, wd)


def _mix_ffn(x, mods, y_pool, y_gla, y_att, y_fnet, og, gn, e_blk, w_out, g, wg, wu, wd, fg,
             *, layer, n_tiles, mod_map, final, ctx_parts=None):
    d = x.shape[1]
    n_batch, gla_rows = y_gla.shape[0], y_gla.shape[1]
    lat_tiles = y_pool.shape[0] // TM
    tpb = lat_tiles // n_batch
    clamp = lambda t: jnp.minimum(t, lat_tiles - 1)
    lat_part = pl.BlockSpec((TM, 256), lambda t: (clamp(t), 0))
    gla_part = pl.BlockSpec((None, TM, 256), lambda t: (clamp(t) // tpb, clamp(t) % tpb, 0))
    parts, part_specs = [y_pool, y_gla, y_att, y_fnet], [lat_part, gla_part, lat_part, lat_part]
    if ctx_parts is not None:
        ctx_rows = gla_rows - tpb * TM
        assert n_batch * ctx_rows == TM and (tpb * TM) % ctx_rows == 0
        ctx_part = pl.BlockSpec((TM, 256), lambda t: (jnp.maximum(t - lat_tiles, 0), 0))
        gla_ctx = pl.BlockSpec((n_batch, ctx_rows, 256), lambda t: (0, tpb * TM // ctx_rows, 0))
        c_pool, c_att, c_fnet = ctx_parts
        parts += [c_pool, y_gla, c_att, c_fnet]
        part_specs += [ctx_part, gla_ctx, ctx_part, ctx_part]
    return pl.pallas_call(
        functools.partial(_mix_ffn_kernel, n_lat_tiles=None if ctx_parts is None else lat_tiles, final=final),
        out_shape=jax.ShapeDtypeStruct((n_tiles * TM, d), F32),
        grid=(n_tiles,),
        in_specs=[pl.BlockSpec((TM, d), lambda t: (t, 0)), mod_map] + part_specs
                 + [pl.BlockSpec((TM, GLA_W), lambda t: (t, 0)), pl.BlockSpec((1, GLA_W), lambda t: (0, 0)),
                    pl.BlockSpec(e_blk.shape, lambda t: (0, 0))]
                 + [_resident(w_out, (layer,))] + _ffn_weight_specs(g, wg, wu, wd, layer, 1)
                 + [pl.BlockSpec((1, d), lambda t: (0, 0))],
        out_specs=pl.BlockSpec((TM, d), lambda t: (t, 0)),
        compiler_params=_cparams(("arbitrary",)),
        name="mix_ffn",
    )(x, mods, *parts, og, gn, e_blk, w_out, g, wg, wu, wd, fg.reshape(1, d))


def _fold_gate_kernel(wr_ref, wa_ref, o_ref):
    a, b = wr_ref[...], wa_ref[...]
    a_hi, b_hi = a.astype(BF16), b.astype(BF16)
    a_lo, b_lo = (a - a_hi.astype(F32)).astype(BF16), (b - b_hi.astype(F32)).astype(BF16)
    o_ref[...] = _dot(a_hi, b_hi) + (_dot(a_hi, b_lo) + _dot(a_lo, b_hi))


def _fold_gates(w_rank, wa_blk):
    depth, d, kr = w_rank.shape
    return pl.pallas_call(
        _fold_gate_kernel,
        out_shape=jax.ShapeDtypeStruct((depth, d, wa_blk.shape[2]), F32),
        grid=(depth,),
        in_specs=[pl.BlockSpec((None, d, kr), lambda l: (l, 0, 0)),
                  pl.BlockSpec((None,) + wa_blk.shape[1:], lambda l: (l, 0, 0))],
        out_specs=pl.BlockSpec((None, d, wa_blk.shape[2]), lambda l: (l, 0, 0)),
        compiler_params=_cparams(("arbitrary",)),
        name="fold_gates",
    )(w_rank, wa_blk)


def _rope(x, c, sa, sb):
    w = x.shape[1]
    return x * c + pltpu.roll(x, w - ROPE_FREQS, 1) * sa + pltpu.roll(x, ROPE_FREQS, 1) * sb


def _tile_rope_table(lines, col):
    return jnp.concatenate([col + lines[r:r + 1, :] for r in range(lines.shape[0])], axis=0)


def _inproj_kernel(x_ref, m_ref, g_ref, w_ref, ba_ref, e_ref, qg_ref, kg_ref,
                   lc_ref, lsa_ref, lsb_ref, cc_ref, csa_ref, csb_ref,
                   up_ref, gq_ref, gk_ref, gv_ref, og_ref, gf_ref, gb_ref,
                   aq_ref, ak_ref, av_ref, uf_ref):
    shift = m_ref[3:4, :]
    scale = m_ref[4:5, :]
    e = e_ref[...]
    n_parts = 2
    rp = TM // n_parts
    lp = rp // GRID_W
    for part in range(n_parts):
        rows = slice(part * rp, (part + 1) * rp)
        lines = slice(part * lp, (part + 1) * lp)
        h = ((_rms(x_ref[rows, :]) * g_ref[...]) * (1.0 + scale) + shift).astype(BF16)
        p = _dot(h, w_ref[...])

        up_ref[rows, :] = p[:, O_POOL:O_POOL + POOL_W]
        uf_ref[rows, :] = p[:, O_FN:O_FN + FNET_W]

        gq_ref[rows, :] = p[:, O_GQ:O_GQ + GLA_KW] * (GLA_DK ** -0.5)
        gk_ref[rows, :] = p[:, O_GK:O_GK + GLA_KW]
        gv_ref[rows, :] = p[:, O_GV:O_GV + GLA_W]
        og_ref[rows, :] = p[:, O_OG:O_OG + GLA_W]
        z = p[:, O_Z:O_Z + 2 * GLA_KW] + ba_ref[...]
        logsig = jnp.minimum(z, 0.0) - jnp.log(1.0 + jnp.exp(-jnp.abs(z)))
        gdec = logsig * (1.0 / GLA_TAU)
        gf_ref[rows, :] = gdec[:, :GLA_KW]
        gb_ref[rows, :] = gdec[:, GLA_KW:]

        rc = _tile_rope_table(lc_ref[lines, :], cc_ref[...])
        rsa = _tile_rope_table(lsa_ref[lines, :], csa_ref[...])
        rsb = _tile_rope_table(lsb_ref[lines, :], csb_ref[...])
        q = _group_rms(p[:, O_AQ:O_AQ + ATT_W], e, ATT_HD) * qg_ref[...]
        q = _rope(q, rc, rsa, rsb) * (ATT_HD ** -0.5 * math.log2(math.e))
        aq_ref[:, rows] = q.T.astype(BF16)
        k = _group_rms(p[:, O_AK:O_AK + ATT_KW], e[:ATT_KW, :ATT_KW], ATT_HD) * kg_ref[...]
        k = _rope(k, rc[:, :ATT_KW], rsa[:, :ATT_KW], rsb[:, :ATT_KW])
        ak_ref[rows, :] = k.astype(BF16)
        v_t = p[:, O_AV:O_AV + ATT_KW].T
        ones = jnp.ones((ATT_VROWS - ATT_HD, rp), F32)
        for kvh in range(ATT_KVH):
            av_ref[kvh, :, rows] = jnp.concatenate(
                [v_t[kvh * ATT_HD:(kvh + 1) * ATT_HD, :], ones], axis=0).astype(BF16)


def _in_projection(x, mods, g, w_in, ba_blk, e256, qg, kg, rope_lines, rope_cols,
                   *, layer, n_tiles, mod_map, line_map, col_map):
    t_rows, d = x.shape
    row = lambda w: pl.BlockSpec((TM, w), lambda t: (t, 0))
    const = lambda shape: pl.BlockSpec(shape, lambda t: (0,) * len(shape))
    per_layer = lambda arr: pl.BlockSpec((None,) + arr.shape[1:], lambda t: (layer,) + (0,) * (arr.ndim - 1))
    out_shapes = (
        jax.ShapeDtypeStruct((t_rows, POOL_W), F32),
        jax.ShapeDtypeStruct((t_rows, GLA_KW), F32),
        jax.ShapeDtypeStruct((t_rows, GLA_KW), F32),
        jax.ShapeDtypeStruct((t_rows, GLA_W), F32),
        jax.ShapeDtypeStruct((t_rows, GLA_W), F32),
        jax.ShapeDtypeStruct((t_rows, GLA_KW), F32),
        jax.ShapeDtypeStruct((t_rows, GLA_KW), F32),
        jax.ShapeDtypeStruct((n_tiles, ATT_W, TM), BF16),
        jax.ShapeDtypeStruct((t_rows, ATT_KW), BF16),
        jax.ShapeDtypeStruct((n_tiles, ATT_KVH, ATT_VROWS, TM), BF16),
        jax.ShapeDtypeStruct((t_rows, FNET_W), F32),
    )
    out_specs = (
        row(POOL_W), row(GLA_KW), row(GLA_KW), row(GLA_W), row(GLA_W), row(GLA_KW), row(GLA_KW),
        pl.BlockSpec((None, ATT_W, TM), lambda t: (t, 0, 0)),
        row(ATT_KW),
        pl.BlockSpec((None, ATT_KVH, ATT_VROWS, TM), lambda t: (t, 0, 0, 0)),
        row(FNET_W),
    )
    return pl.pallas_call(
        _inproj_kernel,
        out_shape=out_shapes,
        grid=(n_tiles,),
        in_specs=[
            pl.BlockSpec((TM, d), lambda t: (t, 0)),
            mod_map,
            pl.BlockSpec((None, None, 1, d), lambda t: (layer, 1, 0, 0)),
            per_layer(w_in),
            per_layer(ba_blk),
            const(e256.shape),
            per_layer(qg),
            per_layer(kg),
        ] + [pl.BlockSpec((TM // GRID_W, ATT_W), line_map)] * 3
          + [pl.BlockSpec((None, GRID_W, ATT_W), col_map)] * 3,
        out_specs=out_specs,
        compiler_params=_cparams(("arbitrary",)),
        name="in_projection",
    )(x, mods, g, w_in, ba_blk, e256, qg, kg, *rope_lines, *rope_cols)


def _pool_kernel(u_ref, inv_ref, w_ref, s_ref, o_ref, pad_ref, *, seq, chunk):
    halo = POOL_HALO
    pad_ref[0:halo, :] = jnp.zeros((halo, POOL_W), F32)
    pad_ref[halo + seq:halo + seq + halo, :] = jnp.zeros((halo, POOL_W), F32)
    pad_ref[halo:halo + seq, :] = u_ref[...]
    rows = chunk + 2 * halo
    lane = lax.broadcasted_iota(jnp.int32, (chunk, POOL_W), 1)

    def body(ci, carry):
        c0 = pl.multiple_of(ci * chunk, chunk)
        xp = pad_ref[pl.ds(c0, rows), :]
        u = xp[halo:halo + chunk, :]
        acc = xp
        wsum = None
        for gi, w in enumerate(POOL_WINDOWS):
            acc = acc + pltpu.roll(acc, w // 2, 0)
            lead = w // 2 - 1
            win = acc if lead == 0 else pltpu.roll(acc, rows - lead, 0)
            win = win[halo:halo + chunk, :]
            wsum = win if wsum is None else jnp.where(lane >= gi * POOL_GW, win, wsum)
        m = wsum * inv_ref[pl.ds(c0, chunk), :] - u
        y = _dot(m.astype(BF16), w_ref[...]) * s_ref[...]
        o_ref[pl.ds(c0, chunk), :] = y
        return carry

    lax.fori_loop(0, seq // chunk, body, 0)


def _pool_inverse_counts(seq):
    t = jnp.arange(seq, dtype=jnp.int32)
    cols = []
    for w in POOL_WINDOWS:
        cnt = jnp.minimum(t + w // 2, seq) - jnp.maximum(t - w // 2, 0)
        cols.append(jnp.broadcast_to((1.0 / cnt.astype(F32))[:, None], (seq, POOL_GW)))
    return jnp.concatenate(cols, axis=1)


def _pool_mix(u, inv_cnt, w_blk, s_pool, *, layer, seq, n_seq, first_block):
    chunk = min(1024, seq)
    return pl.pallas_call(
        functools.partial(_pool_kernel, seq=seq, chunk=chunk),
        out_shape=jax.ShapeDtypeStruct((n_seq * seq, POOL_W), F32),
        grid=(n_seq,),
        in_specs=[
            pl.BlockSpec((seq, POOL_W), lambda b: (first_block + b, 0)),
            pl.BlockSpec((seq, POOL_W), lambda b: (0, 0)),
            pl.BlockSpec((None, POOL_W, POOL_W), lambda b: (layer, 0, 0)),
            pl.BlockSpec((None, 1, POOL_W), lambda b: (layer, 0, 0)),
        ],
        out_specs=pl.BlockSpec((seq, POOL_W), lambda b: (b, 0)),
        scratch_shapes=[pltpu.VMEM((seq + 2 * POOL_HALO, POOL_W), F32)],
        compiler_params=_cparams(("arbitrary",)),
        name="pool_mix",
    )(u, inv_cnt, w_blk, s_pool)


def _chunk_cumsums(g):
    sl = 8
    row = lax.broadcasted_iota(jnp.int32, (sl, g.shape[1]), 0)
    pre, suf = [], []
    for i in range(g.shape[0] // sl):
        p = q = g[i * sl:(i + 1) * sl, :]
        s = 1
        while s < sl:
            p = p + jnp.where(row >= s, pltpu.roll(p, s, 0), 0.0)
            q = q + jnp.where(row < sl - s, pltpu.roll(q, sl - s, 0), 0.0)
            s *= 2
        pre.append(p)
        suf.append(q)
    per_chunk = GLA_CHUNK // sl
    for c in range(len(pre) // per_chunk):
        for v in range(1, per_chunk):
            i = c * per_chunk + v
            pre[i] = pre[i] + jnp.broadcast_to(pre[i - 1][sl - 1:sl, :], pre[i].shape)
            j = (c + 1) * per_chunk - 1 - v
            suf[j] = suf[j] + jnp.broadcast_to(suf[j + 1][0:1, :], suf[j].shape)
    return jnp.concatenate(pre, axis=0), jnp.concatenate(suf, axis=0)


def _gla_kernel(*refs, n_batch, reverse, finalize):
    refs = list(refs)
    take = lambda: [refs.pop(0) for _ in range(n_batch)]
    q_refs, k_refs, v_refs, g_refs = take(), take(), take(), take()
    ex_ref, mk_ref, bm_ref = refs.pop(0), refs.pop(0), refs.pop(0)
    if finalize:
        of_ref = refs.pop(0)
        o_ref, st_ref, ob_ref = refs
    else:
        o_ref, st_ref = refs
        ob_ref = o_ref

    @pl.when(pl.program_id(0) == 0)
    def _():
        st_ref[...] = jnp.zeros(st_ref.shape, F32)

    sides, tots = [], []
    for b in range(n_batch):
        g = g_refs[b][...]
        pre, suf = _chunk_cumsums(g)
        sides.append(suf if reverse else pre)
        tots.append(pre + suf - g)
    worst = jnp.max(jnp.abs(tots[0]))
    for b in range(1, n_batch):
        worst = jnp.maximum(worst, jnp.max(jnp.abs(tots[b])))
    in_range = worst <= GLA_FAST_RANGE

    @pl.when(in_range)
    def _():
        for b in range(n_batch):
            _gla_fast_tile(q_refs[b], k_refs[b], v_refs[b], mk_ref, bm_ref, st_ref.at[b], ob_ref.at[b],
                           sides[b], tots[b], reverse)

    @pl.when(jnp.logical_not(in_range))
    def _():
        for b in range(n_batch):
            _gla_exact_tile(q_refs[b], k_refs[b], v_refs[b], g_refs[b], ex_ref, mk_ref, st_ref.at[b],
                            ob_ref.at[b], reverse)

    if finalize:
        o_ref[...] = ob_ref[...] + of_ref[...]


def _gla_fast_tile(q_ref, k_ref, v_ref, mk_ref, bm_ref, st_ref, ob_ref, b, tot, reverse):
    ch = GLA_CHUNK
    half = 0.5 * tot
    q, k = q_ref[...], k_ref[...]
    q_in = (q * jnp.exp(b - half)).astype(BF16)
    k_in = (k * jnp.exp(half - b)).astype(BF16)
    q_st = (q * jnp.exp(b)).astype(BF16)
    k_st = (k * jnp.exp(tot - b)).astype(BF16)
    dec = jnp.exp(tot)
    mk = mk_ref[...]
    mk16 = mk.astype(BF16)
    bm16 = bm_ref[...]
    irow = lax.broadcasted_iota(jnp.int32, (ch, GLA_HEADS * ch), 0)
    jcol = lax.broadcasted_iota(jnp.int32, (ch, GLA_HEADS * ch), 1) % ch
    seen = (jcol >= irow) if reverse else (jcol <= irow)
    nt = (((1,), (1,)), ((), ()))
    st = st_ref[...]
    n_chunks = TP // ch
    for ci in range(n_chunks):
        c = (n_chunks - 1 - ci) if reverse else ci
        rows = slice(c * ch, (c + 1) * ch)
        kb = jnp.concatenate([k_in[rows]] * GLA_HEADS, axis=0) * mk16
        a = lax.dot_general(q_in[rows], kb, nt, preferred_element_type=F32)
        a = jnp.where(seen, a, 0.0).astype(BF16)
        v = v_ref[rows, :]
        vb = jnp.concatenate([v.astype(BF16)] * GLA_HEADS, axis=0) * bm16
        o = _dot(a, vb) + lax.dot_general(q_st[rows], st.astype(BF16), nt, preferred_element_type=F32)
        kv = lax.dot_general(v.astype(BF16), k_st[rows], (((0,), (0,)), ((), ())),
                             preferred_element_type=F32)
        st = st * dec[c * ch:c * ch + 1, :] + kv * mk
        ob_ref[rows, :] = o
    st_ref[...] = st


def _gla_exact_tile(q_ref, k_ref, v_ref, g_ref, ex_ref, mk_ref, st_ref, ob_ref, reverse):
    sub = GLA_SUB
    n_chunks = TP // sub
    row = lax.broadcasted_iota(jnp.int32, (sub, GLA_KW), 0)
    ex = ex_ref[...]
    mk = mk_ref[...]

    def body(ci, carry):
        c = (n_chunks - 1 - ci) if reverse else ci
        r0 = pl.multiple_of(c * sub, sub)
        q = q_ref[pl.ds(r0, sub), :]
        k = k_ref[pl.ds(r0, sub), :]
        v = v_ref[pl.ds(r0, sub), :]
        b = g_ref[pl.ds(r0, sub), :]
        s = 1
        while s < sub:
            if reverse:
                b = b + jnp.where(row < sub - s, pltpu.roll(b, sub - s, 0), 0.0)
            else:
                b = b + jnp.where(row >= s, pltpu.roll(b, s, 0), 0.0)
            s *= 2
        edge = b[0:1, :] if reverse else b[sub - 1:sub, :]
        st = st_ref[...]
        qd = (q * jnp.exp(b)).astype(BF16)
        o = lax.dot_general(qd, st.astype(BF16), (((1,), (1,)), ((), ())),
                            preferred_element_type=F32)
        parts = []
        for j in range(sub):
            seen = (row <= j) if reverse else (row >= j)
            dec = jnp.exp(jnp.where(seen, b - b[j:j + 1, :], NEG_BIG))
            parts.append((dec * q * k[j:j + 1, :]).astype(BF16))
        r = _dot(jnp.concatenate(parts, axis=0), ex)
        for j in range(sub):
            o = o + r[j * sub:(j + 1) * sub, :] * v[j:j + 1, :]
        kd = (k * jnp.exp(edge - b)).astype(BF16)
        kv = lax.dot_general(v.astype(BF16), kd, (((0,), (0,)), ((), ())),
                             preferred_element_type=F32)
        st_ref[...] = st * jnp.exp(edge) + kv * mk
        ob_ref[pl.ds(r0, sub), :] = o
        return carry

    lax.fori_loop(0, n_chunks, body, 0, unroll=2)


def _gla_step_tile(n_lat, n_ctx, reverse):
    def tile(s):
        is_ctx = s < n_ctx
        sc = jnp.where(is_ctx, s, 0)
        sl = jnp.where(is_ctx, 0, s - n_ctx)
        if reverse:
            sc = n_ctx - 1 - sc
            sl = n_lat - 1 - sl
        return is_ctx, sc, sl
    return tile


def _gla_mix(gq, gk, gv, gf, gb, ex, mk, e256, *, n_batch, seq, ctx):
    n_lat, n_ctx = seq // TP, ctx // TP
    const = lambda shape: pl.BlockSpec(shape, lambda s: (0,) * len(shape))

    def run(reverse, finalize, gate, extra):
        tile = _gla_step_tile(n_lat, n_ctx, reverse)

        def rows(w, b):
            def index_map(s):
                is_ctx, sc, sl = tile(s)
                return (jnp.where(is_ctx, n_batch * n_lat + b * n_ctx + sc, b * n_lat + sl), 0)
            return pl.BlockSpec((TP, w), index_map)

        def per_batch(arr, w):
            return [arr] * n_batch, [rows(w, b) for b in range(n_batch)]

        def out_map(s):
            is_ctx, sc, sl = tile(s)
            return (0, jnp.where(is_ctx, n_lat + sc, sl), 0)

        stacked = pl.BlockSpec((n_batch, TP, GLA_W), out_map)
        args, in_specs = [], []
        for arr, w in ((gq, GLA_KW), (gk, GLA_KW), (gv, GLA_W), (gate, GLA_KW)):
            a, sp = per_batch(arr, w)
            args += a
            in_specs += sp
        args += [ex, mk, e256]
        in_specs += [const(ex.shape), const(mk.shape), const(e256.shape)]
        scratch = [pltpu.VMEM((n_batch, GLA_W, GLA_KW), F32)]
        if finalize:
            args.append(extra)
            in_specs.append(stacked)
            scratch.append(pltpu.VMEM((n_batch, TP, GLA_W), F32))
        return pl.pallas_call(
            functools.partial(_gla_kernel, n_batch=n_batch, reverse=reverse, finalize=finalize),
            out_shape=jax.ShapeDtypeStruct((n_batch, seq + ctx, GLA_W), F32),
            grid=(n_lat + n_ctx,),
            in_specs=in_specs,
            out_specs=stacked,
            scratch_shapes=scratch,
            compiler_params=_cparams(("arbitrary",)),
            name="gla_bwd" if reverse else "gla_fwd",
        )(*args)

    o_f = run(False, False, gf, None)
    return run(True, True, gb, o_f)


def _att_query_groups(q_t):
    tq = q_t.shape[1]
    zero = jnp.zeros((ATT_HD, tq), BF16)
    groups = []
    for g in range(ATT_KVH):
        halves = []
        for h in (2 * g, 2 * g + 1):
            qh = q_t[h * ATT_HD:(h + 1) * ATT_HD, :]
            halves.append(jnp.concatenate([qh, zero] if g == 0 else [zero, qh], axis=0))
        groups.append(jnp.concatenate(halves, axis=1))
    return groups


def _att_store(o_ref, weighted, denom, tq):
    heads = []
    for g in range(ATT_KVH):
        og = weighted[g] / denom[g]
        heads += [og[:, :tq], og[:, tq:]]
    o_ref[...] = jnp.concatenate(heads, axis=0).T.astype(o_ref.dtype)


def _att_kernel(*refs, n_main, has_ctx, bounded):
    refs = list(refs)
    q_ref, km_ref, vm_ref = refs.pop(0), refs.pop(0), refs.pop(0)
    kc_ref, vc_ref = (refs.pop(0), refs.pop(0)) if has_ctx else (None, None)
    o_ref, s0_ref, s1_ref = refs
    q_t = q_ref[...]
    tq = q_t.shape[1]
    tk = vm_ref.shape[3]
    groups = _att_query_groups(q_t)

    def scores(kblk, buf):
        for g in range(ATT_KVH):
            s = _dot(kblk, groups[g])
            buf[g, 0:kblk.shape[0], :] = jnp.exp2(s).astype(BF16) if bounded else s

    def consume(buf, rows, v_of, carry):
        out = []
        for g in range(ATT_KVH):
            m, l, acc = carry[g]
            s = buf[g, 0:rows, :]
            if bounded:
                pv = _dot(v_of(g), s)
                out.append((m, l + pv[ATT_HD:ATT_HD + 1, :], acc + pv[:ATT_HD, :]))
                continue
            m_new = jnp.maximum(m, jnp.max(s, axis=0, keepdims=True))
            alpha = jnp.exp2(m - m_new)
            p = jnp.exp2(s - m_new).astype(BF16)
            pv = _dot(v_of(g), p)
            acc = alpha * acc + pv[:ATT_HD, :]
            l = alpha * l + pv[ATT_HD:ATT_HD + 1, :]
            out.append((m_new, l, acc))
        return tuple(out)

    def k_main(j):
        return km_ref[pl.ds(pl.multiple_of(j * tk, tk), tk), :]

    def v_main(j):
        return lambda g: vm_ref[j, g]

    def pair(j, carry, issue_next):
        scores(k_main(j + 1), s1_ref)
        carry = consume(s0_ref, tk, v_main(j), carry)
        issue_next(j + 2)
        return consume(s1_ref, tk, v_main(j + 1), carry)

    carry = tuple((jnp.full((1, 2 * tq), -jnp.inf, F32), jnp.zeros((1, 2 * tq), F32),
                   jnp.zeros((ATT_HD, 2 * tq), F32)) for _ in range(ATT_KVH))
    next_main = lambda j: scores(k_main(j), s0_ref)
    if has_ctx:
        after_main = lambda j: scores(kc_ref[...], s0_ref)
    else:
        after_main = lambda j: None

    scores(k_main(0), s0_ref)
    if n_main > 1:
        pairs_per_trip = 4 if n_main % 8 == 0 else (2 if n_main % 4 == 0 else 1)

        def body(i, carry):
            for u in range(pairs_per_trip):
                carry = pair(2 * (pairs_per_trip * i + u), carry, next_main)
            return carry

        n_trips = n_main // (2 * pairs_per_trip)
        carry = lax.fori_loop(0, n_trips - 1, body, carry)
        for u in range(pairs_per_trip):
            last = u == pairs_per_trip - 1
            carry = pair(2 * (pairs_per_trip * (n_trips - 1) + u), carry, after_main if last else next_main)
    else:
        assert not has_ctx
        carry = consume(s0_ref, tk, v_main(0), carry)
    if has_ctx:
        carry = consume(s0_ref, kc_ref.shape[0], lambda g: vc_ref[g], carry)
    _att_store(o_ref, [c[2] for c in carry], [c[1] for c in carry], tq)


def _attention(aq, ak, av, *, n_batch, seq, ctx, ctx_queries, bounded=False):
    lat_tiles = seq // TM
    ctx_tile = n_batch * lat_tiles
    ctx_cols = lambda b, i: (ctx_tile + (b * ctx) // TM, 0, ((b * ctx) % TM) // ctx)
    ctx_vcols = lambda b, i: (ctx_tile + (b * ctx) // TM, 0, 0, ((b * ctx) % TM) // ctx)
    if not ctx_queries:
        sub = TM // TQ
        grid = (n_batch, seq // TQ)
        in_specs = [
            pl.BlockSpec((None, ATT_W, TQ), lambda b, i: (b * lat_tiles + i // sub, 0, i % sub)),
            pl.BlockSpec((seq, ATT_KW), lambda b, i: (b, 0)),
            pl.BlockSpec((lat_tiles, ATT_KVH, ATT_VROWS, TM), lambda b, i: (b, 0, 0, 0)),
            pl.BlockSpec((ctx, ATT_KW), lambda b, i: (n_batch * seq // ctx + b, 0)),
            pl.BlockSpec((None, ATT_KVH, ATT_VROWS, ctx), ctx_vcols),
        ]
        out_spec = pl.BlockSpec((TQ, ATT_W), lambda b, i: (b * (seq // TQ) + i, 0))
        assert lat_tiles % 2 == 0
        out_rows = n_batch * seq
        args = (aq, ak, av, ak, av)
        kern = functools.partial(_att_kernel, n_main=lat_tiles, has_ctx=True, bounded=bounded)
        s_shape = (ATT_KVH, TM, 2 * TQ)
    else:
        grid = (n_batch, 1)
        in_specs = [
            pl.BlockSpec((None, ATT_W, ctx), ctx_cols),
            pl.BlockSpec((ctx, ATT_KW), lambda b, i: (n_batch * seq // ctx + b, 0)),
            pl.BlockSpec((1, ATT_KVH, ATT_VROWS, ctx), ctx_vcols),
        ]
        out_spec = pl.BlockSpec((ctx, ATT_W), lambda b, i: (b, 0))
        kern = functools.partial(_att_kernel, n_main=1, has_ctx=False, bounded=False)
        out_rows = n_batch * ctx
        args = (aq, ak, av)
        s_shape = (ATT_KVH, ctx, 2 * ctx)
    return pl.pallas_call(
        kern,
        out_shape=jax.ShapeDtypeStruct((out_rows, ATT_W), BF16),
        grid=grid,
        in_specs=in_specs,
        out_specs=out_spec,
        scratch_shapes=[pltpu.VMEM(s_shape, BF16 if bounded else F32)] * 2,
        compiler_params=_cparams(("arbitrary", "arbitrary")),
        name="attention_ctx" if ctx_queries else ("attention_bounded" if bounded else "attention"),
    )(*args)


def _fft_a_kernel(x_ref, d_ref, tr_ref, ti_ref, yr_ref, yi_ref):
    rb, w = x_ref.shape[1], x_ref.shape[2]
    x = jnp.concatenate([x_ref[:, r, :] for r in range(rb)], axis=1)
    y = _dot(d_ref[...], x.astype(BF16))
    tr, ti = tr_ref[...], ti_ref[...]
    for r in range(rb):
        yr, yi = y[:FFT_N1, r * w:(r + 1) * w], y[FFT_N1:, r * w:(r + 1) * w]
        c, s = tr[:, r:r + 1], ti[:, r:r + 1]
        yr_ref[:, r, :] = yr * c - yi * s
        yi_ref[:, r, :] = yr * s + yi * c


def _channel_mix(xr, xi, c_ref, wf_ref, norm):
    xc = jnp.concatenate([xr, xi], axis=1).astype(BF16)
    f = _dot(xc, c_ref[...]) * norm
    return _dot(f.astype(BF16), wf_ref[...])


def _fft_c_kernel(yr_ref, yi_ref, m_ref, c_ref, wf_ref, o_ref, *, n2, norm):
    w = yr_ref.shape[1]
    blk = jnp.concatenate(
        [jnp.concatenate([yr_ref[j * n2:(j + 1) * n2, :] for j in range(8)], axis=1),
         jnp.concatenate([yi_ref[j * n2:(j + 1) * n2, :] for j in range(8)], axis=1)], axis=0)
    x = _dot(m_ref[...], blk.astype(BF16))
    xr = jnp.concatenate([x[:n2, j * w:(j + 1) * w] for j in range(8)], axis=0)
    xi = jnp.concatenate([x[n2:, j * w:(j + 1) * w] for j in range(8)], axis=0)
    y = _channel_mix(xr, xi, c_ref, wf_ref, norm)
    for j in range(8):
        o_ref[:, j, :] = y[j * n2:(j + 1) * n2, :]


def _fft_dense_kernel(x_ref, d_ref, c_ref, wf_ref, o_ref, *, n, norm):
    x = _dot(d_ref[...], x_ref[...].astype(BF16))
    o_ref[...] = _channel_mix(x[:n, :], x[n:, :], c_ref, wf_ref, norm)


def _dft_parts(n):
    idx = np.arange(n)
    ang = 2.0 * np.pi * ((idx[:, None] * idx[None, :]) % n) / n
    return np.cos(ang), np.sin(ang)


def _mxu_const(a):
    return jnp.asarray(a, F32).astype(BF16)


def _fnet_consts(seq):
    n1, n2 = FFT_N1, seq // FFT_N1
    c1, s1 = _dft_parts(n1)
    da = np.concatenate([c1, -s1], axis=0)
    k1 = np.arange(n1)[:, None]
    m2 = np.arange(n2)[None, :]
    ang = 2.0 * np.pi * ((k1 * m2) % seq) / seq
    tw_r, tw_i = np.cos(ang), -np.sin(ang)
    c2, s2 = _dft_parts(n2)
    mc = np.block([[c2, s2], [-s2, c2]])
    return da, tw_r, tw_i, mc


def _channel_consts():
    cc, sc = _dft_parts(FNET_HD)
    eye = np.eye(FNET_HEADS)
    return np.concatenate([np.kron(eye, cc), np.kron(eye, sc)], axis=0)


def _fnet_latent(uf3, wf, *, layer, n_batch, seq):
    n1, n2 = FFT_N1, seq // FFT_N1
    da, tw_r, tw_i, mc = _fnet_consts(seq)
    da, mc, cc = _mxu_const(da), _mxu_const(mc), _mxu_const(_channel_consts())
    rb = min(n2, 32)
    blocked = lambda tw: jnp.asarray(tw.reshape(n1, n2 // rb, rb).transpose(1, 0, 2), F32)
    const2 = lambda shape: pl.BlockSpec(shape, lambda b, j: (0,) * len(shape))
    slab = pl.BlockSpec((n1, rb, FNET_W), lambda b, j: (b, j, 0))
    yr, yi = pl.pallas_call(
        _fft_a_kernel,
        out_shape=(jax.ShapeDtypeStruct((n_batch * n1, n2, FNET_W), F32),) * 2,
        grid=(n_batch, n2 // rb),
        in_specs=[
            slab,
            const2(da.shape),
            pl.BlockSpec((None, n1, rb), lambda b, j: (j, 0, 0)),
            pl.BlockSpec((None, n1, rb), lambda b, j: (j, 0, 0)),
        ],
        out_specs=(slab, slab),
        compiler_params=_cparams(("arbitrary", "arbitrary")),
        name="fft_stage_a",
    )(uf3, da, blocked(tw_r), blocked(tw_i))
    yr = yr.reshape(n_batch * n1 * n2, FNET_W)
    yi = yi.reshape(n_batch * n1 * n2, FNET_W)
    norm = 1.0 / math.sqrt(seq * FNET_HD)
    out = pl.pallas_call(
        functools.partial(_fft_c_kernel, n2=n2, norm=norm),
        out_shape=jax.ShapeDtypeStruct((n_batch, n2, n1, FNET_W), F32),
        grid=(n_batch, n1 // 8),
        in_specs=[
            pl.BlockSpec((8 * n2, FNET_W), lambda b, i: (b * (n1 // 8) + i, 0)),
            pl.BlockSpec((8 * n2, FNET_W), lambda b, i: (b * (n1 // 8) + i, 0)),
            const2(mc.shape), const2(cc.shape),
            pl.BlockSpec((None,) + wf.shape[1:], lambda b, i: (layer, 0, 0)),
        ],
        out_specs=pl.BlockSpec((None, n2, 8, FNET_W), lambda b, i: (b, 0, i, 0)),
        compiler_params=_cparams(("arbitrary", "arbitrary")),
        name="fft_stage_c",
    )(yr, yi, mc, cc, wf)
    return out.reshape(n_batch * seq, FNET_W)


def _fnet_context(uf, wf, *, layer, n_batch, ctx, first_block):
    c, s = _dft_parts(ctx)
    dd, cc = _mxu_const(np.concatenate([c, -s], axis=0)), _mxu_const(_channel_consts())
    const = lambda shape: pl.BlockSpec(shape, lambda b: (0,) * len(shape))
    return pl.pallas_call(
        functools.partial(_fft_dense_kernel, n=ctx, norm=1.0 / math.sqrt(ctx * FNET_HD)),
        out_shape=jax.ShapeDtypeStruct((n_batch * ctx, FNET_W), F32),
        grid=(n_batch,),
        in_specs=[pl.BlockSpec((ctx, FNET_W), lambda b: (first_block + b, 0)),
                  const(dd.shape), const(cc.shape),
                  pl.BlockSpec((None,) + wf.shape[1:], lambda b: (layer, 0, 0))],
        out_specs=pl.BlockSpec((ctx, FNET_W), lambda b: (b, 0)),
        compiler_params=_cparams(("arbitrary",)),
        name="fft_context",
    )(uf, dd, cc, wf)


def _rope_tables(seq):
    freqs = ROPE_THETA ** (-jnp.arange(ROPE_FREQS, dtype=F32) / ROPE_FREQS)
    row_ang = jnp.arange(seq // GRID_W, dtype=F32)[:, None] * freqs
    col_ang = jnp.arange(GRID_W, dtype=F32)[:, None] * freqs
    lanes = np.arange(ATT_HD)
    is_row = jnp.asarray(np.tile(lanes < 2 * ROPE_FREQS, ATT_QH)[None, :], F32)
    first_half = jnp.asarray(np.tile((lanes // ROPE_FREQS) % 2 == 0, ATT_QH)[None, :], F32)
    spread = lambda a: jnp.tile(a, (1, ATT_W // ROPE_FREQS))

    def tables(ang, mask, ident_lines):
        cos, sin = spread(jnp.cos(ang)) * mask, spread(jnp.sin(ang)) * mask
        ident = jnp.zeros((ident_lines, ATT_W), F32)
        return (jnp.concatenate([cos, ident + mask], axis=0),
                jnp.concatenate([-sin * first_half, ident], axis=0),
                jnp.concatenate([sin * (1.0 - first_half), ident], axis=0))

    rows = tables(row_ang, is_row, TM // GRID_W)
    cols = tuple(t.reshape(2, GRID_W, ATT_W) for t in tables(col_ang, 1.0 - is_row, GRID_W))
    return rows, cols


def _block_ones(width, group):
    return jnp.asarray(np.kron(np.eye(width // group), np.ones((group, group))), BF16)


def kernel(x, c, ctx, c_ctx, w_mod, b_mod, norm_g, ffn_wg, ffn_wu, ffn_wd, w_in, w_out,
           pool_w, pool_scale, gla_wa, gla_ba, gla_norm, att_qnorm, att_knorm, fnet_w, final_norm):
    n_batch, seq, d = x.shape
    n_ctx = ctx.shape[1]
    assert d == D_MODEL and seq % TM == 0 and (n_batch * n_ctx) % TM == 0 and n_ctx % TP == 0
    assert seq % (8 * FFT_N1) == 0 and n_ctx <= TM and TM % n_ctx == 0 and n_batch + 1 <= 8
    lat_rows = n_batch * seq
    lat_tiles = lat_rows // TM
    all_tiles = lat_tiles + (n_batch * n_ctx) // TM
    tiles_per_batch = seq // TM
    line_map = lambda t: (jnp.where(t < lat_tiles, t % tiles_per_batch, tiles_per_batch), 0)
    col_map = lambda t: (jnp.where(t < lat_tiles, 0, 1), 0, 0)

    cvec = jnp.concatenate([c, c_ctx[None, :], jnp.zeros((8 - n_batch - 1, d), F32)], axis=0)
    mods = _modulation(cvec, w_mod, b_mod).reshape(DEPTH, 8, N_MOD, d)

    rope_lines, rope_cols = _rope_tables(seq)
    e256 = _block_ones(ATT_W, ATT_HD)
    gla_ex = _block_ones(GLA_W, GLA_DV)[::2, :]
    gla_mk = jnp.asarray(np.kron(np.eye(GLA_HEADS), np.ones((GLA_DV, GLA_DK))), F32)

    wg, wu, wd = ffn_wg.astype(BF16), ffn_wu.astype(BF16), ffn_wd.astype(BF16)
    w_rank = jnp.concatenate([w_in[..., 768:800], jnp.zeros((DEPTH, d, 128 - 2 * GLA_RANK), F32)], axis=-1)
    wa_blk = jnp.zeros((DEPTH, 128, 2 * GLA_KW), F32)
    wa_blk = wa_blk.at[:, :GLA_RANK, :GLA_KW].set(gla_wa[:, 0]).at[:, GLA_RANK:2 * GLA_RANK, GLA_KW:].set(gla_wa[:, 1])
    wi = jnp.concatenate([w_in[..., :768], w_in[..., 800:D_IN], _fold_gates(w_rank, wa_blk)], axis=-1).astype(BF16)
    ba_blk = gla_ba.reshape(DEPTH, 1, 2 * GLA_KW)
    qg = jnp.tile(att_qnorm, (1, ATT_QH))[:, None, :]
    kg = jnp.tile(att_knorm, (1, ATT_KVH))[:, None, :]
    pool_blk = jnp.zeros((DEPTH, POOL_W, POOL_W), F32)
    for gi in range(len(POOL_WINDOWS)):
        sl = slice(gi * POOL_GW, (gi + 1) * POOL_GW)
        pool_blk = pool_blk.at[:, sl, sl].set(pool_w[:, gi])
    pool_blk = pool_blk.astype(BF16)
    pool_s = pool_scale[:, None, :]
    wf = fnet_w.astype(BF16)
    wo = w_out.astype(BF16)
    norm4 = norm_g[:, :, None, :]
    n2 = seq // FFT_N1
    inv_lat = _pool_inverse_counts(seq)

    xs = x.reshape(lat_rows, d)
    ctx_rows = ctx.reshape(n_batch * n_ctx, d)
    for i in range(DEPTH):
        ctx_out = i < DEPTH - 1
        mod_map = _mod_spec(i, lat_tiles, tiles_per_batch, n_batch, d)
        gn = jnp.tile(gla_norm[i], GLA_HEADS)[None, :]

        xs = _half_ffn(xs, mods, norm4, wg, wu, wd, layer=i, n_tiles=all_tiles,
                       mod_map=mod_map, ctx_rows=ctx_rows if i == 0 else None)
        (u_pool, gq, gk, gv, og, gf, gb, aq, ak, av, uf) = _in_projection(
            xs, mods, norm4, wi, ba_blk, e256, qg, kg, rope_lines, rope_cols,
            layer=i, n_tiles=all_tiles, mod_map=mod_map, line_map=line_map, col_map=col_map)

        y_pool = _pool_mix(u_pool, inv_lat, pool_blk, pool_s, layer=i, seq=seq, n_seq=n_batch, first_block=0)
        y_gla = _gla_mix(gq, gk, gv, gf, gb, gla_ex, gla_mk, e256, n_batch=n_batch, seq=seq, ctx=n_ctx)
        score_bound = (ATT_HD * ATT_HD ** -0.5 * math.log2(math.e) * ATT_BOUND_SLACK
                       * jnp.max(jnp.abs(att_qnorm[i])) * jnp.max(jnp.abs(att_knorm[i])))
        attend = functools.partial(_attention, aq, ak, av, n_batch=n_batch, seq=seq, ctx=n_ctx, ctx_queries=False)
        y_att = lax.cond(score_bound <= ATT_BOUND_MAX, lambda: attend(bounded=True), lambda: attend())
        y_fnet = _fnet_latent(uf.reshape(uf.shape[0] // n2, n2, FNET_W), wf, layer=i, n_batch=n_batch, seq=seq)
        n_tiles, ctx_parts = lat_tiles, None
        if ctx_out:
            first_ctx = lat_rows // n_ctx
            ctx_parts = (
                _pool_mix(u_pool, _pool_inverse_counts(n_ctx), pool_blk, pool_s, layer=i, seq=n_ctx, n_seq=n_batch,
                          first_block=first_ctx),
                _attention(aq, ak, av, n_batch=n_batch, seq=seq, ctx=n_ctx, ctx_queries=True),
                _fnet_context(uf, wf, layer=i, n_batch=n_batch, ctx=n_ctx, first_block=first_ctx),
            )
            n_tiles = all_tiles
        xs = _mix_ffn(xs, mods, y_pool, y_gla, y_att, y_fnet, og, gn, e256, wo, norm4, wg, wu, wd, final_norm,
                      layer=i, n_tiles=n_tiles, mod_map=mod_map, final=not ctx_out, ctx_parts=ctx_parts)
    return xs.reshape(n_batch, seq, d)
```

```python
import functools
import math

import jax
import jax.numpy as jnp
import numpy as np
from jax import lax
from jax.experimental import pallas as pl
from jax.experimental.pallas import tpu as pltpu

F32 = jnp.float32
BF16 = jnp.bfloat16

D_MODEL = 1024
DEPTH = 2
GRID_W = 64
EPS = 1e-6
N_MOD = 9
D_FF = 2816

POOL_W = 256
POOL_WINDOWS = (2, 4, 8, 16)
POOL_GW = 64
POOL_HALO = 8

GLA_HEADS = 4
GLA_W = 256
GLA_DV = 64
GLA_DK = 32
GLA_RANK = 16
GLA_TAU = 16.0
GLA_KW = GLA_HEADS * GLA_DK
GLA_SUB = 16
GLA_CHUNK = 64
GLA_FAST_RANGE = 150.0

ATT_W = 256
ATT_HD = 64
ATT_QH = 4
ATT_KVH = 2
ATT_KW = ATT_KVH * ATT_HD
ATT_VROWS = ATT_HD + 16
ATT_BOUND_SLACK = 1.02
ATT_BOUND_MAX = 50.0
ROPE_FREQS = 16
ROPE_THETA = 10000.0

FNET_W = 256
FNET_HEADS = 4
FNET_HD = 64
FFT_N1 = 64

D_IN = 1824
D_IN_PAD = 2048

O_POOL, O_GQ, O_GK, O_GV, O_OG, O_AQ, O_AK, O_AV, O_FN, O_Z = 0, 256, 384, 512, 768, 1024, 1280, 1408, 1536, 1792

TM = 512
TQ = 512
TP = 256
MXU_TILE = 256
FF_CHUNKS = ((0, 6 * MXU_TILE), (6 * MXU_TILE, D_FF))
NEG_BIG = -1e30

VMEM_LIMIT = 56 * 1024 * 1024


def _cparams(sem):
    return pltpu.CompilerParams(dimension_semantics=sem, vmem_limit_bytes=VMEM_LIMIT)


def _dot(a, b):
    return jnp.dot(a, b, preferred_element_type=F32)


def _rms(x):
    return x * lax.rsqrt(jnp.mean(x * x, axis=-1, keepdims=True) + EPS)


def _silu(x):
    return x * jax.nn.sigmoid(x)


def _group_rms(x, e, width):
    ss = _dot((x * x).astype(BF16), e)
    return x * lax.rsqrt(ss * (1.0 / width) + EPS)


def _mod_kernel(c_ref, w_ref, b_ref, o_ref):
    s = _silu(c_ref[...]).astype(BF16)
    o_ref[...] = _dot(s, w_ref[...].astype(BF16)) + b_ref[...]


def _modulation(cvec, w_mod, b_mod):
    depth, d, nd = w_mod.shape
    tn = nd // 8
    return pl.pallas_call(
        _mod_kernel,
        out_shape=jax.ShapeDtypeStruct((depth, 8, nd), F32),
        grid=(depth, nd // tn),
        in_specs=[
            pl.BlockSpec((8, d), lambda l, j: (0, 0)),
            pl.BlockSpec((None, d, tn), lambda l, j: (l, 0, j)),
            pl.BlockSpec((None, 1, tn), lambda l, j: (l, 0, j)),
        ],
        out_specs=pl.BlockSpec((None, 8, tn), lambda l, j: (l, 0, j)),
        compiler_params=_cparams(("arbitrary", "arbitrary")),
        name="modulation",
    )(cvec, w_mod, b_mod.reshape(depth, 1, nd))


def _half_ffn_rows(x, m_ref, g_ref, wg_ref, wu_ref, wd_ref, mod_base):
    shift = m_ref[mod_base:mod_base + 1, :]
    scale = m_ref[mod_base + 1:mod_base + 2, :]
    gate = m_ref[mod_base + 2:mod_base + 3, :]
    h = ((_rms(x) * g_ref[...]) * (1.0 + scale) + shift).astype(BF16)
    y = jnp.zeros(x.shape, F32)
    for lo, hi in FF_CHUNKS:
        sl = slice(lo, hi)
        a = _dot(h, wg_ref[:, sl])
        u = _dot(h, wu_ref[:, sl])
        y = y + _dot((_silu(a) * u).astype(BF16), wd_ref[sl, :])
    return x + (0.5 * gate) * y


def _ffn_kernel(*refs, n_lat_tiles):
    if n_lat_tiles is None:
        x_ref, m_ref, g_ref, wg_ref, wu_ref, wd_ref, o_ref = refs
        x = x_ref[...]
    else:
        x_ref, c_ref, m_ref, g_ref, wg_ref, wu_ref, wd_ref, o_ref = refs
        x = jnp.where(pl.program_id(0) < n_lat_tiles, x_ref[...], c_ref[...])
    o_ref[...] = _half_ffn_rows(x, m_ref, g_ref, wg_ref, wu_ref, wd_ref, 0)


def _mix_ffn_kernel(*refs, n_lat_tiles, final):
    if n_lat_tiles is None:
        (x_ref, m_ref, yp_ref, yg_ref, ya_ref, yf_ref, og_ref, gn_ref, e_ref, wo_ref,
         g_ref, wg_ref, wu_ref, wd_ref, fg_ref, o_ref) = refs
        yp, yg, ya, yf = yp_ref[...], yg_ref[...], ya_ref[...], yf_ref[...]
    else:
        (x_ref, m_ref, yp_ref, yg_ref, ya_ref, yf_ref, cp_ref, cg_ref, ca_ref, cf_ref, og_ref, gn_ref, e_ref,
         wo_ref, g_ref, wg_ref, wu_ref, wd_ref, fg_ref, o_ref) = refs
        lat = pl.program_id(0) < n_lat_tiles
        yp = jnp.where(lat, yp_ref[...], cp_ref[...])
        yg = jnp.where(lat, yg_ref[...], cg_ref[...].reshape(yg_ref.shape))
        ya = jnp.where(lat, ya_ref[...], ca_ref[...])
        yf = jnp.where(lat, yf_ref[...], cf_ref[...])
    yg = _group_rms(yg, e_ref[...], GLA_DV) * gn_ref[...] * _silu(og_ref[...])
    y = jnp.concatenate([yp.astype(BF16), yg.astype(BF16), ya.astype(BF16), yf.astype(BF16)], axis=1)
    x = x_ref[...] + m_ref[5:6, :] * _dot(y, wo_ref[...])
    out = _half_ffn_rows(x, m_ref, g_ref, wg_ref, wu_ref, wd_ref, 6)
    if final:
        out = _rms(out) * fg_ref[...]
    o_ref[...] = out


def _mod_spec(layer, n_lat_tiles, tiles_per_batch, n_batch, d):
    def index_map(t):
        return (layer, jnp.where(t < n_lat_tiles, t // tiles_per_batch, n_batch), 0, 0)
    return pl.BlockSpec((None, None, N_MOD, d), index_map)


def _resident(arr, lead):
    block = (None,) * len(lead) + arr.shape[len(lead):]
    index = tuple(lead) + (0,) * (arr.ndim - len(lead))
    return pl.BlockSpec(block, lambda t: index, pipeline_mode=pl.Buffered(1))


def _ffn_weight_specs(g, wg, wu, wd, layer, half):
    d = g.shape[-1]
    lead = (layer, half)
    return [pl.BlockSpec((None, None, 1, d), lambda t: (layer, 2 * half, 0, 0)),
            _resident(wg, lead), _resident(wu, lead), _resident(wd, lead)]


def _half_ffn(x, mods, g, wg, wu, wd, *, layer, n_tiles, mod_map, ctx_rows=None):
    d = x.shape[1]
    n_lat = None if ctx_rows is None else x.shape[0] // TM
    if ctx_rows is None:
        rows, row_specs = [x], [pl.BlockSpec((TM, d), lambda t: (t, 0))]
    else:
        rows = [x, ctx_rows]
        row_specs = [pl.BlockSpec((TM, d), lambda t: (jnp.minimum(t, n_lat - 1), 0)),
                     pl.BlockSpec((TM, d), lambda t: (jnp.maximum(t - n_lat, 0), 0))]
    return pl.pallas_call(
        functools.partial(_ffn_kernel, n_lat_tiles=n_lat),
        out_shape=jax.ShapeDtypeStruct((n_tiles * TM, d), F32),
        grid=(n_tiles,),
        in_specs=row_specs + [mod_map] + _ffn_weight_specs(g, wg, wu, wd, layer, 0),
        out_specs=pl.BlockSpec((TM, d), lambda t: (t, 0)),
        compiler_params=_cparams(("arbitrary",)),
        name="half_ffn",
    )(*rows, mods, g, wg, wu, wd)


def _mix_ffn(x, mods, y_pool, y_gla, y_att, y_fnet, og, gn, e_blk, w_out, g, wg, wu, wd, fg,
             *, layer, n_tiles, mod_map, final, ctx_parts=None):
    d = x.shape[1]
    n_batch, gla_rows = y_gla.shape[0], y_gla.shape[1]
    lat_tiles = y_pool.shape[0] // TM
    tpb = lat_tiles // n_batch
    clamp = lambda t: jnp.minimum(t, lat_tiles - 1)
    lat_part = pl.BlockSpec((TM, 256), lambda t: (clamp(t), 0))
    gla_part = pl.BlockSpec((None, TM, 256), lambda t: (clamp(t) // tpb, clamp(t) % tpb, 0))
    parts, part_specs = [y_pool, y_gla, y_att, y_fnet], [lat_part, gla_part, lat_part, lat_part]
    if ctx_parts is not None:
        ctx_rows = gla_rows - tpb * TM
        assert n_batch * ctx_rows == TM and (tpb * TM) % ctx_rows == 0
        ctx_part = pl.BlockSpec((TM, 256), lambda t: (jnp.maximum(t - lat_tiles, 0), 0))
        gla_ctx = pl.BlockSpec((n_batch, ctx_rows, 256), lambda t: (0, tpb * TM // ctx_rows, 0))
        c_pool, c_att, c_fnet = ctx_parts
        parts += [c_pool, y_gla, c_att, c_fnet]
        part_specs += [ctx_part, gla_ctx, ctx_part, ctx_part]
    return pl.pallas_call(
        functools.partial(_mix_ffn_kernel, n_lat_tiles=None if ctx_parts is None else lat_tiles, final=final),
        out_shape=jax.ShapeDtypeStruct((n_tiles * TM, d), F32),
        grid=(n_tiles,),
        in_specs=[pl.BlockSpec((TM, d), lambda t: (t, 0)), mod_map] + part_specs
                 + [pl.BlockSpec((TM, GLA_W), lambda t: (t, 0)), pl.BlockSpec((1, GLA_W), lambda t: (0, 0)),
                    pl.BlockSpec(e_blk.shape, lambda t: (0, 0))]
                 + [_resident(w_out, (layer,))] + _ffn_weight_specs(g, wg, wu, wd, layer, 1)
                 + [pl.BlockSpec((1, d), lambda t: (0, 0))],
        out_specs=pl.BlockSpec((TM, d), lambda t: (t, 0)),
        compiler_params=_cparams(("arbitrary",)),
        name="mix_ffn",
    )(x, mods, *parts, og, gn, e_blk, w_out, g, wg, wu, wd, fg.reshape(1, d))


def _fold_gate_kernel(wr_ref, wa_ref, o_ref):
    a, b = wr_ref[...], wa_ref[...]
    a_hi, b_hi = a.astype(BF16), b.astype(BF16)
    a_lo, b_lo = (a - a_hi.astype(F32)).astype(BF16), (b - b_hi.astype(F32)).astype(BF16)
    o_ref[...] = _dot(a_hi, b_hi) + (_dot(a_hi, b_lo) + _dot(a_lo, b_hi))


def _fold_gates(w_rank, wa_blk):
    depth, d, kr = w_rank.shape
    return pl.pallas_call(
        _fold_gate_kernel,
        out_shape=jax.ShapeDtypeStruct((depth, d, wa_blk.shape[2]), F32),
        grid=(depth,),
        in_specs=[pl.BlockSpec((None, d, kr), lambda l: (l, 0, 0)),
                  pl.BlockSpec((None,) + wa_blk.shape[1:], lambda l: (l, 0, 0))],
        out_specs=pl.BlockSpec((None, d, wa_blk.shape[2]), lambda l: (l, 0, 0)),
        compiler_params=_cparams(("arbitrary",)),
        name="fold_gates",
    )(w_rank, wa_blk)


def _rope(x, c, sa, sb):
    w = x.shape[1]
    return x * c + pltpu.roll(x, w - ROPE_FREQS, 1) * sa + pltpu.roll(x, ROPE_FREQS, 1) * sb


def _tile_rope_table(lines, col):
    return jnp.concatenate([col + lines[r:r + 1, :] for r in range(lines.shape[0])], axis=0)


def _inproj_kernel(x_ref, m_ref, g_ref, w_ref, ba_ref, e_ref, qg_ref, kg_ref,
                   lc_ref, lsa_ref, lsb_ref, cc_ref, csa_ref, csb_ref,
                   up_ref, gq_ref, gk_ref, gv_ref, og_ref, gf_ref, gb_ref,
                   aq_ref, ak_ref, av_ref, uf_ref):
    shift = m_ref[3:4, :]
    scale = m_ref[4:5, :]
    e = e_ref[...]
    n_parts = 2
    rp = TM // n_parts
    lp = rp // GRID_W
    for part in range(n_parts):
        rows = slice(part * rp, (part + 1) * rp)
        lines = slice(part * lp, (part + 1) * lp)
        h = ((_rms(x_ref[rows, :]) * g_ref[...]) * (1.0 + scale) + shift).astype(BF16)
        p = _dot(h, w_ref[...])

        up_ref[rows, :] = p[:, O_POOL:O_POOL + POOL_W]
        uf_ref[rows, :] = p[:, O_FN:O_FN + FNET_W]

        gq_ref[rows, :] = p[:, O_GQ:O_GQ + GLA_KW] * (GLA_DK ** -0.5)
        gk_ref[rows, :] = p[:, O_GK:O_GK + GLA_KW]
        gv_ref[rows, :] = p[:, O_GV:O_GV + GLA_W]
        og_ref[rows, :] = p[:, O_OG:O_OG + GLA_W]
        z = p[:, O_Z:O_Z + 2 * GLA_KW] + ba_ref[...]
        logsig = jnp.minimum(z, 0.0) - jnp.log(1.0 + jnp.exp(-jnp.abs(z)))
        gdec = logsig * (1.0 / GLA_TAU)
        gf_ref[rows, :] = gdec[:, :GLA_KW]
        gb_ref[rows, :] = gdec[:, GLA_KW:]

        rc = _tile_rope_table(lc_ref[lines, :], cc_ref[...])
        rsa = _tile_rope_table(lsa_ref[lines, :], csa_ref[...])
        rsb = _tile_rope_table(lsb_ref[lines, :], csb_ref[...])
        q = _group_rms(p[:, O_AQ:O_AQ + ATT_W], e, ATT_HD) * qg_ref[...]
        q = _rope(q, rc, rsa, rsb) * (ATT_HD ** -0.5 * math.log2(math.e))
        aq_ref[:, rows] = q.T.astype(BF16)
        k = _group_rms(p[:, O_AK:O_AK + ATT_KW], e[:ATT_KW, :ATT_KW], ATT_HD) * kg_ref[...]
        k = _rope(k, rc[:, :ATT_KW], rsa[:, :ATT_KW], rsb[:, :ATT_KW])
        ak_ref[rows, :] = k.astype(BF16)
        v_t = p[:, O_AV:O_AV + ATT_KW].T
        ones = jnp.ones((ATT_VROWS - ATT_HD, rp), F32)
        for kvh in range(ATT_KVH):
            av_ref[kvh, :, rows] = jnp.concatenate(
                [v_t[kvh * ATT_HD:(kvh + 1) * ATT_HD, :], ones], axis=0).astype(BF16)


def _in_projection(x, mods, g, w_in, ba_blk, e256, qg, kg, rope_lines, rope_cols,
                   *, layer, n_tiles, mod_map, line_map, col_map):
    t_rows, d = x.shape
    row = lambda w: pl.BlockSpec((TM, w), lambda t: (t, 0))
    const = lambda shape: pl.BlockSpec(shape, lambda t: (0,) * len(shape))
    per_layer = lambda arr: pl.BlockSpec((None,) + arr.shape[1:], lambda t: (layer,) + (0,) * (arr.ndim - 1))
    out_shapes = (
        jax.ShapeDtypeStruct((t_rows, POOL_W), F32),
        jax.ShapeDtypeStruct((t_rows, GLA_KW), F32),
        jax.ShapeDtypeStruct((t_rows, GLA_KW), F32),
        jax.ShapeDtypeStruct((t_rows, GLA_W), F32),
        jax.ShapeDtypeStruct((t_rows, GLA_W), F32),
        jax.ShapeDtypeStruct((t_rows, GLA_KW), F32),
        jax.ShapeDtypeStruct((t_rows, GLA_KW), F32),
        jax.ShapeDtypeStruct((n_tiles, ATT_W, TM), BF16),
        jax.ShapeDtypeStruct((t_rows, ATT_KW), BF16),
        jax.ShapeDtypeStruct((n_tiles, ATT_KVH, ATT_VROWS, TM), BF16),
        jax.ShapeDtypeStruct((t_rows, FNET_W), F32),
    )
    out_specs = (
        row(POOL_W), row(GLA_KW), row(GLA_KW), row(GLA_W), row(GLA_W), row(GLA_KW), row(GLA_KW),
        pl.BlockSpec((None, ATT_W, TM), lambda t: (t, 0, 0)),
        row(ATT_KW),
        pl.BlockSpec((None, ATT_KVH, ATT_VROWS, TM), lambda t: (t, 0, 0, 0)),
        row(FNET_W),
    )
    return pl.pallas_call(
        _inproj_kernel,
        out_shape=out_shapes,
        grid=(n_tiles,),
        in_specs=[
            pl.BlockSpec((TM, d), lambda t: (t, 0)),
            mod_map,
            pl.BlockSpec((None, None, 1, d), lambda t: (layer, 1, 0, 0)),
            per_layer(w_in),
            per_layer(ba_blk),
            const(e256.shape),
            per_layer(qg),
            per_layer(kg),
        ] + [pl.BlockSpec((TM // GRID_W, ATT_W), line_map)] * 3
          + [pl.BlockSpec((None, GRID_W, ATT_W), col_map)] * 3,
        out_specs=out_specs,
        compiler_params=_cparams(("arbitrary",)),
        name="in_projection",
    )(x, mods, g, w_in, ba_blk, e256, qg, kg, *rope_lines, *rope_cols)


def _pool_kernel(prev_ref, u_ref, next_ref, inv_ref, w_ref, s_ref, o_ref, *, n_chunks):
    halo = POOL_HALO
    chunk = u_ref.shape[0]
    ci = pl.program_id(1)
    u = u_ref[...]
    before = jnp.where(ci > 0, prev_ref[...], 0.0)
    after = jnp.where(ci < n_chunks - 1, next_ref[...], 0.0)
    xp = jnp.concatenate([before, u, after], axis=0)
    rows = chunk + 2 * halo
    lane = lax.broadcasted_iota(jnp.int32, (chunk, POOL_W), 1)
    acc = xp
    wsum = None
    for gi, w in enumerate(POOL_WINDOWS):
        acc = acc + pltpu.roll(acc, w // 2, 0)
        lead = w // 2 - 1
        win = acc if lead == 0 else pltpu.roll(acc, rows - lead, 0)
        win = win[halo:halo + chunk, :]
        wsum = win if wsum is None else jnp.where(lane >= gi * POOL_GW, win, wsum)
    m = wsum * inv_ref[...] - u
    o_ref[...] = _dot(m.astype(BF16), w_ref[...]) * s_ref[...]


def _pool_inverse_counts(seq):
    t = jnp.arange(seq, dtype=jnp.int32)
    cols = []
    for w in POOL_WINDOWS:
        cnt = jnp.minimum(t + w // 2, seq) - jnp.maximum(t - w // 2, 0)
        cols.append(jnp.broadcast_to((1.0 / cnt.astype(F32))[:, None], (seq, POOL_GW)))
    return jnp.concatenate(cols, axis=1)


def _pool_mix(u, inv_cnt, w_blk, s_pool, *, layer, seq, n_seq, first_block):
    chunk = min(1024, seq)
    n_chunks = seq // chunk
    halo = POOL_HALO
    last_halo = u.shape[0] // halo - 1
    start = lambda b, i: (first_block + b) * seq + i * chunk
    return pl.pallas_call(
        functools.partial(_pool_kernel, n_chunks=n_chunks),
        out_shape=jax.ShapeDtypeStruct((n_seq * seq, POOL_W), F32),
        grid=(n_seq, n_chunks),
        in_specs=[
            pl.BlockSpec((halo, POOL_W), lambda b, i: (jnp.maximum(start(b, i) // halo - 1, 0), 0)),
            pl.BlockSpec((chunk, POOL_W), lambda b, i: ((first_block + b) * n_chunks + i, 0)),
            pl.BlockSpec((halo, POOL_W), lambda b, i: (jnp.minimum((start(b, i) + chunk) // halo, last_halo), 0)),
            pl.BlockSpec((chunk, POOL_W), lambda b, i: (i, 0)),
            pl.BlockSpec((None, POOL_W, POOL_W), lambda b, i: (layer, 0, 0)),
            pl.BlockSpec((None, 1, POOL_W), lambda b, i: (layer, 0, 0)),
        ],
        out_specs=pl.BlockSpec((chunk, POOL_W), lambda b, i: (b * n_chunks + i, 0)),
        compiler_params=_cparams(("arbitrary", "arbitrary")),
        name="pool_mix",
    )(u, u, u, inv_cnt, w_blk, s_pool)


def _chunk_cumsums(g):
    sl = 8
    row = lax.broadcasted_iota(jnp.int32, (sl, g.shape[1]), 0)
    pre, suf = [], []
    for i in range(g.shape[0] // sl):
        p = q = g[i * sl:(i + 1) * sl, :]
        s = 1
        while s < sl:
            p = p + jnp.where(row >= s, pltpu.roll(p, s, 0), 0.0)
            q = q + jnp.where(row < sl - s, pltpu.roll(q, sl - s, 0), 0.0)
            s *= 2
        pre.append(p)
        suf.append(q)
    per_chunk = GLA_CHUNK // sl
    for c in range(len(pre) // per_chunk):
        for v in range(1, per_chunk):
            i = c * per_chunk + v
            pre[i] = pre[i] + jnp.broadcast_to(pre[i - 1][sl - 1:sl, :], pre[i].shape)
            j = (c + 1) * per_chunk - 1 - v
            suf[j] = suf[j] + jnp.broadcast_to(suf[j + 1][0:1, :], suf[j].shape)
    return jnp.concatenate(pre, axis=0), jnp.concatenate(suf, axis=0)


def _gla_kernel(*refs, n_batch, reverse, finalize):
    refs = list(refs)
    take = lambda: [refs.pop(0) for _ in range(n_batch)]
    q_refs, k_refs, v_refs, g_refs = take(), take(), take(), take()
    ex_ref, mk_ref, bm_ref = refs.pop(0), refs.pop(0), refs.pop(0)
    if finalize:
        of_ref = refs.pop(0)
        o_ref, st_ref, ob_ref = refs
    else:
        o_ref, st_ref = refs
        ob_ref = o_ref

    @pl.when(pl.program_id(0) == 0)
    def _():
        st_ref[...] = jnp.zeros(st_ref.shape, F32)

    sides, tots = [], []
    for b in range(n_batch):
        g = g_refs[b][...]
        pre, suf = _chunk_cumsums(g)
        sides.append(suf if reverse else pre)
        tots.append(pre + suf - g)
    worst = jnp.max(jnp.abs(tots[0]))
    for b in range(1, n_batch):
        worst = jnp.maximum(worst, jnp.max(jnp.abs(tots[b])))
    in_range = worst <= GLA_FAST_RANGE

    @pl.when(in_range)
    def _():
        for b in range(n_batch):
            _gla_fast_tile(q_refs[b], k_refs[b], v_refs[b], mk_ref, bm_ref, st_ref.at[b], ob_ref.at[b],
                           sides[b], tots[b], reverse)

    @pl.when(jnp.logical_not(in_range))
    def _():
        for b in range(n_batch):
            _gla_exact_tile(q_refs[b], k_refs[b], v_refs[b], g_refs[b], ex_ref, mk_ref, st_ref.at[b],
                            ob_ref.at[b], reverse)

    if finalize:
        o_ref[...] = ob_ref[...] + of_ref[...]


def _gla_fast_tile(q_ref, k_ref, v_ref, mk_ref, bm_ref, st_ref, ob_ref, b, tot, reverse):
    ch = GLA_CHUNK
    half = 0.5 * tot
    q, k = q_ref[...], k_ref[...]
    q_in = (q * jnp.exp(b - half)).astype(BF16)
    k_in = (k * jnp.exp(half - b)).astype(BF16)
    q_st = (q * jnp.exp(b)).astype(BF16)
    k_st = (k * jnp.exp(tot - b)).astype(BF16)
    dec = jnp.exp(tot)
    mk = mk_ref[...]
    mk16 = mk.astype(BF16)
    bm16 = bm_ref[...]
    irow = lax.broadcasted_iota(jnp.int32, (ch, GLA_HEADS * ch), 0)
    jcol = lax.broadcasted_iota(jnp.int32, (ch, GLA_HEADS * ch), 1) % ch
    seen = (jcol >= irow) if reverse else (jcol <= irow)
    nt = (((1,), (1,)), ((), ()))
    st = st_ref[...]
    n_chunks = TP // ch
    for ci in range(n_chunks):
        c = (n_chunks - 1 - ci) if reverse else ci
        rows = slice(c * ch, (c + 1) * ch)
        kb = jnp.concatenate([k_in[rows]] * GLA_HEADS, axis=0) * mk16
        a = lax.dot_general(q_in[rows], kb, nt, preferred_element_type=F32)
        a = jnp.where(seen, a, 0.0).astype(BF16)
        v = v_ref[rows, :]
        vb = jnp.concatenate([v.astype(BF16)] * GLA_HEADS, axis=0) * bm16
        o = _dot(a, vb) + lax.dot_general(q_st[rows], st.astype(BF16), nt, preferred_element_type=F32)
        kv = lax.dot_general(v.astype(BF16), k_st[rows], (((0,), (0,)), ((), ())),
                             preferred_element_type=F32)
        st = st * dec[c * ch:c * ch + 1, :] + kv * mk
        ob_ref[rows, :] = o
    st_ref[...] = st


def _gla_exact_tile(q_ref, k_ref, v_ref, g_ref, ex_ref, mk_ref, st_ref, ob_ref, reverse):
    sub = GLA_SUB
    n_chunks = TP // sub
    row = lax.broadcasted_iota(jnp.int32, (sub, GLA_KW), 0)
    ex = ex_ref[...]
    mk = mk_ref[...]

    def body(ci, carry):
        c = (n_chunks - 1 - ci) if reverse else ci
        r0 = pl.multiple_of(c * sub, sub)
        q = q_ref[pl.ds(r0, sub), :]
        k = k_ref[pl.ds(r0, sub), :]
        v = v_ref[pl.ds(r0, sub), :]
        b = g_ref[pl.ds(r0, sub), :]
        s = 1
        while s < sub:
            if reverse:
                b = b + jnp.where(row < sub - s, pltpu.roll(b, sub - s, 0), 0.0)
            else:
                b = b + jnp.where(row >= s, pltpu.roll(b, s, 0), 0.0)
            s *= 2
        edge = b[0:1, :] if reverse else b[sub - 1:sub, :]
        st = st_ref[...]
        qd = (q * jnp.exp(b)).astype(BF16)
        o = lax.dot_general(qd, st.astype(BF16), (((1,), (1,)), ((), ())),
                            preferred_element_type=F32)
        parts = []
        for j in range(sub):
            seen = (row <= j) if reverse else (row >= j)
            dec = jnp.exp(jnp.where(seen, b - b[j:j + 1, :], NEG_BIG))
            parts.append((dec * q * k[j:j + 1, :]).astype(BF16))
        r = _dot(jnp.concatenate(parts, axis=0), ex)
        for j in range(sub):
            o = o + r[j * sub:(j + 1) * sub, :] * v[j:j + 1, :]
        kd = (k * jnp.exp(edge - b)).astype(BF16)
        kv = lax.dot_general(v.astype(BF16), kd, (((0,), (0,)), ((), ())),
                             preferred_element_type=F32)
        st_ref[...] = st * jnp.exp(edge) + kv * mk
        ob_ref[pl.ds(r0, sub), :] = o
        return carry

    lax.fori_loop(0, n_chunks, body, 0, unroll=2)


def _gla_step_tile(n_lat, n_ctx, reverse):
    def tile(s):
        is_ctx = s < n_ctx
        sc = jnp.where(is_ctx, s, 0)
        sl = jnp.where(is_ctx, 0, s - n_ctx)
        if reverse:
            sc = n_ctx - 1 - sc
            sl = n_lat - 1 - sl
        return is_ctx, sc, sl
    return tile


def _gla_mix(gq, gk, gv, gf, gb, ex, mk, e256, *, n_batch, seq, ctx):
    n_lat, n_ctx = seq // TP, ctx // TP
    const = lambda shape: pl.BlockSpec(shape, lambda s: (0,) * len(shape))

    def run(reverse, finalize, gate, extra):
        tile = _gla_step_tile(n_lat, n_ctx, reverse)

        def rows(w, b):
            def index_map(s):
                is_ctx, sc, sl = tile(s)
                return (jnp.where(is_ctx, n_batch * n_lat + b * n_ctx + sc, b * n_lat + sl), 0)
            return pl.BlockSpec((TP, w), index_map)

        def per_batch(arr, w):
            return [arr] * n_batch, [rows(w, b) for b in range(n_batch)]

        def out_map(s):
            is_ctx, sc, sl = tile(s)
            return (0, jnp.where(is_ctx, n_lat + sc, sl), 0)

        stacked = pl.BlockSpec((n_batch, TP, GLA_W), out_map)
        args, in_specs = [], []
        for arr, w in ((gq, GLA_KW), (gk, GLA_KW), (gv, GLA_W), (gate, GLA_KW)):
            a, sp = per_batch(arr, w)
            args += a
            in_specs += sp
        args += [ex, mk, e256]
        in_specs += [const(ex.shape), const(mk.shape), const(e256.shape)]
        scratch = [pltpu.VMEM((n_batch, GLA_W, GLA_KW), F32)]
        if finalize:
            args.append(extra)
            in_specs.append(stacked)
            scratch.append(pltpu.VMEM((n_batch, TP, GLA_W), F32))
        return pl.pallas_call(
            functools.partial(_gla_kernel, n_batch=n_batch, reverse=reverse, finalize=finalize),
            out_shape=jax.ShapeDtypeStruct((n_batch, seq + ctx, GLA_W), F32),
            grid=(n_lat + n_ctx,),
            in_specs=in_specs,
            out_specs=stacked,
            scratch_shapes=scratch,
            compiler_params=_cparams(("arbitrary",)),
            name="gla_bwd" if reverse else "gla_fwd",
        )(*args)

    o_f = run(False, False, gf, None)
    return run(True, True, gb, o_f)


def _att_query_groups(q_t):
    tq = q_t.shape[1]
    zero = jnp.zeros((ATT_HD, tq), BF16)
    groups = []
    for g in range(ATT_KVH):
        halves = []
        for h in (2 * g, 2 * g + 1):
            qh = q_t[h * ATT_HD:(h + 1) * ATT_HD, :]
            halves.append(jnp.concatenate([qh, zero] if g == 0 else [zero, qh], axis=0))
        groups.append(jnp.concatenate(halves, axis=1))
    return groups


def _att_store(o_ref, weighted, denom, tq):
    heads = []
    for g in range(ATT_KVH):
        og = weighted[g] / denom[g]
        heads += [og[:, :tq], og[:, tq:]]
    o_ref[...] = jnp.concatenate(heads, axis=0).T.astype(o_ref.dtype)


def _att_kernel(*refs, n_main, has_ctx, bounded):
    refs = list(refs)
    q_ref, km_ref, vm_ref = refs.pop(0), refs.pop(0), refs.pop(0)
    kc_ref, vc_ref = (refs.pop(0), refs.pop(0)) if has_ctx else (None, None)
    o_ref, s0_ref, s1_ref = refs
    q_t = q_ref[...]
    tq = q_t.shape[1]
    tk = vm_ref.shape[3]
    groups = _att_query_groups(q_t)

    def scores(kblk, buf):
        for g in range(ATT_KVH):
            s = _dot(kblk, groups[g])
            buf[g, 0:kblk.shape[0], :] = jnp.exp2(s).astype(BF16) if bounded else s

    def consume(buf, rows, v_of, carry):
        out = []
        for g in range(ATT_KVH):
            m, l, acc = carry[g]
            s = buf[g, 0:rows, :]
            if bounded:
                pv = _dot(v_of(g), s)
                out.append((m, l + pv[ATT_HD:ATT_HD + 1, :], acc + pv[:ATT_HD, :]))
                continue
            m_new = jnp.maximum(m, jnp.max(s, axis=0, keepdims=True))
            alpha = jnp.exp2(m - m_new)
            p = jnp.exp2(s - m_new).astype(BF16)
            pv = _dot(v_of(g), p)
            acc = alpha * acc + pv[:ATT_HD, :]
            l = alpha * l + pv[ATT_HD:ATT_HD + 1, :]
            out.append((m_new, l, acc))
        return tuple(out)

    def k_main(j):
        return km_ref[pl.ds(pl.multiple_of(j * tk, tk), tk), :]

    def v_main(j):
        return lambda g: vm_ref[j, g]

    def pair(j, carry, issue_next):
        scores(k_main(j + 1), s1_ref)
        carry = consume(s0_ref, tk, v_main(j), carry)
        issue_next(j + 2)
        return consume(s1_ref, tk, v_main(j + 1), carry)

    carry = tuple((jnp.full((1, 2 * tq), -jnp.inf, F32), jnp.zeros((1, 2 * tq), F32),
                   jnp.zeros((ATT_HD, 2 * tq), F32)) for _ in range(ATT_KVH))
    next_main = lambda j: scores(k_main(j), s0_ref)
    if has_ctx:
        after_main = lambda j: scores(kc_ref[...], s0_ref)
    else:
        after_main = lambda j: None

    scores(k_main(0), s0_ref)
    if n_main > 1:
        pairs_per_trip = 4 if n_main % 8 == 0 else (2 if n_main % 4 == 0 else 1)

        def body(i, carry):
            for u in range(pairs_per_trip):
                carry = pair(2 * (pairs_per_trip * i + u), carry, next_main)
            return carry

        n_trips = n_main // (2 * pairs_per_trip)
        carry = lax.fori_loop(0, n_trips - 1, body, carry)
        for u in range(pairs_per_trip):
            last = u == pairs_per_trip - 1
            carry = pair(2 * (pairs_per_trip * (n_trips - 1) + u), carry, after_main if last else next_main)
    else:
        assert not has_ctx
        carry = consume(s0_ref, tk, v_main(0), carry)
    if has_ctx:
        carry = consume(s0_ref, kc_ref.shape[0], lambda g: vc_ref[g], carry)
    _att_store(o_ref, [c[2] for c in carry], [c[1] for c in carry], tq)


def _attention(aq, ak, av, *, n_batch, seq, ctx, ctx_queries, bounded=False):
    lat_tiles = seq // TM
    ctx_tile = n_batch * lat_tiles
    ctx_cols = lambda b, i: (ctx_tile + (b * ctx) // TM, 0, ((b * ctx) % TM) // ctx)
    ctx_vcols = lambda b, i: (ctx_tile + (b * ctx) // TM, 0, 0, ((b * ctx) % TM) // ctx)
    if not ctx_queries:
        sub = TM // TQ
        grid = (n_batch, seq // TQ)
        in_specs = [
            pl.BlockSpec((None, ATT_W, TQ), lambda b, i: (b * lat_tiles + i // sub, 0, i % sub)),
            pl.BlockSpec((seq, ATT_KW), lambda b, i: (b, 0)),
            pl.BlockSpec((lat_tiles, ATT_KVH, ATT_VROWS, TM), lambda b, i: (b, 0, 0, 0)),
            pl.BlockSpec((ctx, ATT_KW), lambda b, i: (n_batch * seq // ctx + b, 0)),
            pl.BlockSpec((None, ATT_KVH, ATT_VROWS, ctx), ctx_vcols),
        ]
        out_spec = pl.BlockSpec((TQ, ATT_W), lambda b, i: (b * (seq // TQ) + i, 0))
        assert lat_tiles % 2 == 0
        out_rows = n_batch * seq
        args = (aq, ak, av, ak, av)
        kern = functools.partial(_att_kernel, n_main=lat_tiles, has_ctx=True, bounded=bounded)
        s_shape = (ATT_KVH, TM, 2 * TQ)
    else:
        grid = (n_batch, 1)
        in_specs = [
            pl.BlockSpec((None, ATT_W, ctx), ctx_cols),
            pl.BlockSpec((ctx, ATT_KW), lambda b, i: (n_batch * seq // ctx + b, 0)),
            pl.BlockSpec((1, ATT_KVH, ATT_VROWS, ctx), ctx_vcols),
        ]
        out_spec = pl.BlockSpec((ctx, ATT_W), lambda b, i: (b, 0))
        kern = functools.partial(_att_kernel, n_main=1, has_ctx=False, bounded=False)
        out_rows = n_batch * ctx
        args = (aq, ak, av)
        s_shape = (ATT_KVH, ctx, 2 * ctx)
    return pl.pallas_call(
        kern,
        out_shape=jax.ShapeDtypeStruct((out_rows, ATT_W), BF16),
        grid=grid,
        in_specs=in_specs,
        out_specs=out_spec,
        scratch_shapes=[pltpu.VMEM(s_shape, BF16 if bounded else F32)] * 2,
        compiler_params=_cparams(("arbitrary", "arbitrary")),
        name="attention_ctx" if ctx_queries else ("attention_bounded" if bounded else "attention"),
    )(*args)


def _fft_a_kernel(x_ref, d_ref, tr_ref, ti_ref, yr_ref, yi_ref):
    rb, w = x_ref.shape[1], x_ref.shape[2]
    x = jnp.concatenate([x_ref[:, r, :] for r in range(rb)], axis=1)
    y = _dot(d_ref[...], x.astype(BF16))
    tr, ti = tr_ref[...], ti_ref[...]
    for r in range(rb):
        yr, yi = y[:FFT_N1, r * w:(r + 1) * w], y[FFT_N1:, r * w:(r + 1) * w]
        c, s = tr[:, r:r + 1], ti[:, r:r + 1]
        yr_ref[:, r, :] = yr * c - yi * s
        yi_ref[:, r, :] = yr * s + yi * c


def _channel_mix(xr, xi, c_ref, wf_ref, norm):
    xc = jnp.concatenate([xr, xi], axis=1).astype(BF16)
    f = _dot(xc, c_ref[...]) * norm
    return _dot(f.astype(BF16), wf_ref[...])


def _fft_c_kernel(yr_ref, yi_ref, m_ref, c_ref, wf_ref, o_ref, *, n2, norm):
    w = yr_ref.shape[1]
    blk = jnp.concatenate(
        [jnp.concatenate([yr_ref[j * n2:(j + 1) * n2, :] for j in range(8)], axis=1),
         jnp.concatenate([yi_ref[j * n2:(j + 1) * n2, :] for j in range(8)], axis=1)], axis=0)
    x = _dot(m_ref[...], blk.astype(BF16))
    xr = jnp.concatenate([x[:n2, j * w:(j + 1) * w] for j in range(8)], axis=0)
    xi = jnp.concatenate([x[n2:, j * w:(j + 1) * w] for j in range(8)], axis=0)
    y = _channel_mix(xr, xi, c_ref, wf_ref, norm)
    for j in range(8):
        o_ref[:, j, :] = y[j * n2:(j + 1) * n2, :]


def _fft_dense_kernel(x_ref, d_ref, c_ref, wf_ref, o_ref, *, n, norm):
    x = _dot(d_ref[...], x_ref[...].astype(BF16))
    o_ref[...] = _channel_mix(x[:n, :], x[n:, :], c_ref, wf_ref, norm)


def _dft_parts(n):
    idx = np.arange(n)
    ang = 2.0 * np.pi * ((idx[:, None] * idx[None, :]) % n) / n
    return np.cos(ang), np.sin(ang)


def _mxu_const(a):
    return jnp.asarray(a, F32).astype(BF16)


def _fnet_consts(seq):
    n1, n2 = FFT_N1, seq // FFT_N1
    c1, s1 = _dft_parts(n1)
    da = np.concatenate([c1, -s1], axis=0)
    k1 = np.arange(n1)[:, None]
    m2 = np.arange(n2)[None, :]
    ang = 2.0 * np.pi * ((k1 * m2) % seq) / seq
    tw_r, tw_i = np.cos(ang), -np.sin(ang)
    c2, s2 = _dft_parts(n2)
    mc = np.block([[c2, s2], [-s2, c2]])
    return da, tw_r, tw_i, mc


def _channel_consts():
    cc, sc = _dft_parts(FNET_HD)
    eye = np.eye(FNET_HEADS)
    return np.concatenate([np.kron(eye, cc), np.kron(eye, sc)], axis=0)


def _fnet_latent(uf3, wf, *, layer, n_batch, seq):
    n1, n2 = FFT_N1, seq // FFT_N1
    da, tw_r, tw_i, mc = _fnet_consts(seq)
    da, mc, cc = _mxu_const(da), _mxu_const(mc), _mxu_const(_channel_consts())
    rb = min(n2, 32)
    blocked = lambda tw: jnp.asarray(tw.reshape(n1, n2 // rb, rb).transpose(1, 0, 2), F32)
    const2 = lambda shape: pl.BlockSpec(shape, lambda b, j: (0,) * len(shape))
    slab = pl.BlockSpec((n1, rb, FNET_W), lambda b, j: (b, j, 0))
    yr, yi = pl.pallas_call(
        _fft_a_kernel,
        out_shape=(jax.ShapeDtypeStruct((n_batch * n1, n2, FNET_W), F32),) * 2,
        grid=(n_batch, n2 // rb),
        in_specs=[
            slab,
            const2(da.shape),
            pl.BlockSpec((None, n1, rb), lambda b, j: (j, 0, 0)),
            pl.BlockSpec((None, n1, rb), lambda b, j: (j, 0, 0)),
        ],
        out_specs=(slab, slab),
        compiler_params=_cparams(("arbitrary", "arbitrary")),
        name="fft_stage_a",
    )(uf3, da, blocked(tw_r), blocked(tw_i))
    yr = yr.reshape(n_batch * n1 * n2, FNET_W)
    yi = yi.reshape(n_batch * n1 * n2, FNET_W)
    norm = 1.0 / math.sqrt(seq * FNET_HD)
    out = pl.pallas_call(
        functools.partial(_fft_c_kernel, n2=n2, norm=norm),
        out_shape=jax.ShapeDtypeStruct((n_batch, n2, n1, FNET_W), F32),
        grid=(n_batch, n1 // 8),
        in_specs=[
            pl.BlockSpec((8 * n2, FNET_W), lambda b, i: (b * (n1 // 8) + i, 0)),
            pl.BlockSpec((8 * n2, FNET_W), lambda b, i: (b * (n1 // 8) + i, 0)),
            const2(mc.shape), const2(cc.shape),
            pl.BlockSpec((None,) + wf.shape[1:], lambda b, i: (layer, 0, 0)),
        ],
        out_specs=pl.BlockSpec((None, n2, 8, FNET_W), lambda b, i: (b, 0, i, 0)),
        compiler_params=_cparams(("arbitrary", "arbitrary")),
        name="fft_stage_c",
    )(yr, yi, mc, cc, wf)
    return out.reshape(n_batch * seq, FNET_W)


def _fnet_context(uf, wf, *, layer, n_batch, ctx, first_block):
    c, s = _dft_parts(ctx)
    dd, cc = _mxu_const(np.concatenate([c, -s], axis=0)), _mxu_const(_channel_consts())
    const = lambda shape: pl.BlockSpec(shape, lambda b: (0,) * len(shape))
    return pl.pallas_call(
        functools.partial(_fft_dense_kernel, n=ctx, norm=1.0 / math.sqrt(ctx * FNET_HD)),
        out_shape=jax.ShapeDtypeStruct((n_batch * ctx, FNET_W), F32),
        grid=(n_batch,),
        in_specs=[pl.BlockSpec((ctx, FNET_W), lambda b: (first_block + b, 0)),
                  const(dd.shape), const(cc.shape),
                  pl.BlockSpec((None,) + wf.shape[1:], lambda b: (layer, 0, 0))],
        out_specs=pl.BlockSpec((ctx, FNET_W), lambda b: (b, 0)),
        compiler_params=_cparams(("arbitrary",)),
        name="fft_context",
    )(uf, dd, cc, wf)


def _rope_tables(seq):
    freqs = ROPE_THETA ** (-jnp.arange(ROPE_FREQS, dtype=F32) / ROPE_FREQS)
    row_ang = jnp.arange(seq // GRID_W, dtype=F32)[:, None] * freqs
    col_ang = jnp.arange(GRID_W, dtype=F32)[:, None] * freqs
    lanes = np.arange(ATT_HD)
    is_row = jnp.asarray(np.tile(lanes < 2 * ROPE_FREQS, ATT_QH)[None, :], F32)
    first_half = jnp.asarray(np.tile((lanes // ROPE_FREQS) % 2 == 0, ATT_QH)[None, :], F32)
    spread = lambda a: jnp.tile(a, (1, ATT_W // ROPE_FREQS))

    def tables(ang, mask, ident_lines):
        cos, sin = spread(jnp.cos(ang)) * mask, spread(jnp.sin(ang)) * mask
        ident = jnp.zeros((ident_lines, ATT_W), F32)
        return (jnp.concatenate([cos, ident + mask], axis=0),
                jnp.concatenate([-sin * first_half, ident], axis=0),
                jnp.concatenate([sin * (1.0 - first_half), ident], axis=0))

    rows = tables(row_ang, is_row, TM // GRID_W)
    cols = tuple(t.reshape(2, GRID_W, ATT_W) for t in tables(col_ang, 1.0 - is_row, GRID_W))
    return rows, cols


def _block_ones(width, group):
    return jnp.asarray(np.kron(np.eye(width // group), np.ones((group, group))), BF16)


def kernel(x, c, ctx, c_ctx, w_mod, b_mod, norm_g, ffn_wg, ffn_wu, ffn_wd, w_in, w_out,
           pool_w, pool_scale, gla_wa, gla_ba, gla_norm, att_qnorm, att_knorm, fnet_w, final_norm):
    n_batch, seq, d = x.shape
    n_ctx = ctx.shape[1]
    assert d == D_MODEL and seq % TM == 0 and (n_batch * n_ctx) % TM == 0 and n_ctx % TP == 0
    assert seq % (8 * FFT_N1) == 0 and n_ctx <= TM and TM % n_ctx == 0 and n_batch + 1 <= 8
    lat_rows = n_batch * seq
    lat_tiles = lat_rows // TM
    all_tiles = lat_tiles + (n_batch * n_ctx) // TM
    tiles_per_batch = seq // TM
    line_map = lambda t: (jnp.where(t < lat_tiles, t % tiles_per_batch, tiles_per_batch), 0)
    col_map = lambda t: (jnp.where(t < lat_tiles, 0, 1), 0, 0)

    cvec = jnp.concatenate([c, c_ctx[None, :], jnp.zeros((8 - n_batch - 1, d), F32)], axis=0)
    mods = _modulation(cvec, w_mod, b_mod).reshape(DEPTH, 8, N_MOD, d)

    rope_lines, rope_cols = _rope_tables(seq)
    e256 = _block_ones(ATT_W, ATT_HD)
    gla_ex = _block_ones(GLA_W, GLA_DV)[::2, :]
    gla_mk = jnp.asarray(np.kron(np.eye(GLA_HEADS), np.ones((GLA_DV, GLA_DK))), F32)

    wg, wu, wd = ffn_wg.astype(BF16), ffn_wu.astype(BF16), ffn_wd.astype(BF16)
    w_rank = jnp.concatenate([w_in[..., 768:800], jnp.zeros((DEPTH, d, 128 - 2 * GLA_RANK), F32)], axis=-1)
    wa_blk = jnp.zeros((DEPTH, 128, 2 * GLA_KW), F32)
    wa_blk = wa_blk.at[:, :GLA_RANK, :GLA_KW].set(gla_wa[:, 0]).at[:, GLA_RANK:2 * GLA_RANK, GLA_KW:].set(gla_wa[:, 1])
    wi = jnp.concatenate([w_in[..., :768], w_in[..., 800:D_IN], _fold_gates(w_rank, wa_blk)], axis=-1).astype(BF16)
    ba_blk = gla_ba.reshape(DEPTH, 1, 2 * GLA_KW)
    qg = jnp.tile(att_qnorm, (1, ATT_QH))[:, None, :]
    kg = jnp.tile(att_knorm, (1, ATT_KVH))[:, None, :]
    pool_blk = jnp.zeros((DEPTH, POOL_W, POOL_W), F32)
    for gi in range(len(POOL_WINDOWS)):
        sl = slice(gi * POOL_GW, (gi + 1) * POOL_GW)
        pool_blk = pool_blk.at[:, sl, sl].set(pool_w[:, gi])
    pool_blk = pool_blk.astype(BF16)
    pool_s = pool_scale[:, None, :]
    wf = fnet_w.astype(BF16)
    wo = w_out.astype(BF16)
    norm4 = norm_g[:, :, None, :]
    n2 = seq // FFT_N1
    inv_lat = _pool_inverse_counts(seq)

    xs = x.reshape(lat_rows, d)
    ctx_rows = ctx.reshape(n_batch * n_ctx, d)
    for i in range(DEPTH):
        ctx_out = i < DEPTH - 1
        mod_map = _mod_spec(i, lat_tiles, tiles_per_batch, n_batch, d)
        gn = jnp.tile(gla_norm[i], GLA_HEADS)[None, :]

        xs = _half_ffn(xs, mods, norm4, wg, wu, wd, layer=i, n_tiles=all_tiles,
                       mod_map=mod_map, ctx_rows=ctx_rows if i == 0 else None)
        (u_pool, gq, gk, gv, og, gf, gb, aq, ak, av, uf) = _in_projection(
            xs, mods, norm4, wi, ba_blk, e256, qg, kg, rope_lines, rope_cols,
            layer=i, n_tiles=all_tiles, mod_map=mod_map, line_map=line_map, col_map=col_map)

        y_pool = _pool_mix(u_pool, inv_lat, pool_blk, pool_s, layer=i, seq=seq, n_seq=n_batch, first_block=0)
        y_gla = _gla_mix(gq, gk, gv, gf, gb, gla_ex, gla_mk, e256, n_batch=n_batch, seq=seq, ctx=n_ctx)
        score_bound = (ATT_HD * ATT_HD ** -0.5 * math.log2(math.e) * ATT_BOUND_SLACK
                       * jnp.max(jnp.abs(att_qnorm[i])) * jnp.max(jnp.abs(att_knorm[i])))
        attend = functools.partial(_attention, aq, ak, av, n_batch=n_batch, seq=seq, ctx=n_ctx, ctx_queries=False)
        y_att = lax.cond(score_bound <= ATT_BOUND_MAX, lambda: attend(bounded=True), lambda: attend())
        y_fnet = _fnet_latent(uf.reshape(uf.shape[0] // n2, n2, FNET_W), wf, layer=i, n_batch=n_batch, seq=seq)
        n_tiles, ctx_parts = lat_tiles, None
        if ctx_out:
            first_ctx = lat_rows // n_ctx
            ctx_parts = (
                _pool_mix(u_pool, _pool_inverse_counts(n_ctx), pool_blk, pool_s, layer=i, seq=n_ctx, n_seq=n_batch,
                          first_block=first_ctx),
                _attention(aq, ak, av, n_batch=n_batch, seq=seq, ctx=n_ctx, ctx_queries=True),
                _fnet_context(uf, wf, layer=i, n_batch=n_batch, ctx=n_ctx, first_block=first_ctx),
            )
            n_tiles = all_tiles
        xs = _mix_ffn(xs, mods, y_pool, y_gla, y_att, y_fnet, og, gn, e256, wo, norm4, wg, wu, wd, final_norm,
                      layer=i, n_tiles=n_tiles, mod_map=mod_map, final=not ctx_out, ctx_parts=ctx_parts)
    return xs.reshape(n_batch, seq, d)
```
